```python
import jax, jax.numpy as jnp
from jax import lax
import numpy as np

D_MODEL = 1024
BATCH = 4
SEQ = 4096
DEPTH = 2

SB_HEAD_DIM = 64
SB_HEADS = D_MODEL // 128
SB_WIDTH = SB_HEADS * SB_HEAD_DIM
SB_BLOCK = 128
SSM_GROUP = 16
SSM_WIDTH = D_MODEL // 4
SSM_GROUPS = SSM_WIDTH // SSM_GROUP
SSM_STATE = 64
SSM_DT_MIN = 1e-3
SSM_DT_MAX = 1e-1
SSM_MAX_RE = -1e-4
CONV_WIDTH = D_MODEL // 4
CONV_K = 3
N_BRANCH = 3
IN_SIZES = (SB_WIDTH, SB_WIDTH, SB_WIDTH, SSM_WIDTH, CONV_WIDTH, CONV_WIDTH, CONV_WIDTH, N_BRANCH * D_MODEL)
IN_COLS = sum(IN_SIZES)
IN_SPLITS = [sum(IN_SIZES[:i + 1]) for i in range(len(IN_SIZES) - 1)]
FF_DIM = 7 * D_MODEL // 2
N_EXPERTS = 8
TOP_K = 2
N_DENSE = (DEPTH + 1) // 2
N_MOE = DEPTH // 2
EPS = 1e-6

kernel_name = 'hybrid_sb_s5_shortconv_moe_trunk'


def rms_norm(x, g):
    xf = x.astype(jnp.float32)
    y = xf * lax.rsqrt(jnp.mean(xf * xf, axis=-1, keepdims=True) + EPS)
    return (y * g.astype(jnp.float32)).astype(x.dtype)


def stick_breaking_attention(q, k, v):
    b, l, h, dh = q.shape
    nb = l // SB_BLOCK
    scale = dh ** -0.5
    kt = jnp.transpose(k, (0, 2, 1, 3))
    vt = jnp.transpose(v, (0, 2, 1, 3))
    qb = q.reshape(b, nb, SB_BLOCK, h, dh).transpose(1, 0, 3, 2, 4)
    key_pos = jnp.arange(l)

    def block(args):
        q_blk, blk = args
        z = jnp.einsum('bhqd,bhkd->bhqk', q_blk, kt).astype(jnp.float32) * scale
        q_pos = blk * SB_BLOCK + jnp.arange(SB_BLOCK)
        valid = key_pos[None, :] < q_pos[:, None]
        log_stay = jnp.where(valid, jax.nn.log_sigmoid(-z), 0.0)
        tail = lax.cumsum(log_stay, axis=3, reverse=True) - log_stay
        w = jnp.where(valid, jnp.exp(jax.nn.log_sigmoid(z) + tail), 0.0)
        return jnp.einsum('bhqk,bhkd->bhqd', w.astype(vt.dtype), vt)

    out = lax.map(block, (qb, jnp.arange(nb)))
    return out.transpose(1, 0, 3, 2, 4).reshape(b, l, h * dh)


def s5_ssm(u, lam_re, lam_im, log_dt, b_re, b_im, c_re, c_im, d_skip, w_glu):
    f32 = jnp.float32
    bsz, l, _ = u.shape
    uf = u.astype(f32)
    ug = uf.reshape(bsz, l, SSM_GROUPS, SSM_GROUP)
    lr = jnp.minimum(lam_re.astype(f32), SSM_MAX_RE)
    li = lam_im.astype(f32)
    dt = jnp.exp(log_dt.astype(f32))[:, None]
    mag = jnp.exp(lr * dt)
    ab_re = mag * jnp.cos(li * dt)
    ab_im = mag * jnp.sin(li * dt)
    den = lr * lr + li * li
    nr = ab_re - 1.0
    ni = ab_im
    f_re = (nr * lr + ni * li) / den
    f_im = (ni * lr - nr * li) / den
    br = b_re.astype(f32)
    bi = b_im.astype(f32)
    bb_re = f_re[..., None] * br - f_im[..., None] * bi
    bb_im = f_re[..., None] * bi + f_im[..., None] * br
    bu_re = jnp.einsum('gph,blgh->blgp', bb_re, ug)
    bu_im = jnp.einsum('gph,blgh->blgp', bb_im, ug)
    a_re = jnp.broadcast_to(ab_re, bu_re.shape)
    a_im = jnp.broadcast_to(ab_im, bu_im.shape)

    def combine(left, right):
        a1r, a1i, b1r, b1i = left
        a2r, a2i, b2r, b2i = right
        return (a2r * a1r - a2i * a1i, a2r * a1i + a2i * a1r,
                a2r * b1r - a2i * b1i + b2r, a2r * b1i + a2i * b1r + b2i)

    _, _, s_re, s_im = lax.associative_scan(combine, (a_re, a_im, bu_re, bu_im), axis=1)
    y = (jnp.einsum('ghp,blgp->blgh', c_re.astype(f32), s_re)
         - jnp.einsum('ghp,blgp->blgh', c_im.astype(f32), s_im))
    y = y.reshape(bsz, l, SSM_WIDTH) + d_skip.astype(f32) * uf
    y = jax.nn.gelu(y)
    y = y * jax.nn.sigmoid(y @ w_glu.astype(f32))
    return y.astype(u.dtype)


def short_conv(h, b_gate, c_gate, conv_w):
    u = c_gate * h
    y = lax.conv_general_dilated(u, conv_w[:, None, :].astype(u.dtype), window_strides=(1,),
                                 padding=((CONV_K - 1, 0),),
                                 dimension_numbers=('NWC', 'WIO', 'NWC'),
                                 feature_group_count=CONV_WIDTH)
    return b_gate * y


def hybrid_mixer(xn, w_in, lam_re, lam_im, log_dt, b_re, b_im, c_re, c_im, d_skip, w_glu,
                 conv_w, w_br_a, w_br_b, w_br_c, w_out):
    bsz, l, _ = xn.shape
    proj = xn @ w_in
    q, k, v, u_ssm, h_conv, b_gate, c_gate, gate_logits = jnp.split(proj, IN_SPLITS, axis=-1)
    heads = (bsz, l, SB_HEADS, SB_HEAD_DIM)
    y_a = stick_breaking_attention(q.reshape(heads), k.reshape(heads), v.reshape(heads)) @ w_br_a
    y_b = s5_ssm(u_ssm, lam_re, lam_im, log_dt, b_re, b_im, c_re, c_im, d_skip, w_glu) @ w_br_b
    y_c = short_conv(h_conv, b_gate, c_gate, conv_w) @ w_br_c
    g = jax.nn.sigmoid(gate_logits).reshape(bsz, l, N_BRANCH, D_MODEL)
    merged = g[:, :, 0] * y_a + g[:, :, 1] * y_b + g[:, :, 2] * y_c
    return merged @ w_out


def swiglu(x, w_gate, w_up, w_down):
    return (jax.nn.silu(x @ w_gate) * (x @ w_up)) @ w_down


def moe_swiglu(xn, w_router, w_gate, w_up, w_down):
    logits = (xn @ w_router).astype(jnp.float32)
    top_vals, top_idx = lax.top_k(logits, TOP_K)
    top_w = jax.nn.softmax(top_vals, axis=-1)
    combine = jnp.sum(jax.nn.one_hot(top_idx, N_EXPERTS, dtype=jnp.float32) * top_w[..., None], axis=-2)
    out = jnp.zeros_like(xn)
    for e in range(N_EXPERTS):
        out = out + combine[..., e:e + 1].astype(xn.dtype) * swiglu(xn, w_gate[e], w_up[e], w_down[e])
    return out


def setup_inputs(seed: int = 0) -> dict:
    key = jax.random.key(seed)
    ks = jax.random.split(key, 26)
    f32 = jnp.float32

    def nrm(k, shape, scale):
        return jax.random.normal(k, shape, f32) * scale

    def gain(k, shape):
        return 1.0 + 0.05 * jax.random.normal(k, shape, f32)

    ssm_shape = (DEPTH, SSM_GROUPS, SSM_STATE)
    n_idx = jnp.arange(SSM_STATE, dtype=f32)
    return {
        'x': nrm(ks[0], (BATCH, SEQ, D_MODEL), 1.0),
        'norm_mix': gain(ks[1], (DEPTH, D_MODEL)),
        'w_in': nrm(ks[2], (DEPTH, D_MODEL, IN_COLS), D_MODEL ** -0.5),
        'ssm_lambda_re': -0.5 + 0.01 * jax.random.normal(ks[3], ssm_shape, f32),
        'ssm_lambda_im': jnp.pi * n_idx + 0.01 * jax.random.normal(ks[4], ssm_shape, f32),
        'ssm_log_dt': jax.random.uniform(ks[5], (DEPTH, SSM_GROUPS), f32,
                                         float(np.log(SSM_DT_MIN)), float(np.log(SSM_DT_MAX))),
        'ssm_b_re': nrm(ks[6], (DEPTH, SSM_GROUPS, SSM_STATE, SSM_GROUP), (2 * SSM_GROUP) ** -0.5),
        'ssm_b_im': nrm(ks[7], (DEPTH, SSM_GROUPS, SSM_STATE, SSM_GROUP), (2 * SSM_GROUP) ** -0.5),
        'ssm_c_re': nrm(ks[8], (DEPTH, SSM_GROUPS, SSM_GROUP, SSM_STATE), SSM_STATE ** -0.5),
        'ssm_c_im': nrm(ks[9], (DEPTH, SSM_GROUPS, SSM_GROUP, SSM_STATE), SSM_STATE ** -0.5),
        'ssm_d': nrm(ks[10], (DEPTH, SSM_WIDTH), 1.0),
        'ssm_w_glu': nrm(ks[11], (DEPTH, SSM_WIDTH, SSM_WIDTH), SSM_WIDTH ** -0.5),
        'conv_w': nrm(ks[12], (DEPTH, CONV_K, CONV_WIDTH), CONV_K ** -0.5),
        'w_br_a': nrm(ks[13], (DEPTH, SB_WIDTH, D_MODEL), SB_WIDTH ** -0.5),
        'w_br_b': nrm(ks[14], (DEPTH, SSM_WIDTH, D_MODEL), SSM_WIDTH ** -0.5),
        'w_br_c': nrm(ks[15], (DEPTH, CONV_WIDTH, D_MODEL), CONV_WIDTH ** -0.5),
        'w_out': nrm(ks[16], (DEPTH, D_MODEL, D_MODEL), D_MODEL ** -0.5),
        'norm_ffn': gain(ks[17], (DEPTH, D_MODEL)),
        'dense_w_gate': nrm(ks[18], (N_DENSE, D_MODEL, FF_DIM), D_MODEL ** -0.5),
        'dense_w_up': nrm(ks[19], (N_DENSE, D_MODEL, FF_DIM), D_MODEL ** -0.5),
        'dense_w_down': nrm(ks[20], (N_DENSE, FF_DIM, D_MODEL), FF_DIM ** -0.5),
        'moe_w_router': nrm(ks[21], (N_MOE, D_MODEL, N_EXPERTS), D_MODEL ** -0.5),
        'moe_w_gate': nrm(ks[22], (N_MOE, N_EXPERTS, D_MODEL, FF_DIM), D_MODEL ** -0.5),
        'moe_w_up': nrm(ks[23], (N_MOE, N_EXPERTS, D_MODEL, FF_DIM), D_MODEL ** -0.5),
        'moe_w_down': nrm(ks[24], (N_MOE, N_EXPERTS, FF_DIM, D_MODEL), FF_DIM ** -0.5),
        'final_norm': gain(ks[25], (D_MODEL,)),
    }


def reference(x, norm_mix, w_in, ssm_lambda_re, ssm_lambda_im, ssm_log_dt, ssm_b_re, ssm_b_im,
              ssm_c_re, ssm_c_im, ssm_d, ssm_w_glu, conv_w, w_br_a, w_br_b, w_br_c, w_out,
              norm_ffn, dense_w_gate, dense_w_up, dense_w_down, moe_w_router, moe_w_gate,
              moe_w_up, moe_w_down, final_norm):
    h = x
    for i in range(DEPTH):
        xn = rms_norm(h, norm_mix[i])
        h = h + hybrid_mixer(xn, w_in[i], ssm_lambda_re[i], ssm_lambda_im[i], ssm_log_dt[i],
                             ssm_b_re[i], ssm_b_im[i], ssm_c_re[i], ssm_c_im[i], ssm_d[i],
                             ssm_w_glu[i], conv_w[i], w_br_a[i], w_br_b[i], w_br_c[i], w_out[i])
        xn = rms_norm(h, norm_ffn[i])
        if i % 2 == 0:
            j = i // 2
            h = h + swiglu(xn, dense_w_gate[j], dense_w_up[j], dense_w_down[j])
        else:
            j = i // 2
            h = h + moe_swiglu(xn, moe_w_router[j], moe_w_gate[j], moe_w_up[j], moe_w_down[j])
    return rms_norm(h, final_norm)
```

```python
import functools

import jax
import jax.numpy as jnp
import numpy as np
from jax import lax
from jax.experimental import pallas as pl
from jax.experimental.pallas import tpu as pltpu

F32 = jnp.float32
BF16 = jnp.bfloat16

EPS = 1e-6
HEAD_DIM = 64
SSM_GROUP = 16
SSM_CHUNK = 16
SSM_MAX_RE = -1e-4
CONV_K = 3
N_BRANCH = 3
TOP_K = 2

LANES = 128
SUBLANES = 8
VMEM_LIMIT_BYTES = 56 * 1024 * 1024

TM_PROJ = 512
TQ_ATTN = 256
TM_MERGE = 512
TM_FFN = 1024
TF_FFN = 512
TM_MOE = 512
TM_ROUTE = 512
TM_NORM = 512


def _params(*sem):
    return pltpu.CompilerParams(dimension_semantics=sem, vmem_limit_bytes=VMEM_LIMIT_BYTES)


def _rms(x, g):
    ms = jnp.mean(x * x, axis=-1, keepdims=True)
    return x * lax.rsqrt(ms + EPS) * g


def _inproj_body(x_ref, g_ref, w_ref, qkv_ref, sc_ref, gate_ref, *, n_qkv, n_sc, n_gate, n_q, q_scale):
    xn = _rms(x_ref[...], g_ref[...]).astype(BF16)
    chunk = 512
    for c0 in range(0, n_qkv, chunk):
        r = jnp.dot(xn, w_ref[:, c0:c0 + chunk], preferred_element_type=F32)
        if c0 < n_q:
            r = r * q_scale
        qkv_ref[:, c0:c0 + chunk] = r.astype(BF16)
    for c0 in range(0, n_sc, chunk):
        r = jnp.dot(xn, w_ref[:, n_qkv + c0:n_qkv + c0 + chunk], preferred_element_type=F32)
        sc_ref[:, c0:c0 + chunk] = r.astype(BF16)
    for c0 in range(0, n_gate, chunk):
        o = n_qkv + n_sc + c0
        r = jnp.dot(xn, w_ref[:, o:o + chunk], preferred_element_type=F32)
        gate_ref[:, c0:c0 + chunk] = jax.nn.sigmoid(r).astype(BF16)


def _inproj(h, g, w_bf16, *, n_q, n_qkv, n_sc, n_gate):
    t, d = h.shape
    tm = TM_PROJ
    body = functools.partial(_inproj_body, n_qkv=n_qkv, n_sc=n_sc, n_gate=n_gate, n_q=n_q,
                             q_scale=HEAD_DIM ** -0.5)
    return pl.pallas_call(
        body,
        grid=(t // tm,),
        in_specs=[
            pl.BlockSpec((tm, d), lambda i: (i, 0)),
            pl.BlockSpec((1, d), lambda i: (0, 0)),
            pl.BlockSpec((d, n_qkv + n_sc + n_gate), lambda i: (0, 0), pipeline_mode=pl.Buffered(1)),
        ],
        out_specs=[
            pl.BlockSpec((tm, n_qkv), lambda i: (i, 0)),
            pl.BlockSpec((tm, n_sc), lambda i: (i, 0)),
            pl.BlockSpec((tm, n_gate), lambda i: (i, 0)),
        ],
        out_shape=[
            jax.ShapeDtypeStruct((t, n_qkv), BF16),
            jax.ShapeDtypeStruct((t, n_sc), BF16),
            jax.ShapeDtypeStruct((t, n_gate), BF16),
        ],
        compiler_params=_params("parallel"),
        name="inproj",
    )(h, g.reshape(1, d), w_bf16)


def _sb_attn_body(q_ref, k_ref, v_ref, o_ref, acc_ref, carry_ref, *, tq):
    i = pl.program_id(2)
    q = q_ref[...]
    lane = lax.broadcasted_iota(jnp.int32, (tq, LANES), 1)
    row = lax.broadcasted_iota(jnp.int32, (tq, tq), 0)
    col = lax.broadcasted_iota(jnp.int32, (tq, tq), 1)
    below_diag = col < row
    later_key = jnp.where(row > col, 1.0, 0.0).astype(BF16)
    ones = jnp.ones((tq, LANES), BF16)
    zero_q = jnp.zeros_like(q)
    q_heads = (jnp.where(lane < HEAD_DIM, q, zero_q), jnp.where(lane >= HEAD_DIM, q, zero_q))

    def tile(j, masked):
        start = pl.multiple_of(j * tq, tq)
        kblk = k_ref[pl.ds(start, tq), :]
        vblk = v_ref[pl.ds(start, tq), :]
        for hh in range(2):
            z = lax.dot_general(q_heads[hh], kblk, (((1,), (1,)), ((), ())),
                                preferred_element_type=F32)
            l1p = jnp.log(1.0 + jnp.exp(-jnp.abs(z)))
            sp = jnp.maximum(z, 0.0) + l1p
            if masked:
                sp = jnp.where(below_diag, sp, 0.0)
            spb = sp.astype(BF16)
            carry = carry_ref[hh]
            tail = jnp.dot(spb, later_key, preferred_element_type=F32) + jnp.concatenate([carry, carry], axis=1)
            w = jnp.exp(jnp.minimum(z, 0.0) - l1p - tail)
            if masked:
                w = jnp.where(below_diag, w, 0.0)
            acc_ref[hh] += jnp.dot(w.astype(BF16), vblk, preferred_element_type=F32)
            carry_ref[hh] = carry + jnp.dot(spb, ones, preferred_element_type=F32)

    acc_ref[...] = jnp.zeros_like(acc_ref)
    carry_ref[...] = jnp.zeros_like(carry_ref)
    tile(i, True)

    def body(n, c):
        tile(i - 1 - n, False)
        return c

    lax.fori_loop(0, i, body, 0)
    o_ref[...] = jnp.where(lane < HEAD_DIM, acc_ref[0], acc_ref[1]).astype(o_ref.dtype)


def _sb_attention(qkv, *, batch, seq, width):
    t = batch * seq
    tq = TQ_ATTN
    nq = seq // tq
    npair = width // LANES
    return pl.pallas_call(
        functools.partial(_sb_attn_body, tq=tq),
        grid=(batch, npair, nq),
        in_specs=[
            pl.BlockSpec((tq, LANES), lambda b, p, i: (b * nq + i, p)),
            pl.BlockSpec((seq, LANES), lambda b, p, i: (b, npair + p)),
            pl.BlockSpec((seq, LANES), lambda b, p, i: (b, 2 * npair + p)),
        ],
        out_specs=pl.BlockSpec((tq, LANES), lambda b, p, i: (b * nq + i, p)),
        out_shape=jax.ShapeDtypeStruct((t, width), BF16),
        scratch_shapes=[pltpu.VMEM((2, tq, LANES), F32), pltpu.VMEM((2, tq, LANES), F32)],
        compiler_params=_params("parallel", "parallel", "arbitrary"),
        name="sb_attention",
    )(qkv, qkv, qkv)


def _ssm_tables(lam_re, lam_im, log_dt, b_re, b_im, c_re, c_im):
    hp = lax.Precision.HIGHEST
    g, p = lam_re.shape
    hdim = b_re.shape[-1]
    c = SSM_CHUNK
    lr = jnp.minimum(lam_re.astype(F32), SSM_MAX_RE)
    li = lam_im.astype(F32)
    dt = jnp.exp(log_dt.astype(F32))[:, None]

    def lbar_pow(k):
        mag = jnp.exp(k * (lr * dt))
        ang = k * (li * dt)
        return mag * jnp.cos(ang), mag * jnp.sin(ang)

    ab_re, ab_im = lbar_pow(1.0)
    den = lr * lr + li * li
    nr = ab_re - 1.0
    ni = ab_im
    f_re = (nr * lr + ni * li) / den
    f_im = (ni * lr - nr * li) / den
    br = b_re.astype(F32)
    bi = b_im.astype(F32)
    bb_re = f_re[..., None] * br - f_im[..., None] * bi
    bb_im = f_re[..., None] * bi + f_im[..., None] * br
    cr = c_re.astype(F32)
    ci = c_im.astype(F32)

    ks = jnp.arange(c + 1, dtype=F32)[:, None, None]
    pw_re, pw_im = lbar_pow(ks)

    cl_re = cr[None] * pw_re[:c, :, None, :] - ci[None] * pw_im[:c, :, None, :]
    cl_im = cr[None] * pw_im[:c, :, None, :] + ci[None] * pw_re[:c, :, None, :]
    kk = (jnp.einsum('kghp,gpi->kghi', cl_re, bb_re, precision=hp)
          - jnp.einsum('kghp,gpi->kghi', cl_im, bb_im, precision=hp))
    tt = jnp.arange(c)[:, None]
    jj = jnp.arange(c)[None, :]
    lag = jnp.clip(tt - jj, 0, c - 1)
    m = jnp.where((tt >= jj)[:, :, None, None, None], kk[lag], 0.0)
    intra = jnp.transpose(m, (2, 1, 4, 0, 3)).reshape(g, c * hdim, c * hdim)

    rev_re = pw_re[:c][::-1]
    rev_im = pw_im[:c][::-1]
    ps_re = rev_re[..., None] * bb_re[None] - rev_im[..., None] * bb_im[None]
    ps_im = rev_re[..., None] * bb_im[None] + rev_im[..., None] * bb_re[None]
    to_state = jnp.concatenate([jnp.transpose(ps_re, (1, 0, 3, 2)).reshape(g, c * hdim, p),
                                jnp.transpose(ps_im, (1, 0, 3, 2)).reshape(g, c * hdim, p)], axis=-1)

    q_re = cr[None] * pw_re[1:, :, None, :] - ci[None] * pw_im[1:, :, None, :]
    q_im = cr[None] * pw_im[1:, :, None, :] + ci[None] * pw_re[1:, :, None, :]
    from_state = jnp.concatenate([jnp.transpose(q_re, (1, 3, 0, 2)).reshape(g, p, c * hdim),
                                  jnp.transpose(-q_im, (1, 3, 0, 2)).reshape(g, p, c * hdim)], axis=1)
    return intra.astype(BF16), to_state.astype(BF16), from_state.astype(BF16), lbar_pow


def _ssm_step_table(lbar_pow, n_chunks):
    steps = []
    s = 1
    while s < n_chunks:
        a_re, a_im = lbar_pow(float(SSM_CHUNK * s))
        steps.append(jnp.stack([jnp.concatenate([a_re, a_re], axis=-1),
                                jnp.concatenate([-a_im, a_im], axis=-1)], axis=1))
        s *= 2
    return jnp.stack(steps, axis=1)


def _ssm_body(u_ref, intra_ref, to_state_ref, from_state_ref, step_ref, y_ref, *, n_chunks, n_state):
    u = u_ref[0]
    rows = u.shape[0]
    z = jnp.dot(u, to_state_ref[0], preferred_element_type=F32)
    chunk_idx = lax.broadcasted_iota(jnp.int32, (rows, 2 * n_state), 0) % n_chunks
    s = 1
    k = 0
    while s < n_chunks:
        zs = jnp.where(chunk_idx >= s, pltpu.roll(z, s, axis=0), 0.0)
        zs_swapped = pltpu.roll(zs, n_state, axis=1)
        z = z + step_ref[0, k, 0:1, :] * zs + step_ref[0, k, 1:2, :] * zs_swapped
        s *= 2
        k += 1
    prev = jnp.where(chunk_idx >= 1, pltpu.roll(z, 1, axis=0), 0.0)
    y = jnp.dot(u, intra_ref[0], preferred_element_type=F32)
    y = y + jnp.dot(prev.astype(BF16), from_state_ref[0], preferred_element_type=F32)
    y_ref[0] = y.astype(y_ref.dtype)


def _ssm_scan(u, tables, *, batch, seq):
    intra, to_state, from_state, step = tables
    g = intra.shape[0]
    width = u.shape[1]
    hdim = width // g
    c = SSM_CHUNK
    n_chunks = seq // c
    rows = batch * n_chunks
    n_state = to_state.shape[-1] // 2
    n_steps = step.shape[1]
    ug = u.reshape(batch, n_chunks, c, g, hdim).transpose(3, 0, 1, 2, 4).reshape(g, rows, c * hdim)
    y = pl.pallas_call(
        functools.partial(_ssm_body, n_chunks=n_chunks, n_state=n_state),
        grid=(g,),
        in_specs=[
            pl.BlockSpec((1, rows, c * hdim), lambda i: (i, 0, 0)),
            pl.BlockSpec((1, c * hdim, c * hdim), lambda i: (i, 0, 0)),
            pl.BlockSpec((1, c * hdim, 2 * n_state), lambda i: (i, 0, 0)),
            pl.BlockSpec((1, 2 * n_state, c * hdim), lambda i: (i, 0, 0)),
            pl.BlockSpec((1, n_steps, 2, 2 * n_state), lambda i: (i, 0, 0, 0)),
        ],
        out_specs=pl.BlockSpec((1, rows, c * hdim), lambda i: (i, 0, 0)),
        out_shape=jax.ShapeDtypeStruct((g, rows, c * hdim), BF16),
        compiler_params=_params("parallel"),
        name="ssm_scan",
    )(ug, intra, to_state, from_state, step)
    return y.reshape(g, batch, n_chunks, c, hdim).transpose(1, 2, 3, 0, 4).reshape(batch * seq, width)


def _gelu_tanh(x):
    return 0.5 * x * (1.0 + jnp.tanh(0.7978845608028654 * (x + 0.044715 * (x * x * x))))


def _merge_body(attn_ref, yssm_ref, sc_ref, halo_ref, gate_ref, h_ref, dskip_ref, wglu_ref, convw_ref,
                wa_ref, wb_ref, wc_ref, wout_ref, gn_ref, hout_ref, xn_ref, *, tiles_per_seq, w_ssm, w_conv):
    i = pl.program_id(0)
    tm = h_ref.shape[0]
    d = h_ref.shape[1]

    u = sc_ref[:, 0:w_ssm].astype(F32)
    yb = _gelu_tanh(yssm_ref[...].astype(F32) + dskip_ref[...] * u)
    yb = yb * jax.nn.sigmoid(jnp.dot(yb.astype(BF16), wglu_ref[...], preferred_element_type=F32))
    y_b = jnp.dot(yb.astype(BF16), wb_ref[...], preferred_element_type=F32)

    o_h, o_b, o_c = w_ssm, w_ssm + w_conv, w_ssm + 2 * w_conv
    uc = sc_ref[:, o_c:o_c + w_conv].astype(F32) * sc_ref[:, o_h:o_h + w_conv].astype(F32)
    halo = halo_ref[:, o_c:o_c + w_conv].astype(F32) * halo_ref[:, o_h:o_h + w_conv].astype(F32)
    halo = jnp.where(i % tiles_per_seq == 0, 0.0, halo)
    row = lax.broadcasted_iota(jnp.int32, (tm, w_conv), 0)
    prev1 = jnp.where(row == 0, halo[SUBLANES - 1:SUBLANES, :], pltpu.roll(uc, 1, axis=0))
    prev2 = jnp.where(row == 0, halo[SUBLANES - 2:SUBLANES - 1, :],
                      jnp.where(row == 1, halo[SUBLANES - 1:SUBLANES, :], pltpu.roll(uc, 2, axis=0)))
    conv = convw_ref[0:1, :] * prev2 + convw_ref[1:2, :] * prev1 + convw_ref[2:3, :] * uc
    yc = sc_ref[:, o_b:o_b + w_conv].astype(F32) * conv
    y_c = jnp.dot(yc.astype(BF16), wc_ref[...], preferred_element_type=F32)

    y_a = jnp.dot(attn_ref[...], wa_ref[...], preferred_element_type=F32)

    merged = (gate_ref[:, 0:d].astype(F32) * y_a + gate_ref[:, d:2 * d].astype(F32) * y_b
              + gate_ref[:, 2 * d:3 * d].astype(F32) * y_c)
    hn = h_ref[...] + jnp.dot(merged.astype(BF16), wout_ref[...], preferred_element_type=F32)
    hout_ref[...] = hn
    xn_ref[...] = _rms(hn, gn_ref[...]).astype(xn_ref.dtype)


def _merge(attn, yssm, sc, gates, h, d_skip, w_glu, conv_w, w_a, w_b, w_c, w_out, g_ffn, *, seq):
    t, d = h.shape
    tm = TM_MERGE
    w_ssm = yssm.shape[1]
    w_conv = conv_w.shape[1]
    w_attn = attn.shape[1]
    n_sc = sc.shape[1]
    body = functools.partial(_merge_body, tiles_per_seq=seq // tm, w_ssm=w_ssm, w_conv=w_conv)
    full = lambda shape: pl.BlockSpec(shape, lambda i: (0,) * len(shape))
    halo_blocks = tm // SUBLANES
    return pl.pallas_call(
        body,
        grid=(t // tm,),
        in_specs=[
            pl.BlockSpec((tm, w_attn), lambda i: (i, 0)),
            pl.BlockSpec((tm, w_ssm), lambda i: (i, 0)),
            pl.BlockSpec((tm, n_sc), lambda i: (i, 0)),
            pl.BlockSpec((SUBLANES, n_sc), lambda i: (jnp.maximum(i * halo_blocks - 1, 0), 0)),
            pl.BlockSpec((tm, N_BRANCH * d), lambda i: (i, 0)),
            pl.BlockSpec((tm, d), lambda i: (i, 0)),
            full((1, w_ssm)), full((w_ssm, w_ssm)), full((CONV_K, w_conv)),
            full((w_attn, d)), full((w_ssm, d)), full((w_conv, d)), full((d, d)), full((1, d)),
        ],
        out_specs=[pl.BlockSpec((tm, d), lambda i: (i, 0)), pl.BlockSpec((tm, d), lambda i: (i, 0))],
        out_shape=[jax.ShapeDtypeStruct((t, d), F32), jax.ShapeDtypeStruct((t, d), BF16)],
        compiler_params=_params("parallel"),
        name="merge",
    )(attn, yssm, sc, sc, gates, h, d_skip.reshape(1, w_ssm).astype(F32), w_glu, conv_w.astype(F32),
      w_a, w_b, w_c, w_out, g_ffn.reshape(1, d))


def _swiglu_tile(x, wg_ref, wu_ref, wd_ref):
    gate = jnp.dot(x, wg_ref[...], preferred_element_type=F32)
    up = jnp.dot(x, wu_ref[...], preferred_element_type=F32)
    act = (gate * jax.nn.sigmoid(gate) * up).astype(BF16)
    return jnp.dot(act, wd_ref[...], preferred_element_type=F32)


def _dense_ffn_body(x_ref, h_ref, wg_ref, wu_ref, wd_ref, o_ref, acc_ref):
    j = pl.program_id(1)
    part = _swiglu_tile(x_ref[...], wg_ref, wu_ref, wd_ref)

    @pl.when(j == 0)
    def _():
        acc_ref[...] = h_ref[...] + part

    @pl.when(j > 0)
    def _():
        acc_ref[...] += part

    @pl.when(j == pl.num_programs(1) - 1)
    def _():
        o_ref[...] = acc_ref[...]


def _dense_ffn(xn, h, w_gate, w_up, w_down):
    t, d = h.shape
    f = w_gate.shape[1]
    tm, tf = TM_FFN, TF_FFN
    return pl.pallas_call(
        _dense_ffn_body,
        grid=(t // tm, f // tf),
        in_specs=[
            pl.BlockSpec((tm, d), lambda i, j: (i, 0)),
            pl.BlockSpec((tm, d), lambda i, j: (i, 0)),
            pl.BlockSpec((d, tf), lambda i, j: (0, j)),
            pl.BlockSpec((d, tf), lambda i, j: (0, j)),
            pl.BlockSpec((tf, d), lambda i, j: (j, 0)),
        ],
        out_specs=pl.BlockSpec((tm, d), lambda i, j: (i, 0)),
        out_shape=jax.ShapeDtypeStruct((t, d), F32),
        scratch_shapes=[pltpu.VMEM((tm, d), F32)],
        compiler_params=_params("parallel", "arbitrary"),
        name="dense_ffn",
    )(xn, h, w_gate, w_up, w_down)


def _router_body(h_ref, g_ref, wr_ref, idx_ref, wgt_ref):
    xn = _rms(h_ref[...], g_ref[...])
    logits = lax.dot_general(wr_ref[...], xn, (((1,), (1,)), ((), ())), preferred_element_type=F32,
                             precision=lax.Precision.HIGHEST)
    n_e = logits.shape[0]
    e_idx = lax.broadcasted_iota(jnp.int32, logits.shape, 0)
    m1 = jnp.max(logits, axis=0, keepdims=True)
    i1 = jnp.min(jnp.where(logits == m1, e_idx, n_e), axis=0, keepdims=True)
    rest = jnp.where(e_idx == i1, -jnp.inf, logits)
    m2 = jnp.max(rest, axis=0, keepdims=True)
    i2 = jnp.min(jnp.where(rest == m2, e_idx, n_e), axis=0, keepdims=True)
    e2 = jnp.exp(m2 - m1)
    w1 = 1.0 / (1.0 + e2)
    idx_ref[0] = jnp.concatenate([i1, i2], axis=0)
    wgt_ref[0] = jnp.concatenate([w1, e2 * w1], axis=0)


def _router(h, g_ffn, w_router):
    t, d = h.shape
    n_e = w_router.shape[1]
    tm = TM_ROUTE
    nt = t // tm
    idx, wgt = pl.pallas_call(
        _router_body,
        grid=(nt,),
        in_specs=[
            pl.BlockSpec((tm, d), lambda i: (i, 0)),
            pl.BlockSpec((1, d), lambda i: (0, 0)),
            pl.BlockSpec((n_e, d), lambda i: (0, 0)),
        ],
        out_specs=[pl.BlockSpec((1, TOP_K, tm), lambda i: (i, 0, 0)),
                   pl.BlockSpec((1, TOP_K, tm), lambda i: (i, 0, 0))],
        out_shape=[jax.ShapeDtypeStruct((nt, TOP_K, tm), jnp.int32),
                   jax.ShapeDtypeStruct((nt, TOP_K, tm), F32)],
        compiler_params=_params("parallel"),
        name="router",
    )(h, g_ffn.reshape(1, d), w_router.T.astype(F32))
    idx = idx.transpose(0, 2, 1).reshape(t, TOP_K)
    wgt = wgt.transpose(0, 2, 1).reshape(t, TOP_K)
    return idx, wgt


def _moe_ffn_body(tile_expert_ref, tile_valid_ref, x_ref, rw_ref, wg_ref, wu_ref, wd_ref, o_ref, acc_ref):
    i = pl.program_id(0)
    j = pl.program_id(1)
    valid = tile_valid_ref[i] > 0

    @pl.when(valid)
    def _():
        part = _swiglu_tile(x_ref[...], wg_ref.at[0], wu_ref.at[0], wd_ref.at[0])

        @pl.when(j == 0)
        def _():
            acc_ref[...] = part

        @pl.when(j > 0)
        def _():
            acc_ref[...] += part

    @pl.when(j == pl.num_programs(1) - 1)
    def _():
        o_ref[...] = jnp.where(valid, acc_ref[...] * rw_ref[...], 0.0).astype(o_ref.dtype)


def _moe_ffn(xs, row_w, tile_expert, tile_valid, w_gate, w_up, w_down):
    cap, d = xs.shape
    f = w_gate.shape[2]
    tm, tf = TM_MOE, TF_FFN
    grid_spec = pltpu.PrefetchScalarGridSpec(
        num_scalar_prefetch=2,
        grid=(cap // tm, f // tf),
        in_specs=[
            pl.BlockSpec((tm, d), lambda i, j, te, tv: (i, 0)),
            pl.BlockSpec((tm, 1), lambda i, j, te, tv: (i, 0)),
            pl.BlockSpec((1, d, tf), lambda i, j, te, tv: (te[i], 0, j)),
            pl.BlockSpec((1, d, tf), lambda i, j, te, tv: (te[i], 0, j)),
            pl.BlockSpec((1, tf, d), lambda i, j, te, tv: (te[i], j, 0)),
        ],
        out_specs=pl.BlockSpec((tm, d), lambda i, j, te, tv: (i, 0)),
        scratch_shapes=[pltpu.VMEM((tm, d), F32)],
    )
    return pl.pallas_call(
        _moe_ffn_body,
        grid_spec=grid_spec,
        out_shape=jax.ShapeDtypeStruct((cap, d), F32),
        compiler_params=_params("parallel", "arbitrary"),
        name="moe_ffn",
    )(tile_expert, tile_valid, xs, row_w, w_gate, w_up, w_down)


def _moe_plan(idx, wgt, n_experts, tm):
    t = idx.shape[0]
    pairs = t * TOP_K
    cap = pairs + n_experts * tm
    e_flat = idx.reshape(pairs)
    onehot = (e_flat[:, None] == jnp.arange(n_experts, dtype=jnp.int32)[None, :]).astype(jnp.int32)
    rank = jnp.take_along_axis(jnp.cumsum(onehot, axis=0) - onehot, e_flat[:, None], axis=1)[:, 0]
    counts = jnp.sum(onehot, axis=0)
    padded = ((counts + tm - 1) // tm) * tm
    ends = jnp.cumsum(padded)
    starts = ends - padded
    slot = starts[e_flat] + rank
    token_of_slot = jnp.zeros((cap,), jnp.int32).at[slot].set(jnp.arange(pairs, dtype=jnp.int32) // TOP_K)
    w_of_slot = jnp.zeros((cap,), F32).at[slot].set(wgt.reshape(pairs))
    tile_start = jnp.arange(cap // tm, dtype=jnp.int32) * tm
    tile_expert = jnp.sum((tile_start[:, None] >= ends[None, :]).astype(jnp.int32), axis=1)
    tile_valid = (tile_expert < n_experts).astype(jnp.int32)
    last_used = jnp.max(jnp.where(counts > 0, jnp.arange(n_experts, dtype=jnp.int32), 0))
    tile_expert = jnp.minimum(tile_expert, last_used).astype(jnp.int32)
    return slot.reshape(t, TOP_K), token_of_slot, w_of_slot, tile_expert, tile_valid


def _final_norm_body(h_ref, y1_ref, y2_ref, g_ref, o_ref):
    o_ref[...] = _rms(h_ref[...] + y1_ref[...] + y2_ref[...], g_ref[...])


def _final_norm(h, y1, y2, g):
    t, d = h.shape
    tm = TM_NORM
    row = pl.BlockSpec((tm, d), lambda i: (i, 0))
    return pl.pallas_call(
        _final_norm_body,
        grid=(t // tm,),
        in_specs=[row, row, row, pl.BlockSpec((1, d), lambda i: (0, 0))],
        out_specs=row,
        out_shape=jax.ShapeDtypeStruct((t, d), F32),
        compiler_params=_params("parallel"),
        name="final_norm",
    )(h, y1, y2, g.reshape(1, d))


def _plain_norm_body(h_ref, g_ref, o_ref):
    o_ref[...] = _rms(h_ref[...], g_ref[...])


def _plain_norm(h, g):
    t, d = h.shape
    tm = TM_NORM
    row = pl.BlockSpec((tm, d), lambda i: (i, 0))
    return pl.pallas_call(
        _plain_norm_body,
        grid=(t // tm,),
        in_specs=[row, pl.BlockSpec((1, d), lambda i: (0, 0))],
        out_specs=row,
        out_shape=jax.ShapeDtypeStruct((t, d), F32),
        compiler_params=_params("parallel"),
        name="plain_norm",
    )(h, g.reshape(1, d))


def kernel(x, norm_mix, w_in, ssm_lambda_re, ssm_lambda_im, ssm_log_dt, ssm_b_re, ssm_b_im, ssm_c_re,
           ssm_c_im, ssm_d, ssm_w_glu, conv_w, w_br_a, w_br_b, w_br_c, w_out, norm_ffn, dense_w_gate,
           dense_w_up, dense_w_down, moe_w_router, moe_w_gate, moe_w_up, moe_w_down, final_norm):
    batch, seq, d = x.shape
    depth = w_in.shape[0]
    t = batch * seq
    w_attn = w_br_a.shape[1]
    w_ssm = w_br_b.shape[1]
    w_conv = w_br_c.shape[1]
    n_qkv = 3 * w_attn
    n_sc = w_ssm + 3 * w_conv
    n_gate = N_BRANCH * d
    n_experts = moe_w_router.shape[-1]

    h = x.reshape(t, d)
    pending = None
    for i in range(depth):
        assert pending is None
        qkv, sc, gates = _inproj(h, norm_mix[i], w_in[i].astype(BF16),
                                 n_q=w_attn, n_qkv=n_qkv, n_sc=n_sc, n_gate=n_gate)
        attn = _sb_attention(qkv, batch=batch, seq=seq, width=w_attn)
        intra, to_state, from_state, lbar_pow = _ssm_tables(
            ssm_lambda_re[i], ssm_lambda_im[i], ssm_log_dt[i], ssm_b_re[i], ssm_b_im[i],
            ssm_c_re[i], ssm_c_im[i])
        step = _ssm_step_table(lbar_pow, seq // SSM_CHUNK)
        yssm = _ssm_scan(sc[:, :w_ssm], (intra, to_state, from_state, step), batch=batch, seq=seq)
        h, xn = _merge(attn, yssm, sc, gates, h, ssm_d[i], ssm_w_glu[i].astype(BF16), conv_w[i],
                       w_br_a[i].astype(BF16), w_br_b[i].astype(BF16), w_br_c[i].astype(BF16),
                       w_out[i].astype(BF16), norm_ffn[i], seq=seq)
        j = i // 2
        if i % 2 == 0:
            h = _dense_ffn(xn, h, dense_w_gate[j].astype(BF16), dense_w_up[j].astype(BF16),
                           dense_w_down[j].astype(BF16))
        else:
            idx, wgt = _router(h, norm_ffn[i], moe_w_router[j])
            slot, token_of_slot, w_of_slot, tile_expert, tile_valid = _moe_plan(idx, wgt, n_experts, TM_MOE)
            xs = jnp.take(xn, token_of_slot, axis=0)
            ys = _moe_ffn(xs, w_of_slot[:, None], tile_expert, tile_valid, moe_w_gate[j].astype(BF16),
                          moe_w_up[j].astype(BF16), moe_w_down[j].astype(BF16))
            y1 = jnp.take(ys, slot[:, 0], axis=0)
            y2 = jnp.take(ys, slot[:, 1], axis=0)
            if i == depth - 1:
                pending = (y1, y2)
            else:
                h = h + y1 + y2
    if pending is not None:
        out = _final_norm(h, pending[0], pending[1], final_norm)
    else:
        out = _plain_norm(h, final_norm)
    return out.reshape(batch, seq, d)
```

```python
import functools

import jax
import jax.numpy as jnp
import numpy as np
from jax import lax
from jax.experimental import pallas as pl
from jax.experimental.pallas import tpu as pltpu

F32 = jnp.float32
BF16 = jnp.bfloat16

EPS = 1e-6
HEAD_DIM = 64
SSM_GROUP = 16
SSM_CHUNK = 16
SSM_MAX_RE = -1e-4
CONV_K = 3
N_BRANCH = 3
TOP_K = 2
LOG2E = 1.4426950408889634

LANES = 128
SUBLANES = 8
VMEM_LIMIT_BYTES = 56 * 1024 * 1024

TM_PROJ = 512
TQ_ATTN = 256
PAIRS_ATTN = 4
TM_MERGE = 512
TM_FFN = 1024
TF_FFN = 512
TM_MOE = 512
TM_ROUTE = 512
TM_SCATTER = 256
TM_COMBINE = 256
TM_NORM = 512


def _params(*sem):
    return pltpu.CompilerParams(dimension_semantics=sem, vmem_limit_bytes=VMEM_LIMIT_BYTES)


def _rms(x, g):
    ms = jnp.mean(x * x, axis=-1, keepdims=True)
    return x * lax.rsqrt(ms + EPS) * g


def _inproj_body(x_ref, g_ref, w_ref, qkv_ref, sc_ref, gate_ref, *, n_qkv, n_sc, n_gate, n_q, q_scale):
    xn = _rms(x_ref[...], g_ref[...]).astype(BF16)
    chunk = 512
    for c0 in range(0, n_qkv, chunk):
        r = jnp.dot(xn, w_ref[:, c0:c0 + chunk], preferred_element_type=F32)
        if c0 < n_q:
            r = r * q_scale
        qkv_ref[:, c0:c0 + chunk] = r.astype(BF16)
    for c0 in range(0, n_sc, chunk):
        r = jnp.dot(xn, w_ref[:, n_qkv + c0:n_qkv + c0 + chunk], preferred_element_type=F32)
        sc_ref[:, c0:c0 + chunk] = r.astype(BF16)
    for c0 in range(0, n_gate, chunk):
        o = n_qkv + n_sc + c0
        r = jnp.dot(xn, w_ref[:, o:o + chunk], preferred_element_type=F32)
        gate_ref[:, c0:c0 + chunk] = jax.nn.sigmoid(r).astype(BF16)


def _inproj(h, g, w_bf16, *, n_q, n_qkv, n_sc, n_gate):
    t, d = h.shape
    tm = TM_PROJ
    body = functools.partial(_inproj_body, n_qkv=n_qkv, n_sc=n_sc, n_gate=n_gate, n_q=n_q,
                             q_scale=HEAD_DIM ** -0.5)
    return pl.pallas_call(
        body,
        grid=(t // tm,),
        in_specs=[
            pl.BlockSpec((tm, d), lambda i: (i, 0)),
            pl.BlockSpec((1, d), lambda i: (0, 0)),
            pl.BlockSpec((d, n_qkv + n_sc + n_gate), lambda i: (0, 0), pipeline_mode=pl.Buffered(1)),
        ],
        out_specs=[
            pl.BlockSpec((tm, n_qkv), lambda i: (i, 0)),
            pl.BlockSpec((tm, n_sc), lambda i: (i, 0)),
            pl.BlockSpec((tm, n_gate), lambda i: (i, 0)),
        ],
        out_shape=[
            jax.ShapeDtypeStruct((t, n_qkv), BF16),
            jax.ShapeDtypeStruct((t, n_sc), BF16),
            jax.ShapeDtypeStruct((t, n_gate), BF16),
        ],
        compiler_params=_params("parallel"),
        name="inproj",
    )(h, g.reshape(1, d), w_bf16)


def _sb_attn_body(qt_ref, k_ref, vt_ref, o_ref, acc_ref, carry_ref, *, tq, n_pairs):
    i = pl.program_id(2)
    n_heads = 2 * n_pairs
    ones_rows = carry_ref.shape[1]
    key = lax.broadcasted_iota(jnp.int32, (tq, tq), 0)
    qry = lax.broadcasted_iota(jnp.int32, (tq, tq), 1)
    valid = key < qry
    er = lax.broadcasted_iota(jnp.int32, (tq + ones_rows, tq), 0)
    ec = lax.broadcasted_iota(jnp.int32, (tq + ones_rows, tq), 1)
    later_and_sum = jnp.where((ec > er) | (er >= tq), 1.0, 0.0).astype(BF16)

    feat = lax.broadcasted_iota(jnp.int32, (LANES, tq), 0)
    q_heads = []
    for p in range(n_pairs):
        qp = qt_ref[0, p * LANES:(p + 1) * LANES, :]
        zero = jnp.zeros_like(qp)
        q_heads.append(jnp.where(feat < HEAD_DIM, qp, zero))
        q_heads.append(jnp.where(feat >= HEAD_DIM, qp, zero))

    def tile(j, masked):
        start = pl.multiple_of(j * tq, tq)
        zs, sps = [], []
        for h in range(n_heads):
            p = h // 2
            kblk = k_ref[pl.ds(start, tq), p * LANES:(p + 1) * LANES]
            z = jnp.dot(kblk, q_heads[h], preferred_element_type=F32)
            l1p = jnp.log(1.0 + jnp.exp2(jnp.abs(z) * (-LOG2E)))
            sp = jnp.maximum(z, 0.0) + l1p
            if masked:
                sp = jnp.where(valid, sp, 0.0)
            zs.append(z - sp)
            sps.append(sp.astype(BF16))
        ws, sums = [], []
        for h in range(n_heads):
            ts = jnp.dot(later_and_sum, sps[h], preferred_element_type=F32)
            w = jnp.exp(zs[h] - ts[0:tq])
            if masked:
                w = jnp.where(valid, w, 0.0)
            ws.append(w.astype(BF16))
            sums.append(ts[tq:tq + ones_rows])
        for h in range(n_heads):
            p, hh = divmod(h, 2)
            vt = vt_ref[0, p, j, hh * HEAD_DIM:(hh + 1) * HEAD_DIM, :]
            pv = jnp.dot(vt, ws[h], preferred_element_type=F32)
            carry = carry_ref[h]
            scale = jnp.exp(-carry[0:1, :])
            rows = slice(h * HEAD_DIM, (h + 1) * HEAD_DIM)
            acc_ref[rows, :] += pv * scale
            carry_ref[h] = carry + sums[h]

    acc_ref[...] = jnp.zeros_like(acc_ref)
    carry_ref[...] = jnp.zeros_like(carry_ref)
    tile(i, True)

    def body(n, c):
        tile(i - 1 - n, False)
        return c

    lax.fori_loop(0, i, body, 0)
    for p in range(n_pairs):
        o_ref[:, p * LANES:(p + 1) * LANES] = acc_ref[p * LANES:(p + 1) * LANES, :].T.astype(o_ref.dtype)


def _sb_attention(qkv, *, batch, seq, width):
    t = batch * seq
    tq = TQ_ATTN
    nq = seq // tq
    n_pairs = PAIRS_ATTN
    cols = n_pairs * LANES
    ngroup = width // cols
    q = qkv[:, 0:width].reshape(batch, seq, width)
    v = qkv[:, 2 * width:3 * width].reshape(batch, nq, tq, ngroup, n_pairs, LANES)
    qt = q.transpose(0, 2, 1)
    vt = v.transpose(0, 3, 4, 1, 5, 2).reshape(batch * ngroup, n_pairs, nq, LANES, tq)
    return pl.pallas_call(
        functools.partial(_sb_attn_body, tq=tq, n_pairs=n_pairs),
        grid=(batch, ngroup, nq),
        in_specs=[
            pl.BlockSpec((1, cols, tq), lambda b, g, i: (b, g, i)),
            pl.BlockSpec((seq, cols), lambda b, g, i: (b, ngroup + g)),
            pl.BlockSpec((1, n_pairs, nq, LANES, tq), lambda b, g, i: (b * ngroup + g, 0, 0, 0, 0)),
        ],
        out_specs=pl.BlockSpec((tq, cols), lambda b, g, i: (b * nq + i, g)),
        out_shape=jax.ShapeDtypeStruct((t, width), BF16),
        scratch_shapes=[pltpu.VMEM((cols, tq), F32), pltpu.VMEM((2 * n_pairs, 16, tq), F32)],
        compiler_params=_params("parallel", "parallel", "arbitrary"),
        name="sb_attention",
    )(qt, qkv, vt)


def _ssm_tables(lam_re, lam_im, log_dt, b_re, b_im, c_re, c_im):
    hp = lax.Precision.HIGHEST
    g, p = lam_re.shape
    hdim = b_re.shape[-1]
    c = SSM_CHUNK
    lr = jnp.minimum(lam_re.astype(F32), SSM_MAX_RE)
    li = lam_im.astype(F32)
    dt = jnp.exp(log_dt.astype(F32))[:, None]

    def lbar_pow(k):
        mag = jnp.exp(k * (lr * dt))
        ang = k * (li * dt)
        return mag * jnp.cos(ang), mag * jnp.sin(ang)

    ab_re, ab_im = lbar_pow(1.0)
    den = lr * lr + li * li
    nr = ab_re - 1.0
    ni = ab_im
    f_re = (nr * lr + ni * li) / den
    f_im = (ni * lr - nr * li) / den
    br = b_re.astype(F32)
    bi = b_im.astype(F32)
    bb_re = f_re[..., None] * br - f_im[..., None] * bi
    bb_im = f_re[..., None] * bi + f_im[..., None] * br
    cr = c_re.astype(F32)
    ci = c_im.astype(F32)

    ks = jnp.arange(c + 1, dtype=F32)[:, None, None]
    pw_re, pw_im = lbar_pow(ks)

    cl_re = cr[None] * pw_re[:c, :, None, :] - ci[None] * pw_im[:c, :, None, :]
    cl_im = cr[None] * pw_im[:c, :, None, :] + ci[None] * pw_re[:c, :, None, :]
    kk = (jnp.einsum('kghp,gpi->kghi', cl_re, bb_re, precision=hp)
          - jnp.einsum('kghp,gpi->kghi', cl_im, bb_im, precision=hp))
    tt = jnp.arange(c)[:, None]
    jj = jnp.arange(c)[None, :]
    lag = jnp.clip(tt - jj, 0, c - 1)
    m = jnp.where((tt >= jj)[:, :, None, None, None], kk[lag], 0.0)
    intra = jnp.transpose(m, (2, 1, 4, 0, 3)).reshape(g, c * hdim, c * hdim)

    rev_re = pw_re[:c][::-1]
    rev_im = pw_im[:c][::-1]
    ps_re = rev_re[..., None] * bb_re[None] - rev_im[..., None] * bb_im[None]
    ps_im = rev_re[..., None] * bb_im[None] + rev_im[..., None] * bb_re[None]
    to_state = jnp.concatenate([jnp.transpose(ps_re, (1, 0, 3, 2)).reshape(g, c * hdim, p),
                                jnp.transpose(ps_im, (1, 0, 3, 2)).reshape(g, c * hdim, p)], axis=-1)

    q_re = cr[None] * pw_re[1:, :, None, :] - ci[None] * pw_im[1:, :, None, :]
    q_im = cr[None] * pw_im[1:, :, None, :] + ci[None] * pw_re[1:, :, None, :]
    from_state = jnp.concatenate([jnp.transpose(q_re, (1, 3, 0, 2)).reshape(g, p, c * hdim),
                                  jnp.transpose(-q_im, (1, 3, 0, 2)).reshape(g, p, c * hdim)], axis=1)
    return intra.astype(BF16), to_state.astype(BF16), from_state.astype(BF16), lbar_pow


def _ssm_step_table(lbar_pow, n_chunks):
    steps = []
    s = 1
    while s < n_chunks:
        a_re, a_im = lbar_pow(float(SSM_CHUNK * s))
        steps.append(jnp.stack([jnp.concatenate([a_re, a_re], axis=-1),
                                jnp.concatenate([-a_im, a_im], axis=-1)], axis=1))
        s *= 2
    return jnp.stack(steps, axis=1)


def _ssm_body(u_ref, intra_ref, to_state_ref, from_state_ref, step_ref, y_ref, *, n_chunks, n_state):
    u = u_ref[0]
    rows = u.shape[0]
    z = jnp.dot(u, to_state_ref[0], preferred_element_type=F32)
    chunk_idx = lax.broadcasted_iota(jnp.int32, (rows, 2 * n_state), 0) % n_chunks
    s = 1
    k = 0
    while s < n_chunks:
        zs = jnp.where(chunk_idx >= s, pltpu.roll(z, s, axis=0), 0.0)
        zs_swapped = pltpu.roll(zs, n_state, axis=1)
        z = z + step_ref[0, k, 0:1, :] * zs + step_ref[0, k, 1:2, :] * zs_swapped
        s *= 2
        k += 1
    prev = jnp.where(chunk_idx >= 1, pltpu.roll(z, 1, axis=0), 0.0)
    y = jnp.dot(u, intra_ref[0], preferred_element_type=F32)
    y = y + jnp.dot(prev.astype(BF16), from_state_ref[0], preferred_element_type=F32)
    y_ref[0] = y.astype(y_ref.dtype)


def _ssm_scan(u, tables, *, batch, seq):
    intra, to_state, from_state, step = tables
    g = intra.shape[0]
    width = u.shape[1]
    hdim = width // g
    c = SSM_CHUNK
    n_chunks = seq // c
    rows = batch * n_chunks
    n_state = to_state.shape[-1] // 2
    n_steps = step.shape[1]
    ug = u.reshape(batch, n_chunks, c, g, hdim).transpose(3, 0, 1, 2, 4).reshape(g, rows, c * hdim)
    y = pl.pallas_call(
        functools.partial(_ssm_body, n_chunks=n_chunks, n_state=n_state),
        grid=(g,),
        in_specs=[
            pl.BlockSpec((1, rows, c * hdim), lambda i: (i, 0, 0)),
            pl.BlockSpec((1, c * hdim, c * hdim), lambda i: (i, 0, 0)),
            pl.BlockSpec((1, c * hdim, 2 * n_state), lambda i: (i, 0, 0)),
            pl.BlockSpec((1, 2 * n_state, c * hdim), lambda i: (i, 0, 0)),
            pl.BlockSpec((1, n_steps, 2, 2 * n_state), lambda i: (i, 0, 0, 0)),
        ],
        out_specs=pl.BlockSpec((1, rows, c * hdim), lambda i: (i, 0, 0)),
        out_shape=jax.ShapeDtypeStruct((g, rows, c * hdim), BF16),
        compiler_params=_params("parallel"),
        name="ssm_scan",
    )(ug, intra, to_state, from_state, step)
    return y.reshape(g, batch, n_chunks, c, hdim).transpose(1, 2, 3, 0, 4).reshape(batch * seq, width)


def _gelu_tanh(x):
    return 0.5 * x * (1.0 + jnp.tanh(0.7978845608028654 * (x + 0.044715 * (x * x * x))))


def _merge_body(attn_ref, yssm_ref, sc_ref, halo_ref, gate_ref, h_ref, dskip_ref, wglu_ref, convw_ref,
                wa_ref, wb_ref, wc_ref, wout_ref, gn_ref, hout_ref, xn_ref, *, tiles_per_seq, w_ssm, w_conv):
    i = pl.program_id(0)
    tm = h_ref.shape[0]
    d = h_ref.shape[1]

    u = sc_ref[:, 0:w_ssm].astype(F32)
    yb = _gelu_tanh(yssm_ref[...].astype(F32) + dskip_ref[...] * u)
    yb = yb * jax.nn.sigmoid(jnp.dot(yb.astype(BF16), wglu_ref[...], preferred_element_type=F32))
    y_b = jnp.dot(yb.astype(BF16), wb_ref[...], preferred_element_type=F32)

    o_h, o_b, o_c = w_ssm, w_ssm + w_conv, w_ssm + 2 * w_conv
    uc = sc_ref[:, o_c:o_c + w_conv].astype(F32) * sc_ref[:, o_h:o_h + w_conv].astype(F32)
    halo = halo_ref[:, o_c:o_c + w_conv].astype(F32) * halo_ref[:, o_h:o_h + w_conv].astype(F32)
    halo = jnp.where(i % tiles_per_seq == 0, 0.0, halo)
    row = lax.broadcasted_iota(jnp.int32, (tm, w_conv), 0)
    prev1 = jnp.where(row == 0, halo[SUBLANES - 1:SUBLANES, :], pltpu.roll(uc, 1, axis=0))
    prev2 = jnp.where(row == 0, halo[SUBLANES - 2:SUBLANES - 1, :],
                      jnp.where(row == 1, halo[SUBLANES - 1:SUBLANES, :], pltpu.roll(uc, 2, axis=0)))
    conv = convw_ref[0:1, :] * prev2 + convw_ref[1:2, :] * prev1 + convw_ref[2:3, :] * uc
    yc = sc_ref[:, o_b:o_b + w_conv].astype(F32) * conv
    y_c = jnp.dot(yc.astype(BF16), wc_ref[...], preferred_element_type=F32)

    y_a = jnp.dot(attn_ref[...], wa_ref[...], preferred_element_type=F32)

    merged = (gate_ref[:, 0:d].astype(F32) * y_a + gate_ref[:, d:2 * d].astype(F32) * y_b
              + gate_ref[:, 2 * d:3 * d].astype(F32) * y_c)
    hn = h_ref[...] + jnp.dot(merged.astype(BF16), wout_ref[...], preferred_element_type=F32)
    hout_ref[...] = hn
    xn_ref[...] = _rms(hn, gn_ref[...]).astype(xn_ref.dtype)


def _merge(attn, yssm, sc, gates, h, d_skip, w_glu, conv_w, w_a, w_b, w_c, w_out, g_ffn, *, seq):
    t, d = h.shape
    tm = TM_MERGE
    w_ssm = yssm.shape[1]
    w_conv = conv_w.shape[1]
    w_attn = attn.shape[1]
    n_sc = sc.shape[1]
    body = functools.partial(_merge_body, tiles_per_seq=seq // tm, w_ssm=w_ssm, w_conv=w_conv)
    full = lambda shape: pl.BlockSpec(shape, lambda i: (0,) * len(shape))
    halo_blocks = tm // SUBLANES
    return pl.pallas_call(
        body,
        grid=(t // tm,),
        in_specs=[
            pl.BlockSpec((tm, w_attn), lambda i: (i, 0)),
            pl.BlockSpec((tm, w_ssm), lambda i: (i, 0)),
            pl.BlockSpec((tm, n_sc), lambda i: (i, 0)),
            pl.BlockSpec((SUBLANES, n_sc), lambda i: (jnp.maximum(i * halo_blocks - 1, 0), 0)),
            pl.BlockSpec((tm, N_BRANCH * d), lambda i: (i, 0)),
            pl.BlockSpec((tm, d), lambda i: (i, 0)),
            full((1, w_ssm)), full((w_ssm, w_ssm)), full((CONV_K, w_conv)),
            full((w_attn, d)), full((w_ssm, d)), full((w_conv, d)), full((d, d)), full((1, d)),
        ],
        out_specs=[pl.BlockSpec((tm, d), lambda i: (i, 0)), pl.BlockSpec((tm, d), lambda i: (i, 0))],
        out_shape=[jax.ShapeDtypeStruct((t, d), F32), jax.ShapeDtypeStruct((t, d), BF16)],
        compiler_params=_params("parallel"),
        name="merge",
    )(attn, yssm, sc, sc, gates, h, d_skip.reshape(1, w_ssm).astype(F32), w_glu, conv_w.astype(F32),
      w_a, w_b, w_c, w_out, g_ffn.reshape(1, d))


def _swiglu_tile(x, wg_ref, wu_ref, wd_ref):
    gate = jnp.dot(x, wg_ref[...], preferred_element_type=F32)
    up = jnp.dot(x, wu_ref[...], preferred_element_type=F32)
    act = (gate * jax.nn.sigmoid(gate) * up).astype(BF16)
    return jnp.dot(act, wd_ref[...], preferred_element_type=F32)


def _dense_ffn_body(x_ref, h_ref, wg_ref, wu_ref, wd_ref, o_ref, acc_ref):
    j = pl.program_id(1)
    part = _swiglu_tile(x_ref[...], wg_ref, wu_ref, wd_ref)

    @pl.when(j == 0)
    def _():
        acc_ref[...] = h_ref[...] + part

    @pl.when(j > 0)
    def _():
        acc_ref[...] += part

    @pl.when(j == pl.num_programs(1) - 1)
    def _():
        o_ref[...] = acc_ref[...]


def _dense_ffn(xn, h, w_gate, w_up, w_down):
    t, d = h.shape
    f = w_gate.shape[1]
    tm, tf = TM_FFN, TF_FFN
    return pl.pallas_call(
        _dense_ffn_body,
        grid=(t // tm, f // tf),
        in_specs=[
            pl.BlockSpec((tm, d), lambda i, j: (i, 0)),
            pl.BlockSpec((tm, d), lambda i, j: (i, 0)),
            pl.BlockSpec((d, tf), lambda i, j: (0, j)),
            pl.BlockSpec((d, tf), lambda i, j: (0, j)),
            pl.BlockSpec((tf, d), lambda i, j: (j, 0)),
        ],
        out_specs=pl.BlockSpec((tm, d), lambda i, j: (i, 0)),
        out_shape=jax.ShapeDtypeStruct((t, d), F32),
        scratch_shapes=[pltpu.VMEM((tm, d), F32)],
        compiler_params=_params("parallel", "arbitrary"),
        name="dense_ffn",
    )(xn, h, w_gate, w_up, w_down)


def _router_body(h_ref, g_ref, wr_ref, idx_ref, wgt_ref):
    xn = _rms(h_ref[...], g_ref[...])
    logits = lax.dot_general(wr_ref[...], xn, (((1,), (1,)), ((), ())), preferred_element_type=F32,
                             precision=lax.Precision.HIGHEST)
    n_e = logits.shape[0]
    e_idx = lax.broadcasted_iota(jnp.int32, logits.shape, 0)
    m1 = jnp.max(logits, axis=0, keepdims=True)
    i1 = jnp.min(jnp.where(logits == m1, e_idx, n_e), axis=0, keepdims=True)
    rest = jnp.where(e_idx == i1, -jnp.inf, logits)
    m2 = jnp.max(rest, axis=0, keepdims=True)
    i2 = jnp.min(jnp.where(rest == m2, e_idx, n_e), axis=0, keepdims=True)
    e2 = jnp.exp(m2 - m1)
    w1 = 1.0 / (1.0 + e2)
    idx_ref[0] = jnp.concatenate([i1, i2], axis=0)
    wgt_ref[0] = jnp.concatenate([w1, e2 * w1], axis=0)


def _router(h, g_ffn, w_router):
    t, d = h.shape
    n_e = w_router.shape[1]
    tm = TM_ROUTE
    nt = t // tm
    idx, wgt = pl.pallas_call(
        _router_body,
        grid=(nt,),
        in_specs=[
            pl.BlockSpec((tm, d), lambda i: (i, 0)),
            pl.BlockSpec((1, d), lambda i: (0, 0)),
            pl.BlockSpec((n_e, d), lambda i: (0, 0)),
        ],
        out_specs=[pl.BlockSpec((1, TOP_K, tm), lambda i: (i, 0, 0)),
                   pl.BlockSpec((1, TOP_K, tm), lambda i: (i, 0, 0))],
        out_shape=[jax.ShapeDtypeStruct((nt, TOP_K, tm), jnp.int32),
                   jax.ShapeDtypeStruct((nt, TOP_K, tm), F32)],
        compiler_params=_params("parallel"),
        name="router",
    )(h, g_ffn.reshape(1, d), w_router.T.astype(F32))
    idx = idx.transpose(0, 2, 1).reshape(t, TOP_K)
    wgt = wgt.transpose(0, 2, 1).reshape(t, TOP_K)
    return idx, wgt


def _scatter_rows_body(s1_ref, s2_ref, h_ref, xs_in_ref, xs_ref, sem):
    del xs_in_ref
    tm = h_ref.shape[0]

    def issue(r, c):
        src = h_ref.at[pl.ds(r, 1)]
        pltpu.make_async_copy(src, xs_ref.at[pl.ds(s1_ref[r], 1)], sem).start()
        pltpu.make_async_copy(src, xs_ref.at[pl.ds(s2_ref[r], 1)], sem).start()
        return c

    lax.fori_loop(0, tm, issue, 0, unroll=8)
    pltpu.make_async_copy(h_ref, xs_ref.at[pl.ds(0, tm)], sem).wait()
    pltpu.make_async_copy(h_ref, xs_ref.at[pl.ds(0, tm)], sem).wait()


def _scatter_rows(h, slot1, slot2, cap):
    t, d = h.shape
    tm = TM_SCATTER
    smem_tile = pl.BlockSpec((tm,), lambda i: (i,), memory_space=pltpu.SMEM)
    return pl.pallas_call(
        _scatter_rows_body,
        grid=(t // tm,),
        in_specs=[smem_tile, smem_tile,
                  pl.BlockSpec((tm, d), lambda i: (i, 0)),
                  pl.BlockSpec(memory_space=pl.ANY)],
        out_specs=pl.BlockSpec(memory_space=pl.ANY),
        out_shape=jax.ShapeDtypeStruct((cap, d), h.dtype),
        scratch_shapes=[pltpu.SemaphoreType.DMA(())],
        input_output_aliases={3: 0},
        compiler_params=_params("arbitrary"),
        name="scatter_rows",
    )(slot1, slot2, h, jnp.zeros((cap, d), h.dtype))


def _moe_ffn_body(tile_expert_ref, tile_valid_ref, x_ref, g_ref, wg_ref, wu_ref, wd_ref, o_ref, xn_ref, acc_ref):
    i = pl.program_id(0)
    j = pl.program_id(1)
    valid = tile_valid_ref[i] > 0

    @pl.when(valid)
    def _():
        @pl.when(j == 0)
        def _():
            xn_ref[...] = _rms(x_ref[...], g_ref[...]).astype(xn_ref.dtype)

        part = _swiglu_tile(xn_ref[...], wg_ref.at[0], wu_ref.at[0], wd_ref.at[0])

        @pl.when(j == 0)
        def _():
            acc_ref[...] = part

        @pl.when(j > 0)
        def _():
            acc_ref[...] += part

    @pl.when(j == pl.num_programs(1) - 1)
    def _():
        o_ref[...] = jnp.where(valid, acc_ref[...], 0.0).astype(o_ref.dtype)


def _moe_ffn(xs, g_ffn, tile_expert, tile_valid, w_gate, w_up, w_down):
    cap, d = xs.shape
    f = w_gate.shape[2]
    tm, tf = TM_MOE, TF_FFN
    grid_spec = pltpu.PrefetchScalarGridSpec(
        num_scalar_prefetch=2,
        grid=(cap // tm, f // tf),
        in_specs=[
            pl.BlockSpec((tm, d), lambda i, j, te, tv: (i, 0)),
            pl.BlockSpec((1, d), lambda i, j, te, tv: (0, 0)),
            pl.BlockSpec((1, d, tf), lambda i, j, te, tv: (te[i], 0, j)),
            pl.BlockSpec((1, d, tf), lambda i, j, te, tv: (te[i], 0, j)),
            pl.BlockSpec((1, tf, d), lambda i, j, te, tv: (te[i], j, 0)),
        ],
        out_specs=pl.BlockSpec((tm, d), lambda i, j, te, tv: (i, 0)),
        scratch_shapes=[pltpu.VMEM((tm, d), BF16), pltpu.VMEM((tm, d), F32)],
    )
    return pl.pallas_call(
        _moe_ffn_body,
        grid_spec=grid_spec,
        out_shape=jax.ShapeDtypeStruct((cap, d), F32),
        compiler_params=_params("parallel", "arbitrary"),
        name="moe_ffn",
    )(tile_expert, tile_valid, xs, g_ffn.reshape(1, d), w_gate, w_up, w_down)


def _moe_plan(idx, n_experts, tm):
    t = idx.shape[0]
    pairs = t * TOP_K
    cap = pairs + n_experts * tm
    e_flat = idx.reshape(pairs)
    onehot = (e_flat[:, None] == jnp.arange(n_experts, dtype=jnp.int32)[None, :]).astype(jnp.int32)
    rank = jnp.sum(onehot * (jnp.cumsum(onehot, axis=0) - onehot), axis=1)
    counts = jnp.sum(onehot, axis=0)
    padded = ((counts + tm - 1) // tm) * tm
    ends = jnp.cumsum(padded)
    starts = ends - padded
    slot = jnp.sum(onehot * starts[None, :], axis=1) + rank
    tile_start = jnp.arange(cap // tm, dtype=jnp.int32) * tm
    tile_expert = jnp.sum((tile_start[:, None] >= ends[None, :]).astype(jnp.int32), axis=1)
    tile_valid = (tile_expert < n_experts).astype(jnp.int32)
    last_used = jnp.max(jnp.where(counts > 0, jnp.arange(n_experts, dtype=jnp.int32), 0))
    tile_expert = jnp.minimum(tile_expert, last_used).astype(jnp.int32)
    return slot.reshape(t, TOP_K).astype(jnp.int32), cap, tile_expert, tile_valid


def _combine_body(s1_ref, s2_ref, h_ref, wgt_ref, g_ref, ys_ref, o_ref, y1_ref, y2_ref, sem, *, apply_norm):
    tm = h_ref.shape[0]

    def issue(r, c):
        pltpu.make_async_copy(ys_ref.at[pl.ds(s1_ref[r], 1)], y1_ref.at[pl.ds(r, 1)], sem).start()
        pltpu.make_async_copy(ys_ref.at[pl.ds(s2_ref[r], 1)], y2_ref.at[pl.ds(r, 1)], sem).start()
        return c

    lax.fori_loop(0, tm, issue, 0, unroll=8)
    pltpu.make_async_copy(ys_ref.at[pl.ds(0, tm)], y1_ref, sem).wait()
    pltpu.make_async_copy(ys_ref.at[pl.ds(0, tm)], y2_ref, sem).wait()
    w = wgt_ref[...]
    hn = h_ref[...] + w[:, 0:1] * y1_ref[...] + w[:, 1:2] * y2_ref[...]
    o_ref[...] = _rms(hn, g_ref[...]) if apply_norm else hn


def _combine(h, ys, slot, wgt, g, *, apply_norm):
    t, d = h.shape
    tm = TM_COMBINE
    smem_tile = pl.BlockSpec((tm,), lambda i: (i,), memory_space=pltpu.SMEM)
    row = pl.BlockSpec((tm, d), lambda i: (i, 0))
    return pl.pallas_call(
        functools.partial(_combine_body, apply_norm=apply_norm),
        grid=(t // tm,),
        in_specs=[smem_tile, smem_tile, row,
                  pl.BlockSpec((tm, TOP_K), lambda i: (i, 0)),
                  pl.BlockSpec((1, d), lambda i: (0, 0)),
                  pl.BlockSpec(memory_space=pl.ANY)],
        out_specs=row,
        out_shape=jax.ShapeDtypeStruct((t, d), F32),
        scratch_shapes=[pltpu.VMEM((tm, d), F32), pltpu.VMEM((tm, d), F32), pltpu.SemaphoreType.DMA(())],
        compiler_params=_params("arbitrary"),
        name="combine",
    )(slot[:, 0], slot[:, 1], h, wgt, g.reshape(1, d), ys)


def _plain_norm_body(h_ref, g_ref, o_ref):
    o_ref[...] = _rms(h_ref[...], g_ref[...])


def _plain_norm(h, g):
    t, d = h.shape
    tm = TM_NORM
    row = pl.BlockSpec((tm, d), lambda i: (i, 0))
    return pl.pallas_call(
        _plain_norm_body,
        grid=(t // tm,),
        in_specs=[row, pl.BlockSpec((1, d), lambda i: (0, 0))],
        out_specs=row,
        out_shape=jax.ShapeDtypeStruct((t, d), F32),
        compiler_params=_params("parallel"),
        name="plain_norm",
    )(h, g.reshape(1, d))


def kernel(x, norm_mix, w_in, ssm_lambda_re, ssm_lambda_im, ssm_log_dt, ssm_b_re, ssm_b_im, ssm_c_re,
           ssm_c_im, ssm_d, ssm_w_glu, conv_w, w_br_a, w_br_b, w_br_c, w_out, norm_ffn, dense_w_gate,
           dense_w_up, dense_w_down, moe_w_router, moe_w_gate, moe_w_up, moe_w_down, final_norm):
    batch, seq, d = x.shape
    depth = w_in.shape[0]
    t = batch * seq
    w_attn = w_br_a.shape[1]
    w_ssm = w_br_b.shape[1]
    w_conv = w_br_c.shape[1]
    n_qkv = 3 * w_attn
    n_sc = w_ssm + 3 * w_conv
    n_gate = N_BRANCH * d
    n_experts = moe_w_router.shape[-1]

    h = x.reshape(t, d)
    normed = False
    for i in range(depth):
        qkv, sc, gates = _inproj(h, norm_mix[i], w_in[i].astype(BF16),
                                 n_q=w_attn, n_qkv=n_qkv, n_sc=n_sc, n_gate=n_gate)
        attn = _sb_attention(qkv, batch=batch, seq=seq, width=w_attn)
        intra, to_state, from_state, lbar_pow = _ssm_tables(
            ssm_lambda_re[i], ssm_lambda_im[i], ssm_log_dt[i], ssm_b_re[i], ssm_b_im[i],
            ssm_c_re[i], ssm_c_im[i])
        step = _ssm_step_table(lbar_pow, seq // SSM_CHUNK)
        yssm = _ssm_scan(sc[:, :w_ssm], (intra, to_state, from_state, step), batch=batch, seq=seq)
        h, xn = _merge(attn, yssm, sc, gates, h, ssm_d[i], ssm_w_glu[i].astype(BF16), conv_w[i],
                       w_br_a[i].astype(BF16), w_br_b[i].astype(BF16), w_br_c[i].astype(BF16),
                       w_out[i].astype(BF16), norm_ffn[i], seq=seq)
        j = i // 2
        if i % 2 == 0:
            h = _dense_ffn(xn, h, dense_w_gate[j].astype(BF16), dense_w_up[j].astype(BF16),
                           dense_w_down[j].astype(BF16))
        else:
            idx, wgt = _router(h, norm_ffn[i], moe_w_router[j])
            slot, cap, tile_expert, tile_valid = _moe_plan(idx, n_experts, TM_MOE)
            xs = _scatter_rows(h, slot[:, 0], slot[:, 1], cap)
            ys = _moe_ffn(xs, norm_ffn[i], tile_expert, tile_valid, moe_w_gate[j].astype(BF16),
                          moe_w_up[j].astype(BF16), moe_w_down[j].astype(BF16))
            last = i == depth - 1
            h = _combine(h, ys, slot, wgt, final_norm if last else norm_ffn[i], apply_norm=last)
            normed = last
    out = h if normed else _plain_norm(h, final_norm)
    return out.reshape(batch, seq, d)
```

```python
import functools

import jax
import jax.numpy as jnp
import numpy as np
from jax import lax
from jax.experimental import pallas as pl
from jax.experimental.pallas import tpu as pltpu

F32 = jnp.float32
BF16 = jnp.bfloat16

EPS = 1e-6
HEAD_DIM = 64
SSM_GROUP = 16
SSM_CHUNK = 16
SSM_MAX_RE = -1e-4
CONV_K = 3
N_BRANCH = 3
TOP_K = 2
LOG2E = 1.4426950408889634
MASKED_LOG_WEIGHT = -1e30

LANES = 128
SUBLANES = 8
VMEM_LIMIT_BYTES = 56 * 1024 * 1024

TM_PROJ = 512
TQ_ATTN = 256
PAIRS_ATTN = 4
TM_MERGE = 512
TM_FFN = 512
TF_FFN = 1792
TM_MOE = 512
TM_ROUTE = 512
TM_SCATTER = 256
TM_COMBINE = 256
TM_NORM = 512


def _params(*sem):
    return pltpu.CompilerParams(dimension_semantics=sem, vmem_limit_bytes=VMEM_LIMIT_BYTES)


def _rms(x, g):
    ms = jnp.mean(x * x, axis=-1, keepdims=True)
    return x * lax.rsqrt(ms + EPS) * g


def _inproj_body(x_ref, g_ref, w_ref, qkv_ref, sc_ref, gate_ref, *, n_qkv, n_sc, n_gate, n_q, q_scale):
    xn = _rms(x_ref[...], g_ref[...]).astype(BF16)
    chunk = 512
    for c0 in range(0, n_qkv, chunk):
        r = jnp.dot(xn, w_ref[:, c0:c0 + chunk], preferred_element_type=F32)
        if c0 < n_q:
            r = r * q_scale
        qkv_ref[:, c0:c0 + chunk] = r.astype(BF16)
    for c0 in range(0, n_sc, chunk):
        r = jnp.dot(xn, w_ref[:, n_qkv + c0:n_qkv + c0 + chunk], preferred_element_type=F32)
        sc_ref[:, c0:c0 + chunk] = r.astype(BF16)
    for c0 in range(0, n_gate, chunk):
        o = n_qkv + n_sc + c0
        r = jnp.dot(xn, w_ref[:, o:o + chunk], preferred_element_type=F32)
        gate_ref[:, c0:c0 + chunk] = jax.nn.sigmoid(r).astype(BF16)


def _inproj(h, g, w_bf16, *, n_q, n_qkv, n_sc, n_gate):
    t, d = h.shape
    tm = TM_PROJ
    body = functools.partial(_inproj_body, n_qkv=n_qkv, n_sc=n_sc, n_gate=n_gate, n_q=n_q,
                             q_scale=HEAD_DIM ** -0.5)
    return pl.pallas_call(
        body,
        grid=(t // tm,),
        in_specs=[
            pl.BlockSpec((tm, d), lambda i: (i, 0)),
            pl.BlockSpec((1, d), lambda i: (0, 0)),
            pl.BlockSpec((d, n_qkv + n_sc + n_gate), lambda i: (0, 0), pipeline_mode=pl.Buffered(1)),
        ],
        out_specs=[
            pl.BlockSpec((tm, n_qkv), lambda i: (i, 0)),
            pl.BlockSpec((tm, n_sc), lambda i: (i, 0)),
            pl.BlockSpec((tm, n_gate), lambda i: (i, 0)),
        ],
        out_shape=[
            jax.ShapeDtypeStruct((t, n_qkv), BF16),
            jax.ShapeDtypeStruct((t, n_sc), BF16),
            jax.ShapeDtypeStruct((t, n_gate), BF16),
        ],
        compiler_params=_params("parallel"),
        name="inproj",
    )(h, g.reshape(1, d), w_bf16)


def _sb_attn_body(qt_ref, k_ref, vt_ref, o_ref, acc_ref, carry_ref, lb_a_ref, sp_a_ref, lb_b_ref, sp_b_ref,
                  *, tq, n_pairs):
    i = pl.program_id(2)
    buf_a = (lb_a_ref, sp_a_ref)
    buf_b = (lb_b_ref, sp_b_ref)
    n_heads = 2 * n_pairs
    ones_rows = carry_ref.shape[1]
    key = lax.broadcasted_iota(jnp.int32, (tq, tq), 0)
    qry = lax.broadcasted_iota(jnp.int32, (tq, tq), 1)
    valid = key < qry
    er = lax.broadcasted_iota(jnp.int32, (tq + ones_rows, tq), 0)
    ec = lax.broadcasted_iota(jnp.int32, (tq + ones_rows, tq), 1)
    later_and_sum = jnp.where((ec > er) | (er >= tq), 1.0, 0.0).astype(BF16)

    feat = lax.broadcasted_iota(jnp.int32, (LANES, tq), 0)
    q_heads = []
    for p in range(n_pairs):
        qp = qt_ref[0, p * LANES:(p + 1) * LANES, :]
        zero = jnp.zeros_like(qp)
        q_heads.append(jnp.where(feat < HEAD_DIM, qp, zero))
        q_heads.append(jnp.where(feat >= HEAD_DIM, qp, zero))

    def scores(j, buf, masked):
        lb_ref, sp_ref = buf
        start = pl.multiple_of(j * tq, tq)
        for h in range(n_heads):
            p = h // 2
            kblk = k_ref[pl.ds(start, tq), p * LANES:(p + 1) * LANES]
            z = jnp.dot(kblk, q_heads[h], preferred_element_type=F32)
            l1p = jnp.log(1.0 + jnp.exp2(jnp.abs(z) * (-LOG2E)))
            sp = jnp.maximum(z, 0.0) + l1p
            lb = z - sp
            if masked:
                sp = jnp.where(valid, sp, 0.0)
                lb = jnp.where(valid, lb, MASKED_LOG_WEIGHT)
            lb_ref[h] = lb
            sp_ref[h] = sp.astype(BF16)

    def weights(j, buf):
        lb_ref, sp_ref = buf
        ws, sums = [], []
        for h in range(n_heads):
            ts = jnp.dot(later_and_sum, sp_ref[h], preferred_element_type=F32)
            ws.append(jnp.exp(lb_ref[h] - ts[0:tq]).astype(BF16))
            sums.append(ts[tq:tq + ones_rows])
        for h in range(n_heads):
            p, hh = divmod(h, 2)
            vt = vt_ref[0, p, j, hh * HEAD_DIM:(hh + 1) * HEAD_DIM, :]
            pv = jnp.dot(vt, ws[h], preferred_element_type=F32)
            carry = carry_ref[h]
            scale = jnp.exp(-carry[0:1, :])
            rows = slice(h * HEAD_DIM, (h + 1) * HEAD_DIM)
            acc_ref[rows, :] += pv * scale
            carry_ref[h] = carry + sums[h]

    acc_ref[...] = jnp.zeros_like(acc_ref)
    carry_ref[...] = jnp.zeros_like(carry_ref)
    scores(i, buf_a, True)

    def earlier_keys_matter():
        return (jnp.max(jnp.exp(-carry_ref[:, 0, :])) > 0.0).astype(jnp.int32)

    def two_tiles(c):
        m, _ = c
        j = i - 1 - 2 * m
        scores(j, buf_b, False)
        weights(j + 1, buf_a)
        scores(j - 1, buf_a, False)
        weights(j, buf_b)
        return m + 1, earlier_keys_matter()

    _, alive = lax.while_loop(lambda c: (c[0] < i // 2) & (c[1] > 0), two_tiles, (jnp.int32(0), jnp.int32(1)))

    @pl.when((alive > 0) & (i % 2 == 1))
    def _():
        scores(0, buf_b, False)
        weights(1, buf_a)
        weights(0, buf_b)

    @pl.when((alive > 0) & (i % 2 == 0))
    def _():
        weights(0, buf_a)

    for p in range(n_pairs):
        o_ref[:, p * LANES:(p + 1) * LANES] = acc_ref[p * LANES:(p + 1) * LANES, :].T.astype(o_ref.dtype)


def _sb_attention(qkv, *, batch, seq, width):
    t = batch * seq
    tq = TQ_ATTN
    nq = seq // tq
    n_pairs = PAIRS_ATTN
    cols = n_pairs * LANES
    ngroup = width // cols
    q = qkv[:, 0:width].reshape(batch, seq, width)
    v = qkv[:, 2 * width:3 * width].reshape(batch, nq, tq, ngroup, n_pairs, LANES)
    qt = q.transpose(0, 2, 1)
    vt = v.transpose(0, 3, 4, 1, 5, 2).reshape(batch * ngroup, n_pairs, nq, LANES, tq)
    return pl.pallas_call(
        functools.partial(_sb_attn_body, tq=tq, n_pairs=n_pairs),
        grid=(batch, ngroup, nq),
        in_specs=[
            pl.BlockSpec((1, cols, tq), lambda b, g, i: (b, g, i)),
            pl.BlockSpec((seq, cols), lambda b, g, i: (b, ngroup + g)),
            pl.BlockSpec((1, n_pairs, nq, LANES, tq), lambda b, g, i: (b * ngroup + g, 0, 0, 0, 0)),
        ],
        out_specs=pl.BlockSpec((tq, cols), lambda b, g, i: (b * nq + i, g)),
        out_shape=jax.ShapeDtypeStruct((t, width), BF16),
        scratch_shapes=[pltpu.VMEM((cols, tq), F32), pltpu.VMEM((2 * n_pairs, 16, tq), F32),
                        pltpu.VMEM((2 * n_pairs, tq, tq), F32), pltpu.VMEM((2 * n_pairs, tq, tq), BF16),
                        pltpu.VMEM((2 * n_pairs, tq, tq), F32), pltpu.VMEM((2 * n_pairs, tq, tq), BF16)],
        compiler_params=_params("parallel", "parallel", "arbitrary"),
        name="sb_attention",
    )(qt, qkv, vt)


def _ssm_tables(lam_re, lam_im, log_dt, b_re, b_im, c_re, c_im):
    hp = lax.Precision.HIGHEST
    g, p = lam_re.shape
    hdim = b_re.shape[-1]
    c = SSM_CHUNK
    lr = jnp.minimum(lam_re.astype(F32), SSM_MAX_RE)
    li = lam_im.astype(F32)
    dt = jnp.exp(log_dt.astype(F32))[:, None]

    def lbar_pow(k):
        mag = jnp.exp(k * (lr * dt))
        ang = k * (li * dt)
        return mag * jnp.cos(ang), mag * jnp.sin(ang)

    ab_re, ab_im = lbar_pow(1.0)
    den = lr * lr + li * li
    nr = ab_re - 1.0
    ni = ab_im
    f_re = (nr * lr + ni * li) / den
    f_im = (ni * lr - nr * li) / den
    br = b_re.astype(F32)
    bi = b_im.astype(F32)
    bb_re = f_re[..., None] * br - f_im[..., None] * bi
    bb_im = f_re[..., None] * bi + f_im[..., None] * br
    cr = c_re.astype(F32)
    ci = c_im.astype(F32)

    ks = jnp.arange(c + 1, dtype=F32)[:, None, None]
    pw_re, pw_im = lbar_pow(ks)

    cl_re = cr[None] * pw_re[:c, :, None, :] - ci[None] * pw_im[:c, :, None, :]
    cl_im = cr[None] * pw_im[:c, :, None, :] + ci[None] * pw_re[:c, :, None, :]
    kk = (jnp.einsum('kghp,gpi->kghi', cl_re, bb_re, precision=hp)
          - jnp.einsum('kghp,gpi->kghi', cl_im, bb_im, precision=hp))
    tt = jnp.arange(c)[:, None]
    jj = jnp.arange(c)[None, :]
    lag = jnp.clip(tt - jj, 0, c - 1)
    m = jnp.where((tt >= jj)[:, :, None, None, None], kk[lag], 0.0)
    intra = jnp.transpose(m, (2, 1, 4, 0, 3)).reshape(g, c * hdim, c * hdim)

    rev_re = pw_re[:c][::-1]
    rev_im = pw_im[:c][::-1]
    ps_re = rev_re[..., None] * bb_re[None] - rev_im[..., None] * bb_im[None]
    ps_im = rev_re[..., None] * bb_im[None] + rev_im[..., None] * bb_re[None]
    to_state = jnp.concatenate([jnp.transpose(ps_re, (1, 0, 3, 2)).reshape(g, c * hdim, p),
                                jnp.transpose(ps_im, (1, 0, 3, 2)).reshape(g, c * hdim, p)], axis=-1)

    q_re = cr[None] * pw_re[1:, :, None, :] - ci[None] * pw_im[1:, :, None, :]
    q_im = cr[None] * pw_im[1:, :, None, :] + ci[None] * pw_re[1:, :, None, :]
    from_state = jnp.concatenate([jnp.transpose(q_re, (1, 3, 0, 2)).reshape(g, p, c * hdim),
                                  jnp.transpose(-q_im, (1, 3, 0, 2)).reshape(g, p, c * hdim)], axis=1)
    return intra.astype(BF16), to_state.astype(BF16), from_state.astype(BF16), lbar_pow


def _ssm_step_table(lbar_pow, n_chunks):
    steps = []
    s = 1
    while s < n_chunks:
        a_re, a_im = lbar_pow(float(SSM_CHUNK * s))
        steps.append(jnp.stack([jnp.concatenate([a_re, a_re], axis=-1),
                                jnp.concatenate([-a_im, a_im], axis=-1)], axis=1))
        s *= 2
    return jnp.stack(steps, axis=1)


def _ssm_body(u_ref, intra_ref, to_state_ref, from_state_ref, step_ref, y_ref, *, n_chunks, n_state):
    u = u_ref[0]
    rows = u.shape[0]
    z = jnp.dot(u, to_state_ref[0], preferred_element_type=F32)
    chunk_idx = lax.broadcasted_iota(jnp.int32, (rows, 2 * n_state), 0) % n_chunks
    s = 1
    k = 0
    while s < n_chunks:
        zs = jnp.where(chunk_idx >= s, pltpu.roll(z, s, axis=0), 0.0)
        zs_swapped = pltpu.roll(zs, n_state, axis=1)
        z = z + step_ref[0, k, 0:1, :] * zs + step_ref[0, k, 1:2, :] * zs_swapped
        s *= 2
        k += 1
    prev = jnp.where(chunk_idx >= 1, pltpu.roll(z, 1, axis=0), 0.0)
    y = jnp.dot(u, intra_ref[0], preferred_element_type=F32)
    y = y + jnp.dot(prev.astype(BF16), from_state_ref[0], preferred_element_type=F32)
    y_ref[0] = y.astype(y_ref.dtype)


def _ssm_scan(u, tables, *, batch, seq):
    intra, to_state, from_state, step = tables
    g = intra.shape[0]
    width = u.shape[1]
    hdim = width // g
    c = SSM_CHUNK
    n_chunks = seq // c
    rows = batch * n_chunks
    n_state = to_state.shape[-1] // 2
    n_steps = step.shape[1]
    ug = u.reshape(batch, n_chunks, c, g, hdim).transpose(3, 0, 1, 2, 4).reshape(g, rows, c * hdim)
    y = pl.pallas_call(
        functools.partial(_ssm_body, n_chunks=n_chunks, n_state=n_state),
        grid=(g,),
        in_specs=[
            pl.BlockSpec((1, rows, c * hdim), lambda i: (i, 0, 0)),
            pl.BlockSpec((1, c * hdim, c * hdim), lambda i: (i, 0, 0)),
            pl.BlockSpec((1, c * hdim, 2 * n_state), lambda i: (i, 0, 0)),
            pl.BlockSpec((1, 2 * n_state, c * hdim), lambda i: (i, 0, 0)),
            pl.BlockSpec((1, n_steps, 2, 2 * n_state), lambda i: (i, 0, 0, 0)),
        ],
        out_specs=pl.BlockSpec((1, rows, c * hdim), lambda i: (i, 0, 0)),
        out_shape=jax.ShapeDtypeStruct((g, rows, c * hdim), BF16),
        compiler_params=_params("parallel"),
        name="ssm_scan",
    )(ug, intra, to_state, from_state, step)
    return y.reshape(g, batch, n_chunks, c, hdim).transpose(1, 2, 3, 0, 4).reshape(batch * seq, width)


def _gelu_tanh(x):
    return 0.5 * x * (1.0 + jnp.tanh(0.7978845608028654 * (x + 0.044715 * (x * x * x))))


def _merge_body(attn_ref, yssm_ref, sc_ref, halo_ref, gate_ref, h_ref, dskip_ref, wglu_ref, convw_ref,
                wa_ref, wb_ref, wc_ref, wout_ref, gn_ref, hout_ref, xn_ref, *, tiles_per_seq, w_ssm, w_conv):
    i = pl.program_id(0)
    tm = h_ref.shape[0]
    d = h_ref.shape[1]

    u = sc_ref[:, 0:w_ssm].astype(F32)
    yb = _gelu_tanh(yssm_ref[...].astype(F32) + dskip_ref[...] * u)
    yb = yb * jax.nn.sigmoid(jnp.dot(yb.astype(BF16), wglu_ref[...], preferred_element_type=F32))
    y_b = jnp.dot(yb.astype(BF16), wb_ref[...], preferred_element_type=F32)

    o_h, o_b, o_c = w_ssm, w_ssm + w_conv, w_ssm + 2 * w_conv
    uc = sc_ref[:, o_c:o_c + w_conv].astype(F32) * sc_ref[:, o_h:o_h + w_conv].astype(F32)
    halo = halo_ref[:, o_c:o_c + w_conv].astype(F32) * halo_ref[:, o_h:o_h + w_conv].astype(F32)
    halo = jnp.where(i % tiles_per_seq == 0, 0.0, halo)
    row = lax.broadcasted_iota(jnp.int32, (tm, w_conv), 0)
    prev1 = jnp.where(row == 0, halo[SUBLANES - 1:SUBLANES, :], pltpu.roll(uc, 1, axis=0))
    prev2 = jnp.where(row == 0, halo[SUBLANES - 2:SUBLANES - 1, :],
                      jnp.where(row == 1, halo[SUBLANES - 1:SUBLANES, :], pltpu.roll(uc, 2, axis=0)))
    conv = convw_ref[0:1, :] * prev2 + convw_ref[1:2, :] * prev1 + convw_ref[2:3, :] * uc
    yc = sc_ref[:, o_b:o_b + w_conv].astype(F32) * conv
    y_c = jnp.dot(yc.astype(BF16), wc_ref[...], preferred_element_type=F32)

    y_a = jnp.dot(attn_ref[...], wa_ref[...], preferred_element_type=F32)

    merged = (gate_ref[:, 0:d].astype(F32) * y_a + gate_ref[:, d:2 * d].astype(F32) * y_b
              + gate_ref[:, 2 * d:3 * d].astype(F32) * y_c)
    hn = h_ref[...] + jnp.dot(merged.astype(BF16), wout_ref[...], preferred_element_type=F32)
    hout_ref[...] = hn
    xn_ref[...] = _rms(hn, gn_ref[...]).astype(xn_ref.dtype)


def _merge(attn, yssm, sc, gates, h, d_skip, w_glu, conv_w, w_a, w_b, w_c, w_out, g_ffn, *, seq):
    t, d = h.shape
    tm = TM_MERGE
    w_ssm = yssm.shape[1]
    w_conv = conv_w.shape[1]
    w_attn = attn.shape[1]
    n_sc = sc.shape[1]
    body = functools.partial(_merge_body, tiles_per_seq=seq // tm, w_ssm=w_ssm, w_conv=w_conv)
    full = lambda shape: pl.BlockSpec(shape, lambda i: (0,) * len(shape))
    halo_blocks = tm // SUBLANES
    return pl.pallas_call(
        body,
        grid=(t // tm,),
        in_specs=[
            pl.BlockSpec((tm, w_attn), lambda i: (i, 0)),
            pl.BlockSpec((tm, w_ssm), lambda i: (i, 0)),
            pl.BlockSpec((tm, n_sc), lambda i: (i, 0)),
            pl.BlockSpec((SUBLANES, n_sc), lambda i: (jnp.maximum(i * halo_blocks - 1, 0), 0)),
            pl.BlockSpec((tm, N_BRANCH * d), lambda i: (i, 0)),
            pl.BlockSpec((tm, d), lambda i: (i, 0)),
            full((1, w_ssm)), full((w_ssm, w_ssm)), full((CONV_K, w_conv)),
            full((w_attn, d)), full((w_ssm, d)), full((w_conv, d)), full((d, d)), full((1, d)),
        ],
        out_specs=[pl.BlockSpec((tm, d), lambda i: (i, 0)), pl.BlockSpec((tm, d), lambda i: (i, 0))],
        out_shape=[jax.ShapeDtypeStruct((t, d), F32), jax.ShapeDtypeStruct((t, d), BF16)],
        compiler_params=_params("parallel"),
        name="merge",
    )(attn, yssm, sc, sc, gates, h, d_skip.reshape(1, w_ssm).astype(F32), w_glu, conv_w.astype(F32),
      w_a, w_b, w_c, w_out, g_ffn.reshape(1, d))


def _swiglu_tile(x, wg_ref, wu_ref, wd_ref):
    gate = jnp.dot(x, wg_ref[...], preferred_element_type=F32)
    up = jnp.dot(x, wu_ref[...], preferred_element_type=F32)
    act = (gate * jax.nn.sigmoid(gate) * up).astype(BF16)
    return jnp.dot(act, wd_ref[...], preferred_element_type=F32)


def _dense_ffn_body(x_ref, h_ref, wg_ref, wu_ref, wd_ref, o_ref, acc_ref):
    j = pl.program_id(1)
    part = _swiglu_tile(x_ref[...], wg_ref, wu_ref, wd_ref)

    @pl.when(j == 0)
    def _():
        acc_ref[...] = h_ref[...] + part

    @pl.when(j > 0)
    def _():
        acc_ref[...] += part

    @pl.when(j == pl.num_programs(1) - 1)
    def _():
        o_ref[...] = acc_ref[...]


def _dense_ffn(xn, h, w_gate, w_up, w_down):
    t, d = h.shape
    f = w_gate.shape[1]
    tm, tf = TM_FFN, TF_FFN
    return pl.pallas_call(
        _dense_ffn_body,
        grid=(t // tm, f // tf),
        in_specs=[
            pl.BlockSpec((tm, d), lambda i, j: (i, 0)),
            pl.BlockSpec((tm, d), lambda i, j: (i, 0)),
            pl.BlockSpec((d, tf), lambda i, j: (0, j)),
            pl.BlockSpec((d, tf), lambda i, j: (0, j)),
            pl.BlockSpec((tf, d), lambda i, j: (j, 0)),
        ],
        out_specs=pl.BlockSpec((tm, d), lambda i, j: (i, 0)),
        out_shape=jax.ShapeDtypeStruct((t, d), F32),
        scratch_shapes=[pltpu.VMEM((tm, d), F32)],
        compiler_params=_params("parallel", "arbitrary"),
        name="dense_ffn",
    )(xn, h, w_gate, w_up, w_down)


def _router_body(h_ref, g_ref, wr_ref, idx_ref, wgt_ref):
    xn = _rms(h_ref[...], g_ref[...])
    logits = lax.dot_general(wr_ref[...], xn, (((1,), (1,)), ((), ())), preferred_element_type=F32,
                             precision=lax.Precision.HIGHEST)
    n_e = logits.shape[0]
    e_idx = lax.broadcasted_iota(jnp.int32, logits.shape, 0)
    m1 = jnp.max(logits, axis=0, keepdims=True)
    i1 = jnp.min(jnp.where(logits == m1, e_idx, n_e), axis=0, keepdims=True)
    rest = jnp.where(e_idx == i1, -jnp.inf, logits)
    m2 = jnp.max(rest, axis=0, keepdims=True)
    i2 = jnp.min(jnp.where(rest == m2, e_idx, n_e), axis=0, keepdims=True)
    e2 = jnp.exp(m2 - m1)
    w1 = 1.0 / (1.0 + e2)
    idx_ref[0] = jnp.concatenate([i1, i2], axis=0)
    wgt_ref[0] = jnp.concatenate([w1, e2 * w1], axis=0)


def _router(h, g_ffn, w_router):
    t, d = h.shape
    n_e = w_router.shape[1]
    tm = TM_ROUTE
    nt = t // tm
    idx, wgt = pl.pallas_call(
        _router_body,
        grid=(nt,),
        in_specs=[
            pl.BlockSpec((tm, d), lambda i: (i, 0)),
            pl.BlockSpec((1, d), lambda i: (0, 0)),
            pl.BlockSpec((n_e, d), lambda i: (0, 0)),
        ],
        out_specs=[pl.BlockSpec((1, TOP_K, tm), lambda i: (i, 0, 0)),
                   pl.BlockSpec((1, TOP_K, tm), lambda i: (i, 0, 0))],
        out_shape=[jax.ShapeDtypeStruct((nt, TOP_K, tm), jnp.int32),
                   jax.ShapeDtypeStruct((nt, TOP_K, tm), F32)],
        compiler_params=_params("parallel"),
        name="router",
    )(h, g_ffn.reshape(1, d), w_router.T.astype(F32))
    idx = idx.transpose(0, 2, 1).reshape(t, TOP_K)
    wgt = wgt.transpose(0, 2, 1).reshape(t, TOP_K)
    return idx, wgt


def _scatter_rows_body(s1_ref, s2_ref, h_ref, xs_in_ref, xs_ref, sem):
    del xs_in_ref
    tm = h_ref.shape[0]

    def issue(r, c):
        src = h_ref.at[pl.ds(r, 1)]
        pltpu.make_async_copy(src, xs_ref.at[pl.ds(s1_ref[r], 1)], sem).start(priority=0)
        pltpu.make_async_copy(src, xs_ref.at[pl.ds(s2_ref[r], 1)], sem).start(priority=1)
        return c

    lax.fori_loop(0, tm, issue, 0, unroll=8)
    pltpu.make_async_copy(h_ref, xs_ref.at[pl.ds(0, tm)], sem).wait()
    pltpu.make_async_copy(h_ref, xs_ref.at[pl.ds(0, tm)], sem).wait()


def _scatter_rows(h, slot1, slot2, cap):
    t, d = h.shape
    tm = TM_SCATTER
    smem_tile = pl.BlockSpec((tm,), lambda i: (i,), memory_space=pltpu.SMEM)
    return pl.pallas_call(
        _scatter_rows_body,
        grid=(t // tm,),
        in_specs=[smem_tile, smem_tile,
                  pl.BlockSpec((tm, d), lambda i: (i, 0)),
                  pl.BlockSpec(memory_space=pl.ANY)],
        out_specs=pl.BlockSpec(memory_space=pl.ANY),
        out_shape=jax.ShapeDtypeStruct((cap, d), h.dtype),
        scratch_shapes=[pltpu.SemaphoreType.DMA(())],
        input_output_aliases={3: 0},
        compiler_params=_params("arbitrary"),
        name="scatter_rows",
    )(slot1, slot2, h, jnp.zeros((cap, d), h.dtype))


def _moe_ffn_body(tile_expert_ref, tile_valid_ref, x_ref, g_ref, wg_ref, wu_ref, wd_ref, o_ref, xn_ref, acc_ref):
    i = pl.program_id(0)
    j = pl.program_id(1)
    valid = tile_valid_ref[i] > 0

    @pl.when(valid)
    def _():
        @pl.when(j == 0)
        def _():
            xn_ref[...] = _rms(x_ref[...], g_ref[...]).astype(xn_ref.dtype)

        part = _swiglu_tile(xn_ref[...], wg_ref.at[0], wu_ref.at[0], wd_ref.at[0])

        @pl.when(j == 0)
        def _():
            acc_ref[...] = part

        @pl.when(j > 0)
        def _():
            acc_ref[...] += part

    @pl.when(j == pl.num_programs(1) - 1)
    def _():
        o_ref[...] = jnp.where(valid, acc_ref[...], 0.0).astype(o_ref.dtype)


def _moe_ffn(xs, g_ffn, tile_expert, tile_valid, w_gate, w_up, w_down):
    cap, d = xs.shape
    f = w_gate.shape[2]
    tm, tf = TM_MOE, TF_FFN
    grid_spec = pltpu.PrefetchScalarGridSpec(
        num_scalar_prefetch=2,
        grid=(cap // tm, f // tf),
        in_specs=[
            pl.BlockSpec((tm, d), lambda i, j, te, tv: (i, 0)),
            pl.BlockSpec((1, d), lambda i, j, te, tv: (0, 0)),
            pl.BlockSpec((1, d, tf), lambda i, j, te, tv: (te[i], 0, j)),
            pl.BlockSpec((1, d, tf), lambda i, j, te, tv: (te[i], 0, j)),
            pl.BlockSpec((1, tf, d), lambda i, j, te, tv: (te[i], j, 0)),
        ],
        out_specs=pl.BlockSpec((tm, d), lambda i, j, te, tv: (i, 0)),
        scratch_shapes=[pltpu.VMEM((tm, d), BF16), pltpu.VMEM((tm, d), F32)],
    )
    return pl.pallas_call(
        _moe_ffn_body,
        grid_spec=grid_spec,
        out_shape=jax.ShapeDtypeStruct((cap, d), F32),
        compiler_params=_params("parallel", "arbitrary"),
        name="moe_ffn",
    )(tile_expert, tile_valid, xs, g_ffn.reshape(1, d), w_gate, w_up, w_down)


def _moe_plan(idx, n_experts, tm):
    t = idx.shape[0]
    pairs = t * TOP_K
    cap = pairs + n_experts * tm
    e_flat = idx.reshape(pairs)
    onehot = (e_flat[:, None] == jnp.arange(n_experts, dtype=jnp.int32)[None, :]).astype(jnp.int32)
    rank = jnp.sum(onehot * (jnp.cumsum(onehot, axis=0) - onehot), axis=1)
    counts = jnp.sum(onehot, axis=0)
    padded = ((counts + tm - 1) // tm) * tm
    ends = jnp.cumsum(padded)
    starts = ends - padded
    slot = jnp.sum(onehot * starts[None, :], axis=1) + rank
    tile_start = jnp.arange(cap // tm, dtype=jnp.int32) * tm
    tile_expert = jnp.sum((tile_start[:, None] >= ends[None, :]).astype(jnp.int32), axis=1)
    tile_valid = (tile_expert < n_experts).astype(jnp.int32)
    last_used = jnp.max(jnp.where(counts > 0, jnp.arange(n_experts, dtype=jnp.int32), 0))
    tile_expert = jnp.minimum(tile_expert, last_used).astype(jnp.int32)
    return slot.reshape(t, TOP_K).astype(jnp.int32), cap, tile_expert, tile_valid


def _combine_body(s1_ref, s2_ref, h_ref, wgt_ref, g_ref, ys_ref, o_ref, y1_ref, y2_ref, sem, *, apply_norm):
    tm = h_ref.shape[0]

    def issue(r, c):
        pltpu.make_async_copy(ys_ref.at[pl.ds(s1_ref[r], 1)], y1_ref.at[pl.ds(r, 1)], sem).start(priority=0)
        pltpu.make_async_copy(ys_ref.at[pl.ds(s2_ref[r], 1)], y2_ref.at[pl.ds(r, 1)], sem).start(priority=1)
        return c

    lax.fori_loop(0, tm, issue, 0, unroll=8)
    pltpu.make_async_copy(ys_ref.at[pl.ds(0, tm)], y1_ref, sem).wait()
    pltpu.make_async_copy(ys_ref.at[pl.ds(0, tm)], y2_ref, sem).wait()
    w = wgt_ref[...]
    hn = h_ref[...] + w[:, 0:1] * y1_ref[...] + w[:, 1:2] * y2_ref[...]
    o_ref[...] = _rms(hn, g_ref[...]) if apply_norm else hn


def _combine(h, ys, slot, wgt, g, *, apply_norm):
    t, d = h.shape
    tm = TM_COMBINE
    smem_tile = pl.BlockSpec((tm,), lambda i: (i,), memory_space=pltpu.SMEM)
    row = pl.BlockSpec((tm, d), lambda i: (i, 0))
    return pl.pallas_call(
        functools.partial(_combine_body, apply_norm=apply_norm),
        grid=(t // tm,),
        in_specs=[smem_tile, smem_tile, row,
                  pl.BlockSpec((tm, TOP_K), lambda i: (i, 0)),
                  pl.BlockSpec((1, d), lambda i: (0, 0)),
                  pl.BlockSpec(memory_space=pl.ANY)],
        out_specs=row,
        out_shape=jax.ShapeDtypeStruct((t, d), F32),
        scratch_shapes=[pltpu.VMEM((tm, d), F32), pltpu.VMEM((tm, d), F32), pltpu.SemaphoreType.DMA(())],
        compiler_params=_params("arbitrary"),
        name="combine",
    )(slot[:, 0], slot[:, 1], h, wgt, g.reshape(1, d), ys)


def _plain_norm_body(h_ref, g_ref, o_ref):
    o_ref[...] = _rms(h_ref[...], g_ref[...])


def _plain_norm(h, g):
    t, d = h.shape
    tm = TM_NORM
    row = pl.BlockSpec((tm, d), lambda i: (i, 0))
    return pl.pallas_call(
        _plain_norm_body,
        grid=(t // tm,),
        in_specs=[row, pl.BlockSpec((1, d), lambda i: (0, 0))],
        out_specs=row,
        out_shape=jax.ShapeDtypeStruct((t, d), F32),
        compiler_params=_params("parallel"),
        name="plain_norm",
    )(h, g.reshape(1, d))


def kernel(x, norm_mix, w_in, ssm_lambda_re, ssm_lambda_im, ssm_log_dt, ssm_b_re, ssm_b_im, ssm_c_re,
           ssm_c_im, ssm_d, ssm_w_glu, conv_w, w_br_a, w_br_b, w_br_c, w_out, norm_ffn, dense_w_gate,
           dense_w_up, dense_w_down, moe_w_router, moe_w_gate, moe_w_up, moe_w_down, final_norm):
    batch, seq, d = x.shape
    depth = w_in.shape[0]
    t = batch * seq
    w_attn = w_br_a.shape[1]
    w_ssm = w_br_b.shape[1]
    w_conv = w_br_c.shape[1]
    n_qkv = 3 * w_attn
    n_sc = w_ssm + 3 * w_conv
    n_gate = N_BRANCH * d
    n_experts = moe_w_router.shape[-1]

    h = x.reshape(t, d)
    normed = False
    for i in range(depth):
        qkv, sc, gates = _inproj(h, norm_mix[i], w_in[i].astype(BF16),
                                 n_q=w_attn, n_qkv=n_qkv, n_sc=n_sc, n_gate=n_gate)
        attn = _sb_attention(qkv, batch=batch, seq=seq, width=w_attn)
        intra, to_state, from_state, lbar_pow = _ssm_tables(
            ssm_lambda_re[i], ssm_lambda_im[i], ssm_log_dt[i], ssm_b_re[i], ssm_b_im[i],
            ssm_c_re[i], ssm_c_im[i])
        step = _ssm_step_table(lbar_pow, seq // SSM_CHUNK)
        yssm = _ssm_scan(sc[:, :w_ssm], (intra, to_state, from_state, step), batch=batch, seq=seq)
        h, xn = _merge(attn, yssm, sc, gates, h, ssm_d[i], ssm_w_glu[i].astype(BF16), conv_w[i],
                       w_br_a[i].astype(BF16), w_br_b[i].astype(BF16), w_br_c[i].astype(BF16),
                       w_out[i].astype(BF16), norm_ffn[i], seq=seq)
        j = i // 2
        if i % 2 == 0:
            h = _dense_ffn(xn, h, dense_w_gate[j].astype(BF16), dense_w_up[j].astype(BF16),
                           dense_w_down[j].astype(BF16))
        else:
            idx, wgt = _router(h, norm_ffn[i], moe_w_router[j])
            slot, cap, tile_expert, tile_valid = _moe_plan(idx, n_experts, TM_MOE)
            xs = _scatter_rows(h, slot[:, 0], slot[:, 1], cap)
            ys = _moe_ffn(xs, norm_ffn[i], tile_expert, tile_valid, moe_w_gate[j].astype(BF16),
                          moe_w_up[j].astype(BF16), moe_w_down[j].astype(BF16))
            last = i == depth - 1
            h = _combine(h, ys, slot, wgt, final_norm if last else norm_ffn[i], apply_norm=last)
            normed = last
    out = h if normed else _plain_norm(h, final_norm)
    return out.reshape(batch, seq, d)
```

```python
import functools

import jax
import jax.numpy as jnp
from jax import lax
from jax.experimental import pallas as pl
from jax.experimental.pallas import tpu as pltpu

F32 = jnp.float32
BF16 = jnp.bfloat16

EPS = 1e-6
HEAD_DIM = 64
SSM_CHUNK = 16
SSM_MAX_RE = -1e-4
CONV_K = 3
N_BRANCH = 3
TOP_K = 2
LOG2E = 1.4426950408889634
MASKED_LOG_WEIGHT = -1e30

LANES = 128
SUBLANES = 8
MXU_DIM = 256
VMEM_LIMIT_BYTES = 56 * 1024 * 1024

TM_PROJ = 512
TQ_ATTN = MXU_DIM
TM_MERGE = 512
TM_FFN = 512
TF_FFN = 1792
TM_MOE = 512
TM_ROUTE = 512
TM_SCATTER = 256
TM_COMBINE = 256
TM_NORM = 512


def _params(*sem):
    return pltpu.CompilerParams(dimension_semantics=sem, vmem_limit_bytes=VMEM_LIMIT_BYTES)


def _rms(x, g):
    ms = jnp.mean(x * x, axis=-1, keepdims=True)
    return x * lax.rsqrt(ms + EPS) * g


def _inproj_body(x_ref, g_ref, w_ref, qt_ref, k_ref, vt_ref, sc_ref, gate_ref, *, width, n_sc, n_gate, q_scale):
    xn = _rms(x_ref[...], g_ref[...]).astype(BF16)
    tm = xn.shape[0]
    tq = vt_ref.shape[-1]
    chunk = 512

    def proj(c0, n):
        return jnp.dot(xn, w_ref[:, c0:c0 + n], preferred_element_type=F32)

    qt_ref[0] = (proj(0, width) * q_scale).T.astype(BF16)
    k_ref[...] = proj(width, width).astype(BF16)
    v = proj(2 * width, width)
    for kt in range(tm // tq):
        for p in range(width // LANES):
            vt_ref[0, p, kt] = v[kt * tq:(kt + 1) * tq, p * LANES:(p + 1) * LANES].T.astype(BF16)
    for c0 in range(0, n_sc, chunk):
        sc_ref[:, c0:c0 + chunk] = proj(3 * width + c0, chunk).astype(BF16)
    for c0 in range(0, n_gate, chunk):
        gate_ref[:, c0:c0 + chunk] = jax.nn.sigmoid(proj(3 * width + n_sc + c0, chunk)).astype(BF16)


def _inproj(h, g, w_bf16, *, batch, seq, width, n_sc, n_gate):
    t, d = h.shape
    tm = TM_PROJ
    tq = TQ_ATTN
    n_pairs = width // LANES
    tiles_per_seq = seq // tm
    body = functools.partial(_inproj_body, width=width, n_sc=n_sc, n_gate=n_gate, q_scale=HEAD_DIM ** -0.5)
    return pl.pallas_call(
        body,
        grid=(t // tm,),
        in_specs=[
            pl.BlockSpec((tm, d), lambda i: (i, 0)),
            pl.BlockSpec((1, d), lambda i: (0, 0)),
            pl.BlockSpec((d, 3 * width + n_sc + n_gate), lambda i: (0, 0), pipeline_mode=pl.Buffered(1)),
        ],
        out_specs=[
            pl.BlockSpec((1, width, tm), lambda i: (i // tiles_per_seq, 0, i % tiles_per_seq)),
            pl.BlockSpec((tm, width), lambda i: (i, 0)),
            pl.BlockSpec((1, n_pairs, tm // tq, LANES, tq),
                         lambda i: (i // tiles_per_seq, 0, i % tiles_per_seq, 0, 0)),
            pl.BlockSpec((tm, n_sc), lambda i: (i, 0)),
            pl.BlockSpec((tm, n_gate), lambda i: (i, 0)),
        ],
        out_shape=[
            jax.ShapeDtypeStruct((batch, width, seq), BF16),
            jax.ShapeDtypeStruct((t, width), BF16),
            jax.ShapeDtypeStruct((batch, n_pairs, seq // tq, LANES, tq), BF16),
            jax.ShapeDtypeStruct((t, n_sc), BF16),
            jax.ShapeDtypeStruct((t, n_gate), BF16),
        ],
        compiler_params=_params("parallel"),
        name="inproj",
    )(h, g.reshape(1, d), w_bf16)


def _sb_attn_body(qt_ref, k_ref, vt_ref, o_ref, acc_ref, carry_ref, lb_a_ref, sp_a_ref, lb_b_ref, sp_b_ref,
                  *, tq, n_pairs):
    i = pl.program_id(1)
    buf_a = (lb_a_ref, sp_a_ref)
    buf_b = (lb_b_ref, sp_b_ref)
    n_heads = 2 * n_pairs
    ones_rows = carry_ref.shape[1]
    key = lax.broadcasted_iota(jnp.int32, (tq, tq), 0)
    qry = lax.broadcasted_iota(jnp.int32, (tq, tq), 1)
    valid = key < qry
    er = lax.broadcasted_iota(jnp.int32, (tq + ones_rows, tq), 0)
    ec = lax.broadcasted_iota(jnp.int32, (tq + ones_rows, tq), 1)
    later_and_sum = jnp.where((ec > er) | (er >= tq), 1.0, 0.0).astype(BF16)

    feat = lax.broadcasted_iota(jnp.int32, (LANES, tq), 0)
    q_heads = []
    for p in range(n_pairs):
        qp = qt_ref[0, p * LANES:(p + 1) * LANES, :]
        zero = jnp.zeros_like(qp)
        q_heads.append(jnp.where(feat < HEAD_DIM, qp, zero))
        q_heads.append(jnp.where(feat >= HEAD_DIM, qp, zero))

    def scores(j, buf, masked):
        lb_ref, sp_ref = buf
        start = pl.multiple_of(j * tq, tq)
        for h in range(n_heads):
            p = h // 2
            kblk = k_ref[pl.ds(start, tq), p * LANES:(p + 1) * LANES]
            z = jnp.dot(kblk, q_heads[h], preferred_element_type=F32)
            l1p = jnp.log(1.0 + jnp.exp2(jnp.abs(z) * (-LOG2E)))
            sp = jnp.maximum(z, 0.0) + l1p
            lb = z - sp
            if masked:
                sp = jnp.where(valid, sp, 0.0)
                lb = jnp.where(valid, lb, MASKED_LOG_WEIGHT)
            lb_ref[h] = lb
            sp_ref[h] = sp.astype(BF16)

    def weights(j, buf):
        lb_ref, sp_ref = buf
        ws, sums = [], []
        for h in range(n_heads):
            ts = jnp.dot(later_and_sum, sp_ref[h], preferred_element_type=F32)
            ws.append(jnp.exp(lb_ref[h] - ts[0:tq]).astype(BF16))
            sums.append(ts[tq:tq + ones_rows])
        for h in range(n_heads):
            p, hh = divmod(h, 2)
            vt = vt_ref[0, p, j, hh * HEAD_DIM:(hh + 1) * HEAD_DIM, :]
            pv = jnp.dot(vt, ws[h], preferred_element_type=F32)
            carry = carry_ref[h]
            scale = jnp.exp(-carry[0:1, :])
            rows = slice(h * HEAD_DIM, (h + 1) * HEAD_DIM)
            acc_ref[rows, :] += pv * scale
            carry_ref[h] = carry + sums[h]

    acc_ref[...] = jnp.zeros_like(acc_ref)
    carry_ref[...] = jnp.zeros_like(carry_ref)
    scores(i, buf_a, True)

    def earlier_keys_matter():
        return (jnp.max(jnp.exp(-carry_ref[:, 0, :])) > 0.0).astype(jnp.int32)

    def two_tiles(c):
        m, _ = c
        j = i - 1 - 2 * m
        scores(j, buf_b, False)
        weights(j + 1, buf_a)
        scores(j - 1, buf_a, False)
        weights(j, buf_b)
        return m + 1, earlier_keys_matter()

    _, alive = lax.while_loop(lambda c: (c[0] < i // 2) & (c[1] > 0), two_tiles, (jnp.int32(0), jnp.int32(1)))

    @pl.when((alive > 0) & (i % 2 == 1))
    def _():
        scores(0, buf_b, False)
        weights(1, buf_a)
        weights(0, buf_b)

    @pl.when((alive > 0) & (i % 2 == 0))
    def _():
        weights(0, buf_a)

    for p in range(n_pairs):
        o_ref[:, p * LANES:(p + 1) * LANES] = acc_ref[p * LANES:(p + 1) * LANES, :].T.astype(o_ref.dtype)


def _sb_attention(qt, k, vt):
    batch, width, seq = qt.shape
    n_pairs, nq, _, tq = vt.shape[1:]
    return pl.pallas_call(
        functools.partial(_sb_attn_body, tq=tq, n_pairs=n_pairs),
        grid=(batch, nq),
        in_specs=[
            pl.BlockSpec((1, width, tq), lambda b, i: (b, 0, i)),
            pl.BlockSpec((seq, width), lambda b, i: (b, 0)),
            pl.BlockSpec((1, n_pairs, nq, LANES, tq), lambda b, i: (b, 0, 0, 0, 0)),
        ],
        out_specs=pl.BlockSpec((tq, width), lambda b, i: (b * nq + i, 0)),
        out_shape=jax.ShapeDtypeStruct((batch * seq, width), BF16),
        scratch_shapes=[pltpu.VMEM((width, tq), F32), pltpu.VMEM((2 * n_pairs, 16, tq), F32),
                        pltpu.VMEM((2 * n_pairs, tq, tq), F32), pltpu.VMEM((2 * n_pairs, tq, tq), BF16),
                        pltpu.VMEM((2 * n_pairs, tq, tq), F32), pltpu.VMEM((2 * n_pairs, tq, tq), BF16)],
        compiler_params=_params("parallel", "arbitrary"),
        name="sb_attention",
    )(qt, k, vt)


def _ssm_tables(lam_re, lam_im, log_dt, b_re, b_im, c_re, c_im, n_chunks):
    hp = lax.Precision.HIGHEST
    g, p = lam_re.shape
    hdim = b_re.shape[-1]
    c = SSM_CHUNK
    lr = jnp.minimum(lam_re.astype(F32), SSM_MAX_RE)
    li = lam_im.astype(F32)
    dt = jnp.exp(log_dt.astype(F32))[:, None]

    def lbar_pow(k):
        mag = jnp.exp(k * (lr * dt))
        ang = k * (li * dt)
        return mag * jnp.cos(ang), mag * jnp.sin(ang)

    ab_re, ab_im = lbar_pow(1.0)
    den = lr * lr + li * li
    nr = ab_re - 1.0
    ni = ab_im
    f_re = (nr * lr + ni * li) / den
    f_im = (ni * lr - nr * li) / den
    br = b_re.astype(F32)
    bi = b_im.astype(F32)
    bb_re = f_re[..., None] * br - f_im[..., None] * bi
    bb_im = f_re[..., None] * bi + f_im[..., None] * br
    cr = c_re.astype(F32)
    ci = c_im.astype(F32)

    ks = jnp.arange(c + 1, dtype=F32)[:, None, None]
    pw_re, pw_im = lbar_pow(ks)
    eye = jnp.eye(g, dtype=F32)

    def block_diag(x):
        n, _, a, b = x.shape
        return (x[:, :, :, None, :] * eye[None, :, None, :, None]).reshape(n, g * a, g * b)

    cl_re = cr[None] * pw_re[:c, :, None, :] - ci[None] * pw_im[:c, :, None, :]
    cl_im = cr[None] * pw_im[:c, :, None, :] + ci[None] * pw_re[:c, :, None, :]
    kk = (jnp.einsum('kghp,gpi->kgih', cl_re, bb_re, precision=hp)
          - jnp.einsum('kghp,gpi->kgih', cl_im, bb_im, precision=hp))
    intra = block_diag(kk)

    rev_re = pw_re[:c][::-1]
    rev_im = pw_im[:c][::-1]
    ps_re = rev_re[..., None] * bb_re[None] - rev_im[..., None] * bb_im[None]
    ps_im = rev_re[..., None] * bb_im[None] + rev_im[..., None] * bb_re[None]
    to_state = jnp.concatenate([block_diag(ps_re.transpose(0, 1, 3, 2)),
                                block_diag(ps_im.transpose(0, 1, 3, 2))], axis=-1)

    q_re = cr[None] * pw_re[1:, :, None, :] - ci[None] * pw_im[1:, :, None, :]
    q_im = cr[None] * pw_im[1:, :, None, :] + ci[None] * pw_re[1:, :, None, :]
    from_state = jnp.concatenate([block_diag(q_re.transpose(0, 1, 3, 2)),
                                  block_diag(-q_im.transpose(0, 1, 3, 2))], axis=1)

    steps = []
    s = 1
    while s < n_chunks:
        a_re, a_im = lbar_pow(float(c * s))
        steps.append(jnp.stack([a_re.reshape(-1, MXU_DIM), a_im.reshape(-1, MXU_DIM)], axis=1))
        s *= 2
    step = jnp.stack(steps, axis=1)
    return intra.astype(BF16), to_state.astype(BF16), from_state.astype(BF16), step


def _ssm_body(u_ref, intra_ref, to_state_ref, from_state_ref, step_ref, y_ref, z_ref, prev_ref, acc_ref,
              *, n_chunks, batch):
    s = pl.program_id(0)
    c = u_ref.shape[0]
    n_blk = z_ref.shape[0]
    half = n_blk // 2
    bw = z_ref.shape[2]

    @pl.when(s == 0)
    def _():
        z_ref[...] = jnp.zeros_like(z_ref)

    @pl.when(s < c)
    def _():
        ut = u_ref[s]
        for blk in range(n_blk):
            z_ref[blk] += jnp.dot(ut, to_state_ref[0, :, blk * bw:(blk + 1) * bw], preferred_element_type=F32)

    @pl.when(s == c)
    def _():
        chunk = lax.broadcasted_iota(jnp.int32, (n_chunks, bw), 0)

        def shifted(x, n):
            return jnp.where(chunk >= n, pltpu.roll(x, n, axis=0), 0.0)

        def scan_block(idx, carry):
            b = idx // half
            pb = idx % half
            rows = pl.ds(pl.multiple_of(b * n_chunks, n_chunks), n_chunks)
            re = z_ref[pb, rows, :]
            im = z_ref[half + pb, rows, :]
            n = 1
            k = 0
            while n < n_chunks:
                a_re = step_ref[pb, k, 0:1, :]
                a_im = step_ref[pb, k, 1:2, :]
                re_s = shifted(re, n)
                im_s = shifted(im, n)
                re, im = re + a_re * re_s - a_im * im_s, im + a_re * im_s + a_im * re_s
                n *= 2
                k += 1
            prev_ref[pb, rows, :] = shifted(re, 1).astype(prev_ref.dtype)
            prev_ref[half + pb, rows, :] = shifted(im, 1).astype(prev_ref.dtype)
            return carry

        lax.fori_loop(0, batch * half, scan_block, 0)

    @pl.when(s >= c)
    def _():
        t = s - c
        acc = jnp.dot(prev_ref[0], from_state_ref[0, 0:bw, :], preferred_element_type=F32)
        for blk in range(1, n_blk):
            acc += jnp.dot(prev_ref[blk], from_state_ref[0, blk * bw:(blk + 1) * bw, :],
                           preferred_element_type=F32)
        acc_ref[...] = acc

        def lag(j, carry):
            acc_ref[...] += jnp.dot(u_ref[j], intra_ref[t - j], preferred_element_type=F32)
            return carry

        lax.fori_loop(0, t + 1, lag, 0)
        y_ref[0] = acc_ref[...].astype(y_ref.dtype)


def _ssm_scan(u, tables, *, batch, seq):
    intra, to_state, from_state, step = tables
    width = u.shape[1]
    c = SSM_CHUNK
    n_chunks = seq // c
    rows = batch * n_chunks
    n_state2 = to_state.shape[-1]
    n_blk = n_state2 // MXU_DIM
    u3 = u.reshape(rows, c, width).transpose(1, 0, 2)
    whole = lambda shape: pl.BlockSpec(shape, lambda s: (0,) * len(shape), pipeline_mode=pl.Buffered(1))
    y3 = pl.pallas_call(
        functools.partial(_ssm_body, n_chunks=n_chunks, batch=batch),
        grid=(2 * c,),
        in_specs=[
            whole((c, rows, width)),
            whole((c, width, width)),
            pl.BlockSpec((1, width, n_state2), lambda s: (jnp.minimum(s, c - 1), 0, 0)),
            pl.BlockSpec((1, n_state2, width), lambda s: (jnp.maximum(s - c, 0), 0, 0)),
            whole(step.shape),
        ],
        out_specs=pl.BlockSpec((1, rows, width), lambda s: (jnp.maximum(s - c, 0), 0, 0)),
        out_shape=jax.ShapeDtypeStruct((c, rows, width), BF16),
        scratch_shapes=[pltpu.VMEM((n_blk, rows, MXU_DIM), F32), pltpu.VMEM((n_blk, rows, MXU_DIM), BF16),
                        pltpu.VMEM((rows, width), F32)],
        compiler_params=_params("arbitrary"),
        name="ssm_scan",
    )(u3, intra, to_state, from_state, step)
    return y3.transpose(1, 0, 2).reshape(batch * seq, width)


def _gelu_tanh(x):
    return 0.5 * x * (1.0 + jnp.tanh(0.7978845608028654 * (x + 0.044715 * (x * x * x))))


def _merge_body(attn_ref, yssm_ref, sc_ref, halo_ref, gate_ref, h_ref, dskip_ref, wglu_ref, convw_ref,
                wa_ref, wb_ref, wc_ref, wout_ref, gn_ref, hout_ref, xn_ref, *, tiles_per_seq, w_ssm, w_conv):
    i = pl.program_id(0)
    tm = h_ref.shape[0]
    d = h_ref.shape[1]

    u = sc_ref[:, 0:w_ssm].astype(F32)
    yb = _gelu_tanh(yssm_ref[...].astype(F32) + dskip_ref[...] * u)
    yb = yb * jax.nn.sigmoid(jnp.dot(yb.astype(BF16), wglu_ref[...], preferred_element_type=F32))
    y_b = jnp.dot(yb.astype(BF16), wb_ref[...], preferred_element_type=F32)

    o_h, o_b, o_c = w_ssm, w_ssm + w_conv, w_ssm + 2 * w_conv
    uc = sc_ref[:, o_c:o_c + w_conv].astype(F32) * sc_ref[:, o_h:o_h + w_conv].astype(F32)
    halo = halo_ref[:, o_c:o_c + w_conv].astype(F32) * halo_ref[:, o_h:o_h + w_conv].astype(F32)
    halo = jnp.where(i % tiles_per_seq == 0, 0.0, halo)
    row = lax.broadcasted_iota(jnp.int32, (tm, w_conv), 0)
    prev1 = jnp.where(row == 0, halo[SUBLANES - 1:SUBLANES, :], pltpu.roll(uc, 1, axis=0))
    prev2 = jnp.where(row == 0, halo[SUBLANES - 2:SUBLANES - 1, :],
                      jnp.where(row == 1, halo[SUBLANES - 1:SUBLANES, :], pltpu.roll(uc, 2, axis=0)))
    conv = convw_ref[0:1, :] * prev2 + convw_ref[1:2, :] * prev1 + convw_ref[2:3, :] * uc
    yc = sc_ref[:, o_b:o_b + w_conv].astype(F32) * conv
    y_c = jnp.dot(yc.astype(BF16), wc_ref[...], preferred_element_type=F32)

    y_a = jnp.dot(attn_ref[...], wa_ref[...], preferred_element_type=F32)

    merged = (gate_ref[:, 0:d].astype(F32) * y_a + gate_ref[:, d:2 * d].astype(F32) * y_b
              + gate_ref[:, 2 * d:3 * d].astype(F32) * y_c)
    hn = h_ref[...] + jnp.dot(merged.astype(BF16), wout_ref[...], preferred_element_type=F32)
    hout_ref[...] = hn
    xn_ref[...] = _rms(hn, gn_ref[...]).astype(xn_ref.dtype)


def _merge(attn, yssm, sc, gates, h, d_skip, w_glu, conv_w, w_a, w_b, w_c, w_out, g_ffn, *, seq):
    t, d = h.shape
    tm = TM_MERGE
    w_ssm = yssm.shape[1]
    w_conv = conv_w.shape[1]
    w_attn = attn.shape[1]
    n_sc = sc.shape[1]
    body = functools.partial(_merge_body, tiles_per_seq=seq // tm, w_ssm=w_ssm, w_conv=w_conv)
    full = lambda shape: pl.BlockSpec(shape, lambda i: (0,) * len(shape))
    halo_blocks = tm // SUBLANES
    return pl.pallas_call(
        body,
        grid=(t // tm,),
        in_specs=[
            pl.BlockSpec((tm, w_attn), lambda i: (i, 0)),
            pl.BlockSpec((tm, w_ssm), lambda i: (i, 0)),
            pl.BlockSpec((tm, n_sc), lambda i: (i, 0)),
            pl.BlockSpec((SUBLANES, n_sc), lambda i: (jnp.maximum(i * halo_blocks - 1, 0), 0)),
            pl.BlockSpec((tm, N_BRANCH * d), lambda i: (i, 0)),
            pl.BlockSpec((tm, d), lambda i: (i, 0)),
            full((1, w_ssm)), full((w_ssm, w_ssm)), full((CONV_K, w_conv)),
            full((w_attn, d)), full((w_ssm, d)), full((w_conv, d)), full((d, d)), full((1, d)),
        ],
        out_specs=[pl.BlockSpec((tm, d), lambda i: (i, 0)), pl.BlockSpec((tm, d), lambda i: (i, 0))],
        out_shape=[jax.ShapeDtypeStruct((t, d), F32), jax.ShapeDtypeStruct((t, d), BF16)],
        compiler_params=_params("parallel"),
        name="merge",
    )(attn, yssm, sc, sc, gates, h, d_skip.reshape(1, w_ssm).astype(F32), w_glu, conv_w.astype(F32),
      w_a, w_b, w_c, w_out, g_ffn.reshape(1, d))


def _swiglu_tile(x, wg_ref, wu_ref, wd_ref):
    gate = jnp.dot(x, wg_ref[...], preferred_element_type=F32)
    up = jnp.dot(x, wu_ref[...], preferred_element_type=F32)
    act = (gate * jax.nn.sigmoid(gate) * up).astype(BF16)
    return jnp.dot(act, wd_ref[...], preferred_element_type=F32)


def _dense_ffn_body(x_ref, h_ref, wg_ref, wu_ref, wd_ref, o_ref, acc_ref):
    j = pl.program_id(1)
    part = _swiglu_tile(x_ref[...], wg_ref, wu_ref, wd_ref)

    @pl.when(j == 0)
    def _():
        acc_ref[...] = h_ref[...] + part

    @pl.when(j > 0)
    def _():
        acc_ref[...] += part

    @pl.when(j == pl.num_programs(1) - 1)
    def _():
        o_ref[...] = acc_ref[...]


def _dense_ffn(xn, h, w_gate, w_up, w_down):
    t, d = h.shape
    f = w_gate.shape[1]
    tm, tf = TM_FFN, TF_FFN
    return pl.pallas_call(
        _dense_ffn_body,
        grid=(t // tm, f // tf),
        in_specs=[
            pl.BlockSpec((tm, d), lambda i, j: (i, 0)),
            pl.BlockSpec((tm, d), lambda i, j: (i, 0)),
            pl.BlockSpec((d, tf), lambda i, j: (0, j)),
            pl.BlockSpec((d, tf), lambda i, j: (0, j)),
            pl.BlockSpec((tf, d), lambda i, j: (j, 0)),
        ],
        out_specs=pl.BlockSpec((tm, d), lambda i, j: (i, 0)),
        out_shape=jax.ShapeDtypeStruct((t, d), F32),
        scratch_shapes=[pltpu.VMEM((tm, d), F32)],
        compiler_params=_params("parallel", "arbitrary"),
        name="dense_ffn",
    )(xn, h, w_gate, w_up, w_down)


def _router_body(h_ref, g_ref, wr_ref, idx_ref, wgt_ref):
    xn = _rms(h_ref[...], g_ref[...])
    logits = lax.dot_general(wr_ref[...], xn, (((1,), (1,)), ((), ())), preferred_element_type=F32,
                             precision=lax.Precision.HIGHEST)
    n_e = logits.shape[0]
    e_idx = lax.broadcasted_iota(jnp.int32, logits.shape, 0)
    m1 = jnp.max(logits, axis=0, keepdims=True)
    i1 = jnp.min(jnp.where(logits == m1, e_idx, n_e), axis=0, keepdims=True)
    rest = jnp.where(e_idx == i1, -jnp.inf, logits)
    m2 = jnp.max(rest, axis=0, keepdims=True)
    i2 = jnp.min(jnp.where(rest == m2, e_idx, n_e), axis=0, keepdims=True)
    e2 = jnp.exp(m2 - m1)
    w1 = 1.0 / (1.0 + e2)
    idx_ref[0] = jnp.concatenate([i1, i2], axis=0)
    wgt_ref[0] = jnp.concatenate([w1, e2 * w1], axis=0)


def _router(h, g_ffn, w_router):
    t, d = h.shape
    n_e = w_router.shape[1]
    tm = TM_ROUTE
    nt = t // tm
    idx, wgt = pl.pallas_call(
        _router_body,
        grid=(nt,),
        in_specs=[
            pl.BlockSpec((tm, d), lambda i: (i, 0)),
            pl.BlockSpec((1, d), lambda i: (0, 0)),
            pl.BlockSpec((n_e, d), lambda i: (0, 0)),
        ],
        out_specs=[pl.BlockSpec((1, TOP_K, tm), lambda i: (i, 0, 0)),
                   pl.BlockSpec((1, TOP_K, tm), lambda i: (i, 0, 0))],
        out_shape=[jax.ShapeDtypeStruct((nt, TOP_K, tm), jnp.int32),
                   jax.ShapeDtypeStruct((nt, TOP_K, tm), F32)],
        compiler_params=_params("parallel"),
        name="router",
    )(h, g_ffn.reshape(1, d), w_router.T.astype(F32))
    idx = idx.transpose(0, 2, 1).reshape(t, TOP_K)
    wgt = wgt.transpose(0, 2, 1).reshape(t, TOP_K)
    return idx, wgt


def _scatter_rows_body(s1_ref, s2_ref, h_ref, xs_in_ref, xs_ref, sem):
    del xs_in_ref
    tm = h_ref.shape[0]

    def issue(r, c):
        src = h_ref.at[pl.ds(r, 1)]
        pltpu.make_async_copy(src, xs_ref.at[pl.ds(s1_ref[r], 1)], sem).start(priority=0)
        pltpu.make_async_copy(src, xs_ref.at[pl.ds(s2_ref[r], 1)], sem).start(priority=1)
        return c

    lax.fori_loop(0, tm, issue, 0, unroll=8)
    pltpu.make_async_copy(h_ref, xs_ref.at[pl.ds(0, tm)], sem).wait()
    pltpu.make_async_copy(h_ref, xs_ref.at[pl.ds(0, tm)], sem).wait()


def _scatter_rows(h, slot1, slot2, cap):
    t, d = h.shape
    tm = TM_SCATTER
    smem_tile = pl.BlockSpec((tm,), lambda i: (i,), memory_space=pltpu.SMEM)
    return pl.pallas_call(
        _scatter_rows_body,
        grid=(t // tm,),
        in_specs=[smem_tile, smem_tile,
                  pl.BlockSpec((tm, d), lambda i: (i, 0)),
                  pl.BlockSpec(memory_space=pl.ANY)],
        out_specs=pl.BlockSpec(memory_space=pl.ANY),
        out_shape=jax.ShapeDtypeStruct((cap, d), h.dtype),
        scratch_shapes=[pltpu.SemaphoreType.DMA(())],
        input_output_aliases={3: 0},
        compiler_params=_params("arbitrary"),
        name="scatter_rows",
    )(slot1, slot2, h, jnp.zeros((cap, d), h.dtype))


def _moe_ffn_body(tile_expert_ref, tile_valid_ref, x_ref, g_ref, wg_ref, wu_ref, wd_ref, o_ref, xn_ref, acc_ref):
    i = pl.program_id(0)
    j = pl.program_id(1)
    valid = tile_valid_ref[i] > 0

    @pl.when(valid)
    def _():
        @pl.when(j == 0)
        def _():
            xn_ref[...] = _rms(x_ref[...], g_ref[...]).astype(xn_ref.dtype)

        part = _swiglu_tile(xn_ref[...], wg_ref.at[0], wu_ref.at[0], wd_ref.at[0])

        @pl.when(j == 0)
        def _():
            acc_ref[...] = part

        @pl.when(j > 0)
        def _():
            acc_ref[...] += part

    @pl.when(j == pl.num_programs(1) - 1)
    def _():
        o_ref[...] = jnp.where(valid, acc_ref[...], 0.0).astype(o_ref.dtype)


def _moe_ffn(xs, g_ffn, tile_expert, tile_valid, w_gate, w_up, w_down):
    cap, d = xs.shape
    f = w_gate.shape[2]
    tm, tf = TM_MOE, TF_FFN
    grid_spec = pltpu.PrefetchScalarGridSpec(
        num_scalar_prefetch=2,
        grid=(cap // tm, f // tf),
        in_specs=[
            pl.BlockSpec((tm, d), lambda i, j, te, tv: (i, 0)),
            pl.BlockSpec((1, d), lambda i, j, te, tv: (0, 0)),
            pl.BlockSpec((1, d, tf), lambda i, j, te, tv: (te[i], 0, j)),
            pl.BlockSpec((1, d, tf), lambda i, j, te, tv: (te[i], 0, j)),
            pl.BlockSpec((1, tf, d), lambda i, j, te, tv: (te[i], j, 0)),
        ],
        out_specs=pl.BlockSpec((tm, d), lambda i, j, te, tv: (i, 0)),
        scratch_shapes=[pltpu.VMEM((tm, d), BF16), pltpu.VMEM((tm, d), F32)],
    )
    return pl.pallas_call(
        _moe_ffn_body,
        grid_spec=grid_spec,
        out_shape=jax.ShapeDtypeStruct((cap, d), F32),
        compiler_params=_params("parallel", "arbitrary"),
        name="moe_ffn",
    )(tile_expert, tile_valid, xs, g_ffn.reshape(1, d), w_gate, w_up, w_down)


def _moe_plan(idx, n_experts, tm):
    t = idx.shape[0]
    pairs = t * TOP_K
    cap = pairs + n_experts * tm
    e_flat = idx.reshape(pairs)
    onehot = (e_flat[:, None] == jnp.arange(n_experts, dtype=jnp.int32)[None, :]).astype(jnp.int32)
    rank = jnp.sum(onehot * (jnp.cumsum(onehot, axis=0) - onehot), axis=1)
    counts = jnp.sum(onehot, axis=0)
    padded = ((counts + tm - 1) // tm) * tm
    ends = jnp.cumsum(padded)
    starts = ends - padded
    slot = jnp.sum(onehot * starts[None, :], axis=1) + rank
    tile_start = jnp.arange(cap // tm, dtype=jnp.int32) * tm
    tile_expert = jnp.sum((tile_start[:, None] >= ends[None, :]).astype(jnp.int32), axis=1)
    tile_valid = (tile_expert < n_experts).astype(jnp.int32)
    last_used = jnp.max(jnp.where(counts > 0, jnp.arange(n_experts, dtype=jnp.int32), 0))
    tile_expert = jnp.minimum(tile_expert, last_used).astype(jnp.int32)
    return slot.reshape(t, TOP_K).astype(jnp.int32), cap, tile_expert, tile_valid


def _combine_body(s1_ref, s2_ref, h_ref, wgt_ref, g_ref, ys_ref, o_ref, y1_ref, y2_ref, sem, *, apply_norm):
    tm = h_ref.shape[0]

    def issue(r, c):
        pltpu.make_async_copy(ys_ref.at[pl.ds(s1_ref[r], 1)], y1_ref.at[pl.ds(r, 1)], sem).start(priority=0)
        pltpu.make_async_copy(ys_ref.at[pl.ds(s2_ref[r], 1)], y2_ref.at[pl.ds(r, 1)], sem).start(priority=1)
        return c

    lax.fori_loop(0, tm, issue, 0, unroll=8)
    pltpu.make_async_copy(ys_ref.at[pl.ds(0, tm)], y1_ref, sem).wait()
    pltpu.make_async_copy(ys_ref.at[pl.ds(0, tm)], y2_ref, sem).wait()
    w = wgt_ref[...]
    hn = h_ref[...] + w[:, 0:1] * y1_ref[...] + w[:, 1:2] * y2_ref[...]
    o_ref[...] = _rms(hn, g_ref[...]) if apply_norm else hn


def _combine(h, ys, slot, wgt, g, *, apply_norm):
    t, d = h.shape
    tm = TM_COMBINE
    smem_tile = pl.BlockSpec((tm,), lambda i: (i,), memory_space=pltpu.SMEM)
    row = pl.BlockSpec((tm, d), lambda i: (i, 0))
    return pl.pallas_call(
        functools.partial(_combine_body, apply_norm=apply_norm),
        grid=(t // tm,),
        in_specs=[smem_tile, smem_tile, row,
                  pl.BlockSpec((tm, TOP_K), lambda i: (i, 0)),
                  pl.BlockSpec((1, d), lambda i: (0, 0)),
                  pl.BlockSpec(memory_space=pl.ANY)],
        out_specs=row,
        out_shape=jax.ShapeDtypeStruct((t, d), F32),
        scratch_shapes=[pltpu.VMEM((tm, d), F32), pltpu.VMEM((tm, d), F32), pltpu.SemaphoreType.DMA(())],
        compiler_params=_params("arbitrary"),
        name="combine",
    )(slot[:, 0], slot[:, 1], h, wgt, g.reshape(1, d), ys)


def _plain_norm_body(h_ref, g_ref, o_ref):
    o_ref[...] = _rms(h_ref[...], g_ref[...])


def _plain_norm(h, g):
    t, d = h.shape
    tm = TM_NORM
    row = pl.BlockSpec((tm, d), lambda i: (i, 0))
    return pl.pallas_call(
        _plain_norm_body,
        grid=(t // tm,),
        in_specs=[row, pl.BlockSpec((1, d), lambda i: (0, 0))],
        out_specs=row,
        out_shape=jax.ShapeDtypeStruct((t, d), F32),
        compiler_params=_params("parallel"),
        name="plain_norm",
    )(h, g.reshape(1, d))


def kernel(x, norm_mix, w_in, ssm_lambda_re, ssm_lambda_im, ssm_log_dt, ssm_b_re, ssm_b_im, ssm_c_re,
           ssm_c_im, ssm_d, ssm_w_glu, conv_w, w_br_a, w_br_b, w_br_c, w_out, norm_ffn, dense_w_gate,
           dense_w_up, dense_w_down, moe_w_router, moe_w_gate, moe_w_up, moe_w_down, final_norm):
    batch, seq, d = x.shape
    depth = w_in.shape[0]
    t = batch * seq
    w_attn = w_br_a.shape[1]
    w_ssm = w_br_b.shape[1]
    w_conv = w_br_c.shape[1]
    n_sc = w_ssm + 3 * w_conv
    n_gate = N_BRANCH * d
    n_experts = moe_w_router.shape[-1]

    h = x.reshape(t, d)
    normed = False
    for i in range(depth):
        qt, k, vt, sc, gates = _inproj(h, norm_mix[i], w_in[i].astype(BF16), batch=batch, seq=seq,
                                       width=w_attn, n_sc=n_sc, n_gate=n_gate)
        attn = _sb_attention(qt, k, vt)
        tables = _ssm_tables(ssm_lambda_re[i], ssm_lambda_im[i], ssm_log_dt[i], ssm_b_re[i], ssm_b_im[i],
                             ssm_c_re[i], ssm_c_im[i], seq // SSM_CHUNK)
        yssm = _ssm_scan(sc[:, :w_ssm], tables, batch=batch, seq=seq)
        h, xn = _merge(attn, yssm, sc, gates, h, ssm_d[i], ssm_w_glu[i].astype(BF16), conv_w[i],
                       w_br_a[i].astype(BF16), w_br_b[i].astype(BF16), w_br_c[i].astype(BF16),
                       w_out[i].astype(BF16), norm_ffn[i], seq=seq)
        j = i // 2
        if i % 2 == 0:
            h = _dense_ffn(xn, h, dense_w_gate[j].astype(BF16), dense_w_up[j].astype(BF16),
                           dense_w_down[j].astype(BF16))
        else:
            idx, wgt = _router(h, norm_ffn[i], moe_w_router[j])
            slot, cap, tile_expert, tile_valid = _moe_plan(idx, n_experts, TM_MOE)
            xs = _scatter_rows(h, slot[:, 0], slot[:, 1], cap)
            ys = _moe_ffn(xs, norm_ffn[i], tile_expert, tile_valid, moe_w_gate[j].astype(BF16),
                          moe_w_up[j].astype(BF16), moe_w_down[j].astype(BF16))
            last = i == depth - 1
            h = _combine(h, ys, slot, wgt, final_norm if last else norm_ffn[i], apply_norm=last)
            normed = last
    out = h if normed else _plain_norm(h, final_norm)
    return out.reshape(batch, seq, d)
```

```python
import functools

import jax
import jax.numpy as jnp
from jax import lax
from jax.experimental import pallas as pl
from jax.experimental.pallas import tpu as pltpu

F32 = jnp.float32
BF16 = jnp.bfloat16

EPS = 1e-6
HEAD_DIM = 64
SSM_CHUNK = 16
SSM_MAX_RE = -1e-4
CONV_K = 3
N_BRANCH = 3
TOP_K = 2
LOG2E = 1.4426950408889634
MASKED_LOG_WEIGHT = -1e30

LANES = 128
SUBLANES = 8
MXU_DIM = 256
VMEM_LIMIT_BYTES = 56 * 1024 * 1024

TM_PROJ = 512
TQ_ATTN = MXU_DIM
TM_MERGE = 512
TM_FFN = 512
TF_FFN = 1792
TM_MOE = 512
TM_ROUTE = 512
TM_SCATTER = 256
TM_COMBINE = 256
TM_NORM = 512


def _params(*sem):
    return pltpu.CompilerParams(dimension_semantics=sem, vmem_limit_bytes=VMEM_LIMIT_BYTES)


def _rms(x, g):
    ms = jnp.mean(x * x, axis=-1, keepdims=True)
    return x * lax.rsqrt(ms + EPS) * g


def _inproj_body(x_ref, g_ref, w_ref, qt_ref, k_ref, vt_ref, sc_ref, gate_ref, *, width, n_sc, n_gate, q_scale):
    xn = _rms(x_ref[...], g_ref[...]).astype(BF16)
    tm = xn.shape[0]
    tq = vt_ref.shape[-1]
    chunk = 512

    def proj(c0, n):
        return jnp.dot(xn, w_ref[:, c0:c0 + n], preferred_element_type=F32)

    qt_ref[0] = (proj(0, width) * q_scale).T.astype(BF16)
    k_ref[...] = proj(width, width).astype(BF16)
    v = proj(2 * width, width)
    for kt in range(tm // tq):
        for p in range(width // LANES):
            vt_ref[0, p, kt] = v[kt * tq:(kt + 1) * tq, p * LANES:(p + 1) * LANES].T.astype(BF16)
    for c0 in range(0, n_sc, chunk):
        sc_ref[:, c0:c0 + chunk] = proj(3 * width + c0, chunk).astype(BF16)
    for c0 in range(0, n_gate, chunk):
        gate_ref[:, c0:c0 + chunk] = jax.nn.sigmoid(proj(3 * width + n_sc + c0, chunk)).astype(BF16)


def _inproj(h, g, w_bf16, *, batch, seq, width, n_sc, n_gate):
    t, d = h.shape
    tm = TM_PROJ
    tq = TQ_ATTN
    n_pairs = width // LANES
    tiles_per_seq = seq // tm
    body = functools.partial(_inproj_body, width=width, n_sc=n_sc, n_gate=n_gate, q_scale=HEAD_DIM ** -0.5)
    return pl.pallas_call(
        body,
        grid=(t // tm,),
        in_specs=[
            pl.BlockSpec((tm, d), lambda i: (i, 0)),
            pl.BlockSpec((1, d), lambda i: (0, 0)),
            pl.BlockSpec((d, 3 * width + n_sc + n_gate), lambda i: (0, 0), pipeline_mode=pl.Buffered(1)),
        ],
        out_specs=[
            pl.BlockSpec((1, width, tm), lambda i: (i // tiles_per_seq, 0, i % tiles_per_seq)),
            pl.BlockSpec((tm, width), lambda i: (i, 0)),
            pl.BlockSpec((1, n_pairs, tm // tq, LANES, tq),
                         lambda i: (i // tiles_per_seq, 0, i % tiles_per_seq, 0, 0)),
            pl.BlockSpec((tm, n_sc), lambda i: (i, 0)),
            pl.BlockSpec((tm, n_gate), lambda i: (i, 0)),
        ],
        out_shape=[
            jax.ShapeDtypeStruct((batch, width, seq), BF16),
            jax.ShapeDtypeStruct((t, width), BF16),
            jax.ShapeDtypeStruct((batch, n_pairs, seq // tq, LANES, tq), BF16),
            jax.ShapeDtypeStruct((t, n_sc), BF16),
            jax.ShapeDtypeStruct((t, n_gate), BF16),
        ],
        compiler_params=_params("parallel"),
        name="inproj",
    )(h, g.reshape(1, d), w_bf16)


def _sb_attn_body(qt_ref, k_ref, vt_ref, o_ref, acc_ref, carry_ref, lb_a_ref, sp_a_ref, lb_b_ref, sp_b_ref,
                  *, tq, n_pairs):
    i = pl.program_id(1)
    buf_a = (lb_a_ref, sp_a_ref)
    buf_b = (lb_b_ref, sp_b_ref)
    n_heads = 2 * n_pairs
    ones_rows = carry_ref.shape[1]
    key = lax.broadcasted_iota(jnp.int32, (tq, tq), 0)
    qry = lax.broadcasted_iota(jnp.int32, (tq, tq), 1)
    valid = key < qry
    er = lax.broadcasted_iota(jnp.int32, (tq + ones_rows, tq), 0)
    ec = lax.broadcasted_iota(jnp.int32, (tq + ones_rows, tq), 1)
    later_and_sum = jnp.where((ec > er) | (er >= tq), 1.0, 0.0).astype(BF16)

    feat = lax.broadcasted_iota(jnp.int32, (LANES, tq), 0)
    q_heads = []
    for p in range(n_pairs):
        qp = qt_ref[0, p * LANES:(p + 1) * LANES, :]
        zero = jnp.zeros_like(qp)
        q_heads.append(jnp.where(feat < HEAD_DIM, qp, zero))
        q_heads.append(jnp.where(feat >= HEAD_DIM, qp, zero))

    def scores(j, buf, masked):
        lb_ref, sp_ref = buf
        start = pl.multiple_of(j * tq, tq)
        for h in range(n_heads):
            p = h // 2
            kblk = k_ref[pl.ds(start, tq), p * LANES:(p + 1) * LANES]
            z = jnp.dot(kblk, q_heads[h], preferred_element_type=F32)
            l1p = jnp.log(1.0 + jnp.exp2(jnp.abs(z) * (-LOG2E)))
            sp = jnp.maximum(z, 0.0) + l1p
            lb = z - sp
            if masked:
                sp = jnp.where(valid, sp, 0.0)
                lb = jnp.where(valid, lb, MASKED_LOG_WEIGHT)
            lb_ref[h] = lb
            sp_ref[h] = sp.astype(BF16)

    def weights(j, buf):
        lb_ref, sp_ref = buf
        ws, sums = [], []
        for h in range(n_heads):
            ts = jnp.dot(later_and_sum, sp_ref[h], preferred_element_type=F32)
            ws.append(jnp.exp(lb_ref[h] - ts[0:tq]).astype(BF16))
            sums.append(ts[tq:tq + ones_rows])
        for h in range(n_heads):
            p, hh = divmod(h, 2)
            vt = vt_ref[0, p, j, hh * HEAD_DIM:(hh + 1) * HEAD_DIM, :]
            pv = jnp.dot(vt, ws[h], preferred_element_type=F32)
            carry = carry_ref[h]
            scale = jnp.exp(-carry[0:1, :])
            rows = slice(h * HEAD_DIM, (h + 1) * HEAD_DIM)
            acc_ref[rows, :] += pv * scale
            carry_ref[h] = carry + sums[h]

    acc_ref[...] = jnp.zeros_like(acc_ref)
    carry_ref[...] = jnp.zeros_like(carry_ref)
    scores(i, buf_a, True)

    def earlier_keys_matter():
        return (jnp.max(jnp.exp(-carry_ref[:, 0, :])) > 0.0).astype(jnp.int32)

    def two_tiles(c):
        m, _ = c
        j = i - 1 - 2 * m
        scores(j, buf_b, False)
        weights(j + 1, buf_a)
        scores(j - 1, buf_a, False)
        weights(j, buf_b)
        return m + 1, earlier_keys_matter()

    _, alive = lax.while_loop(lambda c: (c[0] < i // 2) & (c[1] > 0), two_tiles, (jnp.int32(0), jnp.int32(1)))

    @pl.when((alive > 0) & (i % 2 == 1))
    def _():
        scores(0, buf_b, False)
        weights(1, buf_a)
        weights(0, buf_b)

    @pl.when((alive > 0) & (i % 2 == 0))
    def _():
        weights(0, buf_a)

    for p in range(n_pairs):
        o_ref[:, p * LANES:(p + 1) * LANES] = acc_ref[p * LANES:(p + 1) * LANES, :].T.astype(o_ref.dtype)


def _sb_attention(qt, k, vt):
    batch, width, seq = qt.shape
    n_pairs, nq, _, tq = vt.shape[1:]
    return pl.pallas_call(
        functools.partial(_sb_attn_body, tq=tq, n_pairs=n_pairs),
        grid=(batch, nq),
        in_specs=[
            pl.BlockSpec((1, width, tq), lambda b, i: (b, 0, i)),
            pl.BlockSpec((seq, width), lambda b, i: (b, 0)),
            pl.BlockSpec((1, n_pairs, nq, LANES, tq), lambda b, i: (b, 0, 0, 0, 0)),
        ],
        out_specs=pl.BlockSpec((tq, width), lambda b, i: (b * nq + i, 0)),
        out_shape=jax.ShapeDtypeStruct((batch * seq, width), BF16),
        scratch_shapes=[pltpu.VMEM((width, tq), F32), pltpu.VMEM((2 * n_pairs, 16, tq), F32),
                        pltpu.VMEM((2 * n_pairs, tq, tq), F32), pltpu.VMEM((2 * n_pairs, tq, tq), BF16),
                        pltpu.VMEM((2 * n_pairs, tq, tq), F32), pltpu.VMEM((2 * n_pairs, tq, tq), BF16)],
        compiler_params=_params("parallel", "arbitrary"),
        name="sb_attention",
    )(qt, k, vt)


def _ssm_tables(lam_re, lam_im, log_dt, b_re, b_im, c_re, c_im, n_chunks):
    g, p = lam_re.shape
    c = SSM_CHUNK
    lr = jnp.minimum(lam_re.astype(F32), SSM_MAX_RE)
    li = lam_im.astype(F32)
    dt = jnp.exp(log_dt.astype(F32))[:, None]

    def lbar_pow(k):
        mag = jnp.exp(k * (lr * dt))
        ang = k * (li * dt)
        return mag * jnp.cos(ang), mag * jnp.sin(ang)

    ab_re, ab_im = lbar_pow(1.0)
    den = lr * lr + li * li
    nr = ab_re - 1.0
    ni = ab_im
    f_re = (nr * lr + ni * li) / den
    f_im = (ni * lr - nr * li) / den
    br = b_re.astype(F32)
    bi = b_im.astype(F32)
    bb_re = f_re[..., None] * br - f_im[..., None] * bi
    bb_im = f_re[..., None] * bi + f_im[..., None] * br
    cr = c_re.astype(F32)
    ci = c_im.astype(F32)

    eye = jnp.eye(g, dtype=F32)

    def block_diag(x):
        _, a, b = x.shape
        return (x[:, :, None, :] * eye[:, None, :, None]).reshape(g * a, g * b)

    b_in = jnp.concatenate([block_diag(bb_re.transpose(0, 2, 1)),
                            block_diag(bb_im.transpose(0, 2, 1))], axis=1)
    c_out = jnp.concatenate([block_diag(cr.transpose(0, 2, 1)),
                             block_diag(-ci.transpose(0, 2, 1))], axis=0)
    lbar = jnp.stack([ab_re.reshape(-1), ab_im.reshape(-1)], axis=0)

    steps = []
    s = 1
    while s < n_chunks:
        a_re, a_im = lbar_pow(float(c * s))
        steps.append(jnp.stack([a_re.reshape(-1, MXU_DIM), a_im.reshape(-1, MXU_DIM)], axis=1))
        s *= 2
    step = jnp.stack(steps, axis=1)
    return b_in.astype(BF16), c_out.astype(BF16), lbar, step


def _ssm_body(u_ref, b_ref, c_ref, lbar_ref, step_ref, y_ref, z_ref, *, n_chunks, batch):
    s = pl.program_id(0)
    c = u_ref.shape[0]
    n_blk = z_ref.shape[0]
    half = n_blk // 2
    bw = z_ref.shape[2]

    @pl.when(s == 0)
    def _():
        z_ref[...] = jnp.zeros_like(z_ref)

    def advance(ut):
        for pb in range(half):
            lo, hi = pb * bw, (pb + 1) * bw
            re = z_ref[pb]
            im = z_ref[half + pb]
            a_re = lbar_ref[0:1, lo:hi]
            a_im = lbar_ref[1:2, lo:hi]
            x_re = jnp.dot(ut, b_ref[:, lo:hi], preferred_element_type=F32)
            x_im = jnp.dot(ut, b_ref[:, half * bw + lo:half * bw + hi], preferred_element_type=F32)
            z_ref[pb] = a_re * re - a_im * im + x_re
            z_ref[half + pb] = a_re * im + a_im * re + x_im

    @pl.when(s < c)
    def _():
        advance(u_ref[s])

    @pl.when(s == c)
    def _():
        chunk = lax.broadcasted_iota(jnp.int32, (n_chunks, bw), 0)

        def shifted(x, n):
            return jnp.where(chunk >= n, pltpu.roll(x, n, axis=0), 0.0)

        def scan_block(idx, carry):
            b = idx // half
            pb = idx % half
            rows = pl.ds(pl.multiple_of(b * n_chunks, n_chunks), n_chunks)
            re = z_ref[pb, rows, :]
            im = z_ref[half + pb, rows, :]
            n = 1
            k = 0
            while n < n_chunks:
                a_re = step_ref[pb, k, 0:1, :]
                a_im = step_ref[pb, k, 1:2, :]
                re_s = shifted(re, n)
                im_s = shifted(im, n)
                re, im = re + a_re * re_s - a_im * im_s, im + a_re * im_s + a_im * re_s
                n *= 2
                k += 1
            z_ref[pb, rows, :] = shifted(re, 1)
            z_ref[half + pb, rows, :] = shifted(im, 1)
            return carry

        lax.fori_loop(0, batch * half, scan_block, 0)

    @pl.when(s >= c)
    def _():
        advance(u_ref[s - c])
        y = jnp.dot(z_ref[0].astype(BF16), c_ref[0:bw, :], preferred_element_type=F32)
        for blk in range(1, n_blk):
            y += jnp.dot(z_ref[blk].astype(BF16), c_ref[blk * bw:(blk + 1) * bw, :],
                         preferred_element_type=F32)
        y_ref[0] = y.astype(y_ref.dtype)


def _ssm_scan(u, tables, *, batch, seq):
    b_in, c_out, lbar, step = tables
    width = u.shape[1]
    c = SSM_CHUNK
    n_chunks = seq // c
    rows = batch * n_chunks
    n_blk = b_in.shape[1] // MXU_DIM
    u3 = u.reshape(rows, c, width).transpose(1, 0, 2)
    whole = lambda shape: pl.BlockSpec(shape, lambda s: (0,) * len(shape), pipeline_mode=pl.Buffered(1))
    y3 = pl.pallas_call(
        functools.partial(_ssm_body, n_chunks=n_chunks, batch=batch),
        grid=(2 * c,),
        in_specs=[whole(u3.shape), whole(b_in.shape), whole(c_out.shape), whole(lbar.shape), whole(step.shape)],
        out_specs=pl.BlockSpec((1, rows, width), lambda s: (jnp.maximum(s - c, 0), 0, 0)),
        out_shape=jax.ShapeDtypeStruct((c, rows, width), BF16),
        scratch_shapes=[pltpu.VMEM((n_blk, rows, MXU_DIM), F32)],
        compiler_params=_params("arbitrary"),
        name="ssm_scan",
    )(u3, b_in, c_out, lbar, step)
    return y3.transpose(1, 0, 2).reshape(batch * seq, width)


def _gelu_tanh(x):
    return 0.5 * x * (1.0 + jnp.tanh(0.7978845608028654 * (x + 0.044715 * (x * x * x))))


def _merge_body(attn_ref, yssm_ref, sc_ref, halo_ref, gate_ref, h_ref, dskip_ref, wglu_ref, convw_ref,
                wa_ref, wb_ref, wc_ref, wout_ref, gn_ref, hout_ref, xn_ref, *, tiles_per_seq, w_ssm, w_conv):
    i = pl.program_id(0)
    tm = h_ref.shape[0]
    d = h_ref.shape[1]

    u = sc_ref[:, 0:w_ssm].astype(F32)
    yb = _gelu_tanh(yssm_ref[...].astype(F32) + dskip_ref[...] * u)
    yb = yb * jax.nn.sigmoid(jnp.dot(yb.astype(BF16), wglu_ref[...], preferred_element_type=F32))
    y_b = jnp.dot(yb.astype(BF16), wb_ref[...], preferred_element_type=F32)

    o_h, o_b, o_c = w_ssm, w_ssm + w_conv, w_ssm + 2 * w_conv
    uc = sc_ref[:, o_c:o_c + w_conv].astype(F32) * sc_ref[:, o_h:o_h + w_conv].astype(F32)
    halo = halo_ref[:, o_c:o_c + w_conv].astype(F32) * halo_ref[:, o_h:o_h + w_conv].astype(F32)
    halo = jnp.where(i % tiles_per_seq == 0, 0.0, halo)
    row = lax.broadcasted_iota(jnp.int32, (tm, w_conv), 0)
    prev1 = jnp.where(row == 0, halo[SUBLANES - 1:SUBLANES, :], pltpu.roll(uc, 1, axis=0))
    prev2 = jnp.where(row == 0, halo[SUBLANES - 2:SUBLANES - 1, :],
                      jnp.where(row == 1, halo[SUBLANES - 1:SUBLANES, :], pltpu.roll(uc, 2, axis=0)))
    conv = convw_ref[0:1, :] * prev2 + convw_ref[1:2, :] * prev1 + convw_ref[2:3, :] * uc
    yc = sc_ref[:, o_b:o_b + w_conv].astype(F32) * conv
    y_c = jnp.dot(yc.astype(BF16), wc_ref[...], preferred_element_type=F32)

    y_a = jnp.dot(attn_ref[...], wa_ref[...], preferred_element_type=F32)

    merged = (gate_ref[:, 0:d].astype(F32) * y_a + gate_ref[:, d:2 * d].astype(F32) * y_b
              + gate_ref[:, 2 * d:3 * d].astype(F32) * y_c)
    hn = h_ref[...] + jnp.dot(merged.astype(BF16), wout_ref[...], preferred_element_type=F32)
    hout_ref[...] = hn
    xn_ref[...] = _rms(hn, gn_ref[...]).astype(xn_ref.dtype)


def _merge(attn, yssm, sc, gates, h, d_skip, w_glu, conv_w, w_a, w_b, w_c, w_out, g_ffn, *, seq):
    t, d = h.shape
    tm = TM_MERGE
    w_ssm = yssm.shape[1]
    w_conv = conv_w.shape[1]
    w_attn = attn.shape[1]
    n_sc = sc.shape[1]
    body = functools.partial(_merge_body, tiles_per_seq=seq // tm, w_ssm=w_ssm, w_conv=w_conv)
    full = lambda shape: pl.BlockSpec(shape, lambda i: (0,) * len(shape))
    halo_blocks = tm // SUBLANES
    return pl.pallas_call(
        body,
        grid=(t // tm,),
        in_specs=[
            pl.BlockSpec((tm, w_attn), lambda i: (i, 0)),
            pl.BlockSpec((tm, w_ssm), lambda i: (i, 0)),
            pl.BlockSpec((tm, n_sc), lambda i: (i, 0)),
            pl.BlockSpec((SUBLANES, n_sc), lambda i: (jnp.maximum(i * halo_blocks - 1, 0), 0)),
            pl.BlockSpec((tm, N_BRANCH * d), lambda i: (i, 0)),
            pl.BlockSpec((tm, d), lambda i: (i, 0)),
            full((1, w_ssm)), full((w_ssm, w_ssm)), full((CONV_K, w_conv)),
            full((w_attn, d)), full((w_ssm, d)), full((w_conv, d)), full((d, d)), full((1, d)),
        ],
        out_specs=[pl.BlockSpec((tm, d), lambda i: (i, 0)), pl.BlockSpec((tm, d), lambda i: (i, 0))],
        out_shape=[jax.ShapeDtypeStruct((t, d), F32), jax.ShapeDtypeStruct((t, d), BF16)],
        compiler_params=_params("parallel"),
        name="merge",
    )(attn, yssm, sc, sc, gates, h, d_skip.reshape(1, w_ssm).astype(F32), w_glu, conv_w.astype(F32),
      w_a, w_b, w_c, w_out, g_ffn.reshape(1, d))


def _swiglu_tile(x, wg_ref, wu_ref, wd_ref):
    gate = jnp.dot(x, wg_ref[...], preferred_element_type=F32)
    up = jnp.dot(x, wu_ref[...], preferred_element_type=F32)
    act = (gate * jax.nn.sigmoid(gate) * up).astype(BF16)
    return jnp.dot(act, wd_ref[...], preferred_element_type=F32)


def _dense_ffn_body(x_ref, h_ref, wg_ref, wu_ref, wd_ref, o_ref, acc_ref):
    j = pl.program_id(1)
    part = _swiglu_tile(x_ref[...], wg_ref, wu_ref, wd_ref)

    @pl.when(j == 0)
    def _():
        acc_ref[...] = h_ref[...] + part

    @pl.when(j > 0)
    def _():
        acc_ref[...] += part

    @pl.when(j == pl.num_programs(1) - 1)
    def _():
        o_ref[...] = acc_ref[...]


def _dense_ffn(xn, h, w_gate, w_up, w_down):
    t, d = h.shape
    f = w_gate.shape[1]
    tm, tf = TM_FFN, TF_FFN
    return pl.pallas_call(
        _dense_ffn_body,
        grid=(t // tm, f // tf),
        in_specs=[
            pl.BlockSpec((tm, d), lambda i, j: (i, 0)),
            pl.BlockSpec((tm, d), lambda i, j: (i, 0)),
            pl.BlockSpec((d, tf), lambda i, j: (0, j)),
            pl.BlockSpec((d, tf), lambda i, j: (0, j)),
            pl.BlockSpec((tf, d), lambda i, j: (j, 0)),
        ],
        out_specs=pl.BlockSpec((tm, d), lambda i, j: (i, 0)),
        out_shape=jax.ShapeDtypeStruct((t, d), F32),
        scratch_shapes=[pltpu.VMEM((tm, d), F32)],
        compiler_params=_params("parallel", "arbitrary"),
        name="dense_ffn",
    )(xn, h, w_gate, w_up, w_down)


def _router_body(h_ref, g_ref, wr_ref, idx_ref, wgt_ref):
    xn = _rms(h_ref[...], g_ref[...])
    logits = lax.dot_general(wr_ref[...], xn, (((1,), (1,)), ((), ())), preferred_element_type=F32,
                             precision=lax.Precision.HIGHEST)
    n_e = logits.shape[0]
    e_idx = lax.broadcasted_iota(jnp.int32, logits.shape, 0)
    m1 = jnp.max(logits, axis=0, keepdims=True)
    i1 = jnp.min(jnp.where(logits == m1, e_idx, n_e), axis=0, keepdims=True)
    rest = jnp.where(e_idx == i1, -jnp.inf, logits)
    m2 = jnp.max(rest, axis=0, keepdims=True)
    i2 = jnp.min(jnp.where(rest == m2, e_idx, n_e), axis=0, keepdims=True)
    e2 = jnp.exp(m2 - m1)
    w1 = 1.0 / (1.0 + e2)
    idx_ref[0] = jnp.concatenate([i1, i2], axis=0)
    wgt_ref[0] = jnp.concatenate([w1, e2 * w1], axis=0)


def _router(h, g_ffn, w_router):
    t, d = h.shape
    n_e = w_router.shape[1]
    tm = TM_ROUTE
    nt = t // tm
    idx, wgt = pl.pallas_call(
        _router_body,
        grid=(nt,),
        in_specs=[
            pl.BlockSpec((tm, d), lambda i: (i, 0)),
            pl.BlockSpec((1, d), lambda i: (0, 0)),
            pl.BlockSpec((n_e, d), lambda i: (0, 0)),
        ],
        out_specs=[pl.BlockSpec((1, TOP_K, tm), lambda i: (i, 0, 0)),
                   pl.BlockSpec((1, TOP_K, tm), lambda i: (i, 0, 0))],
        out_shape=[jax.ShapeDtypeStruct((nt, TOP_K, tm), jnp.int32),
                   jax.ShapeDtypeStruct((nt, TOP_K, tm), F32)],
        compiler_params=_params("parallel"),
        name="router",
    )(h, g_ffn.reshape(1, d), w_router.T.astype(F32))
    idx = idx.transpose(0, 2, 1).reshape(t, TOP_K)
    wgt = wgt.transpose(0, 2, 1).reshape(t, TOP_K)
    return idx, wgt


def _scatter_rows_body(zero_start_ref, s1_ref, s2_ref, h_ref, xs_ref, zeros_ref, sem):
    tm = h_ref.shape[0]

    @pl.when(pl.program_id(0) == 0)
    def _():
        zeros_ref[...] = jnp.zeros_like(zeros_ref)

        def fill(k, c):
            dst = xs_ref.at[pl.ds(pl.multiple_of(zero_start_ref[k], SUBLANES), zeros_ref.shape[0])]
            cp = pltpu.make_async_copy(zeros_ref, dst, sem)
            cp.start()
            cp.wait()
            return c

        lax.fori_loop(0, zero_start_ref.shape[0], fill, 0)

    def issue(r, c):
        src = h_ref.at[pl.ds(r, 1)]
        pltpu.make_async_copy(src, xs_ref.at[pl.ds(s1_ref[r], 1)], sem).start(priority=0)
        pltpu.make_async_copy(src, xs_ref.at[pl.ds(s2_ref[r], 1)], sem).start(priority=1)
        return c

    lax.fori_loop(0, tm, issue, 0, unroll=8)
    pltpu.make_async_copy(h_ref, xs_ref.at[pl.ds(0, tm)], sem).wait()
    pltpu.make_async_copy(h_ref, xs_ref.at[pl.ds(0, tm)], sem).wait()


def _scatter_rows(h, slot1, slot2, zero_start, cap, zero_rows):
    t, d = h.shape
    tm = TM_SCATTER
    smem_tile = pl.BlockSpec((tm,), lambda i, zs: (i,), memory_space=pltpu.SMEM)
    grid_spec = pltpu.PrefetchScalarGridSpec(
        num_scalar_prefetch=1,
        grid=(t // tm,),
        in_specs=[smem_tile, smem_tile, pl.BlockSpec((tm, d), lambda i, zs: (i, 0))],
        out_specs=pl.BlockSpec(memory_space=pl.ANY),
        scratch_shapes=[pltpu.VMEM((zero_rows, d), h.dtype), pltpu.SemaphoreType.DMA(())],
    )
    return pl.pallas_call(
        _scatter_rows_body,
        grid_spec=grid_spec,
        out_shape=jax.ShapeDtypeStruct((cap, d), h.dtype),
        compiler_params=_params("arbitrary"),
        name="scatter_rows",
    )(zero_start, slot1, slot2, h)


def _moe_ffn_body(tile_expert_ref, tile_valid_ref, x_ref, g_ref, wg_ref, wu_ref, wd_ref, o_ref, xn_ref, acc_ref):
    i = pl.program_id(0)
    j = pl.program_id(1)
    valid = tile_valid_ref[i] > 0

    @pl.when(valid)
    def _():
        @pl.when(j == 0)
        def _():
            xn_ref[...] = _rms(x_ref[...], g_ref[...]).astype(xn_ref.dtype)

        part = _swiglu_tile(xn_ref[...], wg_ref.at[0], wu_ref.at[0], wd_ref.at[0])

        @pl.when(j == 0)
        def _():
            acc_ref[...] = part

        @pl.when(j > 0)
        def _():
            acc_ref[...] += part

    @pl.when(j == pl.num_programs(1) - 1)
    def _():
        o_ref[...] = jnp.where(valid, acc_ref[...], 0.0).astype(o_ref.dtype)


def _moe_ffn(xs, g_ffn, tile_expert, tile_valid, w_gate, w_up, w_down):
    cap, d = xs.shape
    f = w_gate.shape[2]
    tm, tf = TM_MOE, TF_FFN
    grid_spec = pltpu.PrefetchScalarGridSpec(
        num_scalar_prefetch=2,
        grid=(cap // tm, f // tf),
        in_specs=[
            pl.BlockSpec((tm, d), lambda i, j, te, tv: (i, 0)),
            pl.BlockSpec((1, d), lambda i, j, te, tv: (0, 0)),
            pl.BlockSpec((1, d, tf), lambda i, j, te, tv: (te[i], 0, j)),
            pl.BlockSpec((1, d, tf), lambda i, j, te, tv: (te[i], 0, j)),
            pl.BlockSpec((1, tf, d), lambda i, j, te, tv: (te[i], j, 0)),
        ],
        out_specs=pl.BlockSpec((tm, d), lambda i, j, te, tv: (i, 0)),
        scratch_shapes=[pltpu.VMEM((tm, d), BF16), pltpu.VMEM((tm, d), F32)],
    )
    return pl.pallas_call(
        _moe_ffn_body,
        grid_spec=grid_spec,
        out_shape=jax.ShapeDtypeStruct((cap, d), F32),
        compiler_params=_params("parallel", "arbitrary"),
        name="moe_ffn",
    )(tile_expert, tile_valid, xs, g_ffn.reshape(1, d), w_gate, w_up, w_down)


def _moe_plan(idx, n_experts, tm):
    t = idx.shape[0]
    pairs = t * TOP_K
    cap = pairs + n_experts * tm
    e_flat = idx.reshape(pairs)
    onehot = (e_flat[:, None] == jnp.arange(n_experts, dtype=jnp.int32)[None, :]).astype(jnp.int32)
    rank = jnp.sum(onehot * (jnp.cumsum(onehot, axis=0) - onehot), axis=1)
    counts = jnp.sum(onehot, axis=0)
    padded = ((counts + tm - 1) // tm) * tm
    ends = jnp.cumsum(padded)
    starts = ends - padded
    slot = jnp.sum(onehot * starts[None, :], axis=1) + rank
    tile_start = jnp.arange(cap // tm, dtype=jnp.int32) * tm
    tile_expert = jnp.sum((tile_start[:, None] >= ends[None, :]).astype(jnp.int32), axis=1)
    tile_valid = (tile_expert < n_experts).astype(jnp.int32)
    last_used = jnp.max(jnp.where(counts > 0, jnp.arange(n_experts, dtype=jnp.int32), 0))
    tile_expert = jnp.minimum(tile_expert, last_used).astype(jnp.int32)
    zero_rows = tm + SUBLANES
    tail = ends[-1] + jnp.arange(n_experts, dtype=jnp.int32) * tm
    first = jnp.concatenate([starts + counts, tail]) // SUBLANES * SUBLANES
    zero_start = jnp.minimum(first, cap - zero_rows).astype(jnp.int32)
    return slot.reshape(t, TOP_K).astype(jnp.int32), cap, tile_expert, tile_valid, zero_start, zero_rows


def _combine_body(s1_ref, s2_ref, h_ref, wgt_ref, g_ref, ys_ref, o_ref, y1_ref, y2_ref, sem, *, apply_norm):
    tm = h_ref.shape[0]

    def issue(r, c):
        pltpu.make_async_copy(ys_ref.at[pl.ds(s1_ref[r], 1)], y1_ref.at[pl.ds(r, 1)], sem).start(priority=0)
        pltpu.make_async_copy(ys_ref.at[pl.ds(s2_ref[r], 1)], y2_ref.at[pl.ds(r, 1)], sem).start(priority=1)
        return c

    lax.fori_loop(0, tm, issue, 0, unroll=8)
    pltpu.make_async_copy(ys_ref.at[pl.ds(0, tm)], y1_ref, sem).wait()
    pltpu.make_async_copy(ys_ref.at[pl.ds(0, tm)], y2_ref, sem).wait()
    w = wgt_ref[...]
    hn = h_ref[...] + w[:, 0:1] * y1_ref[...] + w[:, 1:2] * y2_ref[...]
    o_ref[...] = _rms(hn, g_ref[...]) if apply_norm else hn


def _combine(h, ys, slot, wgt, g, *, apply_norm):
    t, d = h.shape
    tm = TM_COMBINE
    smem_tile = pl.BlockSpec((tm,), lambda i: (i,), memory_space=pltpu.SMEM)
    row = pl.BlockSpec((tm, d), lambda i: (i, 0))
    return pl.pallas_call(
        functools.partial(_combine_body, apply_norm=apply_norm),
        grid=(t // tm,),
        in_specs=[smem_tile, smem_tile, row,
                  pl.BlockSpec((tm, TOP_K), lambda i: (i, 0)),
                  pl.BlockSpec((1, d), lambda i: (0, 0)),
                  pl.BlockSpec(memory_space=pl.ANY)],
        out_specs=row,
        out_shape=jax.ShapeDtypeStruct((t, d), F32),
        scratch_shapes=[pltpu.VMEM((tm, d), F32), pltpu.VMEM((tm, d), F32), pltpu.SemaphoreType.DMA(())],
        compiler_params=_params("arbitrary"),
        name="combine",
    )(slot[:, 0], slot[:, 1], h, wgt, g.reshape(1, d), ys)


def _plain_norm_body(h_ref, g_ref, o_ref):
    o_ref[...] = _rms(h_ref[...], g_ref[...])


def _plain_norm(h, g):
    t, d = h.shape
    tm = TM_NORM
    row = pl.BlockSpec((tm, d), lambda i: (i, 0))
    return pl.pallas_call(
        _plain_norm_body,
        grid=(t // tm,),
        in_specs=[row, pl.BlockSpec((1, d), lambda i: (0, 0))],
        out_specs=row,
        out_shape=jax.ShapeDtypeStruct((t, d), F32),
        compiler_params=_params("parallel"),
        name="plain_norm",
    )(h, g.reshape(1, d))


def kernel(x, norm_mix, w_in, ssm_lambda_re, ssm_lambda_im, ssm_log_dt, ssm_b_re, ssm_b_im, ssm_c_re,
           ssm_c_im, ssm_d, ssm_w_glu, conv_w, w_br_a, w_br_b, w_br_c, w_out, norm_ffn, dense_w_gate,
           dense_w_up, dense_w_down, moe_w_router, moe_w_gate, moe_w_up, moe_w_down, final_norm):
    batch, seq, d = x.shape
    depth = w_in.shape[0]
    t = batch * seq
    w_attn = w_br_a.shape[1]
    w_ssm = w_br_b.shape[1]
    w_conv = w_br_c.shape[1]
    n_sc = w_ssm + 3 * w_conv
    n_gate = N_BRANCH * d
    n_experts = moe_w_router.shape[-1]

    h = x.reshape(t, d)
    normed = False
    for i in range(depth):
        qt, k, vt, sc, gates = _inproj(h, norm_mix[i], w_in[i].astype(BF16), batch=batch, seq=seq,
                                       width=w_attn, n_sc=n_sc, n_gate=n_gate)
        attn = _sb_attention(qt, k, vt)
        tables = _ssm_tables(ssm_lambda_re[i], ssm_lambda_im[i], ssm_log_dt[i], ssm_b_re[i], ssm_b_im[i],
                             ssm_c_re[i], ssm_c_im[i], seq // SSM_CHUNK)
        yssm = _ssm_scan(sc[:, :w_ssm], tables, batch=batch, seq=seq)
        h, xn = _merge(attn, yssm, sc, gates, h, ssm_d[i], ssm_w_glu[i].astype(BF16), conv_w[i],
                       w_br_a[i].astype(BF16), w_br_b[i].astype(BF16), w_br_c[i].astype(BF16),
                       w_out[i].astype(BF16), norm_ffn[i], seq=seq)
        j = i // 2
        if i % 2 == 0:
            h = _dense_ffn(xn, h, dense_w_gate[j].astype(BF16), dense_w_up[j].astype(BF16),
                           dense_w_down[j].astype(BF16))
        else:
            idx, wgt = _router(h, norm_ffn[i], moe_w_router[j])
            slot, cap, tile_expert, tile_valid, zero_start, zero_rows = _moe_plan(idx, n_experts, TM_MOE)
            xs = _scatter_rows(h, slot[:, 0], slot[:, 1], zero_start, cap, zero_rows)
            ys = _moe_ffn(xs, norm_ffn[i], tile_expert, tile_valid, moe_w_gate[j].astype(BF16),
                          moe_w_up[j].astype(BF16), moe_w_down[j].astype(BF16))
            last = i == depth - 1
            h = _combine(h, ys, slot, wgt, final_norm if last else norm_ffn[i], apply_norm=last)
            normed = last
    out = h if normed else _plain_norm(h, final_norm)
    return out.reshape(batch, seq, d)
```

```python
import functools

import jax
import jax.numpy as jnp
from jax import lax
from jax.experimental import pallas as pl
from jax.experimental.pallas import tpu as pltpu

F32 = jnp.float32
BF16 = jnp.bfloat16

EPS = 1e-6
HEAD_DIM = 64
SSM_CHUNK = 16
SSM_MAX_RE = -1e-4
CONV_K = 3
N_BRANCH = 3
TOP_K = 2
LOG2E = 1.4426950408889634
MASKED_LOG_WEIGHT = -1e30

LANES = 128
SUBLANES = 8
MXU_DIM = 256
VMEM_LIMIT_BYTES = 56 * 1024 * 1024

TM_PROJ = 512
TQ_ATTN = MXU_DIM
TM_MERGE = 512
TM_FFN = 512
TF_FFN = 1792
TM_MOE = 512
TM_ROUTE = 512
TM_SCATTER = 256
TM_COMBINE = 256
TM_NORM = 512


def _params(*sem):
    return pltpu.CompilerParams(dimension_semantics=sem, vmem_limit_bytes=VMEM_LIMIT_BYTES)


def _rms(x, g):
    ms = jnp.mean(x * x, axis=-1, keepdims=True)
    return x * lax.rsqrt(ms + EPS) * g


def _inproj_body(x_ref, g_ref, w_ref, qt_ref, k_ref, vt_ref, sc_ref, gate_ref, *, width, n_sc, n_gate, q_scale):
    xn = _rms(x_ref[...], g_ref[...]).astype(BF16)
    tm = xn.shape[0]
    tq = vt_ref.shape[-1]
    chunk = 512

    def proj(c0, n):
        return jnp.dot(xn, w_ref[:, c0:c0 + n], preferred_element_type=F32)

    qt_ref[0] = (proj(0, width) * q_scale).T.astype(BF16)
    k_ref[...] = proj(width, width).astype(BF16)
    v = proj(2 * width, width)
    for kt in range(tm // tq):
        for p in range(width // LANES):
            vt_ref[0, p, kt] = v[kt * tq:(kt + 1) * tq, p * LANES:(p + 1) * LANES].T.astype(BF16)
    for c0 in range(0, n_sc, chunk):
        sc_ref[:, c0:c0 + chunk] = proj(3 * width + c0, chunk).astype(BF16)
    for c0 in range(0, n_gate, chunk):
        gate_ref[:, c0:c0 + chunk] = jax.nn.sigmoid(proj(3 * width + n_sc + c0, chunk)).astype(BF16)


def _inproj(h, g, w_bf16, *, batch, seq, width, n_sc, n_gate):
    t, d = h.shape
    tm = TM_PROJ
    tq = TQ_ATTN
    n_pairs = width // LANES
    tiles_per_seq = seq // tm
    body = functools.partial(_inproj_body, width=width, n_sc=n_sc, n_gate=n_gate, q_scale=HEAD_DIM ** -0.5)
    return pl.pallas_call(
        body,
        grid=(t // tm,),
        in_specs=[
            pl.BlockSpec((tm, d), lambda i: (i, 0)),
            pl.BlockSpec((1, d), lambda i: (0, 0)),
            pl.BlockSpec((d, 3 * width + n_sc + n_gate), lambda i: (0, 0), pipeline_mode=pl.Buffered(1)),
        ],
        out_specs=[
            pl.BlockSpec((1, width, tm), lambda i: (i // tiles_per_seq, 0, i % tiles_per_seq)),
            pl.BlockSpec((tm, width), lambda i: (i, 0)),
            pl.BlockSpec((1, n_pairs, tm // tq, LANES, tq),
                         lambda i: (i // tiles_per_seq, 0, i % tiles_per_seq, 0, 0)),
            pl.BlockSpec((tm, n_sc), lambda i: (i, 0)),
            pl.BlockSpec((tm, n_gate), lambda i: (i, 0)),
        ],
        out_shape=[
            jax.ShapeDtypeStruct((batch, width, seq), BF16),
            jax.ShapeDtypeStruct((t, width), BF16),
            jax.ShapeDtypeStruct((batch, n_pairs, seq // tq, LANES, tq), BF16),
            jax.ShapeDtypeStruct((t, n_sc), BF16),
            jax.ShapeDtypeStruct((t, n_gate), BF16),
        ],
        compiler_params=_params("parallel"),
        name="inproj",
    )(h, g.reshape(1, d), w_bf16)


def _sb_attn_body(qt_ref, k_ref, vt_ref, o_ref, acc_ref, carry_ref, lb_a_ref, sp_a_ref, sum_a_ref,
                  lb_b_ref, sp_b_ref, sum_b_ref, pending_ref, *, tq, n_pairs):
    i = pl.program_id(1)
    buf_a = (lb_a_ref, sp_a_ref, sum_a_ref)
    buf_b = (lb_b_ref, sp_b_ref, sum_b_ref)
    n_heads = 2 * n_pairs
    ones_rows = carry_ref.shape[1]
    key = lax.broadcasted_iota(jnp.int32, (tq, tq), 0)
    qry = lax.broadcasted_iota(jnp.int32, (tq, tq), 1)
    valid = key < qry
    later_key = jnp.where(qry > key, 1.0, 0.0).astype(BF16)
    ones = jnp.ones((ones_rows, tq), BF16)

    feat = lax.broadcasted_iota(jnp.int32, (LANES, tq), 0)
    q_heads = []
    for p in range(n_pairs):
        qp = qt_ref[0, p * LANES:(p + 1) * LANES, :]
        zero = jnp.zeros_like(qp)
        q_heads.append(jnp.where(feat < HEAD_DIM, qp, zero))
        q_heads.append(jnp.where(feat >= HEAD_DIM, qp, zero))

    def scores(j, buf, masked):
        lb_ref, sp_ref, sum_ref = buf
        start = pl.multiple_of(j * tq, tq)
        for h in range(n_heads):
            p = h // 2
            kblk = k_ref[pl.ds(start, tq), p * LANES:(p + 1) * LANES]
            z = jnp.dot(kblk, q_heads[h], preferred_element_type=F32)
            l1p = jnp.log(1.0 + jnp.exp2(jnp.abs(z) * (-LOG2E)))
            sp = jnp.maximum(z, 0.0) + l1p
            lb = z - sp
            if masked:
                sp = jnp.where(valid, sp, 0.0)
                lb = jnp.where(valid, lb, MASKED_LOG_WEIGHT)
            spb = sp.astype(BF16)
            lb_ref[h] = lb
            sp_ref[h] = spb
            sum_ref[h] = jnp.dot(ones, spb, preferred_element_type=F32)

    def weights(j, buf):
        lb_ref, sp_ref, sum_ref = buf
        ws = []
        for h in range(n_heads):
            ts = jnp.dot(later_key, sp_ref[h], preferred_element_type=F32)
            ws.append(jnp.exp(lb_ref[h] - ts).astype(BF16))
        for h in range(n_heads):
            p, hh = divmod(h, 2)
            vt = vt_ref[0, p, j, hh * HEAD_DIM:(hh + 1) * HEAD_DIM, :]
            pv = jnp.dot(vt, ws[h], preferred_element_type=F32)
            carry = carry_ref[h]
            scale = jnp.exp(-carry[0:1, :])
            rows = slice(h * HEAD_DIM, (h + 1) * HEAD_DIM)
            acc_ref[rows, :] += pv * scale
            carry_ref[h] = carry + sum_ref[h]

    def earlier_keys_matter(buf):
        return (jnp.max(jnp.exp(-(carry_ref[:, 0, :] + buf[2][:, 0, :]))) > 0.0).astype(jnp.int32)

    def step(j, new, new_id, old):
        scores(j, new, False)
        weights(j + 1, old)
        pending_ref[1] = new_id
        pending_ref[2] = j
        pending_ref[0] = jnp.where(j > 0, earlier_keys_matter(new), 0)

    acc_ref[...] = jnp.zeros_like(acc_ref)
    carry_ref[...] = jnp.zeros_like(carry_ref)
    scores(i, buf_a, True)
    pending_ref[1] = 0
    pending_ref[2] = i
    pending_ref[0] = jnp.where(i > 0, earlier_keys_matter(buf_a), 0)

    def two_tiles(alive):
        j = pending_ref[2] - 1
        step(j, buf_b, 1, buf_a)

        @pl.when(pending_ref[0] > 0)
        def _():
            step(j - 1, buf_a, 0, buf_b)

        return pending_ref[0]

    lax.while_loop(lambda alive: alive > 0, two_tiles, pending_ref[0])

    @pl.when(pending_ref[1] == 0)
    def _():
        weights(pending_ref[2], buf_a)

    @pl.when(pending_ref[1] == 1)
    def _():
        weights(pending_ref[2], buf_b)

    for p in range(n_pairs):
        o_ref[:, p * LANES:(p + 1) * LANES] = acc_ref[p * LANES:(p + 1) * LANES, :].T.astype(o_ref.dtype)


def _sb_attention(qt, k, vt):
    batch, width, seq = qt.shape
    n_pairs, nq, _, tq = vt.shape[1:]
    n_heads = 2 * n_pairs
    sums = pltpu.VMEM((n_heads, 16, tq), F32)
    return pl.pallas_call(
        functools.partial(_sb_attn_body, tq=tq, n_pairs=n_pairs),
        grid=(batch, nq),
        in_specs=[
            pl.BlockSpec((1, width, tq), lambda b, i: (b, 0, i)),
            pl.BlockSpec((seq, width), lambda b, i: (b, 0)),
            pl.BlockSpec((1, n_pairs, nq, LANES, tq), lambda b, i: (b, 0, 0, 0, 0)),
        ],
        out_specs=pl.BlockSpec((tq, width), lambda b, i: (b * nq + i, 0)),
        out_shape=jax.ShapeDtypeStruct((batch * seq, width), BF16),
        scratch_shapes=[pltpu.VMEM((width, tq), F32), sums,
                        pltpu.VMEM((n_heads, tq, tq), F32), pltpu.VMEM((n_heads, tq, tq), BF16), sums,
                        pltpu.VMEM((n_heads, tq, tq), F32), pltpu.VMEM((n_heads, tq, tq), BF16), sums,
                        pltpu.SMEM((3,), jnp.int32)],
        compiler_params=_params("parallel", "arbitrary"),
        name="sb_attention",
    )(qt, k, vt)


def _ssm_tables(lam_re, lam_im, log_dt, b_re, b_im, c_re, c_im, n_chunks):
    g, p = lam_re.shape
    c = SSM_CHUNK
    lr = jnp.minimum(lam_re.astype(F32), SSM_MAX_RE)
    li = lam_im.astype(F32)
    dt = jnp.exp(log_dt.astype(F32))[:, None]

    def lbar_pow(k):
        mag = jnp.exp(k * (lr * dt))
        ang = k * (li * dt)
        return mag * jnp.cos(ang), mag * jnp.sin(ang)

    ab_re, ab_im = lbar_pow(1.0)
    den = lr * lr + li * li
    nr = ab_re - 1.0
    ni = ab_im
    f_re = (nr * lr + ni * li) / den
    f_im = (ni * lr - nr * li) / den
    br = b_re.astype(F32)
    bi = b_im.astype(F32)
    bb_re = f_re[..., None] * br - f_im[..., None] * bi
    bb_im = f_re[..., None] * bi + f_im[..., None] * br
    cr = c_re.astype(F32)
    ci = c_im.astype(F32)

    eye = jnp.eye(g, dtype=F32)

    def block_diag(x):
        _, a, b = x.shape
        return (x[:, :, None, :] * eye[:, None, :, None]).reshape(g * a, g * b)

    b_in = jnp.concatenate([block_diag(bb_re.transpose(0, 2, 1)),
                            block_diag(bb_im.transpose(0, 2, 1))], axis=1)
    c_out = jnp.concatenate([block_diag(cr.transpose(0, 2, 1)),
                             block_diag(-ci.transpose(0, 2, 1))], axis=0)
    lbar = jnp.stack([ab_re.reshape(-1), ab_im.reshape(-1)], axis=0)

    steps = []
    s = 1
    while s < n_chunks:
        a_re, a_im = lbar_pow(float(c * s))
        steps.append(jnp.stack([a_re.reshape(-1, MXU_DIM), a_im.reshape(-1, MXU_DIM)], axis=1))
        s *= 2
    step = jnp.stack(steps, axis=1)
    return b_in.astype(BF16), c_out.astype(BF16), lbar, step


def _ssm_body(u_ref, b_ref, c_ref, lbar_ref, step_ref, y_ref, z_ref, *, n_chunks, batch):
    s = pl.program_id(0)
    c = u_ref.shape[0]
    n_blk = z_ref.shape[0]
    half = n_blk // 2
    bw = z_ref.shape[2]

    @pl.when(s == 0)
    def _():
        z_ref[...] = jnp.zeros_like(z_ref)

    def advance(ut):
        for pb in range(half):
            lo, hi = pb * bw, (pb + 1) * bw
            re = z_ref[pb]
            im = z_ref[half + pb]
            a_re = lbar_ref[0:1, lo:hi]
            a_im = lbar_ref[1:2, lo:hi]
            x_re = jnp.dot(ut, b_ref[:, lo:hi], preferred_element_type=F32)
            x_im = jnp.dot(ut, b_ref[:, half * bw + lo:half * bw + hi], preferred_element_type=F32)
            z_ref[pb] = a_re * re - a_im * im + x_re
            z_ref[half + pb] = a_re * im + a_im * re + x_im

    @pl.when(s < c)
    def _():
        advance(u_ref[s])

    @pl.when(s == c)
    def _():
        chunk = lax.broadcasted_iota(jnp.int32, (n_chunks, bw), 0)

        def shifted(x, n):
            return jnp.where(chunk >= n, pltpu.roll(x, n, axis=0), 0.0)

        def scan_block(idx, carry):
            b = idx // half
            pb = idx % half
            rows = pl.ds(pl.multiple_of(b * n_chunks, n_chunks), n_chunks)
            re = z_ref[pb, rows, :]
            im = z_ref[half + pb, rows, :]
            n = 1
            k = 0
            while n < n_chunks:
                a_re = step_ref[pb, k, 0:1, :]
                a_im = step_ref[pb, k, 1:2, :]
                re_s = shifted(re, n)
                im_s = shifted(im, n)
                re, im = re + a_re * re_s - a_im * im_s, im + a_re * im_s + a_im * re_s
                n *= 2
                k += 1
            z_ref[pb, rows, :] = shifted(re, 1)
            z_ref[half + pb, rows, :] = shifted(im, 1)
            return carry

        lax.fori_loop(0, batch * half, scan_block, 0)

    @pl.when(s >= c)
    def _():
        advance(u_ref[s - c])
        y = jnp.dot(z_ref[0].astype(BF16), c_ref[0:bw, :], preferred_element_type=F32)
        for blk in range(1, n_blk):
            y += jnp.dot(z_ref[blk].astype(BF16), c_ref[blk * bw:(blk + 1) * bw, :],
                         preferred_element_type=F32)
        y_ref[0] = y.astype(y_ref.dtype)


def _ssm_scan(u, tables, *, batch, seq):
    b_in, c_out, lbar, step = tables
    width = u.shape[1]
    c = SSM_CHUNK
    n_chunks = seq // c
    rows = batch * n_chunks
    n_blk = b_in.shape[1] // MXU_DIM
    u3 = u.reshape(rows, c, width).transpose(1, 0, 2)
    whole = lambda shape: pl.BlockSpec(shape, lambda s: (0,) * len(shape), pipeline_mode=pl.Buffered(1))
    y3 = pl.pallas_call(
        functools.partial(_ssm_body, n_chunks=n_chunks, batch=batch),
        grid=(2 * c,),
        in_specs=[whole(u3.shape), whole(b_in.shape), whole(c_out.shape), whole(lbar.shape), whole(step.shape)],
        out_specs=pl.BlockSpec((1, rows, width), lambda s: (jnp.maximum(s - c, 0), 0, 0)),
        out_shape=jax.ShapeDtypeStruct((c, rows, width), BF16),
        scratch_shapes=[pltpu.VMEM((n_blk, rows, MXU_DIM), F32)],
        compiler_params=_params("arbitrary"),
        name="ssm_scan",
    )(u3, b_in, c_out, lbar, step)
    return y3.transpose(1, 0, 2).reshape(batch * seq, width)


def _gelu_tanh(x):
    return 0.5 * x * (1.0 + jnp.tanh(0.7978845608028654 * (x + 0.044715 * (x * x * x))))


def _merge_body(attn_ref, yssm_ref, sc_ref, halo_ref, gate_ref, h_ref, dskip_ref, wglu_ref, convw_ref,
                wa_ref, wb_ref, wc_ref, wout_ref, gn_ref, hout_ref, xn_ref, *, tiles_per_seq, w_ssm, w_conv):
    i = pl.program_id(0)
    tm = h_ref.shape[0]
    d = h_ref.shape[1]

    u = sc_ref[:, 0:w_ssm].astype(F32)
    yb = _gelu_tanh(yssm_ref[...].astype(F32) + dskip_ref[...] * u)
    yb = yb * jax.nn.sigmoid(jnp.dot(yb.astype(BF16), wglu_ref[...], preferred_element_type=F32))
    y_b = jnp.dot(yb.astype(BF16), wb_ref[...], preferred_element_type=F32)

    o_h, o_b, o_c = w_ssm, w_ssm + w_conv, w_ssm + 2 * w_conv
    uc = sc_ref[:, o_c:o_c + w_conv].astype(F32) * sc_ref[:, o_h:o_h + w_conv].astype(F32)
    halo = halo_ref[:, o_c:o_c + w_conv].astype(F32) * halo_ref[:, o_h:o_h + w_conv].astype(F32)
    halo = jnp.where(i % tiles_per_seq == 0, 0.0, halo)
    row = lax.broadcasted_iota(jnp.int32, (tm, w_conv), 0)
    prev1 = jnp.where(row == 0, halo[SUBLANES - 1:SUBLANES, :], pltpu.roll(uc, 1, axis=0))
    prev2 = jnp.where(row == 0, halo[SUBLANES - 2:SUBLANES - 1, :],
                      jnp.where(row == 1, halo[SUBLANES - 1:SUBLANES, :], pltpu.roll(uc, 2, axis=0)))
    conv = convw_ref[0:1, :] * prev2 + convw_ref[1:2, :] * prev1 + convw_ref[2:3, :] * uc
    yc = sc_ref[:, o_b:o_b + w_conv].astype(F32) * conv
    y_c = jnp.dot(yc.astype(BF16), wc_ref[...], preferred_element_type=F32)

    y_a = jnp.dot(attn_ref[...], wa_ref[...], preferred_element_type=F32)

    merged = (gate_ref[:, 0:d].astype(F32) * y_a + gate_ref[:, d:2 * d].astype(F32) * y_b
              + gate_ref[:, 2 * d:3 * d].astype(F32) * y_c)
    hn = h_ref[...] + jnp.dot(merged.astype(BF16), wout_ref[...], preferred_element_type=F32)
    hout_ref[...] = hn
    xn_ref[...] = _rms(hn, gn_ref[...]).astype(xn_ref.dtype)


def _merge(attn, yssm, sc, gates, h, d_skip, w_glu, conv_w, w_a, w_b, w_c, w_out, g_ffn, *, seq):
    t, d = h.shape
    tm = TM_MERGE
    w_ssm = yssm.shape[1]
    w_conv = conv_w.shape[1]
    w_attn = attn.shape[1]
    n_sc = sc.shape[1]
    body = functools.partial(_merge_body, tiles_per_seq=seq // tm, w_ssm=w_ssm, w_conv=w_conv)
    full = lambda shape: pl.BlockSpec(shape, lambda i: (0,) * len(shape))
    halo_blocks = tm // SUBLANES
    return pl.pallas_call(
        body,
        grid=(t // tm,),
        in_specs=[
            pl.BlockSpec((tm, w_attn), lambda i: (i, 0)),
            pl.BlockSpec((tm, w_ssm), lambda i: (i, 0)),
            pl.BlockSpec((tm, n_sc), lambda i: (i, 0)),
            pl.BlockSpec((SUBLANES, n_sc), lambda i: (jnp.maximum(i * halo_blocks - 1, 0), 0)),
            pl.BlockSpec((tm, N_BRANCH * d), lambda i: (i, 0)),
            pl.BlockSpec((tm, d), lambda i: (i, 0)),
            full((1, w_ssm)), full((w_ssm, w_ssm)), full((CONV_K, w_conv)),
            full((w_attn, d)), full((w_ssm, d)), full((w_conv, d)), full((d, d)), full((1, d)),
        ],
        out_specs=[pl.BlockSpec((tm, d), lambda i: (i, 0)), pl.BlockSpec((tm, d), lambda i: (i, 0))],
        out_shape=[jax.ShapeDtypeStruct((t, d), F32), jax.ShapeDtypeStruct((t, d), BF16)],
        compiler_params=_params("parallel"),
        name="merge",
    )(attn, yssm, sc, sc, gates, h, d_skip.reshape(1, w_ssm).astype(F32), w_glu, conv_w.astype(F32),
      w_a, w_b, w_c, w_out, g_ffn.reshape(1, d))


def _swiglu_hidden(x, wg_ref, wu_ref, act_ref, j):
    gate = jnp.dot(x, wg_ref[...], preferred_element_type=F32)
    up = jnp.dot(x, wu_ref[...], preferred_element_type=F32)
    act_ref[j] = (gate * jax.nn.sigmoid(gate) * up).astype(act_ref.dtype)


def _swiglu_down(act_ref, wd_ref):
    act = jnp.concatenate([act_ref[jj] for jj in range(act_ref.shape[0])], axis=1)
    return jnp.dot(act, wd_ref[...], preferred_element_type=F32)


def _dense_ffn_body(x_ref, h_ref, wg_ref, wu_ref, wd_ref, o_ref, act_ref):
    j = pl.program_id(1)
    _swiglu_hidden(x_ref[...], wg_ref, wu_ref, act_ref, j)

    @pl.when(j == pl.num_programs(1) - 1)
    def _():
        o_ref[...] = h_ref[...] + _swiglu_down(act_ref, wd_ref)


def _dense_ffn(xn, h, w_gate, w_up, w_down):
    t, d = h.shape
    f = w_gate.shape[1]
    tm, tf = TM_FFN, TF_FFN
    return pl.pallas_call(
        _dense_ffn_body,
        grid=(t // tm, f // tf),
        in_specs=[
            pl.BlockSpec((tm, d), lambda i, j: (i, 0)),
            pl.BlockSpec((tm, d), lambda i, j: (i, 0)),
            pl.BlockSpec((d, tf), lambda i, j: (0, j)),
            pl.BlockSpec((d, tf), lambda i, j: (0, j)),
            pl.BlockSpec((f, d), lambda i, j: (0, 0), pipeline_mode=pl.Buffered(1)),
        ],
        out_specs=pl.BlockSpec((tm, d), lambda i, j: (i, 0)),
        out_shape=jax.ShapeDtypeStruct((t, d), F32),
        scratch_shapes=[pltpu.VMEM((f // tf, tm, tf), BF16)],
        compiler_params=_params("parallel", "arbitrary"),
        name="dense_ffn",
    )(xn, h, w_gate, w_up, w_down)


def _router_body(h_ref, g_ref, wr_ref, idx_ref, wgt_ref):
    xn = _rms(h_ref[...], g_ref[...])
    logits = lax.dot_general(wr_ref[...], xn, (((1,), (1,)), ((), ())), preferred_element_type=F32,
                             precision=lax.Precision.HIGHEST)
    n_e = logits.shape[0]
    e_idx = lax.broadcasted_iota(jnp.int32, logits.shape, 0)
    m1 = jnp.max(logits, axis=0, keepdims=True)
    i1 = jnp.min(jnp.where(logits == m1, e_idx, n_e), axis=0, keepdims=True)
    rest = jnp.where(e_idx == i1, -jnp.inf, logits)
    m2 = jnp.max(rest, axis=0, keepdims=True)
    i2 = jnp.min(jnp.where(rest == m2, e_idx, n_e), axis=0, keepdims=True)
    e2 = jnp.exp(m2 - m1)
    w1 = 1.0 / (1.0 + e2)
    idx_ref[0] = jnp.concatenate([i1, i2], axis=0)
    wgt_ref[0] = jnp.concatenate([w1, e2 * w1], axis=0)


def _router(h, g_ffn, w_router):
    t, d = h.shape
    n_e = w_router.shape[1]
    tm = TM_ROUTE
    nt = t // tm
    idx, wgt = pl.pallas_call(
        _router_body,
        grid=(nt,),
        in_specs=[
            pl.BlockSpec((tm, d), lambda i: (i, 0)),
            pl.BlockSpec((1, d), lambda i: (0, 0)),
            pl.BlockSpec((n_e, d), lambda i: (0, 0)),
        ],
        out_specs=[pl.BlockSpec((1, TOP_K, tm), lambda i: (i, 0, 0)),
                   pl.BlockSpec((1, TOP_K, tm), lambda i: (i, 0, 0))],
        out_shape=[jax.ShapeDtypeStruct((nt, TOP_K, tm), jnp.int32),
                   jax.ShapeDtypeStruct((nt, TOP_K, tm), F32)],
        compiler_params=_params("parallel"),
        name="router",
    )(h, g_ffn.reshape(1, d), w_router.T.astype(F32))
    idx = idx.transpose(0, 2, 1).reshape(t, TOP_K)
    wgt = wgt.transpose(0, 2, 1).reshape(t, TOP_K)
    return idx, wgt


def _scatter_rows_body(zero_start_ref, s1_ref, s2_ref, h_ref, xs_ref, zeros_ref, sem):
    tm = h_ref.shape[0]

    @pl.when(pl.program_id(0) == 0)
    def _():
        zeros_ref[...] = jnp.zeros_like(zeros_ref)

        def fill(k, c):
            dst = xs_ref.at[pl.ds(pl.multiple_of(zero_start_ref[k], SUBLANES), zeros_ref.shape[0])]
            cp = pltpu.make_async_copy(zeros_ref, dst, sem)
            cp.start()
            cp.wait()
            return c

        lax.fori_loop(0, zero_start_ref.shape[0], fill, 0)

    def issue(r, c):
        src = h_ref.at[pl.ds(r, 1)]
        pltpu.make_async_copy(src, xs_ref.at[pl.ds(s1_ref[r], 1)], sem).start(priority=0)
        pltpu.make_async_copy(src, xs_ref.at[pl.ds(s2_ref[r], 1)], sem).start(priority=1)
        return c

    lax.fori_loop(0, tm, issue, 0, unroll=8)
    pltpu.make_async_copy(h_ref, xs_ref.at[pl.ds(0, tm)], sem).wait()
    pltpu.make_async_copy(h_ref, xs_ref.at[pl.ds(0, tm)], sem).wait()


def _scatter_rows(h, slot1, slot2, zero_start, cap, zero_rows):
    t, d = h.shape
    tm = TM_SCATTER
    smem_tile = pl.BlockSpec((tm,), lambda i, zs: (i,), memory_space=pltpu.SMEM)
    grid_spec = pltpu.PrefetchScalarGridSpec(
        num_scalar_prefetch=1,
        grid=(t // tm,),
        in_specs=[smem_tile, smem_tile, pl.BlockSpec((tm, d), lambda i, zs: (i, 0))],
        out_specs=pl.BlockSpec(memory_space=pl.ANY),
        scratch_shapes=[pltpu.VMEM((zero_rows, d), h.dtype), pltpu.SemaphoreType.DMA(())],
    )
    return pl.pallas_call(
        _scatter_rows_body,
        grid_spec=grid_spec,
        out_shape=jax.ShapeDtypeStruct((cap, d), h.dtype),
        compiler_params=_params("arbitrary"),
        name="scatter_rows",
    )(zero_start, slot1, slot2, h)


def _moe_ffn_body(tile_expert_ref, tile_valid_ref, x_ref, g_ref, wg_ref, wu_ref, wd_ref, o_ref, xn_ref, act_ref):
    i = pl.program_id(0)
    j = pl.program_id(1)
    last = j == pl.num_programs(1) - 1
    valid = tile_valid_ref[i] > 0

    @pl.when(valid)
    def _():
        @pl.when(j == 0)
        def _():
            xn_ref[...] = _rms(x_ref[...], g_ref[...]).astype(xn_ref.dtype)

        _swiglu_hidden(xn_ref[...], wg_ref.at[0], wu_ref.at[0], act_ref, j)

        @pl.when(last)
        def _():
            o_ref[...] = _swiglu_down(act_ref, wd_ref.at[0])

    @pl.when(last & jnp.logical_not(valid))
    def _():
        o_ref[...] = jnp.zeros_like(o_ref)


def _moe_ffn(xs, g_ffn, tile_expert, tile_valid, w_gate, w_up, w_down):
    cap, d = xs.shape
    f = w_gate.shape[2]
    tm, tf = TM_MOE, TF_FFN
    grid_spec = pltpu.PrefetchScalarGridSpec(
        num_scalar_prefetch=2,
        grid=(cap // tm, f // tf),
        in_specs=[
            pl.BlockSpec((tm, d), lambda i, j, te, tv: (i, 0)),
            pl.BlockSpec((1, d), lambda i, j, te, tv: (0, 0)),
            pl.BlockSpec((1, d, tf), lambda i, j, te, tv: (te[i], 0, j)),
            pl.BlockSpec((1, d, tf), lambda i, j, te, tv: (te[i], 0, j)),
            pl.BlockSpec((1, f, d), lambda i, j, te, tv: (te[i], 0, 0)),
        ],
        out_specs=pl.BlockSpec((tm, d), lambda i, j, te, tv: (i, 0)),
        scratch_shapes=[pltpu.VMEM((tm, d), BF16), pltpu.VMEM((f // tf, tm, tf), BF16)],
    )
    return pl.pallas_call(
        _moe_ffn_body,
        grid_spec=grid_spec,
        out_shape=jax.ShapeDtypeStruct((cap, d), F32),
        compiler_params=_params("parallel", "arbitrary"),
        name="moe_ffn",
    )(tile_expert, tile_valid, xs, g_ffn.reshape(1, d), w_gate, w_up, w_down)


def _moe_plan(idx, n_experts, tm):
    t = idx.shape[0]
    pairs = t * TOP_K
    cap = pairs + n_experts * tm
    e_flat = idx.reshape(pairs)
    onehot = (e_flat[:, None] == jnp.arange(n_experts, dtype=jnp.int32)[None, :]).astype(jnp.int32)
    rank = jnp.sum(onehot * (jnp.cumsum(onehot, axis=0) - onehot), axis=1)
    counts = jnp.sum(onehot, axis=0)
    padded = ((counts + tm - 1) // tm) * tm
    ends = jnp.cumsum(padded)
    starts = ends - padded
    slot = jnp.sum(onehot * starts[None, :], axis=1) + rank
    tile_start = jnp.arange(cap // tm, dtype=jnp.int32) * tm
    tile_expert = jnp.sum((tile_start[:, None] >= ends[None, :]).astype(jnp.int32), axis=1)
    tile_valid = (tile_expert < n_experts).astype(jnp.int32)
    last_used = jnp.max(jnp.where(counts > 0, jnp.arange(n_experts, dtype=jnp.int32), 0))
    tile_expert = jnp.minimum(tile_expert, last_used).astype(jnp.int32)
    zero_rows = tm + SUBLANES
    tail = ends[-1] + jnp.arange(n_experts, dtype=jnp.int32) * tm
    first = jnp.concatenate([starts + counts, tail]) // SUBLANES * SUBLANES
    zero_start = jnp.minimum(first, cap - zero_rows).astype(jnp.int32)
    return slot.reshape(t, TOP_K).astype(jnp.int32), cap, tile_expert, tile_valid, zero_start, zero_rows


def _combine_body(s1_ref, s2_ref, h_ref, wgt_ref, g_ref, ys_ref, o_ref, y1_ref, y2_ref, sem, *, apply_norm):
    tm = h_ref.shape[0]

    def issue(r, c):
        pltpu.make_async_copy(ys_ref.at[pl.ds(s1_ref[r], 1)], y1_ref.at[pl.ds(r, 1)], sem).start(priority=0)
        pltpu.make_async_copy(ys_ref.at[pl.ds(s2_ref[r], 1)], y2_ref.at[pl.ds(r, 1)], sem).start(priority=1)
        return c

    lax.fori_loop(0, tm, issue, 0, unroll=8)
    pltpu.make_async_copy(ys_ref.at[pl.ds(0, tm)], y1_ref, sem).wait()
    pltpu.make_async_copy(ys_ref.at[pl.ds(0, tm)], y2_ref, sem).wait()
    w = wgt_ref[...]
    hn = h_ref[...] + w[:, 0:1] * y1_ref[...] + w[:, 1:2] * y2_ref[...]
    o_ref[...] = _rms(hn, g_ref[...]) if apply_norm else hn


def _combine(h, ys, slot, wgt, g, *, apply_norm):
    t, d = h.shape
    tm = TM_COMBINE
    smem_tile = pl.BlockSpec((tm,), lambda i: (i,), memory_space=pltpu.SMEM)
    row = pl.BlockSpec((tm, d), lambda i: (i, 0))
    return pl.pallas_call(
        functools.partial(_combine_body, apply_norm=apply_norm),
        grid=(t // tm,),
        in_specs=[smem_tile, smem_tile, row,
                  pl.BlockSpec((tm, TOP_K), lambda i: (i, 0)),
                  pl.BlockSpec((1, d), lambda i: (0, 0)),
                  pl.BlockSpec(memory_space=pl.ANY)],
        out_specs=row,
        out_shape=jax.ShapeDtypeStruct((t, d), F32),
        scratch_shapes=[pltpu.VMEM((tm, d), F32), pltpu.VMEM((tm, d), F32), pltpu.SemaphoreType.DMA(())],
        compiler_params=_params("arbitrary"),
        name="combine",
    )(slot[:, 0], slot[:, 1], h, wgt, g.reshape(1, d), ys)


def _plain_norm_body(h_ref, g_ref, o_ref):
    o_ref[...] = _rms(h_ref[...], g_ref[...])


def _plain_norm(h, g):
    t, d = h.shape
    tm = TM_NORM
    row = pl.BlockSpec((tm, d), lambda i: (i, 0))
    return pl.pallas_call(
        _plain_norm_body,
        grid=(t // tm,),
        in_specs=[row, pl.BlockSpec((1, d), lambda i: (0, 0))],
        out_specs=row,
        out_shape=jax.ShapeDtypeStruct((t, d), F32),
        compiler_params=_params("parallel"),
        name="plain_norm",
    )(h, g.reshape(1, d))


def kernel(x, norm_mix, w_in, ssm_lambda_re, ssm_lambda_im, ssm_log_dt, ssm_b_re, ssm_b_im, ssm_c_re,
           ssm_c_im, ssm_d, ssm_w_glu, conv_w, w_br_a, w_br_b, w_br_c, w_out, norm_ffn, dense_w_gate,
           dense_w_up, dense_w_down, moe_w_router, moe_w_gate, moe_w_up, moe_w_down, final_norm):
    batch, seq, d = x.shape
    depth = w_in.shape[0]
    t = batch * seq
    w_attn = w_br_a.shape[1]
    w_ssm = w_br_b.shape[1]
    w_conv = w_br_c.shape[1]
    n_sc = w_ssm + 3 * w_conv
    n_gate = N_BRANCH * d
    n_experts = moe_w_router.shape[-1]

    h = x.reshape(t, d)
    normed = False
    for i in range(depth):
        qt, k, vt, sc, gates = _inproj(h, norm_mix[i], w_in[i].astype(BF16), batch=batch, seq=seq,
                                       width=w_attn, n_sc=n_sc, n_gate=n_gate)
        attn = _sb_attention(qt, k, vt)
        tables = _ssm_tables(ssm_lambda_re[i], ssm_lambda_im[i], ssm_log_dt[i], ssm_b_re[i], ssm_b_im[i],
                             ssm_c_re[i], ssm_c_im[i], seq // SSM_CHUNK)
        yssm = _ssm_scan(sc[:, :w_ssm], tables, batch=batch, seq=seq)
        h, xn = _merge(attn, yssm, sc, gates, h, ssm_d[i], ssm_w_glu[i].astype(BF16), conv_w[i],
                       w_br_a[i].astype(BF16), w_br_b[i].astype(BF16), w_br_c[i].astype(BF16),
                       w_out[i].astype(BF16), norm_ffn[i], seq=seq)
        j = i // 2
        if i % 2 == 0:
            h = _dense_ffn(xn, h, dense_w_gate[j].astype(BF16), dense_w_up[j].astype(BF16),
                           dense_w_down[j].astype(BF16))
        else:
            idx, wgt = _router(h, norm_ffn[i], moe_w_router[j])
            slot, cap, tile_expert, tile_valid, zero_start, zero_rows = _moe_plan(idx, n_experts, TM_MOE)
            xs = _scatter_rows(h, slot[:, 0], slot[:, 1], zero_start, cap, zero_rows)
            ys = _moe_ffn(xs, norm_ffn[i], tile_expert, tile_valid, moe_w_gate[j].astype(BF16),
                          moe_w_up[j].astype(BF16), moe_w_down[j].astype(BF16))
            last = i == depth - 1
            h = _combine(h, ys, slot, wgt, final_norm if last else norm_ffn[i], apply_norm=last)
            normed = last
    out = h if normed else _plain_norm(h, final_norm)
    return out.reshape(batch, seq, d)
```

```python
import functools

import jax
import jax.numpy as jnp
from jax import lax
from jax.experimental import pallas as pl
from jax.experimental.pallas import tpu as pltpu

F32 = jnp.float32
BF16 = jnp.bfloat16

EPS = 1e-6
HEAD_DIM = 64
SSM_CHUNK = 16
SSM_MAX_RE = -1e-4
CONV_K = 3
N_BRANCH = 3
TOP_K = 2
LOG2E = 1.4426950408889634
MASKED_LOG_WEIGHT = -1e30

LANES = 128
SUBLANES = 8
MXU_DIM = 256
VMEM_LIMIT_BYTES = 56 * 1024 * 1024

TM_PROJ = 512
TQ_ATTN = MXU_DIM
TM_MERGE = 512
TM_FFN = 512
TF_FFN = 1792
TM_MOE = 512
TM_ROUTE = 512
TM_SCATTER = 256
TM_COMBINE = 256
TM_NORM = 512


def _params(*sem):
    return pltpu.CompilerParams(dimension_semantics=sem, vmem_limit_bytes=VMEM_LIMIT_BYTES)


def _rms(x, g):
    ms = jnp.mean(x * x, axis=-1, keepdims=True)
    return x * lax.rsqrt(ms + EPS) * g


def _inproj_body(x_ref, g_ref, w_ref, qt_ref, k_ref, vt_ref, sc_ref, gate_ref, *, width, n_sc, n_gate, q_scale):
    xn = _rms(x_ref[...], g_ref[...]).astype(BF16)
    tm = xn.shape[0]
    tq = vt_ref.shape[-1]
    chunk = 512

    def proj(c0, n):
        return jnp.dot(xn, w_ref[:, c0:c0 + n], preferred_element_type=F32)

    qt_ref[0] = (proj(0, width) * q_scale).T.astype(BF16)
    k_ref[...] = proj(width, width).astype(BF16)
    v = proj(2 * width, width)
    for kt in range(tm // tq):
        for p in range(width // LANES):
            vt_ref[0, p, kt] = v[kt * tq:(kt + 1) * tq, p * LANES:(p + 1) * LANES].T.astype(BF16)
    for c0 in range(0, n_sc, chunk):
        sc_ref[:, c0:c0 + chunk] = proj(3 * width + c0, chunk).astype(BF16)
    for c0 in range(0, n_gate, chunk):
        gate_ref[:, c0:c0 + chunk] = jax.nn.sigmoid(proj(3 * width + n_sc + c0, chunk)).astype(BF16)


def _inproj(h, g, w_bf16, *, batch, seq, width, n_sc, n_gate):
    t, d = h.shape
    tm = TM_PROJ
    tq = TQ_ATTN
    n_pairs = width // LANES
    tiles_per_seq = seq // tm
    body = functools.partial(_inproj_body, width=width, n_sc=n_sc, n_gate=n_gate, q_scale=HEAD_DIM ** -0.5)
    return pl.pallas_call(
        body,
        grid=(t // tm,),
        in_specs=[
            pl.BlockSpec((tm, d), lambda i: (i, 0)),
            pl.BlockSpec((1, d), lambda i: (0, 0)),
            pl.BlockSpec((d, 3 * width + n_sc + n_gate), lambda i: (0, 0), pipeline_mode=pl.Buffered(1)),
        ],
        out_specs=[
            pl.BlockSpec((1, width, tm), lambda i: (i // tiles_per_seq, 0, i % tiles_per_seq)),
            pl.BlockSpec((tm, width), lambda i: (i, 0)),
            pl.BlockSpec((1, n_pairs, tm // tq, LANES, tq),
                         lambda i: (i // tiles_per_seq, 0, i % tiles_per_seq, 0, 0)),
            pl.BlockSpec((tm, n_sc), lambda i: (i, 0)),
            pl.BlockSpec((tm, n_gate), lambda i: (i, 0)),
        ],
        out_shape=[
            jax.ShapeDtypeStruct((batch, width, seq), BF16),
            jax.ShapeDtypeStruct((t, width), BF16),
            jax.ShapeDtypeStruct((batch, n_pairs, seq // tq, LANES, tq), BF16),
            jax.ShapeDtypeStruct((t, n_sc), BF16),
            jax.ShapeDtypeStruct((t, n_gate), BF16),
        ],
        compiler_params=_params("parallel"),
        name="inproj",
    )(h, g.reshape(1, d), w_bf16)


def _sb_attn_body(qt_ref, k_ref, vt_ref, o_ref, acc_ref, carry_ref, lb_a_ref, sp_a_ref, lb_b_ref, sp_b_ref,
                  *, tq, n_pairs):
    i = pl.program_id(1)
    buf_a = (lb_a_ref, sp_a_ref)
    buf_b = (lb_b_ref, sp_b_ref)
    n_heads = 2 * n_pairs
    ones_rows = carry_ref.shape[1]
    key = lax.broadcasted_iota(jnp.int32, (tq, tq), 0)
    qry = lax.broadcasted_iota(jnp.int32, (tq, tq), 1)
    valid = key < qry
    er = lax.broadcasted_iota(jnp.int32, (tq + ones_rows, tq), 0)
    ec = lax.broadcasted_iota(jnp.int32, (tq + ones_rows, tq), 1)
    later_and_sum = jnp.where((ec > er) | (er >= tq), 1.0, 0.0).astype(BF16)

    feat = lax.broadcasted_iota(jnp.int32, (LANES, tq), 0)
    q_heads = []
    for p in range(n_pairs):
        qp = qt_ref[0, p * LANES:(p + 1) * LANES, :]
        zero = jnp.zeros_like(qp)
        q_heads.append(jnp.where(feat < HEAD_DIM, qp, zero))
        q_heads.append(jnp.where(feat >= HEAD_DIM, qp, zero))

    def scores(j, buf, masked):
        lb_ref, sp_ref = buf
        start = pl.multiple_of(j * tq, tq)
        for h in range(n_heads):
            p = h // 2
            kblk = k_ref[pl.ds(start, tq), p * LANES:(p + 1) * LANES]
            z = jnp.dot(kblk, q_heads[h], preferred_element_type=F32)
            l1p = jnp.log(1.0 + jnp.exp2(jnp.abs(z) * (-LOG2E)))
            sp = jnp.maximum(z, 0.0) + l1p
            lb = z - sp
            if masked:
                sp = jnp.where(valid, sp, 0.0)
                lb = jnp.where(valid, lb, MASKED_LOG_WEIGHT)
            lb_ref[h] = lb
            sp_ref[h] = sp.astype(BF16)

    def weights(j, buf):
        lb_ref, sp_ref = buf
        ws, sums = [], []
        for h in range(n_heads):
            ts = jnp.dot(later_and_sum, sp_ref[h], preferred_element_type=F32)
            ws.append(jnp.exp(lb_ref[h] - ts[0:tq]).astype(BF16))
            sums.append(ts[tq:tq + ones_rows])
        for h in range(n_heads):
            p, hh = divmod(h, 2)
            vt = vt_ref[0, p, j, hh * HEAD_DIM:(hh + 1) * HEAD_DIM, :]
            pv = jnp.dot(vt, ws[h], preferred_element_type=F32)
            carry = carry_ref[h]
            scale = jnp.exp(-carry[0:1, :])
            rows = slice(h * HEAD_DIM, (h + 1) * HEAD_DIM)
            acc_ref[rows, :] += pv * scale
            carry_ref[h] = carry + sums[h]

    def earlier_keys_matter():
        return (jnp.max(jnp.exp(-carry_ref[:, 0, :])) > 0.0).astype(jnp.int32)

    acc_ref[...] = jnp.zeros_like(acc_ref)
    carry_ref[...] = jnp.zeros_like(carry_ref)

    @pl.when(i == 0)
    def _():
        scores(0, buf_a, True)
        weights(0, buf_a)

    @pl.when(i > 0)
    def _():
        scores(i, buf_a, True)
        scores(i - 1, buf_b, False)
        weights(i, buf_a)
        weights(i - 1, buf_b)

    rest = i - 2

    @pl.when(jnp.where(rest >= 0, earlier_keys_matter(), 0) > 0)
    def _():
        scores(rest, buf_a, False)

        def two_tiles(c):
            m, _ = c
            j = rest - 1 - 2 * m
            scores(j, buf_b, False)
            weights(j + 1, buf_a)
            scores(j - 1, buf_a, False)
            weights(j, buf_b)
            return m + 1, earlier_keys_matter()

        _, alive = lax.while_loop(lambda c: (c[0] < rest // 2) & (c[1] > 0), two_tiles,
                                  (jnp.int32(0), jnp.int32(1)))

        @pl.when((alive > 0) & (rest % 2 == 1))
        def _():
            scores(0, buf_b, False)
            weights(1, buf_a)
            weights(0, buf_b)

        @pl.when((alive > 0) & (rest % 2 == 0))
        def _():
            weights(0, buf_a)

    for p in range(n_pairs):
        o_ref[:, p * LANES:(p + 1) * LANES] = acc_ref[p * LANES:(p + 1) * LANES, :].T.astype(o_ref.dtype)


def _sb_attention(qt, k, vt):
    batch, width, seq = qt.shape
    n_pairs, nq, _, tq = vt.shape[1:]
    n_heads = 2 * n_pairs
    return pl.pallas_call(
        functools.partial(_sb_attn_body, tq=tq, n_pairs=n_pairs),
        grid=(batch, nq),
        in_specs=[
            pl.BlockSpec((1, width, tq), lambda b, i: (b, 0, i)),
            pl.BlockSpec((seq, width), lambda b, i: (b, 0)),
            pl.BlockSpec((1, n_pairs, nq, LANES, tq), lambda b, i: (b, 0, 0, 0, 0)),
        ],
        out_specs=pl.BlockSpec((tq, width), lambda b, i: (b * nq + i, 0)),
        out_shape=jax.ShapeDtypeStruct((batch * seq, width), BF16),
        scratch_shapes=[pltpu.VMEM((width, tq), F32), pltpu.VMEM((n_heads, 16, tq), F32),
                        pltpu.VMEM((n_heads, tq, tq), F32), pltpu.VMEM((n_heads, tq, tq), BF16),
                        pltpu.VMEM((n_heads, tq, tq), F32), pltpu.VMEM((n_heads, tq, tq), BF16)],
        compiler_params=_params("parallel", "arbitrary"),
        name="sb_attention",
    )(qt, k, vt)


def _ssm_tables(lam_re, lam_im, log_dt, b_re, b_im, c_re, c_im, n_chunks):
    g, p = lam_re.shape
    c = SSM_CHUNK
    lr = jnp.minimum(lam_re.astype(F32), SSM_MAX_RE)
    li = lam_im.astype(F32)
    dt = jnp.exp(log_dt.astype(F32))[:, None]

    def lbar_pow(k):
        mag = jnp.exp(k * (lr * dt))
        ang = k * (li * dt)
        return mag * jnp.cos(ang), mag * jnp.sin(ang)

    ab_re, ab_im = lbar_pow(1.0)
    den = lr * lr + li * li
    nr = ab_re - 1.0
    ni = ab_im
    f_re = (nr * lr + ni * li) / den
    f_im = (ni * lr - nr * li) / den
    br = b_re.astype(F32)
    bi = b_im.astype(F32)
    bb_re = f_re[..., None] * br - f_im[..., None] * bi
    bb_im = f_re[..., None] * bi + f_im[..., None] * br
    cr = c_re.astype(F32)
    ci = c_im.astype(F32)

    eye = jnp.eye(g, dtype=F32)

    def block_diag(x):
        _, a, b = x.shape
        return (x[:, :, None, :] * eye[:, None, :, None]).reshape(g * a, g * b)

    b_in = jnp.concatenate([block_diag(bb_re.transpose(0, 2, 1)),
                            block_diag(bb_im.transpose(0, 2, 1))], axis=1)
    c_out = jnp.concatenate([block_diag(cr.transpose(0, 2, 1)),
                             block_diag(-ci.transpose(0, 2, 1))], axis=0)
    lbar = jnp.stack([ab_re.reshape(-1), ab_im.reshape(-1)], axis=0)

    steps = []
    s = 1
    while s < n_chunks:
        a_re, a_im = lbar_pow(float(c * s))
        steps.append(jnp.stack([a_re.reshape(-1, MXU_DIM), a_im.reshape(-1, MXU_DIM)], axis=1))
        s *= 2
    step = jnp.stack(steps, axis=1)
    return b_in.astype(BF16), c_out.astype(BF16), lbar, step


def _ssm_body(u_ref, b_ref, c_ref, lbar_ref, step_ref, y_ref, z_ref, *, n_chunks, batch):
    s = pl.program_id(0)
    c = u_ref.shape[0]
    n_blk = z_ref.shape[0]
    half = n_blk // 2
    bw = z_ref.shape[2]

    @pl.when(s == 0)
    def _():
        z_ref[...] = jnp.zeros_like(z_ref)

    def advance(ut):
        for pb in range(half):
            lo, hi = pb * bw, (pb + 1) * bw
            re = z_ref[pb]
            im = z_ref[half + pb]
            a_re = lbar_ref[0:1, lo:hi]
            a_im = lbar_ref[1:2, lo:hi]
            x_re = jnp.dot(ut, b_ref[:, lo:hi], preferred_element_type=F32)
            x_im = jnp.dot(ut, b_ref[:, half * bw + lo:half * bw + hi], preferred_element_type=F32)
            z_ref[pb] = a_re * re - a_im * im + x_re
            z_ref[half + pb] = a_re * im + a_im * re + x_im

    @pl.when(s < c)
    def _():
        advance(u_ref[s])

    @pl.when(s == c)
    def _():
        chunk = lax.broadcasted_iota(jnp.int32, (n_chunks, bw), 0)

        def shifted(x, n):
            return jnp.where(chunk >= n, pltpu.roll(x, n, axis=0), 0.0)

        def scan_block(idx, carry):
            b = idx // half
            pb = idx % half
            rows = pl.ds(pl.multiple_of(b * n_chunks, n_chunks), n_chunks)
            re = z_ref[pb, rows, :]
            im = z_ref[half + pb, rows, :]
            n = 1
            k = 0
            while n < n_chunks:
                a_re = step_ref[pb, k, 0:1, :]
                a_im = step_ref[pb, k, 1:2, :]
                re_s = shifted(re, n)
                im_s = shifted(im, n)
                re, im = re + a_re * re_s - a_im * im_s, im + a_re * im_s + a_im * re_s
                n *= 2
                k += 1
            z_ref[pb, rows, :] = shifted(re, 1)
            z_ref[half + pb, rows, :] = shifted(im, 1)
            return carry

        lax.fori_loop(0, batch * half, scan_block, 0)

    @pl.when(s >= c)
    def _():
        advance(u_ref[s - c])
        y = jnp.dot(z_ref[0].astype(BF16), c_ref[0:bw, :], preferred_element_type=F32)
        for blk in range(1, n_blk):
            y += jnp.dot(z_ref[blk].astype(BF16), c_ref[blk * bw:(blk + 1) * bw, :],
                         preferred_element_type=F32)
        y_ref[0] = y.astype(y_ref.dtype)


def _ssm_scan(u, tables, *, batch, seq):
    b_in, c_out, lbar, step = tables
    width = u.shape[1]
    c = SSM_CHUNK
    n_chunks = seq // c
    rows = batch * n_chunks
    n_blk = b_in.shape[1] // MXU_DIM
    u3 = u.reshape(rows, c, width).transpose(1, 0, 2)
    whole = lambda shape: pl.BlockSpec(shape, lambda s: (0,) * len(shape), pipeline_mode=pl.Buffered(1))
    y3 = pl.pallas_call(
        functools.partial(_ssm_body, n_chunks=n_chunks, batch=batch),
        grid=(2 * c,),
        in_specs=[whole(u3.shape), whole(b_in.shape), whole(c_out.shape), whole(lbar.shape), whole(step.shape)],
        out_specs=pl.BlockSpec((1, rows, width), lambda s: (jnp.maximum(s - c, 0), 0, 0)),
        out_shape=jax.ShapeDtypeStruct((c, rows, width), BF16),
        scratch_shapes=[pltpu.VMEM((n_blk, rows, MXU_DIM), F32)],
        compiler_params=_params("arbitrary"),
        name="ssm_scan",
    )(u3, b_in, c_out, lbar, step)
    return y3.transpose(1, 0, 2).reshape(batch * seq, width)


def _gelu_tanh(x):
    return 0.5 * x * (1.0 + jnp.tanh(0.7978845608028654 * (x + 0.044715 * (x * x * x))))


def _merge_body(attn_ref, yssm_ref, sc_ref, halo_ref, gate_ref, h_ref, dskip_ref, wglu_ref, convw_ref,
                wa_ref, wb_ref, wc_ref, wout_ref, gn_ref, hout_ref, xn_ref, *, tiles_per_seq, w_ssm, w_conv):
    i = pl.program_id(0)
    tm = h_ref.shape[0]
    d = h_ref.shape[1]

    u = sc_ref[:, 0:w_ssm].astype(F32)
    yb = _gelu_tanh(yssm_ref[...].astype(F32) + dskip_ref[...] * u)
    yb = yb * jax.nn.sigmoid(jnp.dot(yb.astype(BF16), wglu_ref[...], preferred_element_type=F32))
    y_b = jnp.dot(yb.astype(BF16), wb_ref[...], preferred_element_type=F32)

    o_h, o_b, o_c = w_ssm, w_ssm + w_conv, w_ssm + 2 * w_conv
    uc = sc_ref[:, o_c:o_c + w_conv].astype(F32) * sc_ref[:, o_h:o_h + w_conv].astype(F32)
    halo = halo_ref[:, o_c:o_c + w_conv].astype(F32) * halo_ref[:, o_h:o_h + w_conv].astype(F32)
    halo = jnp.where(i % tiles_per_seq == 0, 0.0, halo)
    row = lax.broadcasted_iota(jnp.int32, (tm, w_conv), 0)
    prev1 = jnp.where(row == 0, halo[SUBLANES - 1:SUBLANES, :], pltpu.roll(uc, 1, axis=0))
    prev2 = jnp.where(row == 0, halo[SUBLANES - 2:SUBLANES - 1, :],
                      jnp.where(row == 1, halo[SUBLANES - 1:SUBLANES, :], pltpu.roll(uc, 2, axis=0)))
    conv = convw_ref[0:1, :] * prev2 + convw_ref[1:2, :] * prev1 + convw_ref[2:3, :] * uc
    yc = sc_ref[:, o_b:o_b + w_conv].astype(F32) * conv
    y_c = jnp.dot(yc.astype(BF16), wc_ref[...], preferred_element_type=F32)

    y_a = jnp.dot(attn_ref[...], wa_ref[...], preferred_element_type=F32)

    merged = (gate_ref[:, 0:d].astype(F32) * y_a + gate_ref[:, d:2 * d].astype(F32) * y_b
              + gate_ref[:, 2 * d:3 * d].astype(F32) * y_c)
    hn = h_ref[...] + jnp.dot(merged.astype(BF16), wout_ref[...], preferred_element_type=F32)
    hout_ref[...] = hn
    xn_ref[...] = _rms(hn, gn_ref[...]).astype(xn_ref.dtype)


def _merge(attn, yssm, sc, gates, h, d_skip, w_glu, conv_w, w_a, w_b, w_c, w_out, g_ffn, *, seq):
    t, d = h.shape
    tm = TM_MERGE
    w_ssm = yssm.shape[1]
    w_conv = conv_w.shape[1]
    w_attn = attn.shape[1]
    n_sc = sc.shape[1]
    body = functools.partial(_merge_body, tiles_per_seq=seq // tm, w_ssm=w_ssm, w_conv=w_conv)
    full = lambda shape: pl.BlockSpec(shape, lambda i: (0,) * len(shape))
    halo_blocks = tm // SUBLANES
    return pl.pallas_call(
        body,
        grid=(t // tm,),
        in_specs=[
            pl.BlockSpec((tm, w_attn), lambda i: (i, 0)),
            pl.BlockSpec((tm, w_ssm), lambda i: (i, 0)),
            pl.BlockSpec((tm, n_sc), lambda i: (i, 0)),
            pl.BlockSpec((SUBLANES, n_sc), lambda i: (jnp.maximum(i * halo_blocks - 1, 0), 0)),
            pl.BlockSpec((tm, N_BRANCH * d), lambda i: (i, 0)),
            pl.BlockSpec((tm, d), lambda i: (i, 0)),
            full((1, w_ssm)), full((w_ssm, w_ssm)), full((CONV_K, w_conv)),
            full((w_attn, d)), full((w_ssm, d)), full((w_conv, d)), full((d, d)), full((1, d)),
        ],
        out_specs=[pl.BlockSpec((tm, d), lambda i: (i, 0)), pl.BlockSpec((tm, d), lambda i: (i, 0))],
        out_shape=[jax.ShapeDtypeStruct((t, d), F32), jax.ShapeDtypeStruct((t, d), BF16)],
        compiler_params=_params("parallel"),
        name="merge",
    )(attn, yssm, sc, sc, gates, h, d_skip.reshape(1, w_ssm).astype(F32), w_glu, conv_w.astype(F32),
      w_a, w_b, w_c, w_out, g_ffn.reshape(1, d))


def _swiglu_hidden(x, wg_ref, wu_ref, act_ref, j):
    gate = jnp.dot(x, wg_ref[...], preferred_element_type=F32)
    up = jnp.dot(x, wu_ref[...], preferred_element_type=F32)
    act_ref[j] = (gate * jax.nn.sigmoid(gate) * up).astype(act_ref.dtype)


def _swiglu_down(act_ref, wd_ref):
    act = jnp.concatenate([act_ref[jj] for jj in range(act_ref.shape[0])], axis=1)
    return jnp.dot(act, wd_ref[...], preferred_element_type=F32)


def _dense_ffn_body(x_ref, h_ref, wg_ref, wu_ref, wd_ref, o_ref, act_ref):
    j = pl.program_id(1)
    _swiglu_hidden(x_ref[...], wg_ref, wu_ref, act_ref, j)

    @pl.when(j == pl.num_programs(1) - 1)
    def _():
        o_ref[...] = h_ref[...] + _swiglu_down(act_ref, wd_ref)


def _dense_ffn(xn, h, w_gate, w_up, w_down):
    t, d = h.shape
    f = w_gate.shape[1]
    tm, tf = TM_FFN, TF_FFN
    return pl.pallas_call(
        _dense_ffn_body,
        grid=(t // tm, f // tf),
        in_specs=[
            pl.BlockSpec((tm, d), lambda i, j: (i, 0)),
            pl.BlockSpec((tm, d), lambda i, j: (i, 0)),
            pl.BlockSpec((d, tf), lambda i, j: (0, j)),
            pl.BlockSpec((d, tf), lambda i, j: (0, j)),
            pl.BlockSpec((f, d), lambda i, j: (0, 0), pipeline_mode=pl.Buffered(1)),
        ],
        out_specs=pl.BlockSpec((tm, d), lambda i, j: (i, 0)),
        out_shape=jax.ShapeDtypeStruct((t, d), F32),
        scratch_shapes=[pltpu.VMEM((f // tf, tm, tf), BF16)],
        compiler_params=_params("parallel", "arbitrary"),
        name="dense_ffn",
    )(xn, h, w_gate, w_up, w_down)


def _router_body(h_ref, g_ref, wr_ref, idx_ref, wgt_ref):
    xn = _rms(h_ref[...], g_ref[...])
    logits = lax.dot_general(wr_ref[...], xn, (((1,), (1,)), ((), ())), preferred_element_type=F32,
                             precision=lax.Precision.HIGHEST)
    n_e = logits.shape[0]
    e_idx = lax.broadcasted_iota(jnp.int32, logits.shape, 0)
    m1 = jnp.max(logits, axis=0, keepdims=True)
    i1 = jnp.min(jnp.where(logits == m1, e_idx, n_e), axis=0, keepdims=True)
    rest = jnp.where(e_idx == i1, -jnp.inf, logits)
    m2 = jnp.max(rest, axis=0, keepdims=True)
    i2 = jnp.min(jnp.where(rest == m2, e_idx, n_e), axis=0, keepdims=True)
    e2 = jnp.exp(m2 - m1)
    w1 = 1.0 / (1.0 + e2)
    idx_ref[0] = jnp.concatenate([i1, i2], axis=0)
    wgt_ref[0] = jnp.concatenate([w1, e2 * w1], axis=0)


def _router(h, g_ffn, w_router):
    t, d = h.shape
    n_e = w_router.shape[1]
    tm = TM_ROUTE
    nt = t // tm
    idx, wgt = pl.pallas_call(
        _router_body,
        grid=(nt,),
        in_specs=[
            pl.BlockSpec((tm, d), lambda i: (i, 0)),
            pl.BlockSpec((1, d), lambda i: (0, 0)),
            pl.BlockSpec((n_e, d), lambda i: (0, 0)),
        ],
        out_specs=[pl.BlockSpec((1, TOP_K, tm), lambda i: (i, 0, 0)),
                   pl.BlockSpec((1, TOP_K, tm), lambda i: (i, 0, 0))],
        out_shape=[jax.ShapeDtypeStruct((nt, TOP_K, tm), jnp.int32),
                   jax.ShapeDtypeStruct((nt, TOP_K, tm), F32)],
        compiler_params=_params("parallel"),
        name="router",
    )(h, g_ffn.reshape(1, d), w_router.T.astype(F32))
    idx = idx.transpose(0, 2, 1).reshape(t, TOP_K)
    wgt = wgt.transpose(0, 2, 1).reshape(t, TOP_K)
    return idx, wgt


def _scatter_rows_body(zero_start_ref, s1_ref, s2_ref, h_ref, xs_ref, zeros_ref, sem):
    tm = h_ref.shape[0]

    @pl.when(pl.program_id(0) == 0)
    def _():
        zeros_ref[...] = jnp.zeros_like(zeros_ref)

        def fill(k, c):
            dst = xs_ref.at[pl.ds(pl.multiple_of(zero_start_ref[k], SUBLANES), zeros_ref.shape[0])]
            cp = pltpu.make_async_copy(zeros_ref, dst, sem)
            cp.start()
            cp.wait()
            return c

        lax.fori_loop(0, zero_start_ref.shape[0], fill, 0)

    def issue(r, c):
        src = h_ref.at[pl.ds(r, 1)]
        pltpu.make_async_copy(src, xs_ref.at[pl.ds(s1_ref[r], 1)], sem).start(priority=0)
        pltpu.make_async_copy(src, xs_ref.at[pl.ds(s2_ref[r], 1)], sem).start(priority=1)
        return c

    lax.fori_loop(0, tm, issue, 0, unroll=8)
    pltpu.make_async_copy(h_ref, xs_ref.at[pl.ds(0, tm)], sem).wait()
    pltpu.make_async_copy(h_ref, xs_ref.at[pl.ds(0, tm)], sem).wait()


def _scatter_rows(h, slot1, slot2, zero_start, cap, zero_rows):
    t, d = h.shape
    tm = TM_SCATTER
    smem_tile = pl.BlockSpec((tm,), lambda i, zs: (i,), memory_space=pltpu.SMEM)
    grid_spec = pltpu.PrefetchScalarGridSpec(
        num_scalar_prefetch=1,
        grid=(t // tm,),
        in_specs=[smem_tile, smem_tile, pl.BlockSpec((tm, d), lambda i, zs: (i, 0))],
        out_specs=pl.BlockSpec(memory_space=pl.ANY),
        scratch_shapes=[pltpu.VMEM((zero_rows, d), h.dtype), pltpu.SemaphoreType.DMA(())],
    )
    return pl.pallas_call(
        _scatter_rows_body,
        grid_spec=grid_spec,
        out_shape=jax.ShapeDtypeStruct((cap, d), h.dtype),
        compiler_params=_params("arbitrary"),
        name="scatter_rows",
    )(zero_start, slot1, slot2, h)


def _moe_ffn_body(tile_expert_ref, tile_valid_ref, x_ref, g_ref, wg_ref, wu_ref, wd_ref, o_ref, xn_ref, act_ref):
    i = pl.program_id(0)
    j = pl.program_id(1)
    last = j == pl.num_programs(1) - 1
    valid = tile_valid_ref[i] > 0

    @pl.when(valid)
    def _():
        @pl.when(j == 0)
        def _():
            xn_ref[...] = _rms(x_ref[...], g_ref[...]).astype(xn_ref.dtype)

        _swiglu_hidden(xn_ref[...], wg_ref.at[0], wu_ref.at[0], act_ref, j)

        @pl.when(last)
        def _():
            o_ref[...] = _swiglu_down(act_ref, wd_ref.at[0])

    @pl.when(last & jnp.logical_not(valid))
    def _():
        o_ref[...] = jnp.zeros_like(o_ref)


def _moe_ffn(xs, g_ffn, tile_expert, tile_valid, w_gate, w_up, w_down):
    cap, d = xs.shape
    f = w_gate.shape[2]
    tm, tf = TM_MOE, TF_FFN
    grid_spec = pltpu.PrefetchScalarGridSpec(
        num_scalar_prefetch=2,
        grid=(cap // tm, f // tf),
        in_specs=[
            pl.BlockSpec((tm, d), lambda i, j, te, tv: (i, 0)),
            pl.BlockSpec((1, d), lambda i, j, te, tv: (0, 0)),
            pl.BlockSpec((1, d, tf), lambda i, j, te, tv: (te[i], 0, j)),
            pl.BlockSpec((1, d, tf), lambda i, j, te, tv: (te[i], 0, j)),
            pl.BlockSpec((1, f, d), lambda i, j, te, tv: (te[i], 0, 0)),
        ],
        out_specs=pl.BlockSpec((tm, d), lambda i, j, te, tv: (i, 0)),
        scratch_shapes=[pltpu.VMEM((tm, d), BF16), pltpu.VMEM((f // tf, tm, tf), BF16)],
    )
    return pl.pallas_call(
        _moe_ffn_body,
        grid_spec=grid_spec,
        out_shape=jax.ShapeDtypeStruct((cap, d), F32),
        compiler_params=_params("parallel", "arbitrary"),
        name="moe_ffn",
    )(tile_expert, tile_valid, xs, g_ffn.reshape(1, d), w_gate, w_up, w_down)


def _moe_plan(idx, n_experts, tm):
    t = idx.shape[0]
    pairs = t * TOP_K
    cap = pairs + n_experts * tm
    e_flat = idx.reshape(pairs)
    onehot = (e_flat[:, None] == jnp.arange(n_experts, dtype=jnp.int32)[None, :]).astype(jnp.int32)
    rank = jnp.sum(onehot * (jnp.cumsum(onehot, axis=0) - onehot), axis=1)
    counts = jnp.sum(onehot, axis=0)
    padded = ((counts + tm - 1) // tm) * tm
    ends = jnp.cumsum(padded)
    starts = ends - padded
    slot = jnp.sum(onehot * starts[None, :], axis=1) + rank
    tile_start = jnp.arange(cap // tm, dtype=jnp.int32) * tm
    tile_expert = jnp.sum((tile_start[:, None] >= ends[None, :]).astype(jnp.int32), axis=1)
    tile_valid = (tile_expert < n_experts).astype(jnp.int32)
    last_used = jnp.max(jnp.where(counts > 0, jnp.arange(n_experts, dtype=jnp.int32), 0))
    tile_expert = jnp.minimum(tile_expert, last_used).astype(jnp.int32)
    zero_rows = tm + SUBLANES
    tail = ends[-1] + jnp.arange(n_experts, dtype=jnp.int32) * tm
    first = jnp.concatenate([starts + counts, tail]) // SUBLANES * SUBLANES
    zero_start = jnp.minimum(first, cap - zero_rows).astype(jnp.int32)
    return slot.reshape(t, TOP_K).astype(jnp.int32), cap, tile_expert, tile_valid, zero_start, zero_rows


def _combine_body(s1_ref, s2_ref, h_ref, wgt_ref, g_ref, ys_ref, o_ref, y1_ref, y2_ref, sem, *, apply_norm):
    tm = h_ref.shape[0]

    def issue(r, c):
        pltpu.make_async_copy(ys_ref.at[pl.ds(s1_ref[r], 1)], y1_ref.at[pl.ds(r, 1)], sem).start(priority=0)
        pltpu.make_async_copy(ys_ref.at[pl.ds(s2_ref[r], 1)], y2_ref.at[pl.ds(r, 1)], sem).start(priority=1)
        return c

    lax.fori_loop(0, tm, issue, 0, unroll=8)
    pltpu.make_async_copy(ys_ref.at[pl.ds(0, tm)], y1_ref, sem).wait()
    pltpu.make_async_copy(ys_ref.at[pl.ds(0, tm)], y2_ref, sem).wait()
    w = wgt_ref[...]
    hn = h_ref[...] + w[:, 0:1] * y1_ref[...] + w[:, 1:2] * y2_ref[...]
    o_ref[...] = _rms(hn, g_ref[...]) if apply_norm else hn


def _combine(h, ys, slot, wgt, g, *, apply_norm):
    t, d = h.shape
    tm = TM_COMBINE
    smem_tile = pl.BlockSpec((tm,), lambda i: (i,), memory_space=pltpu.SMEM)
    row = pl.BlockSpec((tm, d), lambda i: (i, 0))
    return pl.pallas_call(
        functools.partial(_combine_body, apply_norm=apply_norm),
        grid=(t // tm,),
        in_specs=[smem_tile, smem_tile, row,
                  pl.BlockSpec((tm, TOP_K), lambda i: (i, 0)),
                  pl.BlockSpec((1, d), lambda i: (0, 0)),
                  pl.BlockSpec(memory_space=pl.ANY)],
        out_specs=row,
        out_shape=jax.ShapeDtypeStruct((t, d), F32),
        scratch_shapes=[pltpu.VMEM((tm, d), F32), pltpu.VMEM((tm, d), F32), pltpu.SemaphoreType.DMA(())],
        compiler_params=_params("arbitrary"),
        name="combine",
    )(slot[:, 0], slot[:, 1], h, wgt, g.reshape(1, d), ys)


def _plain_norm_body(h_ref, g_ref, o_ref):
    o_ref[...] = _rms(h_ref[...], g_ref[...])


def _plain_norm(h, g):
    t, d = h.shape
    tm = TM_NORM
    row = pl.BlockSpec((tm, d), lambda i: (i, 0))
    return pl.pallas_call(
        _plain_norm_body,
        grid=(t // tm,),
        in_specs=[row, pl.BlockSpec((1, d), lambda i: (0, 0))],
        out_specs=row,
        out_shape=jax.ShapeDtypeStruct((t, d), F32),
        compiler_params=_params("parallel"),
        name="plain_norm",
    )(h, g.reshape(1, d))


def kernel(x, norm_mix, w_in, ssm_lambda_re, ssm_lambda_im, ssm_log_dt, ssm_b_re, ssm_b_im, ssm_c_re,
           ssm_c_im, ssm_d, ssm_w_glu, conv_w, w_br_a, w_br_b, w_br_c, w_out, norm_ffn, dense_w_gate,
           dense_w_up, dense_w_down, moe_w_router, moe_w_gate, moe_w_up, moe_w_down, final_norm):
    batch, seq, d = x.shape
    depth = w_in.shape[0]
    t = batch * seq
    w_attn = w_br_a.shape[1]
    w_ssm = w_br_b.shape[1]
    w_conv = w_br_c.shape[1]
    n_sc = w_ssm + 3 * w_conv
    n_gate = N_BRANCH * d
    n_experts = moe_w_router.shape[-1]

    h = x.reshape(t, d)
    normed = False
    for i in range(depth):
        qt, k, vt, sc, gates = _inproj(h, norm_mix[i], w_in[i].astype(BF16), batch=batch, seq=seq,
                                       width=w_attn, n_sc=n_sc, n_gate=n_gate)
        attn = _sb_attention(qt, k, vt)
        tables = _ssm_tables(ssm_lambda_re[i], ssm_lambda_im[i], ssm_log_dt[i], ssm_b_re[i], ssm_b_im[i],
                             ssm_c_re[i], ssm_c_im[i], seq // SSM_CHUNK)
        yssm = _ssm_scan(sc[:, :w_ssm], tables, batch=batch, seq=seq)
        h, xn = _merge(attn, yssm, sc, gates, h, ssm_d[i], ssm_w_glu[i].astype(BF16), conv_w[i],
                       w_br_a[i].astype(BF16), w_br_b[i].astype(BF16), w_br_c[i].astype(BF16),
                       w_out[i].astype(BF16), norm_ffn[i], seq=seq)
        j = i // 2
        if i % 2 == 0:
            h = _dense_ffn(xn, h, dense_w_gate[j].astype(BF16), dense_w_up[j].astype(BF16),
                           dense_w_down[j].astype(BF16))
        else:
            idx, wgt = _router(h, norm_ffn[i], moe_w_router[j])
            slot, cap, tile_expert, tile_valid, zero_start, zero_rows = _moe_plan(idx, n_experts, TM_MOE)
            xs = _scatter_rows(h, slot[:, 0], slot[:, 1], zero_start, cap, zero_rows)
            ys = _moe_ffn(xs, norm_ffn[i], tile_expert, tile_valid, moe_w_gate[j].astype(BF16),
                          moe_w_up[j].astype(BF16), moe_w_down[j].astype(BF16))
            last = i == depth - 1
            h = _combine(h, ys, slot, wgt, final_norm if last else norm_ffn[i], apply_norm=last)
            normed = last
    out = h if normed else _plain_norm(h, final_norm)
    return out.reshape(batch, seq, d)
```

```python
import functools

import jax
import jax.numpy as jnp
from jax import lax
from jax.experimental import pallas as pl
from jax.experimental.pallas import tpu as pltpu

F32 = jnp.float32
BF16 = jnp.bfloat16

EPS = 1e-6
HEAD_DIM = 64
SSM_CHUNK = 16
SSM_MAX_RE = -1e-4
CONV_K = 3
N_BRANCH = 3
TOP_K = 2
LOG2E = 1.4426950408889634
MASKED_LOG_WEIGHT = -1e30

LANES = 128
SUBLANES = 8
MXU_DIM = 256
VMEM_LIMIT_BYTES = 56 * 1024 * 1024

TM_PROJ = 512
TQ_ATTN = MXU_DIM
TM_MERGE = 512
TM_FFN = 512
TF_FFN = 1792
TM_MOE = 512
TM_ROUTE = 512
TM_SCATTER = 256
TM_COMBINE = 256
TM_NORM = 512


def _params(*sem):
    return pltpu.CompilerParams(dimension_semantics=sem, vmem_limit_bytes=VMEM_LIMIT_BYTES)


def _rms(x, g):
    ms = jnp.mean(x * x, axis=-1, keepdims=True)
    return x * lax.rsqrt(ms + EPS) * g


def _inproj_body(x_ref, g_ref, w_ref, qt_ref, k_ref, vt_ref, sc_ref, gate_ref, *, width, n_sc, n_gate, q_scale):
    xn = _rms(x_ref[...], g_ref[...]).astype(BF16)
    tm = xn.shape[0]
    tq = vt_ref.shape[-1]
    chunk = 512

    def proj(c0, n):
        return jnp.dot(xn, w_ref[:, c0:c0 + n], preferred_element_type=F32)

    qt_ref[0] = (proj(0, width) * q_scale).T.astype(BF16)
    k_ref[...] = proj(width, width).astype(BF16)
    v = proj(2 * width, width)
    for kt in range(tm // tq):
        for p in range(width // LANES):
            vt_ref[0, p, kt] = v[kt * tq:(kt + 1) * tq, p * LANES:(p + 1) * LANES].T.astype(BF16)
    for c0 in range(0, n_sc, chunk):
        sc_ref[:, c0:c0 + chunk] = proj(3 * width + c0, chunk).astype(BF16)
    for c0 in range(0, n_gate, chunk):
        gate_ref[:, c0:c0 + chunk] = jax.nn.sigmoid(proj(3 * width + n_sc + c0, chunk)).astype(BF16)


def _inproj(h, g, w_bf16, *, batch, seq, width, n_sc, n_gate):
    t, d = h.shape
    tm = TM_PROJ
    tq = TQ_ATTN
    n_pairs = width // LANES
    tiles_per_seq = seq // tm
    body = functools.partial(_inproj_body, width=width, n_sc=n_sc, n_gate=n_gate, q_scale=HEAD_DIM ** -0.5)
    return pl.pallas_call(
        body,
        grid=(t // tm,),
        in_specs=[
            pl.BlockSpec((tm, d), lambda i: (i, 0)),
            pl.BlockSpec((1, d), lambda i: (0, 0)),
            pl.BlockSpec((d, 3 * width + n_sc + n_gate), lambda i: (0, 0), pipeline_mode=pl.Buffered(1)),
        ],
        out_specs=[
            pl.BlockSpec((1, width, tm), lambda i: (i // tiles_per_seq, 0, i % tiles_per_seq)),
            pl.BlockSpec((tm, width), lambda i: (i, 0)),
            pl.BlockSpec((1, n_pairs, tm // tq, LANES, tq),
                         lambda i: (i // tiles_per_seq, 0, i % tiles_per_seq, 0, 0)),
            pl.BlockSpec((tm, n_sc), lambda i: (i, 0)),
            pl.BlockSpec((tm, n_gate), lambda i: (i, 0)),
        ],
        out_shape=[
            jax.ShapeDtypeStruct((batch, width, seq), BF16),
            jax.ShapeDtypeStruct((t, width), BF16),
            jax.ShapeDtypeStruct((batch, n_pairs, seq // tq, LANES, tq), BF16),
            jax.ShapeDtypeStruct((t, n_sc), BF16),
            jax.ShapeDtypeStruct((t, n_gate), BF16),
        ],
        compiler_params=_params("parallel"),
        name="inproj",
    )(h, g.reshape(1, d), w_bf16)


def _sb_attn_body(qt_ref, k_ref, vt_ref, o_ref, acc_ref, carry_ref, lb_a_ref, sp_a_ref, lb_b_ref, sp_b_ref,
                  *, tq, n_pairs):
    i = pl.program_id(1)
    buf_a = (lb_a_ref, sp_a_ref)
    buf_b = (lb_b_ref, sp_b_ref)
    n_heads = 2 * n_pairs
    ones_rows = carry_ref.shape[1]
    key = lax.broadcasted_iota(jnp.int32, (tq, tq), 0)
    qry = lax.broadcasted_iota(jnp.int32, (tq, tq), 1)
    valid = key < qry
    er = lax.broadcasted_iota(jnp.int32, (tq + ones_rows, tq), 0)
    ec = lax.broadcasted_iota(jnp.int32, (tq + ones_rows, tq), 1)
    later_and_sum = jnp.where((ec > er) | (er >= tq), 1.0, 0.0).astype(BF16)

    feat = lax.broadcasted_iota(jnp.int32, (LANES, tq), 0)
    q_heads = []
    for p in range(n_pairs):
        qp = qt_ref[0, p * LANES:(p + 1) * LANES, :]
        zero = jnp.zeros_like(qp)
        q_heads.append(jnp.where(feat < HEAD_DIM, qp, zero))
        q_heads.append(jnp.where(feat >= HEAD_DIM, qp, zero))

    def scores(j, buf, masked):
        lb_ref, sp_ref = buf
        start = pl.multiple_of(j * tq, tq)
        for h in range(n_heads):
            p = h // 2
            kblk = k_ref[pl.ds(start, tq), p * LANES:(p + 1) * LANES]
            z = jnp.dot(kblk, q_heads[h], preferred_element_type=F32)
            l1p = jnp.log(1.0 + jnp.exp2(jnp.abs(z) * (-LOG2E)))
            sp = jnp.maximum(z, 0.0) + l1p
            lb = z - sp
            if masked:
                sp = jnp.where(valid, sp, 0.0)
                lb = jnp.where(valid, lb, MASKED_LOG_WEIGHT)
            lb_ref[h] = lb
            sp_ref[h] = sp.astype(BF16)

    def weights(j, buf):
        lb_ref, sp_ref = buf
        ws, sums = [], []
        for h in range(n_heads):
            ts = jnp.dot(later_and_sum, sp_ref[h], preferred_element_type=F32)
            ws.append(jnp.exp(lb_ref[h] - ts[0:tq]).astype(BF16))
            sums.append(ts[tq:tq + ones_rows])
        for h in range(n_heads):
            p, hh = divmod(h, 2)
            vt = vt_ref[0, p, j, hh * HEAD_DIM:(hh + 1) * HEAD_DIM, :]
            pv = jnp.dot(vt, ws[h], preferred_element_type=F32)
            carry = carry_ref[h]
            scale = jnp.exp(-carry[0:1, :])
            rows = slice(h * HEAD_DIM, (h + 1) * HEAD_DIM)
            acc_ref[rows, :] += pv * scale
            carry_ref[h] = carry + sums[h]

    def earlier_keys_matter():
        return (jnp.max(jnp.exp(-carry_ref[:, 0, :])) > 0.0).astype(jnp.int32)

    acc_ref[...] = jnp.zeros_like(acc_ref)
    carry_ref[...] = jnp.zeros_like(carry_ref)

    @pl.when(i == 0)
    def _():
        scores(0, buf_a, True)
        weights(0, buf_a)

    @pl.when(i > 0)
    def _():
        scores(i, buf_a, True)
        scores(i - 1, buf_b, False)
        weights(i, buf_a)
        weights(i - 1, buf_b)

    rest = i - 2

    @pl.when(jnp.where(rest >= 0, earlier_keys_matter(), 0) > 0)
    def _():
        scores(rest, buf_a, False)

        def two_tiles(c):
            m, _ = c
            j = rest - 1 - 2 * m
            scores(j, buf_b, False)
            weights(j + 1, buf_a)
            scores(j - 1, buf_a, False)
            weights(j, buf_b)
            return m + 1, earlier_keys_matter()

        _, alive = lax.while_loop(lambda c: (c[0] < rest // 2) & (c[1] > 0), two_tiles,
                                  (jnp.int32(0), jnp.int32(1)))

        @pl.when((alive > 0) & (rest % 2 == 1))
        def _():
            scores(0, buf_b, False)
            weights(1, buf_a)
            weights(0, buf_b)

        @pl.when((alive > 0) & (rest % 2 == 0))
        def _():
            weights(0, buf_a)

    for p in range(n_pairs):
        o_ref[:, p * LANES:(p + 1) * LANES] = acc_ref[p * LANES:(p + 1) * LANES, :].T.astype(o_ref.dtype)


def _sb_attention(qt, k, vt):
    batch, width, seq = qt.shape
    n_pairs, nq, _, tq = vt.shape[1:]
    n_heads = 2 * n_pairs
    return pl.pallas_call(
        functools.partial(_sb_attn_body, tq=tq, n_pairs=n_pairs),
        grid=(batch, nq),
        in_specs=[
            pl.BlockSpec((1, width, tq), lambda b, i: (b, 0, i)),
            pl.BlockSpec((seq, width), lambda b, i: (b, 0)),
            pl.BlockSpec((1, n_pairs, nq, LANES, tq), lambda b, i: (b, 0, 0, 0, 0)),
        ],
        out_specs=pl.BlockSpec((tq, width), lambda b, i: (b * nq + i, 0)),
        out_shape=jax.ShapeDtypeStruct((batch * seq, width), BF16),
        scratch_shapes=[pltpu.VMEM((width, tq), F32), pltpu.VMEM((n_heads, 16, tq), F32),
                        pltpu.VMEM((n_heads, tq, tq), F32), pltpu.VMEM((n_heads, tq, tq), BF16),
                        pltpu.VMEM((n_heads, tq, tq), F32), pltpu.VMEM((n_heads, tq, tq), BF16)],
        compiler_params=_params("parallel", "arbitrary"),
        name="sb_attention",
    )(qt, k, vt)


def _ssm_tables(lam_re, lam_im, log_dt, b_re, b_im, c_re, c_im, n_chunks):
    g, p = lam_re.shape
    c = SSM_CHUNK
    lr = jnp.minimum(lam_re.astype(F32), SSM_MAX_RE)
    li = lam_im.astype(F32)
    dt = jnp.exp(log_dt.astype(F32))[:, None]

    def lbar_pow(k):
        mag = jnp.exp(k * (lr * dt))
        ang = k * (li * dt)
        return mag * jnp.cos(ang), mag * jnp.sin(ang)

    ab_re, ab_im = lbar_pow(1.0)
    den = lr * lr + li * li
    nr = ab_re - 1.0
    ni = ab_im
    f_re = (nr * lr + ni * li) / den
    f_im = (ni * lr - nr * li) / den
    br = b_re.astype(F32)
    bi = b_im.astype(F32)
    bb_re = f_re[..., None] * br - f_im[..., None] * bi
    bb_im = f_re[..., None] * bi + f_im[..., None] * br
    cr = c_re.astype(F32)
    ci = c_im.astype(F32)

    eye = jnp.eye(g, dtype=F32)

    def block_diag(x):
        _, a, b = x.shape
        return (x[:, :, None, :] * eye[:, None, :, None]).reshape(g * a, g * b)

    b_in = jnp.concatenate([block_diag(bb_re.transpose(0, 2, 1)),
                            block_diag(bb_im.transpose(0, 2, 1))], axis=1)
    c_out = jnp.concatenate([block_diag(cr.transpose(0, 2, 1)),
                             block_diag(-ci.transpose(0, 2, 1))], axis=0)
    lbar = jnp.stack([ab_re.reshape(-1), ab_im.reshape(-1)], axis=0)

    steps = []
    s = 1
    while s < n_chunks:
        a_re, a_im = lbar_pow(float(c * s))
        steps.append(jnp.stack([a_re.reshape(-1, MXU_DIM), a_im.reshape(-1, MXU_DIM)], axis=1))
        s *= 2
    step = jnp.stack(steps, axis=1)
    return b_in.astype(BF16), c_out.astype(BF16), lbar, step


def _ssm_body(u_ref, b_ref, c_ref, lbar_ref, step_ref, y_ref, z_ref, *, n_chunks, batch):
    s = pl.program_id(0)
    c = u_ref.shape[0]
    n_blk = z_ref.shape[0]
    half = n_blk // 2
    bw = z_ref.shape[2]

    @pl.when(s == 0)
    def _():
        z_ref[...] = jnp.zeros_like(z_ref)

    def advance(ut):
        for pb in range(half):
            lo, hi = pb * bw, (pb + 1) * bw
            re = z_ref[pb]
            im = z_ref[half + pb]
            a_re = lbar_ref[0:1, lo:hi]
            a_im = lbar_ref[1:2, lo:hi]
            x_re = jnp.dot(ut, b_ref[:, lo:hi], preferred_element_type=F32)
            x_im = jnp.dot(ut, b_ref[:, half * bw + lo:half * bw + hi], preferred_element_type=F32)
            z_ref[pb] = a_re * re - a_im * im + x_re
            z_ref[half + pb] = a_re * im + a_im * re + x_im

    @pl.when(s < c)
    def _():
        advance(u_ref[s])

    @pl.when(s == c)
    def _():
        chunk = lax.broadcasted_iota(jnp.int32, (n_chunks, bw), 0)

        def shifted(x, n):
            return jnp.where(chunk >= n, pltpu.roll(x, n, axis=0), 0.0)

        def scan_block(idx, carry):
            b = idx // half
            pb = idx % half
            rows = pl.ds(pl.multiple_of(b * n_chunks, n_chunks), n_chunks)
            re = z_ref[pb, rows, :]
            im = z_ref[half + pb, rows, :]
            n = 1
            k = 0
            while n < n_chunks:
                a_re = step_ref[pb, k, 0:1, :]
                a_im = step_ref[pb, k, 1:2, :]
                re_s = shifted(re, n)
                im_s = shifted(im, n)
                re, im = re + a_re * re_s - a_im * im_s, im + a_re * im_s + a_im * re_s
                n *= 2
                k += 1
            z_ref[pb, rows, :] = shifted(re, 1)
            z_ref[half + pb, rows, :] = shifted(im, 1)
            return carry

        lax.fori_loop(0, batch * half, scan_block, 0)

    @pl.when(s >= c)
    def _():
        advance(u_ref[s - c])
        y = jnp.dot(z_ref[0].astype(BF16), c_ref[0:bw, :], preferred_element_type=F32)
        for blk in range(1, n_blk):
            y += jnp.dot(z_ref[blk].astype(BF16), c_ref[blk * bw:(blk + 1) * bw, :],
                         preferred_element_type=F32)
        y_ref[0] = y.astype(y_ref.dtype)


def _ssm_scan(u, tables, *, batch, seq):
    b_in, c_out, lbar, step = tables
    width = u.shape[1]
    c = SSM_CHUNK
    n_chunks = seq // c
    rows = batch * n_chunks
    n_blk = b_in.shape[1] // MXU_DIM
    u3 = u.reshape(rows, c, width).transpose(1, 0, 2)
    whole = lambda shape: pl.BlockSpec(shape, lambda s: (0,) * len(shape), pipeline_mode=pl.Buffered(1))
    y3 = pl.pallas_call(
        functools.partial(_ssm_body, n_chunks=n_chunks, batch=batch),
        grid=(2 * c,),
        in_specs=[whole(u3.shape), whole(b_in.shape), whole(c_out.shape), whole(lbar.shape), whole(step.shape)],
        out_specs=pl.BlockSpec((1, rows, width), lambda s: (jnp.maximum(s - c, 0), 0, 0)),
        out_shape=jax.ShapeDtypeStruct((c, rows, width), BF16),
        scratch_shapes=[pltpu.VMEM((n_blk, rows, MXU_DIM), F32)],
        compiler_params=_params("arbitrary"),
        name="ssm_scan",
    )(u3, b_in, c_out, lbar, step)
    return y3.transpose(1, 0, 2).reshape(batch * seq, width)


def _gelu_tanh(x):
    return 0.5 * x * (1.0 + jnp.tanh(0.7978845608028654 * (x + 0.044715 * (x * x * x))))


def _merge_body(attn_ref, yssm_ref, sc_ref, halo_ref, gate_ref, h_ref, dskip_ref, wglu_ref, convw_ref,
                wa_ref, wb_ref, wc_ref, wout_ref, gn_ref, hout_ref, xn_ref, *, tiles_per_seq, w_ssm, w_conv):
    i = pl.program_id(0)
    tm = h_ref.shape[0]
    d = h_ref.shape[1]

    u = sc_ref[:, 0:w_ssm].astype(F32)
    yb = _gelu_tanh(yssm_ref[...].astype(F32) + dskip_ref[...] * u)
    yb = yb * jax.nn.sigmoid(jnp.dot(yb.astype(BF16), wglu_ref[...], preferred_element_type=F32))
    y_b = jnp.dot(yb.astype(BF16), wb_ref[...], preferred_element_type=F32)

    o_h, o_b, o_c = w_ssm, w_ssm + w_conv, w_ssm + 2 * w_conv
    uc = sc_ref[:, o_c:o_c + w_conv].astype(F32) * sc_ref[:, o_h:o_h + w_conv].astype(F32)
    halo = halo_ref[:, o_c:o_c + w_conv].astype(F32) * halo_ref[:, o_h:o_h + w_conv].astype(F32)
    halo = jnp.where(i % tiles_per_seq == 0, 0.0, halo)
    row = lax.broadcasted_iota(jnp.int32, (tm, w_conv), 0)
    prev1 = jnp.where(row == 0, halo[SUBLANES - 1:SUBLANES, :], pltpu.roll(uc, 1, axis=0))
    prev2 = jnp.where(row == 0, halo[SUBLANES - 2:SUBLANES - 1, :],
                      jnp.where(row == 1, halo[SUBLANES - 1:SUBLANES, :], pltpu.roll(uc, 2, axis=0)))
    conv = convw_ref[0:1, :] * prev2 + convw_ref[1:2, :] * prev1 + convw_ref[2:3, :] * uc
    yc = sc_ref[:, o_b:o_b + w_conv].astype(F32) * conv
    y_c = jnp.dot(yc.astype(BF16), wc_ref[...], preferred_element_type=F32)

    y_a = jnp.dot(attn_ref[...], wa_ref[...], preferred_element_type=F32)

    merged = (gate_ref[:, 0:d].astype(F32) * y_a + gate_ref[:, d:2 * d].astype(F32) * y_b
              + gate_ref[:, 2 * d:3 * d].astype(F32) * y_c)
    hn = h_ref[...] + jnp.dot(merged.astype(BF16), wout_ref[...], preferred_element_type=F32)
    hout_ref[...] = hn
    xn_ref[...] = _rms(hn, gn_ref[...]).astype(xn_ref.dtype)


def _merge(attn, yssm, sc, gates, h, d_skip, w_glu, conv_w, w_a, w_b, w_c, w_out, g_ffn, *, seq, xn_dtype):
    t, d = h.shape
    tm = TM_MERGE
    w_ssm = yssm.shape[1]
    w_conv = conv_w.shape[1]
    w_attn = attn.shape[1]
    n_sc = sc.shape[1]
    body = functools.partial(_merge_body, tiles_per_seq=seq // tm, w_ssm=w_ssm, w_conv=w_conv)
    full = lambda shape: pl.BlockSpec(shape, lambda i: (0,) * len(shape))
    halo_blocks = tm // SUBLANES
    return pl.pallas_call(
        body,
        grid=(t // tm,),
        in_specs=[
            pl.BlockSpec((tm, w_attn), lambda i: (i, 0)),
            pl.BlockSpec((tm, w_ssm), lambda i: (i, 0)),
            pl.BlockSpec((tm, n_sc), lambda i: (i, 0)),
            pl.BlockSpec((SUBLANES, n_sc), lambda i: (jnp.maximum(i * halo_blocks - 1, 0), 0)),
            pl.BlockSpec((tm, N_BRANCH * d), lambda i: (i, 0)),
            pl.BlockSpec((tm, d), lambda i: (i, 0)),
            full((1, w_ssm)), full((w_ssm, w_ssm)), full((CONV_K, w_conv)),
            full((w_attn, d)), full((w_ssm, d)), full((w_conv, d)), full((d, d)), full((1, d)),
        ],
        out_specs=[pl.BlockSpec((tm, d), lambda i: (i, 0)), pl.BlockSpec((tm, d), lambda i: (i, 0))],
        out_shape=[jax.ShapeDtypeStruct((t, d), F32), jax.ShapeDtypeStruct((t, d), xn_dtype)],
        compiler_params=_params("parallel"),
        name="merge",
    )(attn, yssm, sc, sc, gates, h, d_skip.reshape(1, w_ssm).astype(F32), w_glu, conv_w.astype(F32),
      w_a, w_b, w_c, w_out, g_ffn.reshape(1, d))


def _swiglu_hidden(x, wg_ref, wu_ref, act_ref, j):
    gate = jnp.dot(x, wg_ref[...], preferred_element_type=F32)
    up = jnp.dot(x, wu_ref[...], preferred_element_type=F32)
    act_ref[j] = (gate * jax.nn.sigmoid(gate) * up).astype(act_ref.dtype)


def _swiglu_down(act_ref, wd_ref):
    act = jnp.concatenate([act_ref[jj] for jj in range(act_ref.shape[0])], axis=1)
    return jnp.dot(act, wd_ref[...], preferred_element_type=F32)


def _load_as_bf16(src_hbm, dst_ref, stage_ref, sem, n_chunks):
    rows = src_hbm.shape[0] // n_chunks

    def copy(c):
        return pltpu.make_async_copy(src_hbm.at[pl.ds(c * rows, rows)], stage_ref.at[c % 2], sem.at[c % 2])

    copy(0).start()
    for c in range(n_chunks):
        if c + 1 < n_chunks:
            copy(c + 1).start()
        copy(c).wait()
        dst_ref[c * rows:(c + 1) * rows, :] = stage_ref[c % 2].astype(dst_ref.dtype)


def _dense_ffn_body(x_ref, h_ref, wg_hbm, wu_hbm, wd_hbm, o_ref, wg_ref, wu_ref, wd_ref, act_ref,
                    stage_in_ref, stage_out_ref, sem, *, n_chunks):
    @pl.when(pl.program_id(0) == 0)
    def _():
        _load_as_bf16(wg_hbm, wg_ref, stage_in_ref, sem, n_chunks)
        _load_as_bf16(wu_hbm, wu_ref, stage_in_ref, sem, n_chunks)
        _load_as_bf16(wd_hbm, wd_ref, stage_out_ref, sem, n_chunks)

    x = x_ref[...]
    tf = act_ref.shape[2]
    for j in range(act_ref.shape[0]):
        _swiglu_hidden(x, wg_ref.at[:, j * tf:(j + 1) * tf], wu_ref.at[:, j * tf:(j + 1) * tf], act_ref, j)
    o_ref[...] = h_ref[...] + _swiglu_down(act_ref, wd_ref)


def _dense_ffn(xn, h, w_gate, w_up, w_down):
    t, d = h.shape
    f = w_gate.shape[1]
    tm, tf = TM_FFN, TF_FFN
    n_chunks = 16
    row = pl.BlockSpec((tm, d), lambda i: (i, 0))
    hbm = pl.BlockSpec(memory_space=pl.ANY)
    return pl.pallas_call(
        functools.partial(_dense_ffn_body, n_chunks=n_chunks),
        grid=(t // tm,),
        in_specs=[row, row, hbm, hbm, hbm],
        out_specs=row,
        out_shape=jax.ShapeDtypeStruct((t, d), F32),
        scratch_shapes=[pltpu.VMEM((d, f), BF16), pltpu.VMEM((d, f), BF16), pltpu.VMEM((f, d), BF16),
                        pltpu.VMEM((f // tf, tm, tf), BF16),
                        pltpu.VMEM((2, d // n_chunks, f), F32), pltpu.VMEM((2, f // n_chunks, d), F32),
                        pltpu.SemaphoreType.DMA((2,))],
        compiler_params=_params("arbitrary"),
        name="dense_ffn",
    )(xn, h, w_gate, w_up, w_down)


def _router_body(xn_ref, wr_ref, idx_ref, wgt_ref):
    logits = lax.dot_general(wr_ref[...], xn_ref[...], (((1,), (1,)), ((), ())), preferred_element_type=F32,
                             precision=lax.Precision.HIGHEST)
    n_e = logits.shape[0]
    e_idx = lax.broadcasted_iota(jnp.int32, logits.shape, 0)
    m1 = jnp.max(logits, axis=0, keepdims=True)
    i1 = jnp.min(jnp.where(logits == m1, e_idx, n_e), axis=0, keepdims=True)
    rest = jnp.where(e_idx == i1, -jnp.inf, logits)
    m2 = jnp.max(rest, axis=0, keepdims=True)
    i2 = jnp.min(jnp.where(rest == m2, e_idx, n_e), axis=0, keepdims=True)
    e2 = jnp.exp(m2 - m1)
    w1 = 1.0 / (1.0 + e2)
    idx_ref[0] = jnp.concatenate([i1, i2], axis=0)
    wgt_ref[0] = jnp.concatenate([w1, e2 * w1], axis=0)


def _router(xn, w_router):
    t, d = xn.shape
    n_e = w_router.shape[1]
    tm = TM_ROUTE
    nt = t // tm
    idx, wgt = pl.pallas_call(
        _router_body,
        grid=(nt,),
        in_specs=[
            pl.BlockSpec((tm, d), lambda i: (i, 0)),
            pl.BlockSpec((n_e, d), lambda i: (0, 0)),
        ],
        out_specs=[pl.BlockSpec((1, TOP_K, tm), lambda i: (i, 0, 0)),
                   pl.BlockSpec((1, TOP_K, tm), lambda i: (i, 0, 0))],
        out_shape=[jax.ShapeDtypeStruct((nt, TOP_K, tm), jnp.int32),
                   jax.ShapeDtypeStruct((nt, TOP_K, tm), F32)],
        compiler_params=_params("parallel"),
        name="router",
    )(xn, w_router.T.astype(F32))
    idx = idx.transpose(0, 2, 1).reshape(t, TOP_K)
    wgt = wgt.transpose(0, 2, 1).reshape(t, TOP_K)
    return idx, wgt


def _scatter_rows_body(zero_start_ref, s1_ref, s2_ref, h_ref, xs_ref, zeros_ref, sem):
    tm = h_ref.shape[0]

    @pl.when(pl.program_id(0) == 0)
    def _():
        zeros_ref[...] = jnp.zeros_like(zeros_ref)

        def fill(k, c):
            dst = xs_ref.at[pl.ds(pl.multiple_of(zero_start_ref[k], SUBLANES), zeros_ref.shape[0])]
            cp = pltpu.make_async_copy(zeros_ref, dst, sem)
            cp.start()
            cp.wait()
            return c

        lax.fori_loop(0, zero_start_ref.shape[0], fill, 0)

    def issue(r, c):
        src = h_ref.at[pl.ds(r, 1)]
        pltpu.make_async_copy(src, xs_ref.at[pl.ds(s1_ref[r], 1)], sem).start(priority=0)
        pltpu.make_async_copy(src, xs_ref.at[pl.ds(s2_ref[r], 1)], sem).start(priority=1)
        return c

    lax.fori_loop(0, tm, issue, 0, unroll=8)
    pltpu.make_async_copy(h_ref, xs_ref.at[pl.ds(0, tm)], sem).wait()
    pltpu.make_async_copy(h_ref, xs_ref.at[pl.ds(0, tm)], sem).wait()


def _scatter_rows(h, slot1, slot2, zero_start, cap, zero_rows):
    t, d = h.shape
    tm = TM_SCATTER
    smem_tile = pl.BlockSpec((tm,), lambda i, zs: (i,), memory_space=pltpu.SMEM)
    grid_spec = pltpu.PrefetchScalarGridSpec(
        num_scalar_prefetch=1,
        grid=(t // tm,),
        in_specs=[smem_tile, smem_tile, pl.BlockSpec((tm, d), lambda i, zs: (i, 0))],
        out_specs=pl.BlockSpec(memory_space=pl.ANY),
        scratch_shapes=[pltpu.VMEM((zero_rows, d), h.dtype), pltpu.SemaphoreType.DMA(())],
    )
    return pl.pallas_call(
        _scatter_rows_body,
        grid_spec=grid_spec,
        out_shape=jax.ShapeDtypeStruct((cap, d), h.dtype),
        compiler_params=_params("arbitrary"),
        name="scatter_rows",
    )(zero_start, slot1, slot2, h)


def _moe_ffn_body(tile_expert_ref, tile_valid_ref, x_ref, wg_ref, wu_ref, wd_ref, o_ref, act_ref):
    i = pl.program_id(0)
    j = pl.program_id(1)
    last = j == pl.num_programs(1) - 1
    valid = tile_valid_ref[i] > 0

    @pl.when(valid)
    def _():
        _swiglu_hidden(x_ref[...].astype(BF16), wg_ref.at[0], wu_ref.at[0], act_ref, j)

        @pl.when(last)
        def _():
            o_ref[...] = _swiglu_down(act_ref, wd_ref.at[0])

    @pl.when(last & jnp.logical_not(valid))
    def _():
        o_ref[...] = jnp.zeros_like(o_ref)


def _moe_ffn(xs, tile_expert, tile_valid, w_gate, w_up, w_down):
    cap, d = xs.shape
    f = w_gate.shape[2]
    tm, tf = TM_MOE, TF_FFN
    grid_spec = pltpu.PrefetchScalarGridSpec(
        num_scalar_prefetch=2,
        grid=(cap // tm, f // tf),
        in_specs=[
            pl.BlockSpec((tm, d), lambda i, j, te, tv: (i, 0)),
            pl.BlockSpec((1, d, tf), lambda i, j, te, tv: (te[i], 0, j)),
            pl.BlockSpec((1, d, tf), lambda i, j, te, tv: (te[i], 0, j)),
            pl.BlockSpec((1, f, d), lambda i, j, te, tv: (te[i], 0, 0)),
        ],
        out_specs=pl.BlockSpec((tm, d), lambda i, j, te, tv: (i, 0)),
        scratch_shapes=[pltpu.VMEM((f // tf, tm, tf), BF16)],
    )
    return pl.pallas_call(
        _moe_ffn_body,
        grid_spec=grid_spec,
        out_shape=jax.ShapeDtypeStruct((cap, d), F32),
        compiler_params=_params("parallel", "arbitrary"),
        name="moe_ffn",
    )(tile_expert, tile_valid, xs, w_gate, w_up, w_down)


def _moe_plan(idx, n_experts, tm):
    t = idx.shape[0]
    pairs = t * TOP_K
    cap = pairs + n_experts * tm
    e_flat = idx.reshape(pairs)
    onehot = (e_flat[:, None] == jnp.arange(n_experts, dtype=jnp.int32)[None, :]).astype(jnp.int32)
    rank = jnp.sum(onehot * (jnp.cumsum(onehot, axis=0) - onehot), axis=1)
    counts = jnp.sum(onehot, axis=0)
    padded = ((counts + tm - 1) // tm) * tm
    ends = jnp.cumsum(padded)
    starts = ends - padded
    slot = jnp.sum(onehot * starts[None, :], axis=1) + rank
    tile_start = jnp.arange(cap // tm, dtype=jnp.int32) * tm
    tile_expert = jnp.sum((tile_start[:, None] >= ends[None, :]).astype(jnp.int32), axis=1)
    tile_valid = (tile_expert < n_experts).astype(jnp.int32)
    last_used = jnp.max(jnp.where(counts > 0, jnp.arange(n_experts, dtype=jnp.int32), 0))
    tile_expert = jnp.minimum(tile_expert, last_used).astype(jnp.int32)
    zero_rows = tm + SUBLANES
    tail = ends[-1] + jnp.arange(n_experts, dtype=jnp.int32) * tm
    first = jnp.concatenate([starts + counts, tail]) // SUBLANES * SUBLANES
    zero_start = jnp.minimum(first, cap - zero_rows).astype(jnp.int32)
    return slot.reshape(t, TOP_K).astype(jnp.int32), cap, tile_expert, tile_valid, zero_start, zero_rows


def _combine_body(s1_ref, s2_ref, h_ref, wgt_ref, g_ref, ys_ref, o_ref, y1_ref, y2_ref, sem, *, apply_norm):
    tm = h_ref.shape[0]

    def issue(r, c):
        pltpu.make_async_copy(ys_ref.at[pl.ds(s1_ref[r], 1)], y1_ref.at[pl.ds(r, 1)], sem).start(priority=0)
        pltpu.make_async_copy(ys_ref.at[pl.ds(s2_ref[r], 1)], y2_ref.at[pl.ds(r, 1)], sem).start(priority=1)
        return c

    lax.fori_loop(0, tm, issue, 0, unroll=8)
    pltpu.make_async_copy(ys_ref.at[pl.ds(0, tm)], y1_ref, sem).wait()
    pltpu.make_async_copy(ys_ref.at[pl.ds(0, tm)], y2_ref, sem).wait()
    w = wgt_ref[...]
    hn = h_ref[...] + w[:, 0:1] * y1_ref[...] + w[:, 1:2] * y2_ref[...]
    o_ref[...] = _rms(hn, g_ref[...]) if apply_norm else hn


def _combine(h, ys, slot, wgt, g, *, apply_norm):
    t, d = h.shape
    tm = TM_COMBINE
    smem_tile = pl.BlockSpec((tm,), lambda i: (i,), memory_space=pltpu.SMEM)
    row = pl.BlockSpec((tm, d), lambda i: (i, 0))
    return pl.pallas_call(
        functools.partial(_combine_body, apply_norm=apply_norm),
        grid=(t // tm,),
        in_specs=[smem_tile, smem_tile, row,
                  pl.BlockSpec((tm, TOP_K), lambda i: (i, 0)),
                  pl.BlockSpec((1, d), lambda i: (0, 0)),
                  pl.BlockSpec(memory_space=pl.ANY)],
        out_specs=row,
        out_shape=jax.ShapeDtypeStruct((t, d), F32),
        scratch_shapes=[pltpu.VMEM((tm, d), F32), pltpu.VMEM((tm, d), F32), pltpu.SemaphoreType.DMA(())],
        compiler_params=_params("arbitrary"),
        name="combine",
    )(slot[:, 0], slot[:, 1], h, wgt, g.reshape(1, d), ys)


def _plain_norm_body(h_ref, g_ref, o_ref):
    o_ref[...] = _rms(h_ref[...], g_ref[...])


def _plain_norm(h, g):
    t, d = h.shape
    tm = TM_NORM
    row = pl.BlockSpec((tm, d), lambda i: (i, 0))
    return pl.pallas_call(
        _plain_norm_body,
        grid=(t // tm,),
        in_specs=[row, pl.BlockSpec((1, d), lambda i: (0, 0))],
        out_specs=row,
        out_shape=jax.ShapeDtypeStruct((t, d), F32),
        compiler_params=_params("parallel"),
        name="plain_norm",
    )(h, g.reshape(1, d))


def kernel(x, norm_mix, w_in, ssm_lambda_re, ssm_lambda_im, ssm_log_dt, ssm_b_re, ssm_b_im, ssm_c_re,
           ssm_c_im, ssm_d, ssm_w_glu, conv_w, w_br_a, w_br_b, w_br_c, w_out, norm_ffn, dense_w_gate,
           dense_w_up, dense_w_down, moe_w_router, moe_w_gate, moe_w_up, moe_w_down, final_norm):
    batch, seq, d = x.shape
    depth = w_in.shape[0]
    t = batch * seq
    w_attn = w_br_a.shape[1]
    w_ssm = w_br_b.shape[1]
    w_conv = w_br_c.shape[1]
    n_sc = w_ssm + 3 * w_conv
    n_gate = N_BRANCH * d
    n_experts = moe_w_router.shape[-1]

    h = x.reshape(t, d)
    normed = False
    for i in range(depth):
        qt, k, vt, sc, gates = _inproj(h, norm_mix[i], w_in[i].astype(BF16), batch=batch, seq=seq,
                                       width=w_attn, n_sc=n_sc, n_gate=n_gate)
        attn = _sb_attention(qt, k, vt)
        tables = _ssm_tables(ssm_lambda_re[i], ssm_lambda_im[i], ssm_log_dt[i], ssm_b_re[i], ssm_b_im[i],
                             ssm_c_re[i], ssm_c_im[i], seq // SSM_CHUNK)
        yssm = _ssm_scan(sc[:, :w_ssm], tables, batch=batch, seq=seq)
        dense = i % 2 == 0
        h, xn = _merge(attn, yssm, sc, gates, h, ssm_d[i], ssm_w_glu[i].astype(BF16), conv_w[i],
                       w_br_a[i].astype(BF16), w_br_b[i].astype(BF16), w_br_c[i].astype(BF16),
                       w_out[i].astype(BF16), norm_ffn[i], seq=seq, xn_dtype=BF16 if dense else F32)
        j = i // 2
        if dense:
            h = _dense_ffn(xn, h, dense_w_gate[j], dense_w_up[j], dense_w_down[j])
        else:
            idx, wgt = _router(xn, moe_w_router[j])
            slot, cap, tile_expert, tile_valid, zero_start, zero_rows = _moe_plan(idx, n_experts, TM_MOE)
            xs = _scatter_rows(xn, slot[:, 0], slot[:, 1], zero_start, cap, zero_rows)
            ys = _moe_ffn(xs, tile_expert, tile_valid, moe_w_gate[j].astype(BF16),
                          moe_w_up[j].astype(BF16), moe_w_down[j].astype(BF16))
            last = i == depth - 1
            h = _combine(h, ys, slot, wgt, final_norm if last else norm_ffn[i], apply_norm=last)
            normed = last
    out = h if normed else _plain_norm(h, final_norm)
    return out.reshape(batch, seq, d)
```

```python
import functools

import jax
import jax.numpy as jnp
from jax import lax
from jax.experimental import pallas as pl
from jax.experimental.pallas import tpu as pltpu

F32 = jnp.float32
BF16 = jnp.bfloat16

EPS = 1e-6
HEAD_DIM = 64
SSM_CHUNK = 16
SSM_MAX_RE = -1e-4
CONV_K = 3
N_BRANCH = 3
TOP_K = 2
LOG2E = 1.4426950408889634
MASKED_LOG_WEIGHT = -1e30

LANES = 128
SUBLANES = 8
MXU_DIM = 256
VMEM_LIMIT_BYTES = 56 * 1024 * 1024

TM_PROJ = 512
TQ_ATTN = MXU_DIM
TM_MERGE = 512
TM_FFN = 512
TF_FFN = 1792
TM_MOE = 512
TM_SCATTER = 512
TM_COMBINE = 512
TM_NORM = 512


def _params(*sem):
    return pltpu.CompilerParams(dimension_semantics=sem, vmem_limit_bytes=VMEM_LIMIT_BYTES)


def _rms(x, g):
    ms = jnp.mean(x * x, axis=-1, keepdims=True)
    return x * lax.rsqrt(ms + EPS) * g


def _inproj_body(x_ref, g_ref, w_ref, qt_ref, k_ref, vt_ref, sc_ref, gate_ref, *, width, n_sc, n_gate, q_scale):
    xn = _rms(x_ref[...], g_ref[...]).astype(BF16)
    tm = xn.shape[0]
    tq = vt_ref.shape[-1]
    chunk = 512

    def proj(c0, n):
        return jnp.dot(xn, w_ref[:, c0:c0 + n], preferred_element_type=F32)

    qt_ref[0] = (proj(0, width) * q_scale).T.astype(BF16)
    k_ref[...] = proj(width, width).astype(BF16)
    v = proj(2 * width, width)
    for kt in range(tm // tq):
        for p in range(width // LANES):
            vt_ref[0, p, kt] = v[kt * tq:(kt + 1) * tq, p * LANES:(p + 1) * LANES].T.astype(BF16)
    for c0 in range(0, n_sc, chunk):
        sc_ref[:, c0:c0 + chunk] = proj(3 * width + c0, chunk).astype(BF16)
    for c0 in range(0, n_gate, chunk):
        gate_ref[:, c0:c0 + chunk] = jax.nn.sigmoid(proj(3 * width + n_sc + c0, chunk)).astype(BF16)


def _inproj(h, g, w_bf16, *, batch, seq, width, n_sc, n_gate):
    t, d = h.shape
    tm = TM_PROJ
    tq = TQ_ATTN
    n_pairs = width // LANES
    tiles_per_seq = seq // tm
    body = functools.partial(_inproj_body, width=width, n_sc=n_sc, n_gate=n_gate, q_scale=HEAD_DIM ** -0.5)
    return pl.pallas_call(
        body,
        grid=(t // tm,),
        in_specs=[
            pl.BlockSpec((tm, d), lambda i: (i, 0)),
            pl.BlockSpec((1, d), lambda i: (0, 0)),
            pl.BlockSpec((d, 3 * width + n_sc + n_gate), lambda i: (0, 0), pipeline_mode=pl.Buffered(1)),
        ],
        out_specs=[
            pl.BlockSpec((1, width, tm), lambda i: (i // tiles_per_seq, 0, i % tiles_per_seq)),
            pl.BlockSpec((tm, width), lambda i: (i, 0)),
            pl.BlockSpec((1, n_pairs, tm // tq, LANES, tq),
                         lambda i: (i // tiles_per_seq, 0, i % tiles_per_seq, 0, 0)),
            pl.BlockSpec((tm, n_sc), lambda i: (i, 0)),
            pl.BlockSpec((tm, n_gate), lambda i: (i, 0)),
        ],
        out_shape=[
            jax.ShapeDtypeStruct((batch, width, seq), BF16),
            jax.ShapeDtypeStruct((t, width), BF16),
            jax.ShapeDtypeStruct((batch, n_pairs, seq // tq, LANES, tq), BF16),
            jax.ShapeDtypeStruct((t, n_sc), BF16),
            jax.ShapeDtypeStruct((t, n_gate), BF16),
        ],
        compiler_params=_params("parallel"),
        name="inproj",
    )(h, g.reshape(1, d), w_bf16)


def _sb_attn_body(qt_ref, k_ref, vt_ref, o_ref, acc_ref, carry_ref, lb_a_ref, sp_a_ref, lb_b_ref, sp_b_ref,
                  *, tq, n_pairs):
    i = pl.program_id(1)
    buf_a = (lb_a_ref, sp_a_ref)
    buf_b = (lb_b_ref, sp_b_ref)
    n_heads = 2 * n_pairs
    ones_rows = carry_ref.shape[1]
    key = lax.broadcasted_iota(jnp.int32, (tq, tq), 0)
    qry = lax.broadcasted_iota(jnp.int32, (tq, tq), 1)
    valid = key < qry
    er = lax.broadcasted_iota(jnp.int32, (tq + ones_rows, tq), 0)
    ec = lax.broadcasted_iota(jnp.int32, (tq + ones_rows, tq), 1)
    later_and_sum = jnp.where((ec > er) | (er >= tq), 1.0, 0.0).astype(BF16)

    feat = lax.broadcasted_iota(jnp.int32, (LANES, tq), 0)
    q_heads = []
    for p in range(n_pairs):
        qp = qt_ref[0, p * LANES:(p + 1) * LANES, :]
        zero = jnp.zeros_like(qp)
        q_heads.append(jnp.where(feat < HEAD_DIM, qp, zero))
        q_heads.append(jnp.where(feat >= HEAD_DIM, qp, zero))

    def scores(j, buf, masked):
        lb_ref, sp_ref = buf
        start = pl.multiple_of(j * tq, tq)
        for h in range(n_heads):
            p = h // 2
            kblk = k_ref[pl.ds(start, tq), p * LANES:(p + 1) * LANES]
            z = jnp.dot(kblk, q_heads[h], preferred_element_type=F32)
            l1p = jnp.log(1.0 + jnp.exp2(jnp.abs(z) * (-LOG2E)))
            sp = jnp.maximum(z, 0.0) + l1p
            lb = z - sp
            if masked:
                sp = jnp.where(valid, sp, 0.0)
                lb = jnp.where(valid, lb, MASKED_LOG_WEIGHT)
            lb_ref[h] = lb
            sp_ref[h] = sp.astype(BF16)

    def weights(j, buf):
        lb_ref, sp_ref = buf
        ws, sums = [], []
        for h in range(n_heads):
            ts = jnp.dot(later_and_sum, sp_ref[h], preferred_element_type=F32)
            ws.append(jnp.exp(lb_ref[h] - ts[0:tq]).astype(BF16))
            sums.append(ts[tq:tq + ones_rows])
        for h in range(n_heads):
            p, hh = divmod(h, 2)
            vt = vt_ref[0, p, j, hh * HEAD_DIM:(hh + 1) * HEAD_DIM, :]
            pv = jnp.dot(vt, ws[h], preferred_element_type=F32)
            carry = carry_ref[h]
            scale = jnp.exp(-carry[0:1, :])
            rows = slice(h * HEAD_DIM, (h + 1) * HEAD_DIM)
            acc_ref[rows, :] += pv * scale
            carry_ref[h] = carry + sums[h]

    def earlier_keys_matter():
        return (jnp.max(jnp.exp(-carry_ref[:, 0, :])) > 0.0).astype(jnp.int32)

    acc_ref[...] = jnp.zeros_like(acc_ref)
    carry_ref[...] = jnp.zeros_like(carry_ref)

    @pl.when(i == 0)
    def _():
        scores(0, buf_a, True)
        weights(0, buf_a)

    @pl.when(i > 0)
    def _():
        scores(i, buf_a, True)
        scores(i - 1, buf_b, False)
        weights(i, buf_a)
        weights(i - 1, buf_b)

    rest = i - 2

    @pl.when(jnp.where(rest >= 0, earlier_keys_matter(), 0) > 0)
    def _():
        scores(rest, buf_a, False)

        def two_tiles(c):
            m, _ = c
            j = rest - 1 - 2 * m
            scores(j, buf_b, False)
            weights(j + 1, buf_a)
            scores(j - 1, buf_a, False)
            weights(j, buf_b)
            return m + 1, earlier_keys_matter()

        _, alive = lax.while_loop(lambda c: (c[0] < rest // 2) & (c[1] > 0), two_tiles,
                                  (jnp.int32(0), jnp.int32(1)))

        @pl.when((alive > 0) & (rest % 2 == 1))
        def _():
            scores(0, buf_b, False)
            weights(1, buf_a)
            weights(0, buf_b)

        @pl.when((alive > 0) & (rest % 2 == 0))
        def _():
            weights(0, buf_a)

    for p in range(n_pairs):
        o_ref[:, p * LANES:(p + 1) * LANES] = acc_ref[p * LANES:(p + 1) * LANES, :].T.astype(o_ref.dtype)


def _sb_attention(qt, k, vt):
    batch, width, seq = qt.shape
    n_pairs, nq, _, tq = vt.shape[1:]
    n_heads = 2 * n_pairs
    return pl.pallas_call(
        functools.partial(_sb_attn_body, tq=tq, n_pairs=n_pairs),
        grid=(batch, nq),
        in_specs=[
            pl.BlockSpec((1, width, tq), lambda b, i: (b, 0, i)),
            pl.BlockSpec((seq, width), lambda b, i: (b, 0)),
            pl.BlockSpec((1, n_pairs, nq, LANES, tq), lambda b, i: (b, 0, 0, 0, 0)),
        ],
        out_specs=pl.BlockSpec((tq, width), lambda b, i: (b * nq + i, 0)),
        out_shape=jax.ShapeDtypeStruct((batch * seq, width), BF16),
        scratch_shapes=[pltpu.VMEM((width, tq), F32), pltpu.VMEM((n_heads, 16, tq), F32),
                        pltpu.VMEM((n_heads, tq, tq), F32), pltpu.VMEM((n_heads, tq, tq), BF16),
                        pltpu.VMEM((n_heads, tq, tq), F32), pltpu.VMEM((n_heads, tq, tq), BF16)],
        compiler_params=_params("parallel", "arbitrary"),
        name="sb_attention",
    )(qt, k, vt)


def _ssm_tables(lam_re, lam_im, log_dt, b_re, b_im, c_re, c_im, n_chunks):
    g, p = lam_re.shape
    c = SSM_CHUNK
    lr = jnp.minimum(lam_re.astype(F32), SSM_MAX_RE)
    li = lam_im.astype(F32)
    dt = jnp.exp(log_dt.astype(F32))[:, None]

    def lbar_pow(k):
        mag = jnp.exp(k * (lr * dt))
        ang = k * (li * dt)
        return mag * jnp.cos(ang), mag * jnp.sin(ang)

    ab_re, ab_im = lbar_pow(1.0)
    den = lr * lr + li * li
    nr = ab_re - 1.0
    ni = ab_im
    f_re = (nr * lr + ni * li) / den
    f_im = (ni * lr - nr * li) / den
    br = b_re.astype(F32)
    bi = b_im.astype(F32)
    bb_re = f_re[..., None] * br - f_im[..., None] * bi
    bb_im = f_re[..., None] * bi + f_im[..., None] * br
    cr = c_re.astype(F32)
    ci = c_im.astype(F32)

    eye = jnp.eye(g, dtype=F32)

    def block_diag(x):
        _, a, b = x.shape
        return (x[:, :, None, :] * eye[:, None, :, None]).reshape(g * a, g * b)

    b_in = jnp.concatenate([block_diag(bb_re.transpose(0, 2, 1)),
                            block_diag(bb_im.transpose(0, 2, 1))], axis=1)
    c_out = jnp.concatenate([block_diag(cr.transpose(0, 2, 1)),
                             block_diag(-ci.transpose(0, 2, 1))], axis=0)
    lbar = jnp.stack([ab_re.reshape(-1), ab_im.reshape(-1)], axis=0)

    steps = []
    s = 1
    while s < n_chunks:
        a_re, a_im = lbar_pow(float(c * s))
        steps.append(jnp.stack([a_re.reshape(-1, MXU_DIM), a_im.reshape(-1, MXU_DIM)], axis=1))
        s *= 2
    step = jnp.stack(steps, axis=1)
    return b_in.astype(BF16), c_out.astype(BF16), lbar, step


def _ssm_body(u_ref, b_ref, c_ref, lbar_ref, step_ref, y_ref, z_ref, *, n_chunks, batch):
    s = pl.program_id(0)
    c = u_ref.shape[0]
    n_blk = z_ref.shape[0]
    half = n_blk // 2
    bw = z_ref.shape[2]

    @pl.when(s == 0)
    def _():
        z_ref[...] = jnp.zeros_like(z_ref)

    def advance(ut):
        for pb in range(half):
            lo, hi = pb * bw, (pb + 1) * bw
            re = z_ref[pb]
            im = z_ref[half + pb]
            a_re = lbar_ref[0:1, lo:hi]
            a_im = lbar_ref[1:2, lo:hi]
            x_re = jnp.dot(ut, b_ref[:, lo:hi], preferred_element_type=F32)
            x_im = jnp.dot(ut, b_ref[:, half * bw + lo:half * bw + hi], preferred_element_type=F32)
            z_ref[pb] = a_re * re - a_im * im + x_re
            z_ref[half + pb] = a_re * im + a_im * re + x_im

    @pl.when(s < c)
    def _():
        advance(u_ref[s])

    @pl.when(s == c)
    def _():
        chunk = lax.broadcasted_iota(jnp.int32, (n_chunks, bw), 0)

        def shifted(x, n):
            return jnp.where(chunk >= n, pltpu.roll(x, n, axis=0), 0.0)

        def scan_block(idx, carry):
            b = idx // half
            pb = idx % half
            rows = pl.ds(pl.multiple_of(b * n_chunks, n_chunks), n_chunks)
            re = z_ref[pb, rows, :]
            im = z_ref[half + pb, rows, :]
            n = 1
            k = 0
            while n < n_chunks:
                a_re = step_ref[pb, k, 0:1, :]
                a_im = step_ref[pb, k, 1:2, :]
                re_s = shifted(re, n)
                im_s = shifted(im, n)
                re, im = re + a_re * re_s - a_im * im_s, im + a_re * im_s + a_im * re_s
                n *= 2
                k += 1
            z_ref[pb, rows, :] = shifted(re, 1)
            z_ref[half + pb, rows, :] = shifted(im, 1)
            return carry

        lax.fori_loop(0, batch * half, scan_block, 0)

    @pl.when(s >= c)
    def _():
        advance(u_ref[s - c])
        y = jnp.dot(z_ref[0].astype(BF16), c_ref[0:bw, :], preferred_element_type=F32)
        for blk in range(1, n_blk):
            y += jnp.dot(z_ref[blk].astype(BF16), c_ref[blk * bw:(blk + 1) * bw, :],
                         preferred_element_type=F32)
        y_ref[0] = y.astype(y_ref.dtype)


def _ssm_scan(u, tables, *, batch, seq):
    b_in, c_out, lbar, step = tables
    width = u.shape[1]
    c = SSM_CHUNK
    n_chunks = seq // c
    rows = batch * n_chunks
    n_blk = b_in.shape[1] // MXU_DIM
    u3 = u.reshape(rows, c, width).transpose(1, 0, 2)
    whole = lambda shape: pl.BlockSpec(shape, lambda s: (0,) * len(shape), pipeline_mode=pl.Buffered(1))
    y3 = pl.pallas_call(
        functools.partial(_ssm_body, n_chunks=n_chunks, batch=batch),
        grid=(2 * c,),
        in_specs=[whole(u3.shape), whole(b_in.shape), whole(c_out.shape), whole(lbar.shape), whole(step.shape)],
        out_specs=pl.BlockSpec((1, rows, width), lambda s: (jnp.maximum(s - c, 0), 0, 0)),
        out_shape=jax.ShapeDtypeStruct((c, rows, width), BF16),
        scratch_shapes=[pltpu.VMEM((n_blk, rows, MXU_DIM), F32)],
        compiler_params=_params("arbitrary"),
        name="ssm_scan",
    )(u3, b_in, c_out, lbar, step)
    return y3.transpose(1, 0, 2).reshape(batch * seq, width)


def _gelu_tanh(x):
    return 0.5 * x * (1.0 + jnp.tanh(0.7978845608028654 * (x + 0.044715 * (x * x * x))))


def _top2(xn, wr):
    logits = lax.dot_general(wr, xn, (((1,), (1,)), ((), ())), preferred_element_type=F32,
                             precision=lax.Precision.HIGHEST)
    n_e = logits.shape[0]
    e_idx = lax.broadcasted_iota(jnp.int32, logits.shape, 0)
    m1 = jnp.max(logits, axis=0, keepdims=True)
    i1 = jnp.min(jnp.where(logits == m1, e_idx, n_e), axis=0, keepdims=True)
    rest = jnp.where(e_idx == i1, -jnp.inf, logits)
    m2 = jnp.max(rest, axis=0, keepdims=True)
    i2 = jnp.min(jnp.where(rest == m2, e_idx, n_e), axis=0, keepdims=True)
    e2 = jnp.exp(m2 - m1)
    w1 = 1.0 / (1.0 + e2)
    return jnp.concatenate([i1, i2], axis=0), jnp.concatenate([w1, e2 * w1], axis=0)


def _merge_body(attn_ref, yssm_ref, sc_ref, halo_ref, gate_ref, h_ref, dskip_ref, wglu_ref, convw_ref,
                wa_ref, wb_ref, wc_ref, wout_ref, gn_ref, *rest, tiles_per_seq, w_ssm, w_conv, route):
    if route:
        wr_ref, hout_ref, xn_ref, idx_ref, wgt_ref = rest
    else:
        hout_ref, xn_ref = rest
    i = pl.program_id(0)
    tm = h_ref.shape[0]
    d = h_ref.shape[1]

    u = sc_ref[:, 0:w_ssm].astype(F32)
    yb = _gelu_tanh(yssm_ref[...].astype(F32) + dskip_ref[...] * u)
    yb = yb * jax.nn.sigmoid(jnp.dot(yb.astype(BF16), wglu_ref[...], preferred_element_type=F32))
    y_b = jnp.dot(yb.astype(BF16), wb_ref[...], preferred_element_type=F32)

    o_h, o_b, o_c = w_ssm, w_ssm + w_conv, w_ssm + 2 * w_conv
    uc = sc_ref[:, o_c:o_c + w_conv].astype(F32) * sc_ref[:, o_h:o_h + w_conv].astype(F32)
    halo = halo_ref[:, o_c:o_c + w_conv].astype(F32) * halo_ref[:, o_h:o_h + w_conv].astype(F32)
    halo = jnp.where(i % tiles_per_seq == 0, 0.0, halo)
    row = lax.broadcasted_iota(jnp.int32, (tm, w_conv), 0)
    prev1 = jnp.where(row == 0, halo[SUBLANES - 1:SUBLANES, :], pltpu.roll(uc, 1, axis=0))
    prev2 = jnp.where(row == 0, halo[SUBLANES - 2:SUBLANES - 1, :],
                      jnp.where(row == 1, halo[SUBLANES - 1:SUBLANES, :], pltpu.roll(uc, 2, axis=0)))
    conv = convw_ref[0:1, :] * prev2 + convw_ref[1:2, :] * prev1 + convw_ref[2:3, :] * uc
    yc = sc_ref[:, o_b:o_b + w_conv].astype(F32) * conv
    y_c = jnp.dot(yc.astype(BF16), wc_ref[...], preferred_element_type=F32)

    y_a = jnp.dot(attn_ref[...], wa_ref[...], preferred_element_type=F32)

    merged = (gate_ref[:, 0:d].astype(F32) * y_a + gate_ref[:, d:2 * d].astype(F32) * y_b
              + gate_ref[:, 2 * d:3 * d].astype(F32) * y_c)
    hn = h_ref[...] + jnp.dot(merged.astype(BF16), wout_ref[...], preferred_element_type=F32)
    hout_ref[...] = hn
    xn = _rms(hn, gn_ref[...])
    xn_ref[...] = xn.astype(xn_ref.dtype)
    if route:
        idx, wgt = _top2(xn, wr_ref[...])
        idx_ref[0] = idx
        wgt_ref[0] = wgt


def _merge(attn, yssm, sc, gates, h, d_skip, w_glu, conv_w, w_a, w_b, w_c, w_out, g_ffn, w_router, *, seq):
    t, d = h.shape
    tm = TM_MERGE
    nt = t // tm
    w_ssm = yssm.shape[1]
    w_conv = conv_w.shape[1]
    w_attn = attn.shape[1]
    n_sc = sc.shape[1]
    route = w_router is not None
    body = functools.partial(_merge_body, tiles_per_seq=seq // tm, w_ssm=w_ssm, w_conv=w_conv, route=route)
    full = lambda shape: pl.BlockSpec(shape, lambda i: (0,) * len(shape))
    row = pl.BlockSpec((tm, d), lambda i: (i, 0))
    halo_blocks = tm // SUBLANES
    in_specs = [
        pl.BlockSpec((tm, w_attn), lambda i: (i, 0)),
        pl.BlockSpec((tm, w_ssm), lambda i: (i, 0)),
        pl.BlockSpec((tm, n_sc), lambda i: (i, 0)),
        pl.BlockSpec((SUBLANES, n_sc), lambda i: (jnp.maximum(i * halo_blocks - 1, 0), 0)),
        pl.BlockSpec((tm, N_BRANCH * d), lambda i: (i, 0)),
        row,
        full((1, w_ssm)), full((w_ssm, w_ssm)), full((CONV_K, w_conv)),
        full((w_attn, d)), full((w_ssm, d)), full((w_conv, d)), full((d, d)), full((1, d)),
    ]
    args = [attn, yssm, sc, sc, gates, h, d_skip.reshape(1, w_ssm).astype(F32), w_glu, conv_w.astype(F32),
            w_a, w_b, w_c, w_out, g_ffn.reshape(1, d)]
    out_specs = [row, row]
    out_shape = [jax.ShapeDtypeStruct((t, d), F32), jax.ShapeDtypeStruct((t, d), F32 if route else BF16)]
    if route:
        n_e = w_router.shape[1]
        in_specs.append(full((n_e, d)))
        args.append(w_router.T.astype(F32))
        lanes = pl.BlockSpec((1, TOP_K, tm), lambda i: (i, 0, 0))
        out_specs += [lanes, lanes]
        out_shape += [jax.ShapeDtypeStruct((nt, TOP_K, tm), jnp.int32), jax.ShapeDtypeStruct((nt, TOP_K, tm), F32)]
    outs = pl.pallas_call(
        body,
        grid=(nt,),
        in_specs=in_specs,
        out_specs=out_specs,
        out_shape=out_shape,
        compiler_params=_params("parallel"),
        name="merge",
    )(*args)
    if not route:
        return outs
    hn, xn, idx, wgt = outs
    return hn, xn, idx.transpose(0, 2, 1).reshape(t, TOP_K), wgt.transpose(0, 2, 1).reshape(t, TOP_K)


def _swiglu_hidden(x, wg_ref, wu_ref, act_ref, j):
    gate = jnp.dot(x, wg_ref[...], preferred_element_type=F32)
    up = jnp.dot(x, wu_ref[...], preferred_element_type=F32)
    act_ref[j] = (gate * jax.nn.sigmoid(gate) * up).astype(act_ref.dtype)


def _swiglu_down(act_ref, wd_ref):
    act = jnp.concatenate([act_ref[jj] for jj in range(act_ref.shape[0])], axis=1)
    return jnp.dot(act, wd_ref[...], preferred_element_type=F32)


def _load_as_bf16(src_hbm, dst_ref, stage_ref, sem, n_chunks):
    rows = src_hbm.shape[0] // n_chunks

    def copy(c):
        return pltpu.make_async_copy(src_hbm.at[pl.ds(c * rows, rows)], stage_ref.at[c % 2], sem.at[c % 2])

    copy(0).start()
    for c in range(n_chunks):
        if c + 1 < n_chunks:
            copy(c + 1).start()
        copy(c).wait()
        dst_ref[c * rows:(c + 1) * rows, :] = stage_ref[c % 2].astype(dst_ref.dtype)


def _dense_ffn_body(x_ref, h_ref, wg_hbm, wu_hbm, wd_hbm, o_ref, wg_ref, wu_ref, wd_ref, act_ref,
                    stage_in_ref, stage_out_ref, sem, *, n_chunks):
    @pl.when(pl.program_id(0) == 0)
    def _():
        _load_as_bf16(wg_hbm, wg_ref, stage_in_ref, sem, n_chunks)
        _load_as_bf16(wu_hbm, wu_ref, stage_in_ref, sem, n_chunks)
        _load_as_bf16(wd_hbm, wd_ref, stage_out_ref, sem, n_chunks)

    x = x_ref[...]
    tf = act_ref.shape[2]
    for j in range(act_ref.shape[0]):
        _swiglu_hidden(x, wg_ref.at[:, j * tf:(j + 1) * tf], wu_ref.at[:, j * tf:(j + 1) * tf], act_ref, j)
    o_ref[...] = h_ref[...] + _swiglu_down(act_ref, wd_ref)


def _dense_ffn(xn, h, w_gate, w_up, w_down):
    t, d = h.shape
    f = w_gate.shape[1]
    tm, tf = TM_FFN, TF_FFN
    n_chunks = 16
    row = pl.BlockSpec((tm, d), lambda i: (i, 0))
    hbm = pl.BlockSpec(memory_space=pl.ANY)
    return pl.pallas_call(
        functools.partial(_dense_ffn_body, n_chunks=n_chunks),
        grid=(t // tm,),
        in_specs=[row, row, hbm, hbm, hbm],
        out_specs=row,
        out_shape=jax.ShapeDtypeStruct((t, d), F32),
        scratch_shapes=[pltpu.VMEM((d, f), BF16), pltpu.VMEM((d, f), BF16), pltpu.VMEM((f, d), BF16),
                        pltpu.VMEM((f // tf, tm, tf), BF16),
                        pltpu.VMEM((2, d // n_chunks, f), F32), pltpu.VMEM((2, f // n_chunks, d), F32),
                        pltpu.SemaphoreType.DMA((2,))],
        compiler_params=_params("arbitrary"),
        name="dense_ffn",
    )(xn, h, w_gate, w_up, w_down)


def _scatter_rows_body(zero_start_ref, s1_ref, s2_ref, h_ref, xs_ref, zeros_ref, sem):
    tm = h_ref.shape[0]

    @pl.when(pl.program_id(0) == 0)
    def _():
        zeros_ref[...] = jnp.zeros_like(zeros_ref)

        def fill(k, c):
            dst = xs_ref.at[pl.ds(pl.multiple_of(zero_start_ref[k], SUBLANES), zeros_ref.shape[0])]
            cp = pltpu.make_async_copy(zeros_ref, dst, sem)
            cp.start()
            cp.wait()
            return c

        lax.fori_loop(0, zero_start_ref.shape[0], fill, 0)

    def issue(r, c):
        src = h_ref.at[pl.ds(r, 1)]
        pltpu.make_async_copy(src, xs_ref.at[pl.ds(s1_ref[r], 1)], sem).start(priority=0)
        pltpu.make_async_copy(src, xs_ref.at[pl.ds(s2_ref[r], 1)], sem).start(priority=1)
        return c

    lax.fori_loop(0, tm, issue, 0, unroll=8)
    pltpu.make_async_copy(h_ref, xs_ref.at[pl.ds(0, tm)], sem).wait()
    pltpu.make_async_copy(h_ref, xs_ref.at[pl.ds(0, tm)], sem).wait()


def _scatter_rows(h, slot1, slot2, zero_start, cap, zero_rows):
    t, d = h.shape
    tm = TM_SCATTER
    smem_tile = pl.BlockSpec((tm,), lambda i, zs: (i,), memory_space=pltpu.SMEM)
    grid_spec = pltpu.PrefetchScalarGridSpec(
        num_scalar_prefetch=1,
        grid=(t // tm,),
        in_specs=[smem_tile, smem_tile, pl.BlockSpec((tm, d), lambda i, zs: (i, 0))],
        out_specs=pl.BlockSpec(memory_space=pl.ANY),
        scratch_shapes=[pltpu.VMEM((zero_rows, d), h.dtype), pltpu.SemaphoreType.DMA(())],
    )
    return pl.pallas_call(
        _scatter_rows_body,
        grid_spec=grid_spec,
        out_shape=jax.ShapeDtypeStruct((cap, d), h.dtype),
        compiler_params=_params("arbitrary"),
        name="scatter_rows",
    )(zero_start, slot1, slot2, h)


def _moe_ffn_body(tile_expert_ref, tile_valid_ref, x_ref, wg_ref, wu_ref, wd_ref, o_ref, act_ref):
    i = pl.program_id(0)
    j = pl.program_id(1)
    last = j == pl.num_programs(1) - 1
    valid = tile_valid_ref[i] > 0

    @pl.when(valid)
    def _():
        _swiglu_hidden(x_ref[...].astype(BF16), wg_ref.at[0], wu_ref.at[0], act_ref, j)

        @pl.when(last)
        def _():
            o_ref[...] = _swiglu_down(act_ref, wd_ref.at[0])

    @pl.when(last & jnp.logical_not(valid))
    def _():
        o_ref[...] = jnp.zeros_like(o_ref)


def _moe_ffn(xs, tile_expert, tile_valid, w_gate, w_up, w_down):
    cap, d = xs.shape
    f = w_gate.shape[2]
    tm, tf = TM_MOE, TF_FFN
    grid_spec = pltpu.PrefetchScalarGridSpec(
        num_scalar_prefetch=2,
        grid=(cap // tm, f // tf),
        in_specs=[
            pl.BlockSpec((tm, d), lambda i, j, te, tv: (i, 0)),
            pl.BlockSpec((1, d, tf), lambda i, j, te, tv: (te[i], 0, j)),
            pl.BlockSpec((1, d, tf), lambda i, j, te, tv: (te[i], 0, j)),
            pl.BlockSpec((1, f, d), lambda i, j, te, tv: (te[i], 0, 0)),
        ],
        out_specs=pl.BlockSpec((tm, d), lambda i, j, te, tv: (i, 0)),
        scratch_shapes=[pltpu.VMEM((f // tf, tm, tf), BF16)],
    )
    return pl.pallas_call(
        _moe_ffn_body,
        grid_spec=grid_spec,
        out_shape=jax.ShapeDtypeStruct((cap, d), F32),
        compiler_params=_params("parallel", "arbitrary"),
        name="moe_ffn",
    )(tile_expert, tile_valid, xs, w_gate, w_up, w_down)


def _moe_plan(idx, n_experts, tm):
    t = idx.shape[0]
    pairs = t * TOP_K
    cap = pairs + n_experts * tm
    e_flat = idx.reshape(pairs)
    onehot = (e_flat[:, None] == jnp.arange(n_experts, dtype=jnp.int32)[None, :]).astype(jnp.int32)
    rank = jnp.sum(onehot * (jnp.cumsum(onehot, axis=0) - onehot), axis=1)
    counts = jnp.sum(onehot, axis=0)
    padded = ((counts + tm - 1) // tm) * tm
    ends = jnp.cumsum(padded)
    starts = ends - padded
    slot = jnp.sum(onehot * starts[None, :], axis=1) + rank
    tile_start = jnp.arange(cap // tm, dtype=jnp.int32) * tm
    tile_expert = jnp.sum((tile_start[:, None] >= ends[None, :]).astype(jnp.int32), axis=1)
    tile_valid = (tile_expert < n_experts).astype(jnp.int32)
    last_used = jnp.max(jnp.where(counts > 0, jnp.arange(n_experts, dtype=jnp.int32), 0))
    tile_expert = jnp.minimum(tile_expert, last_used).astype(jnp.int32)
    zero_rows = tm + SUBLANES
    tail = ends[-1] + jnp.arange(n_experts, dtype=jnp.int32) * tm
    first = jnp.concatenate([starts + counts, tail]) // SUBLANES * SUBLANES
    zero_start = jnp.minimum(first, cap - zero_rows).astype(jnp.int32)
    return slot.reshape(t, TOP_K).astype(jnp.int32), cap, tile_expert, tile_valid, zero_start, zero_rows


def _combine_body(s1_ref, s2_ref, h_ref, wgt_ref, g_ref, ys_ref, o_ref, y1_ref, y2_ref, sem, *, apply_norm):
    tm = h_ref.shape[0]

    def issue(r, c):
        pltpu.make_async_copy(ys_ref.at[pl.ds(s1_ref[r], 1)], y1_ref.at[pl.ds(r, 1)], sem).start(priority=0)
        pltpu.make_async_copy(ys_ref.at[pl.ds(s2_ref[r], 1)], y2_ref.at[pl.ds(r, 1)], sem).start(priority=1)
        return c

    lax.fori_loop(0, tm, issue, 0, unroll=8)
    pltpu.make_async_copy(ys_ref.at[pl.ds(0, tm)], y1_ref, sem).wait()
    pltpu.make_async_copy(ys_ref.at[pl.ds(0, tm)], y2_ref, sem).wait()
    w = wgt_ref[...]
    hn = h_ref[...] + w[:, 0:1] * y1_ref[...] + w[:, 1:2] * y2_ref[...]
    o_ref[...] = _rms(hn, g_ref[...]) if apply_norm else hn


def _combine(h, ys, slot, wgt, g, *, apply_norm):
    t, d = h.shape
    tm = TM_COMBINE
    smem_tile = pl.BlockSpec((tm,), lambda i: (i,), memory_space=pltpu.SMEM)
    row = pl.BlockSpec((tm, d), lambda i: (i, 0))
    return pl.pallas_call(
        functools.partial(_combine_body, apply_norm=apply_norm),
        grid=(t // tm,),
        in_specs=[smem_tile, smem_tile, row,
                  pl.BlockSpec((tm, TOP_K), lambda i: (i, 0)),
                  pl.BlockSpec((1, d), lambda i: (0, 0)),
                  pl.BlockSpec(memory_space=pl.ANY)],
        out_specs=row,
        out_shape=jax.ShapeDtypeStruct((t, d), F32),
        scratch_shapes=[pltpu.VMEM((tm, d), F32), pltpu.VMEM((tm, d), F32), pltpu.SemaphoreType.DMA(())],
        compiler_params=_params("arbitrary"),
        name="combine",
    )(slot[:, 0], slot[:, 1], h, wgt, g.reshape(1, d), ys)


def _plain_norm_body(h_ref, g_ref, o_ref):
    o_ref[...] = _rms(h_ref[...], g_ref[...])


def _plain_norm(h, g):
    t, d = h.shape
    tm = TM_NORM
    row = pl.BlockSpec((tm, d), lambda i: (i, 0))
    return pl.pallas_call(
        _plain_norm_body,
        grid=(t // tm,),
        in_specs=[row, pl.BlockSpec((1, d), lambda i: (0, 0))],
        out_specs=row,
        out_shape=jax.ShapeDtypeStruct((t, d), F32),
        compiler_params=_params("parallel"),
        name="plain_norm",
    )(h, g.reshape(1, d))


def kernel(x, norm_mix, w_in, ssm_lambda_re, ssm_lambda_im, ssm_log_dt, ssm_b_re, ssm_b_im, ssm_c_re,
           ssm_c_im, ssm_d, ssm_w_glu, conv_w, w_br_a, w_br_b, w_br_c, w_out, norm_ffn, dense_w_gate,
           dense_w_up, dense_w_down, moe_w_router, moe_w_gate, moe_w_up, moe_w_down, final_norm):
    batch, seq, d = x.shape
    depth = w_in.shape[0]
    t = batch * seq
    w_attn = w_br_a.shape[1]
    w_ssm = w_br_b.shape[1]
    w_conv = w_br_c.shape[1]
    n_sc = w_ssm + 3 * w_conv
    n_gate = N_BRANCH * d
    n_experts = moe_w_router.shape[-1]

    h = x.reshape(t, d)
    normed = False
    for i in range(depth):
        qt, k, vt, sc, gates = _inproj(h, norm_mix[i], w_in[i].astype(BF16), batch=batch, seq=seq,
                                       width=w_attn, n_sc=n_sc, n_gate=n_gate)
        attn = _sb_attention(qt, k, vt)
        tables = _ssm_tables(ssm_lambda_re[i], ssm_lambda_im[i], ssm_log_dt[i], ssm_b_re[i], ssm_b_im[i],
                             ssm_c_re[i], ssm_c_im[i], seq // SSM_CHUNK)
        yssm = _ssm_scan(sc[:, :w_ssm], tables, batch=batch, seq=seq)
        dense = i % 2 == 0
        j = i // 2
        merged = _merge(attn, yssm, sc, gates, h, ssm_d[i], ssm_w_glu[i].astype(BF16), conv_w[i],
                        w_br_a[i].astype(BF16), w_br_b[i].astype(BF16), w_br_c[i].astype(BF16),
                        w_out[i].astype(BF16), norm_ffn[i], None if dense else moe_w_router[j], seq=seq)
        if dense:
            h, xn = merged
            h = _dense_ffn(xn, h, dense_w_gate[j], dense_w_up[j], dense_w_down[j])
        else:
            h, xn, idx, wgt = merged
            slot, cap, tile_expert, tile_valid, zero_start, zero_rows = _moe_plan(idx, n_experts, TM_MOE)
            xs = _scatter_rows(xn, slot[:, 0], slot[:, 1], zero_start, cap, zero_rows)
            ys = _moe_ffn(xs, tile_expert, tile_valid, moe_w_gate[j].astype(BF16),
                          moe_w_up[j].astype(BF16), moe_w_down[j].astype(BF16))
            last = i == depth - 1
            h = _combine(h, ys, slot, wgt, final_norm if last else norm_ffn[i], apply_norm=last)
            normed = last
    out = h if normed else _plain_norm(h, final_norm)
    return out.reshape(batch, seq, d)
```

```python
import functools

import jax
import jax.numpy as jnp
from jax import lax
from jax.experimental import pallas as pl
from jax.experimental.pallas import tpu as pltpu

F32 = jnp.float32
BF16 = jnp.bfloat16

EPS = 1e-6
HEAD_DIM = 64
SSM_CHUNK = 16
SSM_MAX_RE = -1e-4
CONV_K = 3
N_BRANCH = 3
TOP_K = 2
LOG2E = 1.4426950408889634
MASKED_LOG_WEIGHT = -1e30

LANES = 128
SUBLANES = 8
MXU_DIM = 256
VMEM_LIMIT_BYTES = 56 * 1024 * 1024

TM_PROJ = 512
TQ_ATTN = MXU_DIM
TM_MERGE = 512
TM_FFN = 256
TF_FFN = 1792
TM_MOE = 512
TM_SCATTER = 512
TM_COMBINE = 512
TM_NORM = 512


def _params(*sem):
    return pltpu.CompilerParams(dimension_semantics=sem, vmem_limit_bytes=VMEM_LIMIT_BYTES)


def _rms(x, g):
    ms = jnp.mean(x * x, axis=-1, keepdims=True)
    return x * lax.rsqrt(ms + EPS) * g


def _inproj_body(x_ref, g_ref, w_ref, qt_ref, k_ref, vt_ref, sc_ref, gate_ref, *, width, n_sc, n_gate, q_scale):
    xn = _rms(x_ref[...], g_ref[...]).astype(BF16)
    tm = xn.shape[0]
    tq = vt_ref.shape[-1]
    chunk = 512

    def proj(c0, n):
        return jnp.dot(xn, w_ref[:, c0:c0 + n], preferred_element_type=F32)

    qt_ref[0] = (proj(0, width) * q_scale).T.astype(BF16)
    k_ref[...] = proj(width, width).astype(BF16)
    v = proj(2 * width, width)
    for kt in range(tm // tq):
        for p in range(width // LANES):
            vt_ref[0, p, kt] = v[kt * tq:(kt + 1) * tq, p * LANES:(p + 1) * LANES].T.astype(BF16)
    for c0 in range(0, n_sc, chunk):
        sc_ref[:, c0:c0 + chunk] = proj(3 * width + c0, chunk).astype(BF16)
    for c0 in range(0, n_gate, chunk):
        gate_ref[:, c0:c0 + chunk] = jax.nn.sigmoid(proj(3 * width + n_sc + c0, chunk)).astype(BF16)


def _inproj(h, g, w_bf16, *, batch, seq, width, n_sc, n_gate):
    t, d = h.shape
    tm = TM_PROJ
    tq = TQ_ATTN
    n_pairs = width // LANES
    tiles_per_seq = seq // tm
    body = functools.partial(_inproj_body, width=width, n_sc=n_sc, n_gate=n_gate, q_scale=HEAD_DIM ** -0.5)
    return pl.pallas_call(
        body,
        grid=(t // tm,),
        in_specs=[
            pl.BlockSpec((tm, d), lambda i: (i, 0)),
            pl.BlockSpec((1, d), lambda i: (0, 0)),
            pl.BlockSpec((d, 3 * width + n_sc + n_gate), lambda i: (0, 0), pipeline_mode=pl.Buffered(1)),
        ],
        out_specs=[
            pl.BlockSpec((1, width, tm), lambda i: (i // tiles_per_seq, 0, i % tiles_per_seq)),
            pl.BlockSpec((tm, width), lambda i: (i, 0)),
            pl.BlockSpec((1, n_pairs, tm // tq, LANES, tq),
                         lambda i: (i // tiles_per_seq, 0, i % tiles_per_seq, 0, 0)),
            pl.BlockSpec((tm, n_sc), lambda i: (i, 0)),
            pl.BlockSpec((tm, n_gate), lambda i: (i, 0)),
        ],
        out_shape=[
            jax.ShapeDtypeStruct((batch, width, seq), BF16),
            jax.ShapeDtypeStruct((t, width), BF16),
            jax.ShapeDtypeStruct((batch, n_pairs, seq // tq, LANES, tq), BF16),
            jax.ShapeDtypeStruct((t, n_sc), BF16),
            jax.ShapeDtypeStruct((t, n_gate), BF16),
        ],
        compiler_params=_params("parallel"),
        name="inproj",
    )(h, g.reshape(1, d), w_bf16)


def _sb_attn_body(qt_ref, k_ref, vt_ref, o_ref, acc_ref, carry_ref, lb_a_ref, sp_a_ref, lb_b_ref, sp_b_ref,
                  *, tq, n_pairs):
    i = pl.program_id(1)
    buf_a = (lb_a_ref, sp_a_ref)
    buf_b = (lb_b_ref, sp_b_ref)
    n_heads = 2 * n_pairs
    ones_rows = carry_ref.shape[1]
    key = lax.broadcasted_iota(jnp.int32, (tq, tq), 0)
    qry = lax.broadcasted_iota(jnp.int32, (tq, tq), 1)
    valid = key < qry
    er = lax.broadcasted_iota(jnp.int32, (tq + ones_rows, tq), 0)
    ec = lax.broadcasted_iota(jnp.int32, (tq + ones_rows, tq), 1)
    later_and_sum = jnp.where((ec > er) | (er >= tq), 1.0, 0.0).astype(BF16)

    feat = lax.broadcasted_iota(jnp.int32, (LANES, tq), 0)
    q_heads = []
    for p in range(n_pairs):
        qp = qt_ref[0, p * LANES:(p + 1) * LANES, :]
        zero = jnp.zeros_like(qp)
        q_heads.append(jnp.where(feat < HEAD_DIM, qp, zero))
        q_heads.append(jnp.where(feat >= HEAD_DIM, qp, zero))

    def scores(j, buf, masked):
        lb_ref, sp_ref = buf
        start = pl.multiple_of(j * tq, tq)
        for h in range(n_heads):
            p = h // 2
            kblk = k_ref[pl.ds(start, tq), p * LANES:(p + 1) * LANES]
            z = jnp.dot(kblk, q_heads[h], preferred_element_type=F32)
            l1p = jnp.log(1.0 + jnp.exp2(jnp.abs(z) * (-LOG2E)))
            sp = jnp.maximum(z, 0.0) + l1p
            lb = z - sp
            if masked:
                sp = jnp.where(valid, sp, 0.0)
                lb = jnp.where(valid, lb, MASKED_LOG_WEIGHT)
            lb_ref[h] = lb
            sp_ref[h] = sp.astype(BF16)

    def weights(j, buf):
        lb_ref, sp_ref = buf
        ws, sums = [], []
        for h in range(n_heads):
            ts = jnp.dot(later_and_sum, sp_ref[h], preferred_element_type=F32)
            ws.append(jnp.exp(lb_ref[h] - ts[0:tq]).astype(BF16))
            sums.append(ts[tq:tq + ones_rows])
        for h in range(n_heads):
            p, hh = divmod(h, 2)
            vt = vt_ref[0, p, j, hh * HEAD_DIM:(hh + 1) * HEAD_DIM, :]
            pv = jnp.dot(vt, ws[h], preferred_element_type=F32)
            carry = carry_ref[h]
            scale = jnp.exp(-carry[0:1, :])
            rows = slice(h * HEAD_DIM, (h + 1) * HEAD_DIM)
            acc_ref[rows, :] += pv * scale
            carry_ref[h] = carry + sums[h]

    def earlier_keys_matter():
        return (jnp.max(jnp.exp(-carry_ref[:, 0, :])) > 0.0).astype(jnp.int32)

    acc_ref[...] = jnp.zeros_like(acc_ref)
    carry_ref[...] = jnp.zeros_like(carry_ref)

    @pl.when(i == 0)
    def _():
        scores(0, buf_a, True)
        weights(0, buf_a)

    @pl.when(i > 0)
    def _():
        scores(i, buf_a, True)
        scores(i - 1, buf_b, False)
        weights(i, buf_a)
        weights(i - 1, buf_b)

    rest = i - 2

    @pl.when(jnp.where(rest >= 0, earlier_keys_matter(), 0) > 0)
    def _():
        scores(rest, buf_a, False)

        def two_tiles(c):
            m, _ = c
            j = rest - 1 - 2 * m
            scores(j, buf_b, False)
            weights(j + 1, buf_a)
            scores(j - 1, buf_a, False)
            weights(j, buf_b)
            return m + 1, earlier_keys_matter()

        _, alive = lax.while_loop(lambda c: (c[0] < rest // 2) & (c[1] > 0), two_tiles,
                                  (jnp.int32(0), jnp.int32(1)))

        @pl.when((alive > 0) & (rest % 2 == 1))
        def _():
            scores(0, buf_b, False)
            weights(1, buf_a)
            weights(0, buf_b)

        @pl.when((alive > 0) & (rest % 2 == 0))
        def _():
            weights(0, buf_a)

    for p in range(n_pairs):
        o_ref[:, p * LANES:(p + 1) * LANES] = acc_ref[p * LANES:(p + 1) * LANES, :].T.astype(o_ref.dtype)


def _sb_attention(qt, k, vt):
    batch, width, seq = qt.shape
    n_pairs, nq, _, tq = vt.shape[1:]
    n_heads = 2 * n_pairs
    return pl.pallas_call(
        functools.partial(_sb_attn_body, tq=tq, n_pairs=n_pairs),
        grid=(batch, nq),
        in_specs=[
            pl.BlockSpec((1, width, tq), lambda b, i: (b, 0, i)),
            pl.BlockSpec((seq, width), lambda b, i: (b, 0)),
            pl.BlockSpec((1, n_pairs, nq, LANES, tq), lambda b, i: (b, 0, 0, 0, 0)),
        ],
        out_specs=pl.BlockSpec((tq, width), lambda b, i: (b * nq + i, 0)),
        out_shape=jax.ShapeDtypeStruct((batch * seq, width), BF16),
        scratch_shapes=[pltpu.VMEM((width, tq), F32), pltpu.VMEM((n_heads, 16, tq), F32),
                        pltpu.VMEM((n_heads, tq, tq), F32), pltpu.VMEM((n_heads, tq, tq), BF16),
                        pltpu.VMEM((n_heads, tq, tq), F32), pltpu.VMEM((n_heads, tq, tq), BF16)],
        compiler_params=_params("parallel", "arbitrary"),
        name="sb_attention",
    )(qt, k, vt)


def _ssm_tables(lam_re, lam_im, log_dt, b_re, b_im, c_re, c_im, n_chunks):
    g, p = lam_re.shape
    c = SSM_CHUNK
    lr = jnp.minimum(lam_re.astype(F32), SSM_MAX_RE)
    li = lam_im.astype(F32)
    dt = jnp.exp(log_dt.astype(F32))[:, None]

    def lbar_pow(k):
        mag = jnp.exp(k * (lr * dt))
        ang = k * (li * dt)
        return mag * jnp.cos(ang), mag * jnp.sin(ang)

    ab_re, ab_im = lbar_pow(1.0)
    den = lr * lr + li * li
    nr = ab_re - 1.0
    ni = ab_im
    f_re = (nr * lr + ni * li) / den
    f_im = (ni * lr - nr * li) / den
    br = b_re.astype(F32)
    bi = b_im.astype(F32)
    bb_re = f_re[..., None] * br - f_im[..., None] * bi
    bb_im = f_re[..., None] * bi + f_im[..., None] * br
    cr = c_re.astype(F32)
    ci = c_im.astype(F32)

    eye = jnp.eye(g, dtype=F32)

    def block_diag(x):
        _, a, b = x.shape
        return (x[:, :, None, :] * eye[:, None, :, None]).reshape(g * a, g * b)

    b_in = jnp.concatenate([block_diag(bb_re.transpose(0, 2, 1)),
                            block_diag(bb_im.transpose(0, 2, 1))], axis=1)
    c_out = jnp.concatenate([block_diag(cr.transpose(0, 2, 1)),
                             block_diag(-ci.transpose(0, 2, 1))], axis=0)
    lbar = jnp.stack([ab_re.reshape(-1), ab_im.reshape(-1)], axis=0)

    steps = []
    s = 1
    while s < n_chunks:
        a_re, a_im = lbar_pow(float(c * s))
        steps.append(jnp.stack([a_re.reshape(-1, MXU_DIM), a_im.reshape(-1, MXU_DIM)], axis=1))
        s *= 2
    step = jnp.stack(steps, axis=1)
    return b_in.astype(BF16), c_out.astype(BF16), lbar, step


def _ssm_body(u_ref, b_ref, c_ref, lbar_ref, step_ref, y_ref, z_ref, *, n_chunks, batch):
    s = pl.program_id(0)
    c = u_ref.shape[0]
    n_blk = z_ref.shape[0]
    half = n_blk // 2
    bw = z_ref.shape[2]

    @pl.when(s == 0)
    def _():
        z_ref[...] = jnp.zeros_like(z_ref)

    def advance(ut):
        for pb in range(half):
            lo, hi = pb * bw, (pb + 1) * bw
            re = z_ref[pb]
            im = z_ref[half + pb]
            a_re = lbar_ref[0:1, lo:hi]
            a_im = lbar_ref[1:2, lo:hi]
            x_re = jnp.dot(ut, b_ref[:, lo:hi], preferred_element_type=F32)
            x_im = jnp.dot(ut, b_ref[:, half * bw + lo:half * bw + hi], preferred_element_type=F32)
            z_ref[pb] = a_re * re - a_im * im + x_re
            z_ref[half + pb] = a_re * im + a_im * re + x_im

    @pl.when(s < c)
    def _():
        advance(u_ref[s])

    @pl.when(s == c)
    def _():
        chunk = lax.broadcasted_iota(jnp.int32, (n_chunks, bw), 0)

        def shifted(x, n):
            return jnp.where(chunk >= n, pltpu.roll(x, n, axis=0), 0.0)

        def scan_block(idx, carry):
            b = idx // half
            pb = idx % half
            rows = pl.ds(pl.multiple_of(b * n_chunks, n_chunks), n_chunks)
            re = z_ref[pb, rows, :]
            im = z_ref[half + pb, rows, :]
            n = 1
            k = 0
            while n < n_chunks:
                a_re = step_ref[pb, k, 0:1, :]
                a_im = step_ref[pb, k, 1:2, :]
                re_s = shifted(re, n)
                im_s = shifted(im, n)
                re, im = re + a_re * re_s - a_im * im_s, im + a_re * im_s + a_im * re_s
                n *= 2
                k += 1
            z_ref[pb, rows, :] = shifted(re, 1)
            z_ref[half + pb, rows, :] = shifted(im, 1)
            return carry

        lax.fori_loop(0, batch * half, scan_block, 0)

    @pl.when(s >= c)
    def _():
        advance(u_ref[s - c])
        y = jnp.dot(z_ref[0].astype(BF16), c_ref[0:bw, :], preferred_element_type=F32)
        for blk in range(1, n_blk):
            y += jnp.dot(z_ref[blk].astype(BF16), c_ref[blk * bw:(blk + 1) * bw, :],
                         preferred_element_type=F32)
        y_ref[0] = y.astype(y_ref.dtype)


def _ssm_scan(u, tables, *, batch, seq):
    b_in, c_out, lbar, step = tables
    width = u.shape[1]
    c = SSM_CHUNK
    n_chunks = seq // c
    rows = batch * n_chunks
    n_blk = b_in.shape[1] // MXU_DIM
    u3 = u.reshape(rows, c, width).transpose(1, 0, 2)
    whole = lambda shape: pl.BlockSpec(shape, lambda s: (0,) * len(shape), pipeline_mode=pl.Buffered(1))
    y3 = pl.pallas_call(
        functools.partial(_ssm_body, n_chunks=n_chunks, batch=batch),
        grid=(2 * c,),
        in_specs=[whole(u3.shape), whole(b_in.shape), whole(c_out.shape), whole(lbar.shape), whole(step.shape)],
        out_specs=pl.BlockSpec((1, rows, width), lambda s: (jnp.maximum(s - c, 0), 0, 0)),
        out_shape=jax.ShapeDtypeStruct((c, rows, width), BF16),
        scratch_shapes=[pltpu.VMEM((n_blk, rows, MXU_DIM), F32)],
        compiler_params=_params("arbitrary"),
        name="ssm_scan",
    )(u3, b_in, c_out, lbar, step)
    return y3.transpose(1, 0, 2).reshape(batch * seq, width)


def _gelu_tanh(x):
    return 0.5 * x * (1.0 + jnp.tanh(0.7978845608028654 * (x + 0.044715 * (x * x * x))))


def _top2(xn, wr):
    logits = lax.dot_general(wr, xn, (((1,), (1,)), ((), ())), preferred_element_type=F32,
                             precision=lax.Precision.HIGHEST)
    n_e = logits.shape[0]
    e_idx = lax.broadcasted_iota(jnp.int32, logits.shape, 0)
    m1 = jnp.max(logits, axis=0, keepdims=True)
    i1 = jnp.min(jnp.where(logits == m1, e_idx, n_e), axis=0, keepdims=True)
    rest = jnp.where(e_idx == i1, -jnp.inf, logits)
    m2 = jnp.max(rest, axis=0, keepdims=True)
    i2 = jnp.min(jnp.where(rest == m2, e_idx, n_e), axis=0, keepdims=True)
    e2 = jnp.exp(m2 - m1)
    w1 = 1.0 / (1.0 + e2)
    return jnp.concatenate([i1, i2], axis=0), jnp.concatenate([w1, e2 * w1], axis=0)


def _merge_body(attn_ref, yssm_ref, sc_ref, halo_ref, gate_ref, h_ref, dskip_ref, wglu_ref, convw_ref,
                wa_ref, wb_ref, wc_ref, wout_ref, gn_ref, *rest, tiles_per_seq, w_ssm, w_conv, route):
    if route:
        wr_ref, hout_ref, xn_ref, idx_ref, wgt_ref = rest
    else:
        hout_ref, xn_ref = rest
    i = pl.program_id(0)
    tm = h_ref.shape[0]
    d = h_ref.shape[1]

    u = sc_ref[:, 0:w_ssm].astype(F32)
    yb = _gelu_tanh(yssm_ref[...].astype(F32) + dskip_ref[...] * u)
    yb = yb * jax.nn.sigmoid(jnp.dot(yb.astype(BF16), wglu_ref[...], preferred_element_type=F32))
    y_b = jnp.dot(yb.astype(BF16), wb_ref[...], preferred_element_type=F32)

    o_h, o_b, o_c = w_ssm, w_ssm + w_conv, w_ssm + 2 * w_conv
    uc = sc_ref[:, o_c:o_c + w_conv].astype(F32) * sc_ref[:, o_h:o_h + w_conv].astype(F32)
    halo = halo_ref[:, o_c:o_c + w_conv].astype(F32) * halo_ref[:, o_h:o_h + w_conv].astype(F32)
    halo = jnp.where(i % tiles_per_seq == 0, 0.0, halo)
    row = lax.broadcasted_iota(jnp.int32, (tm, w_conv), 0)
    prev1 = jnp.where(row == 0, halo[SUBLANES - 1:SUBLANES, :], pltpu.roll(uc, 1, axis=0))
    prev2 = jnp.where(row == 0, halo[SUBLANES - 2:SUBLANES - 1, :],
                      jnp.where(row == 1, halo[SUBLANES - 1:SUBLANES, :], pltpu.roll(uc, 2, axis=0)))
    conv = convw_ref[0:1, :] * prev2 + convw_ref[1:2, :] * prev1 + convw_ref[2:3, :] * uc
    yc = sc_ref[:, o_b:o_b + w_conv].astype(F32) * conv
    y_c = jnp.dot(yc.astype(BF16), wc_ref[...], preferred_element_type=F32)

    y_a = jnp.dot(attn_ref[...], wa_ref[...], preferred_element_type=F32)

    merged = (gate_ref[:, 0:d].astype(F32) * y_a + gate_ref[:, d:2 * d].astype(F32) * y_b
              + gate_ref[:, 2 * d:3 * d].astype(F32) * y_c)
    hn = h_ref[...] + jnp.dot(merged.astype(BF16), wout_ref[...], preferred_element_type=F32)
    hout_ref[...] = hn
    xn = _rms(hn, gn_ref[...])
    xn_ref[...] = xn.astype(xn_ref.dtype)
    if route:
        idx, wgt = _top2(xn, wr_ref[...])
        idx_ref[0] = idx
        wgt_ref[0] = wgt


def _merge(attn, yssm, sc, gates, h, d_skip, w_glu, conv_w, w_a, w_b, w_c, w_out, g_ffn, w_router, *, seq):
    t, d = h.shape
    tm = TM_MERGE
    nt = t // tm
    w_ssm = yssm.shape[1]
    w_conv = conv_w.shape[1]
    w_attn = attn.shape[1]
    n_sc = sc.shape[1]
    route = w_router is not None
    body = functools.partial(_merge_body, tiles_per_seq=seq // tm, w_ssm=w_ssm, w_conv=w_conv, route=route)
    full = lambda shape: pl.BlockSpec(shape, lambda i: (0,) * len(shape))
    row = pl.BlockSpec((tm, d), lambda i: (i, 0))
    halo_blocks = tm // SUBLANES
    in_specs = [
        pl.BlockSpec((tm, w_attn), lambda i: (i, 0)),
        pl.BlockSpec((tm, w_ssm), lambda i: (i, 0)),
        pl.BlockSpec((tm, n_sc), lambda i: (i, 0)),
        pl.BlockSpec((SUBLANES, n_sc), lambda i: (jnp.maximum(i * halo_blocks - 1, 0), 0)),
        pl.BlockSpec((tm, N_BRANCH * d), lambda i: (i, 0)),
        row,
        full((1, w_ssm)), full((w_ssm, w_ssm)), full((CONV_K, w_conv)),
        full((w_attn, d)), full((w_ssm, d)), full((w_conv, d)), full((d, d)), full((1, d)),
    ]
    args = [attn, yssm, sc, sc, gates, h, d_skip.reshape(1, w_ssm).astype(F32), w_glu, conv_w.astype(F32),
            w_a, w_b, w_c, w_out, g_ffn.reshape(1, d)]
    out_specs = [row, row]
    out_shape = [jax.ShapeDtypeStruct((t, d), F32), jax.ShapeDtypeStruct((t, d), F32 if route else BF16)]
    if route:
        n_e = w_router.shape[1]
        in_specs.append(full((n_e, d)))
        args.append(w_router.T.astype(F32))
        lanes = pl.BlockSpec((1, TOP_K, tm), lambda i: (i, 0, 0))
        out_specs += [lanes, lanes]
        out_shape += [jax.ShapeDtypeStruct((nt, TOP_K, tm), jnp.int32), jax.ShapeDtypeStruct((nt, TOP_K, tm), F32)]
    outs = pl.pallas_call(
        body,
        grid=(nt,),
        in_specs=in_specs,
        out_specs=out_specs,
        out_shape=out_shape,
        compiler_params=_params("parallel"),
        name="merge",
    )(*args)
    if not route:
        return outs
    hn, xn, idx, wgt = outs
    return hn, xn, idx.transpose(0, 2, 1).reshape(t, TOP_K), wgt.transpose(0, 2, 1).reshape(t, TOP_K)


def _swiglu_hidden(x, wg_ref, wu_ref, act_ref, j):
    gate = jnp.dot(x, wg_ref[...], preferred_element_type=F32)
    up = jnp.dot(x, wu_ref[...], preferred_element_type=F32)
    act_ref[j] = (gate * jax.nn.sigmoid(gate) * up).astype(act_ref.dtype)


def _swiglu_down(act_ref, wd_ref):
    act = jnp.concatenate([act_ref[jj] for jj in range(act_ref.shape[0])], axis=1)
    return jnp.dot(act, wd_ref[...], preferred_element_type=F32)


def _load_as_bf16(src_hbm, dst_ref, stage_ref, sem):
    rows = stage_ref.shape[1]
    n_chunks = src_hbm.shape[0] // rows

    def copy(c):
        return pltpu.make_async_copy(src_hbm.at[pl.ds(c * rows, rows)], stage_ref.at[c % 2], sem.at[c % 2])

    copy(0).start()
    for c in range(n_chunks):
        if c + 1 < n_chunks:
            copy(c + 1).start()
        copy(c).wait()
        dst_ref[c * rows:(c + 1) * rows, :] = stage_ref[c % 2].astype(dst_ref.dtype)


def _dense_ffn_body(x_ref, h_ref, wg_hbm, wu_hbm, wd_hbm, e0_hbm, e1_hbm, e2_hbm, o_ref, n0_hbm, n1_hbm, n2_hbm,
                    wg_ref, wu_ref, wd_ref, act_ref, st0_ref, st1_ref, st2_ref, nr0_ref, nr1_ref, nr2_ref,
                    in_sem, out_sem):
    s = pl.program_id(0)
    n_steps = pl.num_programs(0)
    stages = (st0_ref, st1_ref, st2_ref)
    narrow = (nr0_ref, nr1_ref, nr2_ref)

    def fetch(step, slot):
        return [pltpu.make_async_copy(src.at[pl.ds(step * st.shape[1], st.shape[1])], st.at[slot], in_sem.at[slot])
                for src, st in zip((e0_hbm, e1_hbm, e2_hbm), stages)]

    def flush(step, slot):
        return [pltpu.make_async_copy(nr.at[slot], dst.at[pl.ds(step * nr.shape[1], nr.shape[1])], out_sem.at[slot])
                for nr, dst in zip(narrow, (n0_hbm, n1_hbm, n2_hbm))]

    @pl.when(s == 0)
    def _():
        _load_as_bf16(wg_hbm, wg_ref, st0_ref, in_sem)
        _load_as_bf16(wu_hbm, wu_ref, st1_ref, in_sem)
        _load_as_bf16(wd_hbm, wd_ref, st2_ref, in_sem)
        for cp in fetch(0, 0):
            cp.start()

    @pl.when(s + 1 < n_steps)
    def _():
        for cp in fetch(s + 1, (s + 1) % 2):
            cp.start()

    x = x_ref[...]
    tf = act_ref.shape[2]
    for j in range(act_ref.shape[0]):
        _swiglu_hidden(x, wg_ref.at[:, j * tf:(j + 1) * tf], wu_ref.at[:, j * tf:(j + 1) * tf], act_ref, j)
    o_ref[...] = h_ref[...] + _swiglu_down(act_ref, wd_ref)

    slot = s % 2
    for cp in fetch(s, slot):
        cp.wait()

    @pl.when(s >= 2)
    def _():
        for cp in flush(s - 2, slot):
            cp.wait()

    for st, nr in zip(stages, narrow):
        nr[slot] = st[slot].astype(nr.dtype)
    for cp in flush(s, slot):
        cp.start()

    @pl.when(s == n_steps - 1)
    def _():
        @pl.when(s >= 1)
        def _():
            for cp in flush(s - 1, 1 - slot):
                cp.wait()

        for cp in flush(s, slot):
            cp.wait()


def _dense_ffn(xn, h, w_gate, w_up, w_down, extra):
    t, d = h.shape
    f = w_gate.shape[1]
    tm, tf = TM_FFN, TF_FFN
    n_steps = t // tm
    rows = [e.shape[0] // n_steps for e in extra]
    assert all(e.shape[0] == r * n_steps and r % 16 == 0 for e, r in zip(extra, rows))
    assert w_gate.shape[0] % rows[0] == 0 and w_up.shape[0] % rows[1] == 0 and w_down.shape[0] % rows[2] == 0
    assert extra[0].shape[1] == f and extra[1].shape[1] == f and extra[2].shape[1] == d
    row = pl.BlockSpec((tm, d), lambda i: (i, 0))
    hbm = pl.BlockSpec(memory_space=pl.ANY)
    outs = pl.pallas_call(
        _dense_ffn_body,
        grid=(n_steps,),
        in_specs=[row, row] + [hbm] * 6,
        out_specs=[row, hbm, hbm, hbm],
        out_shape=[jax.ShapeDtypeStruct((t, d), F32)] + [jax.ShapeDtypeStruct(e.shape, BF16) for e in extra],
        scratch_shapes=[pltpu.VMEM((d, f), BF16), pltpu.VMEM((d, f), BF16), pltpu.VMEM((f, d), BF16),
                        pltpu.VMEM((f // tf, tm, tf), BF16)]
                       + [pltpu.VMEM((2, r, e.shape[1]), F32) for e, r in zip(extra, rows)]
                       + [pltpu.VMEM((2, r, e.shape[1]), BF16) for e, r in zip(extra, rows)]
                       + [pltpu.SemaphoreType.DMA((2,)), pltpu.SemaphoreType.DMA((2,))],
        compiler_params=_params("arbitrary"),
        name="dense_ffn",
    )(xn, h, w_gate, w_up, w_down, *extra)
    return outs[0], outs[1:]


def _scatter_rows_body(zero_start_ref, s1_ref, s2_ref, h_ref, xs_ref, zeros_ref, sem):
    tm = h_ref.shape[0]

    @pl.when(pl.program_id(0) == 0)
    def _():
        zeros_ref[...] = jnp.zeros_like(zeros_ref)

        def fill(k, c):
            dst = xs_ref.at[pl.ds(pl.multiple_of(zero_start_ref[k], SUBLANES), zeros_ref.shape[0])]
            cp = pltpu.make_async_copy(zeros_ref, dst, sem)
            cp.start()
            cp.wait()
            return c

        lax.fori_loop(0, zero_start_ref.shape[0], fill, 0)

    def issue(r, c):
        src = h_ref.at[pl.ds(r, 1)]
        pltpu.make_async_copy(src, xs_ref.at[pl.ds(s1_ref[r], 1)], sem).start(priority=0)
        pltpu.make_async_copy(src, xs_ref.at[pl.ds(s2_ref[r], 1)], sem).start(priority=1)
        return c

    lax.fori_loop(0, tm, issue, 0, unroll=8)
    pltpu.make_async_copy(h_ref, xs_ref.at[pl.ds(0, tm)], sem).wait()
    pltpu.make_async_copy(h_ref, xs_ref.at[pl.ds(0, tm)], sem).wait()


def _scatter_rows(h, slot1, slot2, zero_start, cap, zero_rows):
    t, d = h.shape
    tm = TM_SCATTER
    smem_tile = pl.BlockSpec((tm,), lambda i, zs: (i,), memory_space=pltpu.SMEM)
    grid_spec = pltpu.PrefetchScalarGridSpec(
        num_scalar_prefetch=1,
        grid=(t // tm,),
        in_specs=[smem_tile, smem_tile, pl.BlockSpec((tm, d), lambda i, zs: (i, 0))],
        out_specs=pl.BlockSpec(memory_space=pl.ANY),
        scratch_shapes=[pltpu.VMEM((zero_rows, d), h.dtype), pltpu.SemaphoreType.DMA(())],
    )
    return pl.pallas_call(
        _scatter_rows_body,
        grid_spec=grid_spec,
        out_shape=jax.ShapeDtypeStruct((cap, d), h.dtype),
        compiler_params=_params("arbitrary"),
        name="scatter_rows",
    )(zero_start, slot1, slot2, h)


def _moe_ffn_body(tile_expert_ref, tile_valid_ref, x_ref, wg_ref, wu_ref, wd_ref, o_ref, act_ref):
    i = pl.program_id(0)
    j = pl.program_id(1)
    last = j == pl.num_programs(1) - 1
    valid = tile_valid_ref[i] > 0

    @pl.when(valid)
    def _():
        _swiglu_hidden(x_ref[...].astype(BF16), wg_ref.at[0], wu_ref.at[0], act_ref, j)

        @pl.when(last)
        def _():
            o_ref[...] = _swiglu_down(act_ref, wd_ref.at[0])

    @pl.when(last & jnp.logical_not(valid))
    def _():
        o_ref[...] = jnp.zeros_like(o_ref)


def _moe_ffn(xs, tile_expert, tile_valid, w_gate, w_up, w_down):
    cap, d = xs.shape
    f = w_gate.shape[2]
    tm, tf = TM_MOE, TF_FFN
    grid_spec = pltpu.PrefetchScalarGridSpec(
        num_scalar_prefetch=2,
        grid=(cap // tm, f // tf),
        in_specs=[
            pl.BlockSpec((tm, d), lambda i, j, te, tv: (i, 0)),
            pl.BlockSpec((1, d, tf), lambda i, j, te, tv: (te[i], 0, j)),
            pl.BlockSpec((1, d, tf), lambda i, j, te, tv: (te[i], 0, j)),
            pl.BlockSpec((1, f, d), lambda i, j, te, tv: (te[i], 0, 0)),
        ],
        out_specs=pl.BlockSpec((tm, d), lambda i, j, te, tv: (i, 0)),
        scratch_shapes=[pltpu.VMEM((f // tf, tm, tf), BF16)],
    )
    return pl.pallas_call(
        _moe_ffn_body,
        grid_spec=grid_spec,
        out_shape=jax.ShapeDtypeStruct((cap, d), F32),
        compiler_params=_params("parallel", "arbitrary"),
        name="moe_ffn",
    )(tile_expert, tile_valid, xs, w_gate, w_up, w_down)


def _moe_plan(idx, n_experts, tm):
    t = idx.shape[0]
    pairs = t * TOP_K
    cap = pairs + n_experts * tm
    e_flat = idx.reshape(pairs)
    onehot = (e_flat[:, None] == jnp.arange(n_experts, dtype=jnp.int32)[None, :]).astype(jnp.int32)
    rank = jnp.sum(onehot * (jnp.cumsum(onehot, axis=0) - onehot), axis=1)
    counts = jnp.sum(onehot, axis=0)
    padded = ((counts + tm - 1) // tm) * tm
    ends = jnp.cumsum(padded)
    starts = ends - padded
    slot = jnp.sum(onehot * starts[None, :], axis=1) + rank
    tile_start = jnp.arange(cap // tm, dtype=jnp.int32) * tm
    tile_expert = jnp.sum((tile_start[:, None] >= ends[None, :]).astype(jnp.int32), axis=1)
    tile_valid = (tile_expert < n_experts).astype(jnp.int32)
    last_used = jnp.max(jnp.where(counts > 0, jnp.arange(n_experts, dtype=jnp.int32), 0))
    tile_expert = jnp.minimum(tile_expert, last_used).astype(jnp.int32)
    zero_rows = tm + SUBLANES
    tail = ends[-1] + jnp.arange(n_experts, dtype=jnp.int32) * tm
    first = jnp.concatenate([starts + counts, tail]) // SUBLANES * SUBLANES
    zero_start = jnp.minimum(first, cap - zero_rows).astype(jnp.int32)
    return slot.reshape(t, TOP_K).astype(jnp.int32), cap, tile_expert, tile_valid, zero_start, zero_rows


def _combine_body(s1_ref, s2_ref, h_ref, wgt_ref, g_ref, ys_ref, o_ref, y1_ref, y2_ref, sem, *, apply_norm):
    tm = h_ref.shape[0]

    def issue(r, c):
        pltpu.make_async_copy(ys_ref.at[pl.ds(s1_ref[r], 1)], y1_ref.at[pl.ds(r, 1)], sem).start(priority=0)
        pltpu.make_async_copy(ys_ref.at[pl.ds(s2_ref[r], 1)], y2_ref.at[pl.ds(r, 1)], sem).start(priority=1)
        return c

    lax.fori_loop(0, tm, issue, 0, unroll=8)
    pltpu.make_async_copy(ys_ref.at[pl.ds(0, tm)], y1_ref, sem).wait()
    pltpu.make_async_copy(ys_ref.at[pl.ds(0, tm)], y2_ref, sem).wait()
    w = wgt_ref[...]
    hn = h_ref[...] + w[:, 0:1] * y1_ref[...] + w[:, 1:2] * y2_ref[...]
    o_ref[...] = _rms(hn, g_ref[...]) if apply_norm else hn


def _combine(h, ys, slot, wgt, g, *, apply_norm):
    t, d = h.shape
    tm = TM_COMBINE
    smem_tile = pl.BlockSpec((tm,), lambda i: (i,), memory_space=pltpu.SMEM)
    row = pl.BlockSpec((tm, d), lambda i: (i, 0))
    return pl.pallas_call(
        functools.partial(_combine_body, apply_norm=apply_norm),
        grid=(t // tm,),
        in_specs=[smem_tile, smem_tile, row,
                  pl.BlockSpec((tm, TOP_K), lambda i: (i, 0)),
                  pl.BlockSpec((1, d), lambda i: (0, 0)),
                  pl.BlockSpec(memory_space=pl.ANY)],
        out_specs=row,
        out_shape=jax.ShapeDtypeStruct((t, d), F32),
        scratch_shapes=[pltpu.VMEM((tm, d), F32), pltpu.VMEM((tm, d), F32), pltpu.SemaphoreType.DMA(())],
        compiler_params=_params("arbitrary"),
        name="combine",
    )(slot[:, 0], slot[:, 1], h, wgt, g.reshape(1, d), ys)


def _plain_norm_body(h_ref, g_ref, o_ref):
    o_ref[...] = _rms(h_ref[...], g_ref[...])


def _plain_norm(h, g):
    t, d = h.shape
    tm = TM_NORM
    row = pl.BlockSpec((tm, d), lambda i: (i, 0))
    return pl.pallas_call(
        _plain_norm_body,
        grid=(t // tm,),
        in_specs=[row, pl.BlockSpec((1, d), lambda i: (0, 0))],
        out_specs=row,
        out_shape=jax.ShapeDtypeStruct((t, d), F32),
        compiler_params=_params("parallel"),
        name="plain_norm",
    )(h, g.reshape(1, d))


def kernel(x, norm_mix, w_in, ssm_lambda_re, ssm_lambda_im, ssm_log_dt, ssm_b_re, ssm_b_im, ssm_c_re,
           ssm_c_im, ssm_d, ssm_w_glu, conv_w, w_br_a, w_br_b, w_br_c, w_out, norm_ffn, dense_w_gate,
           dense_w_up, dense_w_down, moe_w_router, moe_w_gate, moe_w_up, moe_w_down, final_norm):
    batch, seq, d = x.shape
    depth = w_in.shape[0]
    t = batch * seq
    w_attn = w_br_a.shape[1]
    w_ssm = w_br_b.shape[1]
    w_conv = w_br_c.shape[1]
    n_sc = w_ssm + 3 * w_conv
    n_gate = N_BRANCH * d
    n_experts = moe_w_router.shape[-1]

    h = x.reshape(t, d)
    normed = False
    for i in range(depth):
        qt, k, vt, sc, gates = _inproj(h, norm_mix[i], w_in[i].astype(BF16), batch=batch, seq=seq,
                                       width=w_attn, n_sc=n_sc, n_gate=n_gate)
        attn = _sb_attention(qt, k, vt)
        tables = _ssm_tables(ssm_lambda_re[i], ssm_lambda_im[i], ssm_log_dt[i], ssm_b_re[i], ssm_b_im[i],
                             ssm_c_re[i], ssm_c_im[i], seq // SSM_CHUNK)
        yssm = _ssm_scan(sc[:, :w_ssm], tables, batch=batch, seq=seq)
        dense = i % 2 == 0
        j = i // 2
        merged = _merge(attn, yssm, sc, gates, h, ssm_d[i], ssm_w_glu[i].astype(BF16), conv_w[i],
                        w_br_a[i].astype(BF16), w_br_b[i].astype(BF16), w_br_c[i].astype(BF16),
                        w_out[i].astype(BF16), norm_ffn[i], None if dense else moe_w_router[j], seq=seq)
        if dense:
            h, xn = merged
            assert i + 1 < depth
            f = moe_w_gate.shape[-1]
            expert_f32 = (moe_w_gate[j].reshape(n_experts * d, f), moe_w_up[j].reshape(n_experts * d, f),
                          moe_w_down[j].reshape(n_experts * f, d))
            h, expert_bf16 = _dense_ffn(xn, h, dense_w_gate[j], dense_w_up[j], dense_w_down[j], expert_f32)
        else:
            h, xn, idx, wgt = merged
            slot, cap, tile_expert, tile_valid, zero_start, zero_rows = _moe_plan(idx, n_experts, TM_MOE)
            xs = _scatter_rows(xn, slot[:, 0], slot[:, 1], zero_start, cap, zero_rows)
            e_gate, e_up, e_down = expert_bf16
            ys = _moe_ffn(xs, tile_expert, tile_valid, e_gate.reshape(n_experts, d, -1),
                          e_up.reshape(n_experts, d, -1), e_down.reshape(n_experts, -1, d))
            last = i == depth - 1
            h = _combine(h, ys, slot, wgt, final_norm if last else norm_ffn[i], apply_norm=last)
            normed = last
    out = h if normed else _plain_norm(h, final_norm)
    return out.reshape(batch, seq, d)
```

```python
import functools

import jax
import jax.numpy as jnp
from jax import lax
from jax.experimental import pallas as pl
from jax.experimental.pallas import tpu as pltpu

F32 = jnp.float32
BF16 = jnp.bfloat16

EPS = 1e-6
HEAD_DIM = 64
SSM_CHUNK = 16
SSM_MAX_RE = -1e-4
CONV_K = 3
N_BRANCH = 3
TOP_K = 2
LOG2E = 1.4426950408889634
MASKED_LOG_WEIGHT = -1e30

LANES = 128
SUBLANES = 8
MXU_DIM = 256
VMEM_LIMIT_BYTES = 56 * 1024 * 1024

TM_PROJ = 512
TQ_ATTN = MXU_DIM
TM_MERGE = 512
TM_FFN = 256
TF_FFN = 1792
TM_MOE = 512
TM_SCATTER = 512
TM_COMBINE = 512
TM_NORM = 512


def _params(*sem):
    return pltpu.CompilerParams(dimension_semantics=sem, vmem_limit_bytes=VMEM_LIMIT_BYTES)


def _rms(x, g):
    ms = jnp.mean(x * x, axis=-1, keepdims=True)
    return x * lax.rsqrt(ms + EPS) * g


def _inproj_body(x_ref, g_ref, w_hbm, qt_ref, k_ref, vt_ref, u3_ref, sc_ref, gate_ref, w_ref, stage_ref, u_ref, sem,
                 *, layer, width, w_ssm, n_sc, n_gate, q_scale):
    @pl.when(pl.program_id(0) == 0)
    def _():
        _load_as_bf16(w_hbm.at[layer], w_ref, stage_ref, sem)

    xn = _rms(x_ref[...], g_ref[...]).astype(BF16)
    tm = xn.shape[0]
    tq = vt_ref.shape[-1]
    chunk = 512

    def proj(c0, n):
        return jnp.dot(xn, w_ref[:, c0:c0 + n], preferred_element_type=F32)

    qt_ref[0] = (proj(0, width) * q_scale).T.astype(BF16)
    k_ref[...] = proj(width, width).astype(BF16)
    v = proj(2 * width, width)
    for kt in range(tm // tq):
        for p in range(width // LANES):
            vt_ref[0, p, kt] = v[kt * tq:(kt + 1) * tq, p * LANES:(p + 1) * LANES].T.astype(BF16)
    for c0 in range(0, n_sc, chunk):
        r = proj(3 * width + c0, chunk)
        sc_ref[:, c0:c0 + chunk] = r.astype(BF16)
        if c0 == 0:
            for lb in range(w_ssm // LANES):
                u_ref[lb] = r[:, lb * LANES:(lb + 1) * LANES]
    n_t = u3_ref.shape[0]
    for step in range(n_t):
        for lb in range(w_ssm // LANES):
            u3_ref[step, :, lb * LANES:(lb + 1) * LANES] = (
                u_ref[lb, pl.ds(step, tm // n_t, stride=n_t), :].astype(BF16))
    for c0 in range(0, n_gate, chunk):
        gate_ref[:, c0:c0 + chunk] = jax.nn.sigmoid(proj(3 * width + n_sc + c0, chunk)).astype(BF16)


def _inproj(h, g, w_in, layer, *, batch, seq, width, w_ssm, n_sc, n_gate):
    t, d = h.shape
    tm = TM_PROJ
    tq = TQ_ATTN
    n_pairs = width // LANES
    tiles_per_seq = seq // tm
    n_cols = 3 * width + n_sc + n_gate
    stage_rows = 64
    assert w_in.shape[1:] == (d, n_cols) and d % stage_rows == 0 and w_ssm <= 512
    body = functools.partial(_inproj_body, layer=layer, width=width, w_ssm=w_ssm, n_sc=n_sc, n_gate=n_gate,
                             q_scale=HEAD_DIM ** -0.5)
    return pl.pallas_call(
        body,
        grid=(t // tm,),
        in_specs=[
            pl.BlockSpec((tm, d), lambda i: (i, 0)),
            pl.BlockSpec((1, d), lambda i: (0, 0)),
            pl.BlockSpec(memory_space=pl.ANY),
        ],
        out_specs=[
            pl.BlockSpec((1, width, tm), lambda i: (i // tiles_per_seq, 0, i % tiles_per_seq)),
            pl.BlockSpec((tm, width), lambda i: (i, 0)),
            pl.BlockSpec((1, n_pairs, tm // tq, LANES, tq),
                         lambda i: (i // tiles_per_seq, 0, i % tiles_per_seq, 0, 0)),
            pl.BlockSpec((SSM_CHUNK, tm // SSM_CHUNK, w_ssm), lambda i: (0, i, 0)),
            pl.BlockSpec((tm, n_sc), lambda i: (i, 0)),
            pl.BlockSpec((tm, n_gate), lambda i: (i, 0)),
        ],
        out_shape=[
            jax.ShapeDtypeStruct((batch, width, seq), BF16),
            jax.ShapeDtypeStruct((t, width), BF16),
            jax.ShapeDtypeStruct((batch, n_pairs, seq // tq, LANES, tq), BF16),
            jax.ShapeDtypeStruct((SSM_CHUNK, t // SSM_CHUNK, w_ssm), BF16),
            jax.ShapeDtypeStruct((t, n_sc), BF16),
            jax.ShapeDtypeStruct((t, n_gate), BF16),
        ],
        scratch_shapes=[pltpu.VMEM((d, n_cols), BF16), pltpu.VMEM((2, stage_rows, n_cols), F32),
                        pltpu.VMEM((w_ssm // LANES, tm, LANES), F32), pltpu.SemaphoreType.DMA((2,))],
        compiler_params=_params("arbitrary"),
        name="inproj",
    )(h, g.reshape(1, d), w_in)


def _sb_attn_body(qt_ref, k_ref, vt_ref, o_ref, acc_ref, carry_ref, lb_a_ref, sp_a_ref, lb_b_ref, sp_b_ref,
                  *, tq, n_pairs):
    i = pl.program_id(1)
    buf_a = (lb_a_ref, sp_a_ref)
    buf_b = (lb_b_ref, sp_b_ref)
    n_heads = 2 * n_pairs
    ones_rows = carry_ref.shape[1]
    key = lax.broadcasted_iota(jnp.int32, (tq, tq), 0)
    qry = lax.broadcasted_iota(jnp.int32, (tq, tq), 1)
    valid = key < qry
    er = lax.broadcasted_iota(jnp.int32, (tq + ones_rows, tq), 0)
    ec = lax.broadcasted_iota(jnp.int32, (tq + ones_rows, tq), 1)
    later_and_sum = jnp.where((ec > er) | (er >= tq), 1.0, 0.0).astype(BF16)

    feat = lax.broadcasted_iota(jnp.int32, (LANES, tq), 0)
    q_heads = []
    for p in range(n_pairs):
        qp = qt_ref[0, p * LANES:(p + 1) * LANES, :]
        zero = jnp.zeros_like(qp)
        q_heads.append(jnp.where(feat < HEAD_DIM, qp, zero))
        q_heads.append(jnp.where(feat >= HEAD_DIM, qp, zero))

    def scores(j, buf, masked):
        lb_ref, sp_ref = buf
        start = pl.multiple_of(j * tq, tq)
        for h in range(n_heads):
            p = h // 2
            kblk = k_ref[pl.ds(start, tq), p * LANES:(p + 1) * LANES]
            z = jnp.dot(kblk, q_heads[h], preferred_element_type=F32)
            l1p = jnp.log(1.0 + jnp.exp2(jnp.abs(z) * (-LOG2E)))
            sp = jnp.maximum(z, 0.0) + l1p
            lb = z - sp
            if masked:
                sp = jnp.where(valid, sp, 0.0)
                lb = jnp.where(valid, lb, MASKED_LOG_WEIGHT)
            lb_ref[h] = lb
            sp_ref[h] = sp.astype(BF16)

    def weights(j, buf):
        lb_ref, sp_ref = buf
        ws, sums = [], []
        for h in range(n_heads):
            ts = jnp.dot(later_and_sum, sp_ref[h], preferred_element_type=F32)
            ws.append(jnp.exp(lb_ref[h] - ts[0:tq]).astype(BF16))
            sums.append(ts[tq:tq + ones_rows])
        for h in range(n_heads):
            p, hh = divmod(h, 2)
            vt = vt_ref[0, p, j, hh * HEAD_DIM:(hh + 1) * HEAD_DIM, :]
            pv = jnp.dot(vt, ws[h], preferred_element_type=F32)
            carry = carry_ref[h]
            scale = jnp.exp(-carry[0:1, :])
            rows = slice(h * HEAD_DIM, (h + 1) * HEAD_DIM)
            acc_ref[rows, :] += pv * scale
            carry_ref[h] = carry + sums[h]

    def earlier_keys_matter():
        return (jnp.max(jnp.exp(-carry_ref[:, 0, :])) > 0.0).astype(jnp.int32)

    acc_ref[...] = jnp.zeros_like(acc_ref)
    carry_ref[...] = jnp.zeros_like(carry_ref)

    @pl.when(i == 0)
    def _():
        scores(0, buf_a, True)
        weights(0, buf_a)

    @pl.when(i > 0)
    def _():
        scores(i, buf_a, True)
        scores(i - 1, buf_b, False)
        weights(i, buf_a)
        weights(i - 1, buf_b)

    rest = i - 2

    @pl.when(jnp.where(rest >= 0, earlier_keys_matter(), 0) > 0)
    def _():
        scores(rest, buf_a, False)

        def two_tiles(c):
            m, _ = c
            j = rest - 1 - 2 * m
            scores(j, buf_b, False)
            weights(j + 1, buf_a)
            scores(j - 1, buf_a, False)
            weights(j, buf_b)
            return m + 1, earlier_keys_matter()

        _, alive = lax.while_loop(lambda c: (c[0] < rest // 2) & (c[1] > 0), two_tiles,
                                  (jnp.int32(0), jnp.int32(1)))

        @pl.when((alive > 0) & (rest % 2 == 1))
        def _():
            scores(0, buf_b, False)
            weights(1, buf_a)
            weights(0, buf_b)

        @pl.when((alive > 0) & (rest % 2 == 0))
        def _():
            weights(0, buf_a)

    for p in range(n_pairs):
        o_ref[:, p * LANES:(p + 1) * LANES] = acc_ref[p * LANES:(p + 1) * LANES, :].T.astype(o_ref.dtype)


def _sb_attention(qt, k, vt):
    batch, width, seq = qt.shape
    n_pairs, nq, _, tq = vt.shape[1:]
    n_heads = 2 * n_pairs
    return pl.pallas_call(
        functools.partial(_sb_attn_body, tq=tq, n_pairs=n_pairs),
        grid=(batch, nq),
        in_specs=[
            pl.BlockSpec((1, width, tq), lambda b, i: (b, 0, i)),
            pl.BlockSpec((seq, width), lambda b, i: (b, 0)),
            pl.BlockSpec((1, n_pairs, nq, LANES, tq), lambda b, i: (b, 0, 0, 0, 0)),
        ],
        out_specs=pl.BlockSpec((tq, width), lambda b, i: (b * nq + i, 0)),
        out_shape=jax.ShapeDtypeStruct((batch * seq, width), BF16),
        scratch_shapes=[pltpu.VMEM((width, tq), F32), pltpu.VMEM((n_heads, 16, tq), F32),
                        pltpu.VMEM((n_heads, tq, tq), F32), pltpu.VMEM((n_heads, tq, tq), BF16),
                        pltpu.VMEM((n_heads, tq, tq), F32), pltpu.VMEM((n_heads, tq, tq), BF16)],
        compiler_params=_params("parallel", "arbitrary"),
        name="sb_attention",
    )(qt, k, vt)


def _ssm_tables(lam_re, lam_im, log_dt, b_re, b_im, c_re, c_im, n_chunks):
    g, p = lam_re.shape
    c = SSM_CHUNK
    lr = jnp.minimum(lam_re.astype(F32), SSM_MAX_RE)
    li = lam_im.astype(F32)
    dt = jnp.exp(log_dt.astype(F32))[:, None]

    def lbar_pow(k):
        mag = jnp.exp(k * (lr * dt))
        ang = k * (li * dt)
        return mag * jnp.cos(ang), mag * jnp.sin(ang)

    ab_re, ab_im = lbar_pow(1.0)
    den = lr * lr + li * li
    nr = ab_re - 1.0
    ni = ab_im
    f_re = (nr * lr + ni * li) / den
    f_im = (ni * lr - nr * li) / den
    br = b_re.astype(F32)
    bi = b_im.astype(F32)
    bb_re = f_re[..., None] * br - f_im[..., None] * bi
    bb_im = f_re[..., None] * bi + f_im[..., None] * br
    cr = c_re.astype(F32)
    ci = c_im.astype(F32)

    eye = jnp.eye(g, dtype=F32)

    def block_diag(x):
        _, a, b = x.shape
        return (x[:, :, None, :] * eye[:, None, :, None]).reshape(g * a, g * b)

    b_in = jnp.concatenate([block_diag(bb_re.transpose(0, 2, 1)),
                            block_diag(bb_im.transpose(0, 2, 1))], axis=1)
    c_out = jnp.concatenate([block_diag(cr.transpose(0, 2, 1)),
                             block_diag(-ci.transpose(0, 2, 1))], axis=0)
    lbar = jnp.stack([ab_re.reshape(-1), ab_im.reshape(-1)], axis=0)

    steps = []
    s = 1
    while s < n_chunks:
        a_re, a_im = lbar_pow(float(c * s))
        steps.append(jnp.stack([a_re.reshape(-1, MXU_DIM), a_im.reshape(-1, MXU_DIM)], axis=1))
        s *= 2
    step = jnp.stack(steps, axis=1)
    return b_in.astype(BF16), c_out.astype(BF16), lbar, step


def _ssm_body(u_ref, b_ref, c_ref, lbar_ref, step_ref, y_ref, z_ref, *, n_chunks, batch):
    s = pl.program_id(0)
    c = u_ref.shape[0]
    n_blk = z_ref.shape[0]
    half = n_blk // 2
    bw = z_ref.shape[2]

    @pl.when(s == 0)
    def _():
        z_ref[...] = jnp.zeros_like(z_ref)

    def advance(ut):
        for pb in range(half):
            lo, hi = pb * bw, (pb + 1) * bw
            re = z_ref[pb]
            im = z_ref[half + pb]
            a_re = lbar_ref[0:1, lo:hi]
            a_im = lbar_ref[1:2, lo:hi]
            x_re = jnp.dot(ut, b_ref[:, lo:hi], preferred_element_type=F32)
            x_im = jnp.dot(ut, b_ref[:, half * bw + lo:half * bw + hi], preferred_element_type=F32)
            z_ref[pb] = a_re * re - a_im * im + x_re
            z_ref[half + pb] = a_re * im + a_im * re + x_im

    @pl.when(s < c)
    def _():
        advance(u_ref[s])

    @pl.when(s == c)
    def _():
        chunk = lax.broadcasted_iota(jnp.int32, (n_chunks, bw), 0)

        def shifted(x, n):
            return jnp.where(chunk >= n, pltpu.roll(x, n, axis=0), 0.0)

        def scan_block(idx, carry):
            b = idx // half
            pb = idx % half
            rows = pl.ds(pl.multiple_of(b * n_chunks, n_chunks), n_chunks)
            re = z_ref[pb, rows, :]
            im = z_ref[half + pb, rows, :]
            n = 1
            k = 0
            while n < n_chunks:
                a_re = step_ref[pb, k, 0:1, :]
                a_im = step_ref[pb, k, 1:2, :]
                re_s = shifted(re, n)
                im_s = shifted(im, n)
                re, im = re + a_re * re_s - a_im * im_s, im + a_re * im_s + a_im * re_s
                n *= 2
                k += 1
            z_ref[pb, rows, :] = shifted(re, 1)
            z_ref[half + pb, rows, :] = shifted(im, 1)
            return carry

        lax.fori_loop(0, batch * half, scan_block, 0)

    @pl.when(s >= c)
    def _():
        advance(u_ref[s - c])
        y = jnp.dot(z_ref[0].astype(BF16), c_ref[0:bw, :], preferred_element_type=F32)
        for blk in range(1, n_blk):
            y += jnp.dot(z_ref[blk].astype(BF16), c_ref[blk * bw:(blk + 1) * bw, :],
                         preferred_element_type=F32)
        y_ref[0] = y.astype(y_ref.dtype)


def _ssm_scan(u3, tables, *, batch, seq):
    b_in, c_out, lbar, step = tables
    c, rows, width = u3.shape
    n_chunks = seq // c
    assert rows == batch * n_chunks
    n_blk = b_in.shape[1] // MXU_DIM
    whole = lambda shape: pl.BlockSpec(shape, lambda s: (0,) * len(shape), pipeline_mode=pl.Buffered(1))
    return pl.pallas_call(
        functools.partial(_ssm_body, n_chunks=n_chunks, batch=batch),
        grid=(2 * c,),
        in_specs=[whole(u3.shape), whole(b_in.shape), whole(c_out.shape), whole(lbar.shape), whole(step.shape)],
        out_specs=pl.BlockSpec((1, rows, width), lambda s: (jnp.maximum(s - c, 0), 0, 0)),
        out_shape=jax.ShapeDtypeStruct((c, rows, width), BF16),
        scratch_shapes=[pltpu.VMEM((n_blk, rows, MXU_DIM), F32)],
        compiler_params=_params("arbitrary"),
        name="ssm_scan",
    )(u3, b_in, c_out, lbar, step)


def _gelu_tanh(x):
    return 0.5 * x * (1.0 + jnp.tanh(0.7978845608028654 * (x + 0.044715 * (x * x * x))))


def _top2(xn, wr):
    logits = lax.dot_general(wr, xn, (((1,), (1,)), ((), ())), preferred_element_type=F32,
                             precision=lax.Precision.HIGHEST)
    n_e = logits.shape[0]
    e_idx = lax.broadcasted_iota(jnp.int32, logits.shape, 0)
    m1 = jnp.max(logits, axis=0, keepdims=True)
    i1 = jnp.min(jnp.where(logits == m1, e_idx, n_e), axis=0, keepdims=True)
    rest = jnp.where(e_idx == i1, -jnp.inf, logits)
    m2 = jnp.max(rest, axis=0, keepdims=True)
    i2 = jnp.min(jnp.where(rest == m2, e_idx, n_e), axis=0, keepdims=True)
    e2 = jnp.exp(m2 - m1)
    w1 = 1.0 / (1.0 + e2)
    return jnp.concatenate([i1, i2], axis=0), jnp.concatenate([w1, e2 * w1], axis=0)


def _merge_body(attn_ref, yssm_ref, sc_ref, halo_ref, gate_ref, h_ref, dskip_ref, wglu_ref, convw_ref,
                wa_ref, wb_ref, wc_ref, wout_ref, gn_ref, *rest, tiles_per_seq, w_ssm, w_conv, route):
    if route:
        wr_ref, hout_ref, xn_ref, idx_ref, wgt_ref, y_ref = rest
    else:
        hout_ref, xn_ref, y_ref = rest
    i = pl.program_id(0)
    tm = h_ref.shape[0]
    d = h_ref.shape[1]

    n_t = yssm_ref.shape[0]
    n_lb = w_ssm // LANES
    for step in range(n_t):
        for lb in range(n_lb):
            y_ref[lb, pl.ds(step, tm // n_t, stride=n_t), :] = (
                yssm_ref[step, :, lb * LANES:(lb + 1) * LANES].astype(F32))
    y = jnp.concatenate([y_ref[lb] for lb in range(n_lb)], axis=1)
    u = sc_ref[:, 0:w_ssm].astype(F32)
    yb = _gelu_tanh(y + dskip_ref[...] * u)
    yb = yb * jax.nn.sigmoid(jnp.dot(yb.astype(BF16), wglu_ref[...], preferred_element_type=F32))
    y_b = jnp.dot(yb.astype(BF16), wb_ref[...], preferred_element_type=F32)

    o_h, o_b, o_c = w_ssm, w_ssm + w_conv, w_ssm + 2 * w_conv
    uc = sc_ref[:, o_c:o_c + w_conv].astype(F32) * sc_ref[:, o_h:o_h + w_conv].astype(F32)
    halo = halo_ref[:, o_c:o_c + w_conv].astype(F32) * halo_ref[:, o_h:o_h + w_conv].astype(F32)
    halo = jnp.where(i % tiles_per_seq == 0, 0.0, halo)
    row = lax.broadcasted_iota(jnp.int32, (tm, w_conv), 0)
    prev1 = jnp.where(row == 0, halo[SUBLANES - 1:SUBLANES, :], pltpu.roll(uc, 1, axis=0))
    prev2 = jnp.where(row == 0, halo[SUBLANES - 2:SUBLANES - 1, :],
                      jnp.where(row == 1, halo[SUBLANES - 1:SUBLANES, :], pltpu.roll(uc, 2, axis=0)))
    conv = convw_ref[0:1, :] * prev2 + convw_ref[1:2, :] * prev1 + convw_ref[2:3, :] * uc
    yc = sc_ref[:, o_b:o_b + w_conv].astype(F32) * conv
    y_c = jnp.dot(yc.astype(BF16), wc_ref[...], preferred_element_type=F32)

    y_a = jnp.dot(attn_ref[...], wa_ref[...], preferred_element_type=F32)

    merged = (gate_ref[:, 0:d].astype(F32) * y_a + gate_ref[:, d:2 * d].astype(F32) * y_b
              + gate_ref[:, 2 * d:3 * d].astype(F32) * y_c)
    hn = h_ref[...] + jnp.dot(merged.astype(BF16), wout_ref[...], preferred_element_type=F32)
    hout_ref[...] = hn
    xn = _rms(hn, gn_ref[...])
    xn_ref[...] = xn.astype(xn_ref.dtype)
    if route:
        idx, wgt = _top2(xn, wr_ref[...])
        idx_ref[0] = idx
        wgt_ref[0] = wgt


def _merge(attn, yssm, sc, gates, h, d_skip, w_glu, conv_w, w_a, w_b, w_c, w_out, g_ffn, w_router, *, seq):
    t, d = h.shape
    tm = TM_MERGE
    nt = t // tm
    n_t, _, w_ssm = yssm.shape
    w_conv = conv_w.shape[1]
    w_attn = attn.shape[1]
    n_sc = sc.shape[1]
    route = w_router is not None
    body = functools.partial(_merge_body, tiles_per_seq=seq // tm, w_ssm=w_ssm, w_conv=w_conv, route=route)
    full = lambda shape: pl.BlockSpec(shape, lambda i: (0,) * len(shape))
    row = pl.BlockSpec((tm, d), lambda i: (i, 0))
    halo_blocks = tm // SUBLANES
    in_specs = [
        pl.BlockSpec((tm, w_attn), lambda i: (i, 0)),
        pl.BlockSpec((n_t, tm // n_t, w_ssm), lambda i: (0, i, 0)),
        pl.BlockSpec((tm, n_sc), lambda i: (i, 0)),
        pl.BlockSpec((SUBLANES, n_sc), lambda i: (jnp.maximum(i * halo_blocks - 1, 0), 0)),
        pl.BlockSpec((tm, N_BRANCH * d), lambda i: (i, 0)),
        row,
        full((1, w_ssm)), full((w_ssm, w_ssm)), full((CONV_K, w_conv)),
        full((w_attn, d)), full((w_ssm, d)), full((w_conv, d)), full((d, d)), full((1, d)),
    ]
    args = [attn, yssm, sc, sc, gates, h, d_skip.reshape(1, w_ssm).astype(F32), w_glu, conv_w.astype(F32),
            w_a, w_b, w_c, w_out, g_ffn.reshape(1, d)]
    out_specs = [row, row]
    out_shape = [jax.ShapeDtypeStruct((t, d), F32), jax.ShapeDtypeStruct((t, d), F32 if route else BF16)]
    if route:
        n_e = w_router.shape[1]
        in_specs.append(full((n_e, d)))
        args.append(w_router.T.astype(F32))
        lanes = pl.BlockSpec((1, TOP_K, tm), lambda i: (i, 0, 0))
        out_specs += [lanes, lanes]
        out_shape += [jax.ShapeDtypeStruct((nt, TOP_K, tm), jnp.int32), jax.ShapeDtypeStruct((nt, TOP_K, tm), F32)]
    outs = pl.pallas_call(
        body,
        grid=(nt,),
        in_specs=in_specs,
        out_specs=out_specs,
        out_shape=out_shape,
        scratch_shapes=[pltpu.VMEM((w_ssm // LANES, tm, LANES), F32)],
        compiler_params=_params("parallel"),
        name="merge",
    )(*args)
    if not route:
        return outs
    hn, xn, idx, wgt = outs
    return hn, xn, idx.transpose(0, 2, 1).reshape(t, TOP_K), wgt.transpose(0, 2, 1).reshape(t, TOP_K)


def _swiglu_hidden(x, wg_ref, wu_ref, act_ref, j):
    gate = jnp.dot(x, wg_ref[...], preferred_element_type=F32)
    up = jnp.dot(x, wu_ref[...], preferred_element_type=F32)
    act_ref[j] = (gate * jax.nn.sigmoid(gate) * up).astype(act_ref.dtype)


def _swiglu_down(act_ref, wd_ref):
    act = jnp.concatenate([act_ref[jj] for jj in range(act_ref.shape[0])], axis=1)
    return jnp.dot(act, wd_ref[...], preferred_element_type=F32)


def _load_as_bf16(src_hbm, dst_ref, stage_ref, sem):
    rows = stage_ref.shape[1]
    n_chunks = src_hbm.shape[0] // rows

    def copy(c):
        return pltpu.make_async_copy(src_hbm.at[pl.ds(c * rows, rows)], stage_ref.at[c % 2], sem.at[c % 2])

    copy(0).start()
    for c in range(n_chunks):
        if c + 1 < n_chunks:
            copy(c + 1).start()
        copy(c).wait()
        dst_ref[c * rows:(c + 1) * rows, :] = stage_ref[c % 2].astype(dst_ref.dtype)


def _dense_ffn_body(x_ref, h_ref, wg_hbm, wu_hbm, wd_hbm, e0_hbm, e1_hbm, e2_hbm, o_ref, n0_hbm, n1_hbm, n2_hbm,
                    wg_ref, wu_ref, wd_ref, act_ref, st0_ref, st1_ref, st2_ref, nr0_ref, nr1_ref, nr2_ref,
                    in_sem, out_sem):
    s = pl.program_id(0)
    n_steps = pl.num_programs(0)
    stages = (st0_ref, st1_ref, st2_ref)
    narrow = (nr0_ref, nr1_ref, nr2_ref)

    def fetch(step, slot):
        return [pltpu.make_async_copy(src.at[pl.ds(step * st.shape[1], st.shape[1])], st.at[slot], in_sem.at[slot])
                for src, st in zip((e0_hbm, e1_hbm, e2_hbm), stages)]

    def flush(step, slot):
        return [pltpu.make_async_copy(nr.at[slot], dst.at[pl.ds(step * nr.shape[1], nr.shape[1])], out_sem.at[slot])
                for nr, dst in zip(narrow, (n0_hbm, n1_hbm, n2_hbm))]

    @pl.when(s == 0)
    def _():
        _load_as_bf16(wg_hbm, wg_ref, st0_ref, in_sem)
        _load_as_bf16(wu_hbm, wu_ref, st1_ref, in_sem)
        _load_as_bf16(wd_hbm, wd_ref, st2_ref, in_sem)
        for cp in fetch(0, 0):
            cp.start()

    @pl.when(s + 1 < n_steps)
    def _():
        for cp in fetch(s + 1, (s + 1) % 2):
            cp.start()

    x = x_ref[...]
    tf = act_ref.shape[2]
    for j in range(act_ref.shape[0]):
        _swiglu_hidden(x, wg_ref.at[:, j * tf:(j + 1) * tf], wu_ref.at[:, j * tf:(j + 1) * tf], act_ref, j)
    o_ref[...] = h_ref[...] + _swiglu_down(act_ref, wd_ref)

    slot = s % 2
    for cp in fetch(s, slot):
        cp.wait()

    @pl.when(s >= 2)
    def _():
        for cp in flush(s - 2, slot):
            cp.wait()

    for st, nr in zip(stages, narrow):
        nr[slot] = st[slot].astype(nr.dtype)
    for cp in flush(s, slot):
        cp.start()

    @pl.when(s == n_steps - 1)
    def _():
        @pl.when(s >= 1)
        def _():
            for cp in flush(s - 1, 1 - slot):
                cp.wait()

        for cp in flush(s, slot):
            cp.wait()


def _dense_ffn(xn, h, w_gate, w_up, w_down, extra):
    t, d = h.shape
    f = w_gate.shape[1]
    tm, tf = TM_FFN, TF_FFN
    n_steps = t // tm
    rows = [e.shape[0] // n_steps for e in extra]
    assert all(e.shape[0] == r * n_steps and r % 16 == 0 for e, r in zip(extra, rows))
    assert w_gate.shape[0] % rows[0] == 0 and w_up.shape[0] % rows[1] == 0 and w_down.shape[0] % rows[2] == 0
    assert extra[0].shape[1] == f and extra[1].shape[1] == f and extra[2].shape[1] == d
    row = pl.BlockSpec((tm, d), lambda i: (i, 0))
    hbm = pl.BlockSpec(memory_space=pl.ANY)
    outs = pl.pallas_call(
        _dense_ffn_body,
        grid=(n_steps,),
        in_specs=[row, row] + [hbm] * 6,
        out_specs=[row, hbm, hbm, hbm],
        out_shape=[jax.ShapeDtypeStruct((t, d), F32)] + [jax.ShapeDtypeStruct(e.shape, BF16) for e in extra],
        scratch_shapes=[pltpu.VMEM((d, f), BF16), pltpu.VMEM((d, f), BF16), pltpu.VMEM((f, d), BF16),
                        pltpu.VMEM((f // tf, tm, tf), BF16)]
                       + [pltpu.VMEM((2, r, e.shape[1]), F32) for e, r in zip(extra, rows)]
                       + [pltpu.VMEM((2, r, e.shape[1]), BF16) for e, r in zip(extra, rows)]
                       + [pltpu.SemaphoreType.DMA((2,)), pltpu.SemaphoreType.DMA((2,))],
        compiler_params=_params("arbitrary"),
        name="dense_ffn",
    )(xn, h, w_gate, w_up, w_down, *extra)
    return outs[0], outs[1:]


def _scatter_rows_body(zero_start_ref, s1_ref, s2_ref, h_ref, xs_ref, zeros_ref, sem):
    tm = h_ref.shape[0]

    @pl.when(pl.program_id(0) == 0)
    def _():
        zeros_ref[...] = jnp.zeros_like(zeros_ref)

        def fill(k, c):
            dst = xs_ref.at[pl.ds(pl.multiple_of(zero_start_ref[k], SUBLANES), zeros_ref.shape[0])]
            cp = pltpu.make_async_copy(zeros_ref, dst, sem)
            cp.start()
            cp.wait()
            return c

        lax.fori_loop(0, zero_start_ref.shape[0], fill, 0)

    def issue(r, c):
        src = h_ref.at[pl.ds(r, 1)]
        pltpu.make_async_copy(src, xs_ref.at[pl.ds(s1_ref[r], 1)], sem).start(priority=0)
        pltpu.make_async_copy(src, xs_ref.at[pl.ds(s2_ref[r], 1)], sem).start(priority=1)
        return c

    lax.fori_loop(0, tm, issue, 0, unroll=8)
    pltpu.make_async_copy(h_ref, xs_ref.at[pl.ds(0, tm)], sem).wait()
    pltpu.make_async_copy(h_ref, xs_ref.at[pl.ds(0, tm)], sem).wait()


def _scatter_rows(h, slot1, slot2, zero_start, cap, zero_rows):
    t, d = h.shape
    tm = TM_SCATTER
    smem_tile = pl.BlockSpec((tm,), lambda i, zs: (i,), memory_space=pltpu.SMEM)
    grid_spec = pltpu.PrefetchScalarGridSpec(
        num_scalar_prefetch=1,
        grid=(t // tm,),
        in_specs=[smem_tile, smem_tile, pl.BlockSpec((tm, d), lambda i, zs: (i, 0))],
        out_specs=pl.BlockSpec(memory_space=pl.ANY),
        scratch_shapes=[pltpu.VMEM((zero_rows, d), h.dtype), pltpu.SemaphoreType.DMA(())],
    )
    return pl.pallas_call(
        _scatter_rows_body,
        grid_spec=grid_spec,
        out_shape=jax.ShapeDtypeStruct((cap, d), h.dtype),
        compiler_params=_params("arbitrary"),
        name="scatter_rows",
    )(zero_start, slot1, slot2, h)


def _moe_ffn_body(tile_expert_ref, tile_valid_ref, x_ref, wg_ref, wu_ref, wd_ref, o_ref, act_ref):
    i = pl.program_id(0)
    j = pl.program_id(1)
    last = j == pl.num_programs(1) - 1
    valid = tile_valid_ref[i] > 0

    @pl.when(valid)
    def _():
        _swiglu_hidden(x_ref[...].astype(BF16), wg_ref.at[0], wu_ref.at[0], act_ref, j)

        @pl.when(last)
        def _():
            o_ref[...] = _swiglu_down(act_ref, wd_ref.at[0])

    @pl.when(last & jnp.logical_not(valid))
    def _():
        o_ref[...] = jnp.zeros_like(o_ref)


def _moe_ffn(xs, tile_expert, tile_valid, w_gate, w_up, w_down):
    cap, d = xs.shape
    f = w_gate.shape[2]
    tm, tf = TM_MOE, TF_FFN
    grid_spec = pltpu.PrefetchScalarGridSpec(
        num_scalar_prefetch=2,
        grid=(cap // tm, f // tf),
        in_specs=[
            pl.BlockSpec((tm, d), lambda i, j, te, tv: (i, 0)),
            pl.BlockSpec((1, d, tf), lambda i, j, te, tv: (te[i], 0, j)),
            pl.BlockSpec((1, d, tf), lambda i, j, te, tv: (te[i], 0, j)),
            pl.BlockSpec((1, f, d), lambda i, j, te, tv: (te[i], 0, 0)),
        ],
        out_specs=pl.BlockSpec((tm, d), lambda i, j, te, tv: (i, 0)),
        scratch_shapes=[pltpu.VMEM((f // tf, tm, tf), BF16)],
    )
    return pl.pallas_call(
        _moe_ffn_body,
        grid_spec=grid_spec,
        out_shape=jax.ShapeDtypeStruct((cap, d), F32),
        compiler_params=_params("parallel", "arbitrary"),
        name="moe_ffn",
    )(tile_expert, tile_valid, xs, w_gate, w_up, w_down)


def _moe_plan(idx, n_experts, tm):
    t = idx.shape[0]
    pairs = t * TOP_K
    cap = pairs + n_experts * tm
    e_flat = idx.reshape(pairs)
    onehot = (e_flat[:, None] == jnp.arange(n_experts, dtype=jnp.int32)[None, :]).astype(jnp.int32)
    rank = jnp.sum(onehot * (jnp.cumsum(onehot, axis=0) - onehot), axis=1)
    counts = jnp.sum(onehot, axis=0)
    padded = ((counts + tm - 1) // tm) * tm
    ends = jnp.cumsum(padded)
    starts = ends - padded
    slot = jnp.sum(onehot * starts[None, :], axis=1) + rank
    tile_start = jnp.arange(cap // tm, dtype=jnp.int32) * tm
    tile_expert = jnp.sum((tile_start[:, None] >= ends[None, :]).astype(jnp.int32), axis=1)
    tile_valid = (tile_expert < n_experts).astype(jnp.int32)
    last_used = jnp.max(jnp.where(counts > 0, jnp.arange(n_experts, dtype=jnp.int32), 0))
    tile_expert = jnp.minimum(tile_expert, last_used).astype(jnp.int32)
    zero_rows = tm + SUBLANES
    tail = ends[-1] + jnp.arange(n_experts, dtype=jnp.int32) * tm
    first = jnp.concatenate([starts + counts, tail]) // SUBLANES * SUBLANES
    zero_start = jnp.minimum(first, cap - zero_rows).astype(jnp.int32)
    return slot.reshape(t, TOP_K).astype(jnp.int32), cap, tile_expert, tile_valid, zero_start, zero_rows


def _combine_body(s1_ref, s2_ref, h_ref, wgt_ref, g_ref, ys_ref, o_ref, y1_ref, y2_ref, sem, *, apply_norm):
    tm = h_ref.shape[0]

    def issue(r, c):
        pltpu.make_async_copy(ys_ref.at[pl.ds(s1_ref[r], 1)], y1_ref.at[pl.ds(r, 1)], sem).start(priority=0)
        pltpu.make_async_copy(ys_ref.at[pl.ds(s2_ref[r], 1)], y2_ref.at[pl.ds(r, 1)], sem).start(priority=1)
        return c

    lax.fori_loop(0, tm, issue, 0, unroll=8)
    pltpu.make_async_copy(ys_ref.at[pl.ds(0, tm)], y1_ref, sem).wait()
    pltpu.make_async_copy(ys_ref.at[pl.ds(0, tm)], y2_ref, sem).wait()
    w = wgt_ref[...]
    hn = h_ref[...] + w[:, 0:1] * y1_ref[...] + w[:, 1:2] * y2_ref[...]
    o_ref[...] = _rms(hn, g_ref[...]) if apply_norm else hn


def _combine(h, ys, slot, wgt, g, *, apply_norm):
    t, d = h.shape
    tm = TM_COMBINE
    smem_tile = pl.BlockSpec((tm,), lambda i: (i,), memory_space=pltpu.SMEM)
    row = pl.BlockSpec((tm, d), lambda i: (i, 0))
    return pl.pallas_call(
        functools.partial(_combine_body, apply_norm=apply_norm),
        grid=(t // tm,),
        in_specs=[smem_tile, smem_tile, row,
                  pl.BlockSpec((tm, TOP_K), lambda i: (i, 0)),
                  pl.BlockSpec((1, d), lambda i: (0, 0)),
                  pl.BlockSpec(memory_space=pl.ANY)],
        out_specs=row,
        out_shape=jax.ShapeDtypeStruct((t, d), F32),
        scratch_shapes=[pltpu.VMEM((tm, d), F32), pltpu.VMEM((tm, d), F32), pltpu.SemaphoreType.DMA(())],
        compiler_params=_params("arbitrary"),
        name="combine",
    )(slot[:, 0], slot[:, 1], h, wgt, g.reshape(1, d), ys)


def _plain_norm_body(h_ref, g_ref, o_ref):
    o_ref[...] = _rms(h_ref[...], g_ref[...])


def _plain_norm(h, g):
    t, d = h.shape
    tm = TM_NORM
    row = pl.BlockSpec((tm, d), lambda i: (i, 0))
    return pl.pallas_call(
        _plain_norm_body,
        grid=(t // tm,),
        in_specs=[row, pl.BlockSpec((1, d), lambda i: (0, 0))],
        out_specs=row,
        out_shape=jax.ShapeDtypeStruct((t, d), F32),
        compiler_params=_params("parallel"),
        name="plain_norm",
    )(h, g.reshape(1, d))


def kernel(x, norm_mix, w_in, ssm_lambda_re, ssm_lambda_im, ssm_log_dt, ssm_b_re, ssm_b_im, ssm_c_re,
           ssm_c_im, ssm_d, ssm_w_glu, conv_w, w_br_a, w_br_b, w_br_c, w_out, norm_ffn, dense_w_gate,
           dense_w_up, dense_w_down, moe_w_router, moe_w_gate, moe_w_up, moe_w_down, final_norm):
    batch, seq, d = x.shape
    depth = w_in.shape[0]
    t = batch * seq
    w_attn = w_br_a.shape[1]
    w_ssm = w_br_b.shape[1]
    w_conv = w_br_c.shape[1]
    n_sc = w_ssm + 3 * w_conv
    n_gate = N_BRANCH * d
    n_experts = moe_w_router.shape[-1]

    h = x.reshape(t, d)
    normed = False
    for i in range(depth):
        qt, k, vt, u3, sc, gates = _inproj(h, norm_mix[i], w_in, i, batch=batch, seq=seq, width=w_attn,
                                           w_ssm=w_ssm, n_sc=n_sc, n_gate=n_gate)
        attn = _sb_attention(qt, k, vt)
        tables = _ssm_tables(ssm_lambda_re[i], ssm_lambda_im[i], ssm_log_dt[i], ssm_b_re[i], ssm_b_im[i],
                             ssm_c_re[i], ssm_c_im[i], seq // SSM_CHUNK)
        yssm = _ssm_scan(u3, tables, batch=batch, seq=seq)
        dense = i % 2 == 0
        j = i // 2
        merged = _merge(attn, yssm, sc, gates, h, ssm_d[i], ssm_w_glu[i].astype(BF16), conv_w[i],
                        w_br_a[i].astype(BF16), w_br_b[i].astype(BF16), w_br_c[i].astype(BF16),
                        w_out[i].astype(BF16), norm_ffn[i], None if dense else moe_w_router[j], seq=seq)
        if dense:
            h, xn = merged
            assert i + 1 < depth
            f = moe_w_gate.shape[-1]
            expert_f32 = (moe_w_gate[j].reshape(n_experts * d, f), moe_w_up[j].reshape(n_experts * d, f),
                          moe_w_down[j].reshape(n_experts * f, d))
            h, expert_bf16 = _dense_ffn(xn, h, dense_w_gate[j], dense_w_up[j], dense_w_down[j], expert_f32)
        else:
            h, xn, idx, wgt = merged
            slot, cap, tile_expert, tile_valid, zero_start, zero_rows = _moe_plan(idx, n_experts, TM_MOE)
            xs = _scatter_rows(xn, slot[:, 0], slot[:, 1], zero_start, cap, zero_rows)
            e_gate, e_up, e_down = expert_bf16
            ys = _moe_ffn(xs, tile_expert, tile_valid, e_gate.reshape(n_experts, d, -1),
                          e_up.reshape(n_experts, d, -1), e_down.reshape(n_experts, -1, d))
            last = i == depth - 1
            h = _combine(h, ys, slot, wgt, final_norm if last else norm_ffn[i], apply_norm=last)
            normed = last
    out = h if normed else _plain_norm(h, final_norm)
    return out.reshape(batch, seq, d)
```

```python
import functools

import jax
import jax.numpy as jnp
from jax import lax
from jax.experimental import pallas as pl
from jax.experimental.pallas import tpu as pltpu

F32 = jnp.float32
BF16 = jnp.bfloat16

EPS = 1e-6
HEAD_DIM = 64
SSM_CHUNK = 16
SSM_MAX_RE = -1e-4
CONV_K = 3
N_BRANCH = 3
TOP_K = 2
LOG2E = 1.4426950408889634
MASKED_LOG_WEIGHT = -1e30

LANES = 128
SUBLANES = 8
MXU_DIM = 256
VMEM_LIMIT_BYTES = 56 * 1024 * 1024

TM_PROJ = 512
TQ_ATTN = MXU_DIM
TM_MERGE = 512
TM_FFN = 256
TF_FFN = 1792
TM_MOE = 512
TM_SCATTER = 512
TM_COMBINE = 512
TM_NORM = 512


def _params(*sem):
    return pltpu.CompilerParams(dimension_semantics=sem, vmem_limit_bytes=VMEM_LIMIT_BYTES)


def _rms(x, g):
    ms = jnp.mean(x * x, axis=-1, keepdims=True)
    return x * lax.rsqrt(ms + EPS) * g


def _inproj_body(x_ref, g_ref, w_hbm, qt_ref, k_ref, vt_ref, u3_ref, sc_ref, gate_ref, w_ref, stage_ref, u_ref, sem,
                 *, layer, width, w_ssm, n_sc, n_gate, q_scale):
    @pl.when(pl.program_id(0) == 0)
    def _():
        _load_as_bf16(w_hbm.at[layer], w_ref, stage_ref, sem)

    xn = _rms(x_ref[...], g_ref[...]).astype(BF16)
    tm = xn.shape[0]
    tq = vt_ref.shape[-1]
    chunk = 512

    def proj(c0, n):
        return jnp.dot(xn, w_ref[:, c0:c0 + n], preferred_element_type=F32)

    qt_ref[0] = (proj(0, width) * q_scale).T.astype(BF16)
    k_ref[...] = proj(width, width).astype(BF16)
    v = proj(2 * width, width)
    for kt in range(tm // tq):
        for p in range(width // LANES):
            vt_ref[0, p, kt] = v[kt * tq:(kt + 1) * tq, p * LANES:(p + 1) * LANES].T.astype(BF16)
    for c0 in range(0, n_sc, chunk):
        r = proj(3 * width + c0, chunk)
        sc_ref[:, c0:c0 + chunk] = r.astype(BF16)
        if c0 == 0:
            for lb in range(w_ssm // LANES):
                u_ref[lb] = r[:, lb * LANES:(lb + 1) * LANES]
    n_t = u3_ref.shape[0]
    for step in range(n_t):
        for lb in range(w_ssm // LANES):
            u3_ref[step, :, lb * LANES:(lb + 1) * LANES] = (
                u_ref[lb, pl.ds(step, tm // n_t, stride=n_t), :].astype(BF16))
    for c0 in range(0, n_gate, chunk):
        gate_ref[:, c0:c0 + chunk] = jax.nn.sigmoid(proj(3 * width + n_sc + c0, chunk)).astype(BF16)


def _inproj(h, g, w_in, layer, *, batch, seq, width, w_ssm, n_sc, n_gate):
    t, d = h.shape
    tm = TM_PROJ
    tq = TQ_ATTN
    n_pairs = width // LANES
    tiles_per_seq = seq // tm
    n_cols = 3 * width + n_sc + n_gate
    stage_rows = 64
    assert w_in.shape[1:] == (d, n_cols) and d % stage_rows == 0 and w_ssm <= 512
    body = functools.partial(_inproj_body, layer=layer, width=width, w_ssm=w_ssm, n_sc=n_sc, n_gate=n_gate,
                             q_scale=HEAD_DIM ** -0.5)
    return pl.pallas_call(
        body,
        grid=(t // tm,),
        in_specs=[
            pl.BlockSpec((tm, d), lambda i: (i, 0)),
            pl.BlockSpec((1, d), lambda i: (0, 0)),
            pl.BlockSpec(memory_space=pl.ANY),
        ],
        out_specs=[
            pl.BlockSpec((1, width, tm), lambda i: (i // tiles_per_seq, 0, i % tiles_per_seq)),
            pl.BlockSpec((tm, width), lambda i: (i, 0)),
            pl.BlockSpec((1, n_pairs, tm // tq, LANES, tq),
                         lambda i: (i // tiles_per_seq, 0, i % tiles_per_seq, 0, 0)),
            pl.BlockSpec((SSM_CHUNK, tm // SSM_CHUNK, w_ssm), lambda i: (0, i, 0)),
            pl.BlockSpec((tm, n_sc), lambda i: (i, 0)),
            pl.BlockSpec((tm, n_gate), lambda i: (i, 0)),
        ],
        out_shape=[
            jax.ShapeDtypeStruct((batch, width, seq), BF16),
            jax.ShapeDtypeStruct((t, width), BF16),
            jax.ShapeDtypeStruct((batch, n_pairs, seq // tq, LANES, tq), BF16),
            jax.ShapeDtypeStruct((SSM_CHUNK, t // SSM_CHUNK, w_ssm), BF16),
            jax.ShapeDtypeStruct((t, n_sc), BF16),
            jax.ShapeDtypeStruct((t, n_gate), BF16),
        ],
        scratch_shapes=[pltpu.VMEM((d, n_cols), BF16), pltpu.VMEM((2, stage_rows, n_cols), F32),
                        pltpu.VMEM((w_ssm // LANES, tm, LANES), F32), pltpu.SemaphoreType.DMA((2,))],
        compiler_params=_params("arbitrary"),
        name="inproj",
    )(h, g.reshape(1, d), w_in)


def _sb_attn_body(qt_ref, k_ref, vt_ref, o_ref, acc_ref, carry_ref, lb_a_ref, sp_a_ref, lb_b_ref, sp_b_ref,
                  *, tq, n_pairs):
    i = pl.program_id(1)
    buf_a = (lb_a_ref, sp_a_ref)
    buf_b = (lb_b_ref, sp_b_ref)
    n_heads = 2 * n_pairs
    ones_rows = carry_ref.shape[1]
    key = lax.broadcasted_iota(jnp.int32, (tq, tq), 0)
    qry = lax.broadcasted_iota(jnp.int32, (tq, tq), 1)
    valid = key < qry
    er = lax.broadcasted_iota(jnp.int32, (tq + ones_rows, tq), 0)
    ec = lax.broadcasted_iota(jnp.int32, (tq + ones_rows, tq), 1)
    later_and_sum = jnp.where((ec > er) | (er >= tq), 1.0, 0.0).astype(BF16)

    feat = lax.broadcasted_iota(jnp.int32, (LANES, tq), 0)
    q_heads = []
    for p in range(n_pairs):
        qp = qt_ref[0, p * LANES:(p + 1) * LANES, :]
        zero = jnp.zeros_like(qp)
        q_heads.append(jnp.where(feat < HEAD_DIM, qp, zero))
        q_heads.append(jnp.where(feat >= HEAD_DIM, qp, zero))

    def scores(j, buf, masked):
        lb_ref, sp_ref = buf
        start = pl.multiple_of(j * tq, tq)
        for h in range(n_heads):
            p = h // 2
            kblk = k_ref[pl.ds(start, tq), p * LANES:(p + 1) * LANES]
            z = jnp.dot(kblk, q_heads[h], preferred_element_type=F32)
            l1p = jnp.log(1.0 + jnp.exp2(jnp.abs(z) * (-LOG2E)))
            sp = jnp.maximum(z, 0.0) + l1p
            lb = z - sp
            if masked:
                sp = jnp.where(valid, sp, 0.0)
                lb = jnp.where(valid, lb, MASKED_LOG_WEIGHT)
            lb_ref[h] = lb
            sp_ref[h] = sp.astype(BF16)

    def weights(j, buf):
        lb_ref, sp_ref = buf
        ws, sums = [], []
        for h in range(n_heads):
            ts = jnp.dot(later_and_sum, sp_ref[h], preferred_element_type=F32)
            ws.append(jnp.exp(lb_ref[h] - ts[0:tq]).astype(BF16))
            sums.append(ts[tq:tq + ones_rows])
        for h in range(n_heads):
            p, hh = divmod(h, 2)
            vt = vt_ref[0, p, j, hh * HEAD_DIM:(hh + 1) * HEAD_DIM, :]
            pv = jnp.dot(vt, ws[h], preferred_element_type=F32)
            carry = carry_ref[h]
            scale = jnp.exp(-carry[0:1, :])
            rows = slice(h * HEAD_DIM, (h + 1) * HEAD_DIM)
            acc_ref[rows, :] += pv * scale
            carry_ref[h] = carry + sums[h]

    def earlier_keys_matter():
        return (jnp.max(jnp.exp(-carry_ref[:, 0, :])) > 0.0).astype(jnp.int32)

    acc_ref[...] = jnp.zeros_like(acc_ref)
    carry_ref[...] = jnp.zeros_like(carry_ref)

    @pl.when(i == 0)
    def _():
        scores(0, buf_a, True)
        weights(0, buf_a)

    @pl.when(i > 0)
    def _():
        scores(i, buf_a, True)
        scores(i - 1, buf_b, False)
        weights(i, buf_a)
        weights(i - 1, buf_b)

    rest = i - 2

    @pl.when(jnp.where(rest >= 0, earlier_keys_matter(), 0) > 0)
    def _():
        scores(rest, buf_a, False)

        def two_tiles(c):
            m, _ = c
            j = rest - 1 - 2 * m
            scores(j, buf_b, False)
            weights(j + 1, buf_a)
            scores(j - 1, buf_a, False)
            weights(j, buf_b)
            return m + 1, earlier_keys_matter()

        _, alive = lax.while_loop(lambda c: (c[0] < rest // 2) & (c[1] > 0), two_tiles,
                                  (jnp.int32(0), jnp.int32(1)))

        @pl.when((alive > 0) & (rest % 2 == 1))
        def _():
            scores(0, buf_b, False)
            weights(1, buf_a)
            weights(0, buf_b)

        @pl.when((alive > 0) & (rest % 2 == 0))
        def _():
            weights(0, buf_a)

    for p in range(n_pairs):
        o_ref[:, p * LANES:(p + 1) * LANES] = acc_ref[p * LANES:(p + 1) * LANES, :].T.astype(o_ref.dtype)


def _sb_attention(qt, k, vt):
    batch, width, seq = qt.shape
    n_pairs, nq, _, tq = vt.shape[1:]
    n_heads = 2 * n_pairs
    return pl.pallas_call(
        functools.partial(_sb_attn_body, tq=tq, n_pairs=n_pairs),
        grid=(batch, nq),
        in_specs=[
            pl.BlockSpec((1, width, tq), lambda b, i: (b, 0, i)),
            pl.BlockSpec((seq, width), lambda b, i: (b, 0)),
            pl.BlockSpec((1, n_pairs, nq, LANES, tq), lambda b, i: (b, 0, 0, 0, 0)),
        ],
        out_specs=pl.BlockSpec((tq, width), lambda b, i: (b * nq + i, 0)),
        out_shape=jax.ShapeDtypeStruct((batch * seq, width), BF16),
        scratch_shapes=[pltpu.VMEM((width, tq), F32), pltpu.VMEM((n_heads, 16, tq), F32),
                        pltpu.VMEM((n_heads, tq, tq), F32), pltpu.VMEM((n_heads, tq, tq), BF16),
                        pltpu.VMEM((n_heads, tq, tq), F32), pltpu.VMEM((n_heads, tq, tq), BF16)],
        compiler_params=_params("parallel", "arbitrary"),
        name="sb_attention",
    )(qt, k, vt)


def _ssm_tables(lam_re, lam_im, log_dt, b_re, b_im, c_re, c_im, n_chunks):
    g, p = lam_re.shape
    c = SSM_CHUNK
    lr = jnp.minimum(lam_re.astype(F32), SSM_MAX_RE)
    li = lam_im.astype(F32)
    dt = jnp.exp(log_dt.astype(F32))[:, None]

    def lbar_pow(k):
        mag = jnp.exp(k * (lr * dt))
        ang = k * (li * dt)
        return mag * jnp.cos(ang), mag * jnp.sin(ang)

    ab_re, ab_im = lbar_pow(1.0)
    den = lr * lr + li * li
    nr = ab_re - 1.0
    ni = ab_im
    f_re = (nr * lr + ni * li) / den
    f_im = (ni * lr - nr * li) / den
    br = b_re.astype(F32)
    bi = b_im.astype(F32)
    bb_re = f_re[..., None] * br - f_im[..., None] * bi
    bb_im = f_re[..., None] * bi + f_im[..., None] * br
    cr = c_re.astype(F32)
    ci = c_im.astype(F32)

    eye = jnp.eye(g, dtype=F32)

    def block_diag(x):
        _, a, b = x.shape
        return (x[:, :, None, :] * eye[:, None, :, None]).reshape(g * a, g * b)

    b_in = jnp.concatenate([block_diag(bb_re.transpose(0, 2, 1)),
                            block_diag(bb_im.transpose(0, 2, 1))], axis=1)
    c_out = jnp.concatenate([block_diag(cr.transpose(0, 2, 1)),
                             block_diag(-ci.transpose(0, 2, 1))], axis=0)
    lbar = jnp.stack([ab_re.reshape(-1), ab_im.reshape(-1)], axis=0)

    steps = []
    s = 1
    while s < n_chunks:
        a_re, a_im = lbar_pow(float(c * s))
        steps.append(jnp.stack([a_re.reshape(-1, MXU_DIM), a_im.reshape(-1, MXU_DIM)], axis=1))
        s *= 2
    step = jnp.stack(steps, axis=1)
    return b_in.astype(BF16), c_out.astype(BF16), lbar, step


def _ssm_body(u_ref, b_ref, c_ref, lbar_ref, step_ref, y_ref, z_ref, *, n_chunks, batch):
    s = pl.program_id(0)
    c = u_ref.shape[0]
    n_blk = z_ref.shape[0]
    half = n_blk // 2
    bw = z_ref.shape[2]

    @pl.when(s == 0)
    def _():
        z_ref[...] = jnp.zeros_like(z_ref)

    def advance(ut):
        for pb in range(half):
            lo, hi = pb * bw, (pb + 1) * bw
            re = z_ref[pb]
            im = z_ref[half + pb]
            a_re = lbar_ref[0:1, lo:hi]
            a_im = lbar_ref[1:2, lo:hi]
            x_re = jnp.dot(ut, b_ref[:, lo:hi], preferred_element_type=F32)
            x_im = jnp.dot(ut, b_ref[:, half * bw + lo:half * bw + hi], preferred_element_type=F32)
            z_ref[pb] = a_re * re - a_im * im + x_re
            z_ref[half + pb] = a_re * im + a_im * re + x_im

    @pl.when(s < c)
    def _():
        advance(u_ref[s])

    @pl.when(s == c)
    def _():
        chunk = lax.broadcasted_iota(jnp.int32, (n_chunks, bw), 0)

        def shifted(x, n):
            return jnp.where(chunk >= n, pltpu.roll(x, n, axis=0), 0.0)

        def scan_block(idx, carry):
            b = idx // half
            pb = idx % half
            rows = pl.ds(pl.multiple_of(b * n_chunks, n_chunks), n_chunks)
            re = z_ref[pb, rows, :]
            im = z_ref[half + pb, rows, :]
            n = 1
            k = 0
            while n < n_chunks:
                a_re = step_ref[pb, k, 0:1, :]
                a_im = step_ref[pb, k, 1:2, :]
                re_s = shifted(re, n)
                im_s = shifted(im, n)
                re, im = re + a_re * re_s - a_im * im_s, im + a_re * im_s + a_im * re_s
                n *= 2
                k += 1
            z_ref[pb, rows, :] = shifted(re, 1)
            z_ref[half + pb, rows, :] = shifted(im, 1)
            return carry

        lax.fori_loop(0, batch * half, scan_block, 0)

    @pl.when(s >= c)
    def _():
        advance(u_ref[s - c])
        y = jnp.dot(z_ref[0].astype(BF16), c_ref[0:bw, :], preferred_element_type=F32)
        for blk in range(1, n_blk):
            y += jnp.dot(z_ref[blk].astype(BF16), c_ref[blk * bw:(blk + 1) * bw, :],
                         preferred_element_type=F32)
        y_ref[0] = y.astype(y_ref.dtype)


def _ssm_scan(u3, tables, *, batch, seq):
    b_in, c_out, lbar, step = tables
    c, rows, width = u3.shape
    n_chunks = seq // c
    assert rows == batch * n_chunks
    n_blk = b_in.shape[1] // MXU_DIM
    whole = lambda shape: pl.BlockSpec(shape, lambda s: (0,) * len(shape), pipeline_mode=pl.Buffered(1))
    return pl.pallas_call(
        functools.partial(_ssm_body, n_chunks=n_chunks, batch=batch),
        grid=(2 * c,),
        in_specs=[whole(u3.shape), whole(b_in.shape), whole(c_out.shape), whole(lbar.shape), whole(step.shape)],
        out_specs=pl.BlockSpec((1, rows, width), lambda s: (jnp.maximum(s - c, 0), 0, 0)),
        out_shape=jax.ShapeDtypeStruct((c, rows, width), BF16),
        scratch_shapes=[pltpu.VMEM((n_blk, rows, MXU_DIM), F32)],
        compiler_params=_params("arbitrary"),
        name="ssm_scan",
    )(u3, b_in, c_out, lbar, step)


def _gelu_tanh(x):
    return 0.5 * x * (1.0 + jnp.tanh(0.7978845608028654 * (x + 0.044715 * (x * x * x))))


def _top2(xn, wr):
    logits = lax.dot_general(wr, xn, (((1,), (1,)), ((), ())), preferred_element_type=F32,
                             precision=lax.Precision.HIGHEST)
    n_e = logits.shape[0]
    e_idx = lax.broadcasted_iota(jnp.int32, logits.shape, 0)
    m1 = jnp.max(logits, axis=0, keepdims=True)
    i1 = jnp.min(jnp.where(logits == m1, e_idx, n_e), axis=0, keepdims=True)
    rest = jnp.where(e_idx == i1, -jnp.inf, logits)
    m2 = jnp.max(rest, axis=0, keepdims=True)
    i2 = jnp.min(jnp.where(rest == m2, e_idx, n_e), axis=0, keepdims=True)
    e2 = jnp.exp(m2 - m1)
    w1 = 1.0 / (1.0 + e2)
    return jnp.concatenate([i1, i2], axis=0), jnp.concatenate([w1, e2 * w1], axis=0)


def _merge_body(attn_ref, yssm_ref, sc_ref, halo_ref, gate_ref, h_ref, dskip_ref, wglu_ref, convw_ref,
                wa_ref, wb_ref, wc_ref, wout_ref, gn_ref, *rest, tiles_per_seq, w_ssm, w_conv, route):
    if route:
        wr_ref, hout_ref, xn_ref, idx_ref, wgt_ref, y_ref = rest
    else:
        hout_ref, xn_ref, y_ref = rest
    i = pl.program_id(0)
    tm = h_ref.shape[0]
    d = h_ref.shape[1]

    n_t = yssm_ref.shape[0]
    n_lb = w_ssm // LANES
    for step in range(n_t):
        for lb in range(n_lb):
            y_ref[lb, pl.ds(step, tm // n_t, stride=n_t), :] = (
                yssm_ref[step, :, lb * LANES:(lb + 1) * LANES].astype(F32))
    y = jnp.concatenate([y_ref[lb] for lb in range(n_lb)], axis=1)
    u = sc_ref[:, 0:w_ssm].astype(F32)
    yb = _gelu_tanh(y + dskip_ref[...] * u)
    yb = yb * jax.nn.sigmoid(jnp.dot(yb.astype(BF16), wglu_ref[...], preferred_element_type=F32))
    y_b = jnp.dot(yb.astype(BF16), wb_ref[...], preferred_element_type=F32)

    o_h, o_b, o_c = w_ssm, w_ssm + w_conv, w_ssm + 2 * w_conv
    uc = sc_ref[:, o_c:o_c + w_conv].astype(F32) * sc_ref[:, o_h:o_h + w_conv].astype(F32)
    halo = halo_ref[:, o_c:o_c + w_conv].astype(F32) * halo_ref[:, o_h:o_h + w_conv].astype(F32)
    halo = jnp.where(i % tiles_per_seq == 0, 0.0, halo)
    row = lax.broadcasted_iota(jnp.int32, (tm, w_conv), 0)
    prev1 = jnp.where(row == 0, halo[SUBLANES - 1:SUBLANES, :], pltpu.roll(uc, 1, axis=0))
    prev2 = jnp.where(row == 0, halo[SUBLANES - 2:SUBLANES - 1, :],
                      jnp.where(row == 1, halo[SUBLANES - 1:SUBLANES, :], pltpu.roll(uc, 2, axis=0)))
    conv = convw_ref[0:1, :] * prev2 + convw_ref[1:2, :] * prev1 + convw_ref[2:3, :] * uc
    yc = sc_ref[:, o_b:o_b + w_conv].astype(F32) * conv
    y_c = jnp.dot(yc.astype(BF16), wc_ref[...], preferred_element_type=F32)

    y_a = jnp.dot(attn_ref[...], wa_ref[...], preferred_element_type=F32)

    merged = (gate_ref[:, 0:d].astype(F32) * y_a + gate_ref[:, d:2 * d].astype(F32) * y_b
              + gate_ref[:, 2 * d:3 * d].astype(F32) * y_c)
    hn = h_ref[...] + jnp.dot(merged.astype(BF16), wout_ref[...], preferred_element_type=F32)
    hout_ref[...] = hn
    xn = _rms(hn, gn_ref[...])
    xn_ref[...] = xn.astype(xn_ref.dtype)
    if route:
        idx, wgt = _top2(xn, wr_ref[...])
        idx_ref[0] = idx
        wgt_ref[0] = wgt


def _merge(attn, yssm, sc, gates, h, d_skip, w_glu, conv_w, w_a, w_b, w_c, w_out, g_ffn, w_router, *, seq):
    t, d = h.shape
    tm = TM_MERGE
    nt = t // tm
    n_t, _, w_ssm = yssm.shape
    w_conv = conv_w.shape[1]
    w_attn = attn.shape[1]
    n_sc = sc.shape[1]
    route = w_router is not None
    body = functools.partial(_merge_body, tiles_per_seq=seq // tm, w_ssm=w_ssm, w_conv=w_conv, route=route)
    full = lambda shape: pl.BlockSpec(shape, lambda i: (0,) * len(shape))
    row = pl.BlockSpec((tm, d), lambda i: (i, 0))
    halo_blocks = tm // SUBLANES
    in_specs = [
        pl.BlockSpec((tm, w_attn), lambda i: (i, 0)),
        pl.BlockSpec((n_t, tm // n_t, w_ssm), lambda i: (0, i, 0)),
        pl.BlockSpec((tm, n_sc), lambda i: (i, 0)),
        pl.BlockSpec((SUBLANES, n_sc), lambda i: (jnp.maximum(i * halo_blocks - 1, 0), 0)),
        pl.BlockSpec((tm, N_BRANCH * d), lambda i: (i, 0)),
        row,
        full((1, w_ssm)), full((w_ssm, w_ssm)), full((CONV_K, w_conv)),
        full((w_attn, d)), full((w_ssm, d)), full((w_conv, d)), full((d, d)), full((1, d)),
    ]
    args = [attn, yssm, sc, sc, gates, h, d_skip.reshape(1, w_ssm).astype(F32), w_glu, conv_w.astype(F32),
            w_a, w_b, w_c, w_out, g_ffn.reshape(1, d)]
    out_specs = [row, row]
    out_shape = [jax.ShapeDtypeStruct((t, d), F32), jax.ShapeDtypeStruct((t, d), F32 if route else BF16)]
    if route:
        n_e = w_router.shape[1]
        in_specs.append(full((n_e, d)))
        args.append(w_router.T.astype(F32))
        lanes = pl.BlockSpec((1, TOP_K, tm), lambda i: (i, 0, 0))
        out_specs += [lanes, lanes]
        out_shape += [jax.ShapeDtypeStruct((nt, TOP_K, tm), jnp.int32), jax.ShapeDtypeStruct((nt, TOP_K, tm), F32)]
    outs = pl.pallas_call(
        body,
        grid=(nt,),
        in_specs=in_specs,
        out_specs=out_specs,
        out_shape=out_shape,
        scratch_shapes=[pltpu.VMEM((w_ssm // LANES, tm, LANES), F32)],
        compiler_params=_params("parallel"),
        name="merge",
    )(*args)
    if not route:
        return outs
    hn, xn, idx, wgt = outs
    return hn, xn, idx.transpose(0, 2, 1).reshape(t, TOP_K), wgt.transpose(0, 2, 1).reshape(t, TOP_K)


def _swiglu_hidden(x, wg_ref, wu_ref, act_ref, j):
    gate = jnp.dot(x, wg_ref[...], preferred_element_type=F32)
    up = jnp.dot(x, wu_ref[...], preferred_element_type=F32)
    act_ref[j] = (gate * jax.nn.sigmoid(gate) * up).astype(act_ref.dtype)


def _swiglu_down(act_ref, wd_ref):
    act = jnp.concatenate([act_ref[jj] for jj in range(act_ref.shape[0])], axis=1)
    return jnp.dot(act, wd_ref[...], preferred_element_type=F32)


def _load_as_bf16(src_hbm, dst_ref, stage_ref, sem):
    rows = stage_ref.shape[1]
    n_chunks = src_hbm.shape[0] // rows

    def copy(c):
        return pltpu.make_async_copy(src_hbm.at[pl.ds(c * rows, rows)], stage_ref.at[c % 2], sem.at[c % 2])

    copy(0).start()
    for c in range(n_chunks):
        if c + 1 < n_chunks:
            copy(c + 1).start()
        copy(c).wait()
        dst_ref[c * rows:(c + 1) * rows, :] = stage_ref[c % 2].astype(dst_ref.dtype)


def _dense_ffn_body(x_ref, h_ref, wg_hbm, wu_hbm, wd_hbm, e0_hbm, e1_hbm, e2_hbm, o_ref, n0_hbm, n1_hbm, n2_hbm,
                    wg_ref, wu_ref, wd_ref, act_ref, st0_ref, st1_ref, st2_ref, nr0_ref, nr1_ref, nr2_ref,
                    in_sem, out_sem):
    s = pl.program_id(0)
    n_steps = pl.num_programs(0)
    stages = (st0_ref, st1_ref, st2_ref)
    narrow = (nr0_ref, nr1_ref, nr2_ref)

    def fetch(step, slot):
        return [pltpu.make_async_copy(src.at[pl.ds(step * st.shape[1], st.shape[1])], st.at[slot], in_sem.at[slot])
                for src, st in zip((e0_hbm, e1_hbm, e2_hbm), stages)]

    def flush(step, slot):
        return [pltpu.make_async_copy(nr.at[slot], dst.at[pl.ds(step * nr.shape[1], nr.shape[1])], out_sem.at[slot])
                for nr, dst in zip(narrow, (n0_hbm, n1_hbm, n2_hbm))]

    @pl.when(s == 0)
    def _():
        _load_as_bf16(wg_hbm, wg_ref, st0_ref, in_sem)
        _load_as_bf16(wu_hbm, wu_ref, st1_ref, in_sem)
        _load_as_bf16(wd_hbm, wd_ref, st2_ref, in_sem)
        for cp in fetch(0, 0):
            cp.start()

    @pl.when(s + 1 < n_steps)
    def _():
        for cp in fetch(s + 1, (s + 1) % 2):
            cp.start()

    x = x_ref[...]
    tf = act_ref.shape[2]
    for j in range(act_ref.shape[0]):
        _swiglu_hidden(x, wg_ref.at[:, j * tf:(j + 1) * tf], wu_ref.at[:, j * tf:(j + 1) * tf], act_ref, j)
    o_ref[...] = h_ref[...] + _swiglu_down(act_ref, wd_ref)

    slot = s % 2
    for cp in fetch(s, slot):
        cp.wait()

    @pl.when(s >= 2)
    def _():
        for cp in flush(s - 2, slot):
            cp.wait()

    for st, nr in zip(stages, narrow):
        nr[slot] = st[slot].astype(nr.dtype)
    for cp in flush(s, slot):
        cp.start()

    @pl.when(s == n_steps - 1)
    def _():
        @pl.when(s >= 1)
        def _():
            for cp in flush(s - 1, 1 - slot):
                cp.wait()

        for cp in flush(s, slot):
            cp.wait()


def _dense_ffn(xn, h, w_gate, w_up, w_down, extra):
    t, d = h.shape
    f = w_gate.shape[1]
    tm, tf = TM_FFN, TF_FFN
    n_steps = t // tm
    rows = [e.shape[0] // n_steps for e in extra]
    assert all(e.shape[0] == r * n_steps and r % 16 == 0 for e, r in zip(extra, rows))
    assert w_gate.shape[0] % rows[0] == 0 and w_up.shape[0] % rows[1] == 0 and w_down.shape[0] % rows[2] == 0
    assert extra[0].shape[1] == f and extra[1].shape[1] == f and extra[2].shape[1] == d
    row = pl.BlockSpec((tm, d), lambda i: (i, 0))
    hbm = pl.BlockSpec(memory_space=pl.ANY)
    outs = pl.pallas_call(
        _dense_ffn_body,
        grid=(n_steps,),
        in_specs=[row, row] + [hbm] * 6,
        out_specs=[row, hbm, hbm, hbm],
        out_shape=[jax.ShapeDtypeStruct((t, d), F32)] + [jax.ShapeDtypeStruct(e.shape, BF16) for e in extra],
        scratch_shapes=[pltpu.VMEM((d, f), BF16), pltpu.VMEM((d, f), BF16), pltpu.VMEM((f, d), BF16),
                        pltpu.VMEM((f // tf, tm, tf), BF16)]
                       + [pltpu.VMEM((2, r, e.shape[1]), F32) for e, r in zip(extra, rows)]
                       + [pltpu.VMEM((2, r, e.shape[1]), BF16) for e, r in zip(extra, rows)]
                       + [pltpu.SemaphoreType.DMA((2,)), pltpu.SemaphoreType.DMA((2,))],
        compiler_params=_params("arbitrary"),
        name="dense_ffn",
    )(xn, h, w_gate, w_up, w_down, *extra)
    return outs[0], outs[1:]


def _scatter_rows_body(zero_start_ref, s1_ref, s2_ref, h_ref, xs_ref, zeros_ref, tile_ref, zero_sem, sem):
    i = pl.program_id(0)
    tm = h_ref.shape[0]

    @pl.when(i == 0)
    def _():
        zeros_ref[...] = jnp.zeros_like(zeros_ref)

        def fill(k, c):
            dst = xs_ref.at[pl.ds(pl.multiple_of(zero_start_ref[k], SUBLANES), zeros_ref.shape[0])]
            cp = pltpu.make_async_copy(zeros_ref, dst, zero_sem)
            cp.start()
            cp.wait()
            return c

        lax.fori_loop(0, zero_start_ref.shape[0], fill, 0)

    half = i % 2

    def drain(which):
        for _ in range(TOP_K):
            pltpu.make_async_copy(tile_ref.at[which], xs_ref.at[pl.ds(0, tm)], sem.at[which]).wait()

    @pl.when(i >= 2)
    def _():
        drain(half)

    tile_ref[half] = h_ref[...]

    def issue(r, c):
        src = tile_ref.at[half, pl.ds(r, 1)]
        pltpu.make_async_copy(src, xs_ref.at[pl.ds(s1_ref[r], 1)], sem.at[half]).start(priority=0)
        pltpu.make_async_copy(src, xs_ref.at[pl.ds(s2_ref[r], 1)], sem.at[half]).start(priority=1)
        return c

    lax.fori_loop(0, tm, issue, 0, unroll=8)

    @pl.when(i == pl.num_programs(0) - 1)
    def _():
        @pl.when(i >= 1)
        def _():
            drain(1 - half)

        drain(half)


def _scatter_rows(h, slot1, slot2, zero_start, cap, zero_rows):
    t, d = h.shape
    tm = TM_SCATTER
    smem_tile = pl.BlockSpec((tm,), lambda i, zs: (i,), memory_space=pltpu.SMEM)
    grid_spec = pltpu.PrefetchScalarGridSpec(
        num_scalar_prefetch=1,
        grid=(t // tm,),
        in_specs=[smem_tile, smem_tile, pl.BlockSpec((tm, d), lambda i, zs: (i, 0))],
        out_specs=pl.BlockSpec(memory_space=pl.ANY),
        scratch_shapes=[pltpu.VMEM((zero_rows, d), h.dtype), pltpu.VMEM((2, tm, d), h.dtype),
                        pltpu.SemaphoreType.DMA(()), pltpu.SemaphoreType.DMA((2,))],
    )
    return pl.pallas_call(
        _scatter_rows_body,
        grid_spec=grid_spec,
        out_shape=jax.ShapeDtypeStruct((cap, d), h.dtype),
        compiler_params=_params("arbitrary"),
        name="scatter_rows",
    )(zero_start, slot1, slot2, h)


def _moe_ffn_body(tile_expert_ref, tile_valid_ref, x_ref, wg_ref, wu_ref, wd_ref, o_ref, act_ref):
    i = pl.program_id(0)
    j = pl.program_id(1)
    last = j == pl.num_programs(1) - 1
    valid = tile_valid_ref[i] > 0

    @pl.when(valid)
    def _():
        _swiglu_hidden(x_ref[...].astype(BF16), wg_ref.at[0], wu_ref.at[0], act_ref, j)

        @pl.when(last)
        def _():
            o_ref[...] = _swiglu_down(act_ref, wd_ref.at[0])

    @pl.when(last & jnp.logical_not(valid))
    def _():
        o_ref[...] = jnp.zeros_like(o_ref)


def _moe_ffn(xs, tile_expert, tile_valid, w_gate, w_up, w_down):
    cap, d = xs.shape
    f = w_gate.shape[2]
    tm, tf = TM_MOE, TF_FFN
    grid_spec = pltpu.PrefetchScalarGridSpec(
        num_scalar_prefetch=2,
        grid=(cap // tm, f // tf),
        in_specs=[
            pl.BlockSpec((tm, d), lambda i, j, te, tv: (i, 0)),
            pl.BlockSpec((1, d, tf), lambda i, j, te, tv: (te[i], 0, j)),
            pl.BlockSpec((1, d, tf), lambda i, j, te, tv: (te[i], 0, j)),
            pl.BlockSpec((1, f, d), lambda i, j, te, tv: (te[i], 0, 0)),
        ],
        out_specs=pl.BlockSpec((tm, d), lambda i, j, te, tv: (i, 0)),
        scratch_shapes=[pltpu.VMEM((f // tf, tm, tf), BF16)],
    )
    return pl.pallas_call(
        _moe_ffn_body,
        grid_spec=grid_spec,
        out_shape=jax.ShapeDtypeStruct((cap, d), F32),
        compiler_params=_params("parallel", "arbitrary"),
        name="moe_ffn",
    )(tile_expert, tile_valid, xs, w_gate, w_up, w_down)


def _moe_plan(idx, n_experts, tm):
    t = idx.shape[0]
    pairs = t * TOP_K
    cap = pairs + n_experts * tm
    e_flat = idx.reshape(pairs)
    onehot = (e_flat[:, None] == jnp.arange(n_experts, dtype=jnp.int32)[None, :]).astype(jnp.int32)
    rank = jnp.sum(onehot * (jnp.cumsum(onehot, axis=0) - onehot), axis=1)
    counts = jnp.sum(onehot, axis=0)
    padded = ((counts + tm - 1) // tm) * tm
    ends = jnp.cumsum(padded)
    starts = ends - padded
    slot = jnp.sum(onehot * starts[None, :], axis=1) + rank
    tile_start = jnp.arange(cap // tm, dtype=jnp.int32) * tm
    tile_expert = jnp.sum((tile_start[:, None] >= ends[None, :]).astype(jnp.int32), axis=1)
    tile_valid = (tile_expert < n_experts).astype(jnp.int32)
    last_used = jnp.max(jnp.where(counts > 0, jnp.arange(n_experts, dtype=jnp.int32), 0))
    tile_expert = jnp.minimum(tile_expert, last_used).astype(jnp.int32)
    zero_rows = tm + SUBLANES
    tail = ends[-1] + jnp.arange(n_experts, dtype=jnp.int32) * tm
    first = jnp.concatenate([starts + counts, tail]) // SUBLANES * SUBLANES
    zero_start = jnp.minimum(first, cap - zero_rows).astype(jnp.int32)
    return slot.reshape(t, TOP_K).astype(jnp.int32), cap, tile_expert, tile_valid, zero_start, zero_rows


def _combine_body(s1_ref, s2_ref, next1_ref, next2_ref, h_ref, wgt_ref, g_ref, ys_ref, o_ref, y1_ref, y2_ref, sem,
                  *, apply_norm):
    i = pl.program_id(0)
    tm = h_ref.shape[0]

    def request(idx1_ref, idx2_ref, half):
        def issue(r, c):
            pltpu.make_async_copy(ys_ref.at[pl.ds(idx1_ref[r], 1)], y1_ref.at[half, pl.ds(r, 1)],
                                  sem.at[half]).start(priority=0)
            pltpu.make_async_copy(ys_ref.at[pl.ds(idx2_ref[r], 1)], y2_ref.at[half, pl.ds(r, 1)],
                                  sem.at[half]).start(priority=1)
            return c

        lax.fori_loop(0, tm, issue, 0, unroll=8)

    @pl.when(i == 0)
    def _():
        request(s1_ref, s2_ref, 0)

    @pl.when(i + 1 < pl.num_programs(0))
    def _():
        request(next1_ref, next2_ref, (i + 1) % 2)

    half = i % 2
    pltpu.make_async_copy(ys_ref.at[pl.ds(0, tm)], y1_ref.at[half], sem.at[half]).wait()
    pltpu.make_async_copy(ys_ref.at[pl.ds(0, tm)], y2_ref.at[half], sem.at[half]).wait()
    w = wgt_ref[...]
    hn = h_ref[...] + w[:, 0:1] * y1_ref[half] + w[:, 1:2] * y2_ref[half]
    o_ref[...] = _rms(hn, g_ref[...]) if apply_norm else hn


def _combine(h, ys, slot, wgt, g, *, apply_norm):
    t, d = h.shape
    tm = TM_COMBINE
    n_tiles = t // tm
    this_tile = pl.BlockSpec((tm,), lambda i: (i,), memory_space=pltpu.SMEM)
    next_tile = pl.BlockSpec((tm,), lambda i: (jnp.minimum(i + 1, n_tiles - 1),), memory_space=pltpu.SMEM)
    row = pl.BlockSpec((tm, d), lambda i: (i, 0))
    s1, s2 = slot[:, 0], slot[:, 1]
    return pl.pallas_call(
        functools.partial(_combine_body, apply_norm=apply_norm),
        grid=(n_tiles,),
        in_specs=[this_tile, this_tile, next_tile, next_tile, row,
                  pl.BlockSpec((tm, TOP_K), lambda i: (i, 0)),
                  pl.BlockSpec((1, d), lambda i: (0, 0)),
                  pl.BlockSpec(memory_space=pl.ANY)],
        out_specs=row,
        out_shape=jax.ShapeDtypeStruct((t, d), F32),
        scratch_shapes=[pltpu.VMEM((2, tm, d), F32), pltpu.VMEM((2, tm, d), F32), pltpu.SemaphoreType.DMA((2,))],
        compiler_params=_params("arbitrary"),
        name="combine",
    )(s1, s2, s1, s2, h, wgt, g.reshape(1, d), ys)


def _plain_norm_body(h_ref, g_ref, o_ref):
    o_ref[...] = _rms(h_ref[...], g_ref[...])


def _plain_norm(h, g):
    t, d = h.shape
    tm = TM_NORM
    row = pl.BlockSpec((tm, d), lambda i: (i, 0))
    return pl.pallas_call(
        _plain_norm_body,
        grid=(t // tm,),
        in_specs=[row, pl.BlockSpec((1, d), lambda i: (0, 0))],
        out_specs=row,
        out_shape=jax.ShapeDtypeStruct((t, d), F32),
        compiler_params=_params("parallel"),
        name="plain_norm",
    )(h, g.reshape(1, d))


def kernel(x, norm_mix, w_in, ssm_lambda_re, ssm_lambda_im, ssm_log_dt, ssm_b_re, ssm_b_im, ssm_c_re,
           ssm_c_im, ssm_d, ssm_w_glu, conv_w, w_br_a, w_br_b, w_br_c, w_out, norm_ffn, dense_w_gate,
           dense_w_up, dense_w_down, moe_w_router, moe_w_gate, moe_w_up, moe_w_down, final_norm):
    batch, seq, d = x.shape
    depth = w_in.shape[0]
    t = batch * seq
    w_attn = w_br_a.shape[1]
    w_ssm = w_br_b.shape[1]
    w_conv = w_br_c.shape[1]
    n_sc = w_ssm + 3 * w_conv
    n_gate = N_BRANCH * d
    n_experts = moe_w_router.shape[-1]

    h = x.reshape(t, d)
    normed = False
    for i in range(depth):
        qt, k, vt, u3, sc, gates = _inproj(h, norm_mix[i], w_in, i, batch=batch, seq=seq, width=w_attn,
                                           w_ssm=w_ssm, n_sc=n_sc, n_gate=n_gate)
        attn = _sb_attention(qt, k, vt)
        tables = _ssm_tables(ssm_lambda_re[i], ssm_lambda_im[i], ssm_log_dt[i], ssm_b_re[i], ssm_b_im[i],
                             ssm_c_re[i], ssm_c_im[i], seq // SSM_CHUNK)
        yssm = _ssm_scan(u3, tables, batch=batch, seq=seq)
        dense = i % 2 == 0
        j = i // 2
        merged = _merge(attn, yssm, sc, gates, h, ssm_d[i], ssm_w_glu[i].astype(BF16), conv_w[i],
                        w_br_a[i].astype(BF16), w_br_b[i].astype(BF16), w_br_c[i].astype(BF16),
                        w_out[i].astype(BF16), norm_ffn[i], None if dense else moe_w_router[j], seq=seq)
        if dense:
            h, xn = merged
            assert i + 1 < depth
            f = moe_w_gate.shape[-1]
            expert_f32 = (moe_w_gate[j].reshape(n_experts * d, f), moe_w_up[j].reshape(n_experts * d, f),
                          moe_w_down[j].reshape(n_experts * f, d))
            h, expert_bf16 = _dense_ffn(xn, h, dense_w_gate[j], dense_w_up[j], dense_w_down[j], expert_f32)
        else:
            h, xn, idx, wgt = merged
            slot, cap, tile_expert, tile_valid, zero_start, zero_rows = _moe_plan(idx, n_experts, TM_MOE)
            xs = _scatter_rows(xn, slot[:, 0], slot[:, 1], zero_start, cap, zero_rows)
            e_gate, e_up, e_down = expert_bf16
            ys = _moe_ffn(xs, tile_expert, tile_valid, e_gate.reshape(n_experts, d, -1),
                          e_up.reshape(n_experts, d, -1), e_down.reshape(n_experts, -1, d))
            last = i == depth - 1
            h = _combine(h, ys, slot, wgt, final_norm if last else norm_ffn[i], apply_norm=last)
            normed = last
    out = h if normed else _plain_norm(h, final_norm)
    return out.reshape(batch, seq, d)
```

```python
import functools

import jax
import jax.numpy as jnp
from jax import lax
from jax.experimental import pallas as pl
from jax.experimental.pallas import tpu as pltpu

F32 = jnp.float32
BF16 = jnp.bfloat16

EPS = 1e-6
HEAD_DIM = 64
SSM_CHUNK = 16
SSM_MAX_RE = -1e-4
CONV_K = 3
N_BRANCH = 3
TOP_K = 2
LOG2E = 1.4426950408889634
MASKED_LOG_WEIGHT = -1e30

LANES = 128
SUBLANES = 8
MXU_DIM = 256
VMEM_LIMIT_BYTES = 56 * 1024 * 1024

TM_PROJ = 512
TQ_ATTN = MXU_DIM
TM_MERGE = 512
TM_FFN = 256
TF_FFN = 1792
TM_MOE = 512
TM_SCATTER = 512
TM_COMBINE = 512
TM_NORM = 512


def _params(*sem):
    return pltpu.CompilerParams(dimension_semantics=sem, vmem_limit_bytes=VMEM_LIMIT_BYTES)


def _rms(x, g):
    ms = jnp.mean(x * x, axis=-1, keepdims=True)
    return x * lax.rsqrt(ms + EPS) * g


def _inproj_body(x_ref, g_ref, w_hbm, qt_ref, k_ref, vt_ref, u3_ref, sc_ref, gate_ref, w_ref, stage_ref, u_ref, sem,
                 *, layer, width, w_ssm, n_sc, n_gate, q_scale):
    @pl.when(pl.program_id(0) == 0)
    def _():
        _load_as_bf16(w_hbm.at[layer], w_ref, stage_ref, sem)

    xn = _rms(x_ref[...], g_ref[...]).astype(BF16)
    tm = xn.shape[0]
    tq = vt_ref.shape[-1]
    chunk = 512

    def proj(c0, n):
        return jnp.dot(xn, w_ref[:, c0:c0 + n], preferred_element_type=F32)

    qt_ref[0] = (proj(0, width) * q_scale).T.astype(BF16)
    k_ref[...] = proj(width, width).astype(BF16)
    v = proj(2 * width, width)
    for kt in range(tm // tq):
        for p in range(width // LANES):
            vt_ref[0, p, kt] = v[kt * tq:(kt + 1) * tq, p * LANES:(p + 1) * LANES].T.astype(BF16)
    for c0 in range(0, n_sc, chunk):
        r = proj(3 * width + c0, chunk)
        sc_ref[:, c0:c0 + chunk] = r.astype(BF16)
        if c0 == 0:
            for lb in range(w_ssm // LANES):
                u_ref[lb] = r[:, lb * LANES:(lb + 1) * LANES]
    n_t = u3_ref.shape[0]
    for step in range(n_t):
        for lb in range(w_ssm // LANES):
            u3_ref[step, :, lb * LANES:(lb + 1) * LANES] = (
                u_ref[lb, pl.ds(step, tm // n_t, stride=n_t), :].astype(BF16))
    for c0 in range(0, n_gate, chunk):
        gate_ref[:, c0:c0 + chunk] = jax.nn.sigmoid(proj(3 * width + n_sc + c0, chunk)).astype(BF16)


def _inproj(h, g, w_in, layer, *, batch, seq, width, w_ssm, n_sc, n_gate):
    t, d = h.shape
    tm = TM_PROJ
    tq = TQ_ATTN
    n_pairs = width // LANES
    tiles_per_seq = seq // tm
    n_cols = 3 * width + n_sc + n_gate
    stage_rows = 64
    assert w_in.shape[1:] == (d, n_cols) and d % stage_rows == 0 and w_ssm <= 512
    body = functools.partial(_inproj_body, layer=layer, width=width, w_ssm=w_ssm, n_sc=n_sc, n_gate=n_gate,
                             q_scale=HEAD_DIM ** -0.5)
    return pl.pallas_call(
        body,
        grid=(t // tm,),
        in_specs=[
            pl.BlockSpec((tm, d), lambda i: (i, 0)),
            pl.BlockSpec((1, d), lambda i: (0, 0)),
            pl.BlockSpec(memory_space=pl.ANY),
        ],
        out_specs=[
            pl.BlockSpec((1, width, tm), lambda i: (i // tiles_per_seq, 0, i % tiles_per_seq)),
            pl.BlockSpec((tm, width), lambda i: (i, 0)),
            pl.BlockSpec((1, n_pairs, tm // tq, LANES, tq),
                         lambda i: (i // tiles_per_seq, 0, i % tiles_per_seq, 0, 0)),
            pl.BlockSpec((SSM_CHUNK, tm // SSM_CHUNK, w_ssm), lambda i: (0, i, 0)),
            pl.BlockSpec((tm, n_sc), lambda i: (i, 0)),
            pl.BlockSpec((tm, n_gate), lambda i: (i, 0)),
        ],
        out_shape=[
            jax.ShapeDtypeStruct((batch, width, seq), BF16),
            jax.ShapeDtypeStruct((t, width), BF16),
            jax.ShapeDtypeStruct((batch, n_pairs, seq // tq, LANES, tq), BF16),
            jax.ShapeDtypeStruct((SSM_CHUNK, t // SSM_CHUNK, w_ssm), BF16),
            jax.ShapeDtypeStruct((t, n_sc), BF16),
            jax.ShapeDtypeStruct((t, n_gate), BF16),
        ],
        scratch_shapes=[pltpu.VMEM((d, n_cols), BF16), pltpu.VMEM((2, stage_rows, n_cols), F32),
                        pltpu.VMEM((w_ssm // LANES, tm, LANES), F32), pltpu.SemaphoreType.DMA((2,))],
        compiler_params=_params("arbitrary"),
        name="inproj",
    )(h, g.reshape(1, d), w_in)


def _sb_attn_body(qt_ref, k_ref, vt_ref, o_ref, acc_ref, carry_ref, lb_a_ref, sp_a_ref, lb_b_ref, sp_b_ref,
                  *, tq, n_pairs):
    i = pl.program_id(1)
    buf_a = (lb_a_ref, sp_a_ref)
    buf_b = (lb_b_ref, sp_b_ref)
    n_heads = 2 * n_pairs
    ones_rows = carry_ref.shape[1]
    key = lax.broadcasted_iota(jnp.int32, (tq, tq), 0)
    qry = lax.broadcasted_iota(jnp.int32, (tq, tq), 1)
    valid = key < qry
    er = lax.broadcasted_iota(jnp.int32, (tq + ones_rows, tq), 0)
    ec = lax.broadcasted_iota(jnp.int32, (tq + ones_rows, tq), 1)
    later_and_sum = jnp.where((ec > er) | (er >= tq), 1.0, 0.0).astype(BF16)

    feat = lax.broadcasted_iota(jnp.int32, (LANES, tq), 0)
    q_heads = []
    for p in range(n_pairs):
        qp = qt_ref[0, p * LANES:(p + 1) * LANES, :]
        zero = jnp.zeros_like(qp)
        q_heads.append(jnp.where(feat < HEAD_DIM, qp, zero))
        q_heads.append(jnp.where(feat >= HEAD_DIM, qp, zero))

    def scores(j, buf, masked):
        lb_ref, sp_ref = buf
        start = pl.multiple_of(j * tq, tq)
        for h in range(n_heads):
            p = h // 2
            kblk = k_ref[pl.ds(start, tq), p * LANES:(p + 1) * LANES]
            z = jnp.dot(kblk, q_heads[h], preferred_element_type=F32)
            l1p = jnp.log(1.0 + jnp.exp2(jnp.abs(z) * (-LOG2E)))
            sp = jnp.maximum(z, 0.0) + l1p
            lb = z - sp
            if masked:
                sp = jnp.where(valid, sp, 0.0)
                lb = jnp.where(valid, lb, MASKED_LOG_WEIGHT)
            lb_ref[h] = lb
            sp_ref[h] = sp.astype(BF16)

    def weights(j, buf):
        lb_ref, sp_ref = buf
        ws, sums = [], []
        for h in range(n_heads):
            ts = jnp.dot(later_and_sum, sp_ref[h], preferred_element_type=F32)
            ws.append(jnp.exp(lb_ref[h] - ts[0:tq]).astype(BF16))
            sums.append(ts[tq:tq + ones_rows])
        for h in range(n_heads):
            p, hh = divmod(h, 2)
            vt = vt_ref[0, p, j, hh * HEAD_DIM:(hh + 1) * HEAD_DIM, :]
            pv = jnp.dot(vt, ws[h], preferred_element_type=F32)
            carry = carry_ref[h]
            scale = jnp.exp(-carry[0:1, :])
            rows = slice(h * HEAD_DIM, (h + 1) * HEAD_DIM)
            acc_ref[rows, :] += pv * scale
            carry_ref[h] = carry + sums[h]

    def earlier_keys_matter():
        return (jnp.max(jnp.exp(-carry_ref[:, 0, :])) > 0.0).astype(jnp.int32)

    acc_ref[...] = jnp.zeros_like(acc_ref)
    carry_ref[...] = jnp.zeros_like(carry_ref)

    @pl.when(i == 0)
    def _():
        scores(0, buf_a, True)
        weights(0, buf_a)

    @pl.when(i > 0)
    def _():
        scores(i, buf_a, True)
        scores(i - 1, buf_b, False)
        weights(i, buf_a)
        weights(i - 1, buf_b)

    rest = i - 2

    @pl.when(jnp.where(rest >= 0, earlier_keys_matter(), 0) > 0)
    def _():
        scores(rest, buf_a, False)

        def two_tiles(c):
            m, _ = c
            j = rest - 1 - 2 * m
            scores(j, buf_b, False)
            weights(j + 1, buf_a)
            scores(j - 1, buf_a, False)
            weights(j, buf_b)
            return m + 1, earlier_keys_matter()

        _, alive = lax.while_loop(lambda c: (c[0] < rest // 2) & (c[1] > 0), two_tiles,
                                  (jnp.int32(0), jnp.int32(1)))

        @pl.when((alive > 0) & (rest % 2 == 1))
        def _():
            scores(0, buf_b, False)
            weights(1, buf_a)
            weights(0, buf_b)

        @pl.when((alive > 0) & (rest % 2 == 0))
        def _():
            weights(0, buf_a)

    for p in range(n_pairs):
        o_ref[:, p * LANES:(p + 1) * LANES] = acc_ref[p * LANES:(p + 1) * LANES, :].T.astype(o_ref.dtype)


def _sb_attention(qt, k, vt):
    batch, width, seq = qt.shape
    n_pairs, nq, _, tq = vt.shape[1:]
    n_heads = 2 * n_pairs
    return pl.pallas_call(
        functools.partial(_sb_attn_body, tq=tq, n_pairs=n_pairs),
        grid=(batch, nq),
        in_specs=[
            pl.BlockSpec((1, width, tq), lambda b, i: (b, 0, i)),
            pl.BlockSpec((seq, width), lambda b, i: (b, 0)),
            pl.BlockSpec((1, n_pairs, nq, LANES, tq), lambda b, i: (b, 0, 0, 0, 0)),
        ],
        out_specs=pl.BlockSpec((tq, width), lambda b, i: (b * nq + i, 0)),
        out_shape=jax.ShapeDtypeStruct((batch * seq, width), BF16),
        scratch_shapes=[pltpu.VMEM((width, tq), F32), pltpu.VMEM((n_heads, 16, tq), F32),
                        pltpu.VMEM((n_heads, tq, tq), F32), pltpu.VMEM((n_heads, tq, tq), BF16),
                        pltpu.VMEM((n_heads, tq, tq), F32), pltpu.VMEM((n_heads, tq, tq), BF16)],
        compiler_params=_params("parallel", "arbitrary"),
        name="sb_attention",
    )(qt, k, vt)


def _ssm_tables(lam_re, lam_im, log_dt, b_re, b_im, c_re, c_im, n_chunks):
    g, p = lam_re.shape
    c = SSM_CHUNK
    lr = jnp.minimum(lam_re.astype(F32), SSM_MAX_RE)
    li = lam_im.astype(F32)
    dt = jnp.exp(log_dt.astype(F32))[:, None]

    def lbar_pow(k):
        mag = jnp.exp(k * (lr * dt))
        ang = k * (li * dt)
        return mag * jnp.cos(ang), mag * jnp.sin(ang)

    ab_re, ab_im = lbar_pow(1.0)
    den = lr * lr + li * li
    nr = ab_re - 1.0
    ni = ab_im
    f_re = (nr * lr + ni * li) / den
    f_im = (ni * lr - nr * li) / den
    br = b_re.astype(F32)
    bi = b_im.astype(F32)
    bb_re = f_re[..., None] * br - f_im[..., None] * bi
    bb_im = f_re[..., None] * bi + f_im[..., None] * br
    cr = c_re.astype(F32)
    ci = c_im.astype(F32)

    eye = jnp.eye(g, dtype=F32)

    def block_diag(x):
        _, a, b = x.shape
        return (x[:, :, None, :] * eye[:, None, :, None]).reshape(g * a, g * b)

    b_in = jnp.concatenate([block_diag(bb_re.transpose(0, 2, 1)),
                            block_diag(bb_im.transpose(0, 2, 1))], axis=1)
    c_out = jnp.concatenate([block_diag(cr.transpose(0, 2, 1)),
                             block_diag(-ci.transpose(0, 2, 1))], axis=0)
    lbar = jnp.stack([ab_re.reshape(-1), ab_im.reshape(-1)], axis=0)

    steps = []
    s = 1
    while s < n_chunks:
        a_re, a_im = lbar_pow(float(c * s))
        steps.append(jnp.stack([a_re.reshape(-1, MXU_DIM), a_im.reshape(-1, MXU_DIM)], axis=1))
        s *= 2
    step = jnp.stack(steps, axis=1)
    return b_in.astype(BF16), c_out.astype(BF16), lbar, step


def _ssm_body(u_ref, b_ref, c_ref, lbar_ref, step_ref, y_ref, z_ref, *, n_chunks, batch):
    s = pl.program_id(0)
    c = u_ref.shape[0]
    n_blk = z_ref.shape[0]
    half = n_blk // 2
    bw = z_ref.shape[2]

    @pl.when(s == 0)
    def _():
        z_ref[...] = jnp.zeros_like(z_ref)

    def advance(ut):
        for pb in range(half):
            lo, hi = pb * bw, (pb + 1) * bw
            re = z_ref[pb]
            im = z_ref[half + pb]
            a_re = lbar_ref[0:1, lo:hi]
            a_im = lbar_ref[1:2, lo:hi]
            x_re = jnp.dot(ut, b_ref[:, lo:hi], preferred_element_type=F32)
            x_im = jnp.dot(ut, b_ref[:, half * bw + lo:half * bw + hi], preferred_element_type=F32)
            z_ref[pb] = a_re * re - a_im * im + x_re
            z_ref[half + pb] = a_re * im + a_im * re + x_im

    @pl.when(s < c)
    def _():
        advance(u_ref[s])

    @pl.when(s == c)
    def _():
        chunk = lax.broadcasted_iota(jnp.int32, (n_chunks, bw), 0)

        def shifted(x, n):
            return jnp.where(chunk >= n, pltpu.roll(x, n, axis=0), 0.0)

        def scan_block(idx, carry):
            b = idx // half
            pb = idx % half
            rows = pl.ds(pl.multiple_of(b * n_chunks, n_chunks), n_chunks)
            re = z_ref[pb, rows, :]
            im = z_ref[half + pb, rows, :]
            n = 1
            k = 0
            while n < n_chunks:
                a_re = step_ref[pb, k, 0:1, :]
                a_im = step_ref[pb, k, 1:2, :]
                re_s = shifted(re, n)
                im_s = shifted(im, n)
                re, im = re + a_re * re_s - a_im * im_s, im + a_re * im_s + a_im * re_s
                n *= 2
                k += 1
            z_ref[pb, rows, :] = shifted(re, 1)
            z_ref[half + pb, rows, :] = shifted(im, 1)
            return carry

        lax.fori_loop(0, batch * half, scan_block, 0)

    @pl.when(s >= c)
    def _():
        advance(u_ref[s - c])
        state = jnp.concatenate([z_ref[blk].astype(BF16) for blk in range(n_blk)], axis=1)
        y_ref[0] = jnp.dot(state, c_ref[...], preferred_element_type=F32).astype(y_ref.dtype)


def _ssm_scan(u3, tables, *, batch, seq):
    b_in, c_out, lbar, step = tables
    c, rows, width = u3.shape
    n_chunks = seq // c
    assert rows == batch * n_chunks
    n_blk = b_in.shape[1] // MXU_DIM
    whole = lambda shape: pl.BlockSpec(shape, lambda s: (0,) * len(shape), pipeline_mode=pl.Buffered(1))
    return pl.pallas_call(
        functools.partial(_ssm_body, n_chunks=n_chunks, batch=batch),
        grid=(2 * c,),
        in_specs=[whole(u3.shape), whole(b_in.shape), whole(c_out.shape), whole(lbar.shape), whole(step.shape)],
        out_specs=pl.BlockSpec((1, rows, width), lambda s: (jnp.maximum(s - c, 0), 0, 0)),
        out_shape=jax.ShapeDtypeStruct((c, rows, width), BF16),
        scratch_shapes=[pltpu.VMEM((n_blk, rows, MXU_DIM), F32)],
        compiler_params=_params("arbitrary"),
        name="ssm_scan",
    )(u3, b_in, c_out, lbar, step)


def _gelu_tanh(x):
    return 0.5 * x * (1.0 + jnp.tanh(0.7978845608028654 * (x + 0.044715 * (x * x * x))))


def _top2(xn, wr):
    logits = lax.dot_general(wr, xn, (((1,), (1,)), ((), ())), preferred_element_type=F32,
                             precision=lax.Precision.HIGHEST)
    n_e = logits.shape[0]
    e_idx = lax.broadcasted_iota(jnp.int32, logits.shape, 0)
    m1 = jnp.max(logits, axis=0, keepdims=True)
    i1 = jnp.min(jnp.where(logits == m1, e_idx, n_e), axis=0, keepdims=True)
    rest = jnp.where(e_idx == i1, -jnp.inf, logits)
    m2 = jnp.max(rest, axis=0, keepdims=True)
    i2 = jnp.min(jnp.where(rest == m2, e_idx, n_e), axis=0, keepdims=True)
    e2 = jnp.exp(m2 - m1)
    w1 = 1.0 / (1.0 + e2)
    return jnp.concatenate([i1, i2], axis=0), jnp.concatenate([w1, e2 * w1], axis=0)


def _merge_body(attn_ref, yssm_ref, sc_ref, halo_ref, gate_ref, h_ref, dskip_ref, wglu_ref, convw_ref,
                wa_ref, wb_ref, wc_ref, wout_ref, gn_ref, *rest, tiles_per_seq, w_ssm, w_conv, route):
    if route:
        wr_ref, hout_ref, xn_ref, idx_ref, wgt_ref, y_ref = rest
    else:
        hout_ref, xn_ref, y_ref = rest
    i = pl.program_id(0)
    tm = h_ref.shape[0]
    d = h_ref.shape[1]

    n_t = yssm_ref.shape[0]
    n_lb = w_ssm // LANES
    for step in range(n_t):
        for lb in range(n_lb):
            y_ref[lb, pl.ds(step, tm // n_t, stride=n_t), :] = (
                yssm_ref[step, :, lb * LANES:(lb + 1) * LANES].astype(F32))
    y = jnp.concatenate([y_ref[lb] for lb in range(n_lb)], axis=1)
    u = sc_ref[:, 0:w_ssm].astype(F32)
    yb = _gelu_tanh(y + dskip_ref[...] * u)
    yb = yb * jax.nn.sigmoid(jnp.dot(yb.astype(BF16), wglu_ref[...], preferred_element_type=F32))
    y_b = jnp.dot(yb.astype(BF16), wb_ref[...], preferred_element_type=F32)

    o_h, o_b, o_c = w_ssm, w_ssm + w_conv, w_ssm + 2 * w_conv
    uc = sc_ref[:, o_c:o_c + w_conv].astype(F32) * sc_ref[:, o_h:o_h + w_conv].astype(F32)
    halo = halo_ref[:, o_c:o_c + w_conv].astype(F32) * halo_ref[:, o_h:o_h + w_conv].astype(F32)
    halo = jnp.where(i % tiles_per_seq == 0, 0.0, halo)
    row = lax.broadcasted_iota(jnp.int32, (tm, w_conv), 0)
    prev1 = jnp.where(row == 0, halo[SUBLANES - 1:SUBLANES, :], pltpu.roll(uc, 1, axis=0))
    prev2 = jnp.where(row == 0, halo[SUBLANES - 2:SUBLANES - 1, :],
                      jnp.where(row == 1, halo[SUBLANES - 1:SUBLANES, :], pltpu.roll(uc, 2, axis=0)))
    conv = convw_ref[0:1, :] * prev2 + convw_ref[1:2, :] * prev1 + convw_ref[2:3, :] * uc
    yc = sc_ref[:, o_b:o_b + w_conv].astype(F32) * conv
    y_c = jnp.dot(yc.astype(BF16), wc_ref[...], preferred_element_type=F32)

    y_a = jnp.dot(attn_ref[...], wa_ref[...], preferred_element_type=F32)

    merged = (gate_ref[:, 0:d].astype(F32) * y_a + gate_ref[:, d:2 * d].astype(F32) * y_b
              + gate_ref[:, 2 * d:3 * d].astype(F32) * y_c)
    hn = h_ref[...] + jnp.dot(merged.astype(BF16), wout_ref[...], preferred_element_type=F32)
    hout_ref[...] = hn
    xn = _rms(hn, gn_ref[...])
    xn_ref[...] = xn.astype(xn_ref.dtype)
    if route:
        idx, wgt = _top2(xn, wr_ref[...])
        idx_ref[0] = idx
        wgt_ref[0] = wgt


def _merge(attn, yssm, sc, gates, h, d_skip, w_glu, conv_w, w_a, w_b, w_c, w_out, g_ffn, w_router, *, seq):
    t, d = h.shape
    tm = TM_MERGE
    nt = t // tm
    n_t, _, w_ssm = yssm.shape
    w_conv = conv_w.shape[1]
    w_attn = attn.shape[1]
    n_sc = sc.shape[1]
    route = w_router is not None
    body = functools.partial(_merge_body, tiles_per_seq=seq // tm, w_ssm=w_ssm, w_conv=w_conv, route=route)
    full = lambda shape: pl.BlockSpec(shape, lambda i: (0,) * len(shape))
    row = pl.BlockSpec((tm, d), lambda i: (i, 0))
    halo_blocks = tm // SUBLANES
    in_specs = [
        pl.BlockSpec((tm, w_attn), lambda i: (i, 0)),
        pl.BlockSpec((n_t, tm // n_t, w_ssm), lambda i: (0, i, 0)),
        pl.BlockSpec((tm, n_sc), lambda i: (i, 0)),
        pl.BlockSpec((SUBLANES, n_sc), lambda i: (jnp.maximum(i * halo_blocks - 1, 0), 0)),
        pl.BlockSpec((tm, N_BRANCH * d), lambda i: (i, 0)),
        row,
        full((1, w_ssm)), full((w_ssm, w_ssm)), full((CONV_K, w_conv)),
        full((w_attn, d)), full((w_ssm, d)), full((w_conv, d)), full((d, d)), full((1, d)),
    ]
    args = [attn, yssm, sc, sc, gates, h, d_skip.reshape(1, w_ssm).astype(F32), w_glu, conv_w.astype(F32),
            w_a, w_b, w_c, w_out, g_ffn.reshape(1, d)]
    out_specs = [row, row]
    out_shape = [jax.ShapeDtypeStruct((t, d), F32), jax.ShapeDtypeStruct((t, d), F32 if route else BF16)]
    if route:
        n_e = w_router.shape[1]
        in_specs.append(full((n_e, d)))
        args.append(w_router.T.astype(F32))
        lanes = pl.BlockSpec((1, TOP_K, tm), lambda i: (i, 0, 0))
        out_specs += [lanes, lanes]
        out_shape += [jax.ShapeDtypeStruct((nt, TOP_K, tm), jnp.int32), jax.ShapeDtypeStruct((nt, TOP_K, tm), F32)]
    outs = pl.pallas_call(
        body,
        grid=(nt,),
        in_specs=in_specs,
        out_specs=out_specs,
        out_shape=out_shape,
        scratch_shapes=[pltpu.VMEM((w_ssm // LANES, tm, LANES), F32)],
        compiler_params=_params("parallel"),
        name="merge",
    )(*args)
    if not route:
        return outs
    hn, xn, idx, wgt = outs
    return hn, xn, idx.transpose(1, 0, 2).reshape(TOP_K, t), wgt.transpose(0, 2, 1).reshape(t, TOP_K)


def _swiglu_hidden(x, wg_ref, wu_ref, act_ref, j):
    gate = jnp.dot(x, wg_ref[...], preferred_element_type=F32)
    up = jnp.dot(x, wu_ref[...], preferred_element_type=F32)
    act_ref[j] = (gate * jax.nn.sigmoid(gate) * up).astype(act_ref.dtype)


def _swiglu_down(act_ref, wd_ref):
    act = jnp.concatenate([act_ref[jj] for jj in range(act_ref.shape[0])], axis=1)
    return jnp.dot(act, wd_ref[...], preferred_element_type=F32)


def _load_as_bf16(src_hbm, dst_ref, stage_ref, sem):
    rows = stage_ref.shape[1]
    n_chunks = src_hbm.shape[0] // rows

    def copy(c):
        return pltpu.make_async_copy(src_hbm.at[pl.ds(c * rows, rows)], stage_ref.at[c % 2], sem.at[c % 2])

    copy(0).start()
    for c in range(n_chunks):
        if c + 1 < n_chunks:
            copy(c + 1).start()
        copy(c).wait()
        dst_ref[c * rows:(c + 1) * rows, :] = stage_ref[c % 2].astype(dst_ref.dtype)


def _dense_ffn_body(x_ref, h_ref, wg_hbm, wu_hbm, wd_hbm, e0_hbm, e1_hbm, e2_hbm, o_ref, n0_hbm, n1_hbm, n2_hbm,
                    wg_ref, wu_ref, wd_ref, act_ref, st0_ref, st1_ref, st2_ref, nr0_ref, nr1_ref, nr2_ref,
                    in_sem, out_sem):
    s = pl.program_id(0)
    n_steps = pl.num_programs(0)
    stages = (st0_ref, st1_ref, st2_ref)
    narrow = (nr0_ref, nr1_ref, nr2_ref)

    def fetch(step, slot):
        return [pltpu.make_async_copy(src.at[pl.ds(step * st.shape[1], st.shape[1])], st.at[slot], in_sem.at[slot])
                for src, st in zip((e0_hbm, e1_hbm, e2_hbm), stages)]

    def flush(step, slot):
        return [pltpu.make_async_copy(nr.at[slot], dst.at[pl.ds(step * nr.shape[1], nr.shape[1])], out_sem.at[slot])
                for nr, dst in zip(narrow, (n0_hbm, n1_hbm, n2_hbm))]

    @pl.when(s == 0)
    def _():
        _load_as_bf16(wg_hbm, wg_ref, st0_ref, in_sem)
        _load_as_bf16(wu_hbm, wu_ref, st1_ref, in_sem)
        _load_as_bf16(wd_hbm, wd_ref, st2_ref, in_sem)
        for cp in fetch(0, 0):
            cp.start()

    @pl.when(s + 1 < n_steps)
    def _():
        for cp in fetch(s + 1, (s + 1) % 2):
            cp.start()

    x = x_ref[...]
    tf = act_ref.shape[2]
    for j in range(act_ref.shape[0]):
        _swiglu_hidden(x, wg_ref.at[:, j * tf:(j + 1) * tf], wu_ref.at[:, j * tf:(j + 1) * tf], act_ref, j)
    o_ref[...] = h_ref[...] + _swiglu_down(act_ref, wd_ref)

    slot = s % 2
    for cp in fetch(s, slot):
        cp.wait()

    @pl.when(s >= 2)
    def _():
        for cp in flush(s - 2, slot):
            cp.wait()

    for st, nr in zip(stages, narrow):
        nr[slot] = st[slot].astype(nr.dtype)
    for cp in flush(s, slot):
        cp.start()

    @pl.when(s == n_steps - 1)
    def _():
        @pl.when(s >= 1)
        def _():
            for cp in flush(s - 1, 1 - slot):
                cp.wait()

        for cp in flush(s, slot):
            cp.wait()


def _dense_ffn(xn, h, w_gate, w_up, w_down, extra):
    t, d = h.shape
    f = w_gate.shape[1]
    tm, tf = TM_FFN, TF_FFN
    n_steps = t // tm
    rows = [e.shape[0] // n_steps for e in extra]
    assert all(e.shape[0] == r * n_steps and r % 16 == 0 for e, r in zip(extra, rows))
    assert w_gate.shape[0] % rows[0] == 0 and w_up.shape[0] % rows[1] == 0 and w_down.shape[0] % rows[2] == 0
    assert extra[0].shape[1] == f and extra[1].shape[1] == f and extra[2].shape[1] == d
    row = pl.BlockSpec((tm, d), lambda i: (i, 0))
    hbm = pl.BlockSpec(memory_space=pl.ANY)
    outs = pl.pallas_call(
        _dense_ffn_body,
        grid=(n_steps,),
        in_specs=[row, row] + [hbm] * 6,
        out_specs=[row, hbm, hbm, hbm],
        out_shape=[jax.ShapeDtypeStruct((t, d), F32)] + [jax.ShapeDtypeStruct(e.shape, BF16) for e in extra],
        scratch_shapes=[pltpu.VMEM((d, f), BF16), pltpu.VMEM((d, f), BF16), pltpu.VMEM((f, d), BF16),
                        pltpu.VMEM((f // tf, tm, tf), BF16)]
                       + [pltpu.VMEM((2, r, e.shape[1]), F32) for e, r in zip(extra, rows)]
                       + [pltpu.VMEM((2, r, e.shape[1]), BF16) for e, r in zip(extra, rows)]
                       + [pltpu.SemaphoreType.DMA((2,)), pltpu.SemaphoreType.DMA((2,))],
        compiler_params=_params("arbitrary"),
        name="dense_ffn",
    )(xn, h, w_gate, w_up, w_down, *extra)
    return outs[0], outs[1:]


def _scatter_rows_body(zero_start_ref, s1_ref, s2_ref, h_ref, xs_ref, zeros_ref, tile_ref, zero_sem, sem):
    i = pl.program_id(0)
    tm = h_ref.shape[0]

    @pl.when(i == 0)
    def _():
        zeros_ref[...] = jnp.zeros_like(zeros_ref)

        def fill(k, c):
            dst = xs_ref.at[pl.ds(pl.multiple_of(zero_start_ref[k], SUBLANES), zeros_ref.shape[0])]
            cp = pltpu.make_async_copy(zeros_ref, dst, zero_sem)
            cp.start()
            cp.wait()
            return c

        lax.fori_loop(0, zero_start_ref.shape[0], fill, 0)

    half = i % 2

    def drain(which):
        for _ in range(TOP_K):
            pltpu.make_async_copy(tile_ref.at[which], xs_ref.at[pl.ds(0, tm)], sem.at[which]).wait()

    @pl.when(i >= 2)
    def _():
        drain(half)

    tile_ref[half] = h_ref[...]

    def issue(r, c):
        src = tile_ref.at[half, pl.ds(r, 1)]
        pltpu.make_async_copy(src, xs_ref.at[pl.ds(s1_ref[r], 1)], sem.at[half]).start(priority=0)
        pltpu.make_async_copy(src, xs_ref.at[pl.ds(s2_ref[r], 1)], sem.at[half]).start(priority=1)
        return c

    lax.fori_loop(0, tm, issue, 0, unroll=8)

    @pl.when(i == pl.num_programs(0) - 1)
    def _():
        @pl.when(i >= 1)
        def _():
            drain(1 - half)

        drain(half)


def _scatter_rows(h, slot1, slot2, zero_start, cap, zero_rows):
    t, d = h.shape
    tm = TM_SCATTER
    smem_tile = pl.BlockSpec((tm,), lambda i, zs: (i,), memory_space=pltpu.SMEM)
    grid_spec = pltpu.PrefetchScalarGridSpec(
        num_scalar_prefetch=1,
        grid=(t // tm,),
        in_specs=[smem_tile, smem_tile, pl.BlockSpec((tm, d), lambda i, zs: (i, 0))],
        out_specs=pl.BlockSpec(memory_space=pl.ANY),
        scratch_shapes=[pltpu.VMEM((zero_rows, d), h.dtype), pltpu.VMEM((2, tm, d), h.dtype),
                        pltpu.SemaphoreType.DMA(()), pltpu.SemaphoreType.DMA((2,))],
    )
    return pl.pallas_call(
        _scatter_rows_body,
        grid_spec=grid_spec,
        out_shape=jax.ShapeDtypeStruct((cap, d), h.dtype),
        compiler_params=_params("arbitrary"),
        name="scatter_rows",
    )(zero_start, slot1, slot2, h)


def _moe_ffn_body(tile_expert_ref, tile_valid_ref, x_ref, wg_ref, wu_ref, wd_ref, o_ref, act_ref):
    i = pl.program_id(0)
    j = pl.program_id(1)
    last = j == pl.num_programs(1) - 1
    valid = tile_valid_ref[i] > 0

    @pl.when(valid)
    def _():
        _swiglu_hidden(x_ref[...].astype(BF16), wg_ref.at[0], wu_ref.at[0], act_ref, j)

        @pl.when(last)
        def _():
            o_ref[...] = _swiglu_down(act_ref, wd_ref.at[0])

    @pl.when(last & jnp.logical_not(valid))
    def _():
        o_ref[...] = jnp.zeros_like(o_ref)


def _moe_ffn(xs, tile_expert, tile_valid, w_gate, w_up, w_down):
    cap, d = xs.shape
    f = w_gate.shape[2]
    tm, tf = TM_MOE, TF_FFN
    grid_spec = pltpu.PrefetchScalarGridSpec(
        num_scalar_prefetch=2,
        grid=(cap // tm, f // tf),
        in_specs=[
            pl.BlockSpec((tm, d), lambda i, j, te, tv: (i, 0)),
            pl.BlockSpec((1, d, tf), lambda i, j, te, tv: (te[i], 0, j)),
            pl.BlockSpec((1, d, tf), lambda i, j, te, tv: (te[i], 0, j)),
            pl.BlockSpec((1, f, d), lambda i, j, te, tv: (te[i], 0, 0)),
        ],
        out_specs=pl.BlockSpec((tm, d), lambda i, j, te, tv: (i, 0)),
        scratch_shapes=[pltpu.VMEM((f // tf, tm, tf), BF16)],
    )
    return pl.pallas_call(
        _moe_ffn_body,
        grid_spec=grid_spec,
        out_shape=jax.ShapeDtypeStruct((cap, d), F32),
        compiler_params=_params("parallel", "arbitrary"),
        name="moe_ffn",
    )(tile_expert, tile_valid, xs, w_gate, w_up, w_down)


def _moe_plan(idx, n_experts, tm):
    t = idx.shape[1]
    pairs = t * TOP_K
    cap = pairs + n_experts * tm
    e_flat = idx.reshape(1, pairs)
    onehot = (e_flat == jnp.arange(n_experts, dtype=jnp.int32)[:, None]).astype(jnp.int32)
    rank = jnp.sum(onehot * (jnp.cumsum(onehot, axis=1) - onehot), axis=0)
    counts = jnp.sum(onehot, axis=1)
    padded = ((counts + tm - 1) // tm) * tm
    ends = jnp.cumsum(padded)
    starts = ends - padded
    slot = jnp.sum(onehot * starts[:, None], axis=0) + rank
    tile_start = jnp.arange(cap // tm, dtype=jnp.int32) * tm
    tile_expert = jnp.sum((tile_start[:, None] >= ends[None, :]).astype(jnp.int32), axis=1)
    tile_valid = (tile_expert < n_experts).astype(jnp.int32)
    last_used = jnp.max(jnp.where(counts > 0, jnp.arange(n_experts, dtype=jnp.int32), 0))
    tile_expert = jnp.minimum(tile_expert, last_used).astype(jnp.int32)
    zero_rows = tm + SUBLANES
    tail = ends[-1] + jnp.arange(n_experts, dtype=jnp.int32) * tm
    first = jnp.concatenate([starts + counts, tail]) // SUBLANES * SUBLANES
    zero_start = jnp.minimum(first, cap - zero_rows).astype(jnp.int32)
    return slot.reshape(TOP_K, t).astype(jnp.int32), cap, tile_expert, tile_valid, zero_start, zero_rows


def _combine_body(s1_ref, s2_ref, next1_ref, next2_ref, h_ref, wgt_ref, g_ref, ys_ref, o_ref, y1_ref, y2_ref, sem,
                  *, apply_norm):
    i = pl.program_id(0)
    tm = h_ref.shape[0]

    def request(idx1_ref, idx2_ref, half):
        def issue(r, c):
            pltpu.make_async_copy(ys_ref.at[pl.ds(idx1_ref[r], 1)], y1_ref.at[half, pl.ds(r, 1)],
                                  sem.at[half]).start(priority=0)
            pltpu.make_async_copy(ys_ref.at[pl.ds(idx2_ref[r], 1)], y2_ref.at[half, pl.ds(r, 1)],
                                  sem.at[half]).start(priority=1)
            return c

        lax.fori_loop(0, tm, issue, 0, unroll=8)

    @pl.when(i == 0)
    def _():
        request(s1_ref, s2_ref, 0)

    @pl.when(i + 1 < pl.num_programs(0))
    def _():
        request(next1_ref, next2_ref, (i + 1) % 2)

    half = i % 2
    pltpu.make_async_copy(ys_ref.at[pl.ds(0, tm)], y1_ref.at[half], sem.at[half]).wait()
    pltpu.make_async_copy(ys_ref.at[pl.ds(0, tm)], y2_ref.at[half], sem.at[half]).wait()
    w = wgt_ref[...]
    hn = h_ref[...] + w[:, 0:1] * y1_ref[half] + w[:, 1:2] * y2_ref[half]
    o_ref[...] = _rms(hn, g_ref[...]) if apply_norm else hn


def _combine(h, ys, slot, wgt, g, *, apply_norm):
    t, d = h.shape
    tm = TM_COMBINE
    n_tiles = t // tm
    this_tile = pl.BlockSpec((tm,), lambda i: (i,), memory_space=pltpu.SMEM)
    next_tile = pl.BlockSpec((tm,), lambda i: (jnp.minimum(i + 1, n_tiles - 1),), memory_space=pltpu.SMEM)
    row = pl.BlockSpec((tm, d), lambda i: (i, 0))
    s1, s2 = slot[0], slot[1]
    return pl.pallas_call(
        functools.partial(_combine_body, apply_norm=apply_norm),
        grid=(n_tiles,),
        in_specs=[this_tile, this_tile, next_tile, next_tile, row,
                  pl.BlockSpec((tm, TOP_K), lambda i: (i, 0)),
                  pl.BlockSpec((1, d), lambda i: (0, 0)),
                  pl.BlockSpec(memory_space=pl.ANY)],
        out_specs=row,
        out_shape=jax.ShapeDtypeStruct((t, d), F32),
        scratch_shapes=[pltpu.VMEM((2, tm, d), F32), pltpu.VMEM((2, tm, d), F32), pltpu.SemaphoreType.DMA((2,))],
        compiler_params=_params("arbitrary"),
        name="combine",
    )(s1, s2, s1, s2, h, wgt, g.reshape(1, d), ys)


def _plain_norm_body(h_ref, g_ref, o_ref):
    o_ref[...] = _rms(h_ref[...], g_ref[...])


def _plain_norm(h, g):
    t, d = h.shape
    tm = TM_NORM
    row = pl.BlockSpec((tm, d), lambda i: (i, 0))
    return pl.pallas_call(
        _plain_norm_body,
        grid=(t // tm,),
        in_specs=[row, pl.BlockSpec((1, d), lambda i: (0, 0))],
        out_specs=row,
        out_shape=jax.ShapeDtypeStruct((t, d), F32),
        compiler_params=_params("parallel"),
        name="plain_norm",
    )(h, g.reshape(1, d))


def kernel(x, norm_mix, w_in, ssm_lambda_re, ssm_lambda_im, ssm_log_dt, ssm_b_re, ssm_b_im, ssm_c_re,
           ssm_c_im, ssm_d, ssm_w_glu, conv_w, w_br_a, w_br_b, w_br_c, w_out, norm_ffn, dense_w_gate,
           dense_w_up, dense_w_down, moe_w_router, moe_w_gate, moe_w_up, moe_w_down, final_norm):
    batch, seq, d = x.shape
    depth = w_in.shape[0]
    t = batch * seq
    w_attn = w_br_a.shape[1]
    w_ssm = w_br_b.shape[1]
    w_conv = w_br_c.shape[1]
    n_sc = w_ssm + 3 * w_conv
    n_gate = N_BRANCH * d
    n_experts = moe_w_router.shape[-1]

    h = x.reshape(t, d)
    normed = False
    for i in range(depth):
        qt, k, vt, u3, sc, gates = _inproj(h, norm_mix[i], w_in, i, batch=batch, seq=seq, width=w_attn,
                                           w_ssm=w_ssm, n_sc=n_sc, n_gate=n_gate)
        attn = _sb_attention(qt, k, vt)
        tables = _ssm_tables(ssm_lambda_re[i], ssm_lambda_im[i], ssm_log_dt[i], ssm_b_re[i], ssm_b_im[i],
                             ssm_c_re[i], ssm_c_im[i], seq // SSM_CHUNK)
        yssm = _ssm_scan(u3, tables, batch=batch, seq=seq)
        dense = i % 2 == 0
        j = i // 2
        merged = _merge(attn, yssm, sc, gates, h, ssm_d[i], ssm_w_glu[i].astype(BF16), conv_w[i],
                        w_br_a[i].astype(BF16), w_br_b[i].astype(BF16), w_br_c[i].astype(BF16),
                        w_out[i].astype(BF16), norm_ffn[i], None if dense else moe_w_router[j], seq=seq)
        if dense:
            h, xn = merged
            assert i + 1 < depth
            f = moe_w_gate.shape[-1]
            expert_f32 = (moe_w_gate[j].reshape(n_experts * d, f), moe_w_up[j].reshape(n_experts * d, f),
                          moe_w_down[j].reshape(n_experts * f, d))
            h, expert_bf16 = _dense_ffn(xn, h, dense_w_gate[j], dense_w_up[j], dense_w_down[j], expert_f32)
        else:
            h, xn, idx, wgt = merged
            slot, cap, tile_expert, tile_valid, zero_start, zero_rows = _moe_plan(idx, n_experts, TM_MOE)
            xs = _scatter_rows(xn, slot[0], slot[1], zero_start, cap, zero_rows)
            e_gate, e_up, e_down = expert_bf16
            ys = _moe_ffn(xs, tile_expert, tile_valid, e_gate.reshape(n_experts, d, -1),
                          e_up.reshape(n_experts, d, -1), e_down.reshape(n_experts, -1, d))
            last = i == depth - 1
            h = _combine(h, ys, slot, wgt, final_norm if last else norm_ffn[i], apply_norm=last)
            normed = last
    out = h if normed else _plain_norm(h, final_norm)
    return out.reshape(batch, seq, d)
```

```python
import functools

import jax
import jax.numpy as jnp
from jax import lax
from jax.experimental import pallas as pl
from jax.experimental.pallas import tpu as pltpu

F32 = jnp.float32
BF16 = jnp.bfloat16

EPS = 1e-6
HEAD_DIM = 64
SSM_CHUNK = 16
SSM_MAX_RE = -1e-4
CONV_K = 3
N_BRANCH = 3
TOP_K = 2
LOG2E = 1.4426950408889634
MASKED_LOG_WEIGHT = -1e30

LANES = 128
SUBLANES = 8
MXU_DIM = 256
VMEM_LIMIT_BYTES = 56 * 1024 * 1024

TM_PROJ = 512
TQ_ATTN = MXU_DIM
TM_MERGE = 512
TM_FFN = 256
TF_FFN = 1792
TM_MOE = 512
TM_SCATTER = 1024
TM_COMBINE = 1024
TM_NORM = 512


def _params(*sem):
    return pltpu.CompilerParams(dimension_semantics=sem, vmem_limit_bytes=VMEM_LIMIT_BYTES)


def _rms(x, g):
    ms = jnp.mean(x * x, axis=-1, keepdims=True)
    return x * lax.rsqrt(ms + EPS) * g


def _inproj_body(x_ref, g_ref, w_hbm, qt_ref, k_ref, vt_ref, u3_ref, sc_ref, gate_ref, w_ref, stage_ref, u_ref, sem,
                 *, layer, width, w_ssm, n_sc, n_gate, q_scale):
    @pl.when(pl.program_id(0) == 0)
    def _():
        _load_as_bf16(w_hbm.at[layer], w_ref, stage_ref, sem)

    xn = _rms(x_ref[...], g_ref[...]).astype(BF16)
    tm = xn.shape[0]
    tq = vt_ref.shape[-1]
    chunk = 512

    def proj(c0, n):
        return jnp.dot(xn, w_ref[:, c0:c0 + n], preferred_element_type=F32)

    qt_ref[0] = (proj(0, width) * q_scale).T.astype(BF16)
    k_ref[...] = proj(width, width).astype(BF16)
    v = proj(2 * width, width)
    for kt in range(tm // tq):
        for p in range(width // LANES):
            vt_ref[0, p, kt] = v[kt * tq:(kt + 1) * tq, p * LANES:(p + 1) * LANES].T.astype(BF16)
    for c0 in range(0, n_sc, chunk):
        r = proj(3 * width + c0, chunk)
        sc_ref[:, c0:c0 + chunk] = r.astype(BF16)
        if c0 == 0:
            for lb in range(w_ssm // LANES):
                u_ref[lb] = r[:, lb * LANES:(lb + 1) * LANES]
    n_t = u3_ref.shape[0]
    for step in range(n_t):
        for lb in range(w_ssm // LANES):
            u3_ref[step, :, lb * LANES:(lb + 1) * LANES] = (
                u_ref[lb, pl.ds(step, tm // n_t, stride=n_t), :].astype(BF16))
    for c0 in range(0, n_gate, chunk):
        gate_ref[:, c0:c0 + chunk] = jax.nn.sigmoid(proj(3 * width + n_sc + c0, chunk)).astype(BF16)


def _inproj(h, g, w_in, layer, *, batch, seq, width, w_ssm, n_sc, n_gate):
    t, d = h.shape
    tm = TM_PROJ
    tq = TQ_ATTN
    n_pairs = width // LANES
    tiles_per_seq = seq // tm
    n_cols = 3 * width + n_sc + n_gate
    stage_rows = 64
    assert w_in.shape[1:] == (d, n_cols) and d % stage_rows == 0 and w_ssm <= 512
    body = functools.partial(_inproj_body, layer=layer, width=width, w_ssm=w_ssm, n_sc=n_sc, n_gate=n_gate,
                             q_scale=HEAD_DIM ** -0.5)
    return pl.pallas_call(
        body,
        grid=(t // tm,),
        in_specs=[
            pl.BlockSpec((tm, d), lambda i: (i, 0)),
            pl.BlockSpec((1, d), lambda i: (0, 0)),
            pl.BlockSpec(memory_space=pl.ANY),
        ],
        out_specs=[
            pl.BlockSpec((1, width, tm), lambda i: (i // tiles_per_seq, 0, i % tiles_per_seq)),
            pl.BlockSpec((tm, width), lambda i: (i, 0)),
            pl.BlockSpec((1, n_pairs, tm // tq, LANES, tq),
                         lambda i: (i // tiles_per_seq, 0, i % tiles_per_seq, 0, 0)),
            pl.BlockSpec((SSM_CHUNK, tm // SSM_CHUNK, w_ssm), lambda i: (0, i, 0)),
            pl.BlockSpec((tm, n_sc), lambda i: (i, 0)),
            pl.BlockSpec((tm, n_gate), lambda i: (i, 0)),
        ],
        out_shape=[
            jax.ShapeDtypeStruct((batch, width, seq), BF16),
            jax.ShapeDtypeStruct((t, width), BF16),
            jax.ShapeDtypeStruct((batch, n_pairs, seq // tq, LANES, tq), BF16),
            jax.ShapeDtypeStruct((SSM_CHUNK, t // SSM_CHUNK, w_ssm), BF16),
            jax.ShapeDtypeStruct((t, n_sc), BF16),
            jax.ShapeDtypeStruct((t, n_gate), BF16),
        ],
        scratch_shapes=[pltpu.VMEM((d, n_cols), BF16), pltpu.VMEM((2, stage_rows, n_cols), F32),
                        pltpu.VMEM((w_ssm // LANES, tm, LANES), F32), pltpu.SemaphoreType.DMA((2,))],
        compiler_params=_params("arbitrary"),
        name="inproj",
    )(h, g.reshape(1, d), w_in)


def _sb_attn_body(qt_ref, k_ref, vt_ref, o_ref, acc_ref, carry_ref, lb_a_ref, sp_a_ref, lb_b_ref, sp_b_ref,
                  *, tq, n_pairs):
    i = pl.program_id(1)
    buf_a = (lb_a_ref, sp_a_ref)
    buf_b = (lb_b_ref, sp_b_ref)
    n_heads = 2 * n_pairs
    ones_rows = carry_ref.shape[1]
    key = lax.broadcasted_iota(jnp.int32, (tq, tq), 0)
    qry = lax.broadcasted_iota(jnp.int32, (tq, tq), 1)
    valid = key < qry
    er = lax.broadcasted_iota(jnp.int32, (tq + ones_rows, tq), 0)
    ec = lax.broadcasted_iota(jnp.int32, (tq + ones_rows, tq), 1)
    later_and_sum = jnp.where((ec > er) | (er >= tq), 1.0, 0.0).astype(BF16)

    feat = lax.broadcasted_iota(jnp.int32, (LANES, tq), 0)
    q_heads = []
    for p in range(n_pairs):
        qp = qt_ref[0, p * LANES:(p + 1) * LANES, :]
        zero = jnp.zeros_like(qp)
        q_heads.append(jnp.where(feat < HEAD_DIM, qp, zero))
        q_heads.append(jnp.where(feat >= HEAD_DIM, qp, zero))

    def scores(j, buf, masked):
        lb_ref, sp_ref = buf
        start = pl.multiple_of(j * tq, tq)
        for h in range(n_heads):
            p = h // 2
            kblk = k_ref[pl.ds(start, tq), p * LANES:(p + 1) * LANES]
            z = jnp.dot(kblk, q_heads[h], preferred_element_type=F32)
            l1p = jnp.log(1.0 + jnp.exp2(jnp.abs(z) * (-LOG2E)))
            sp = jnp.maximum(z, 0.0) + l1p
            lb = z - sp
            if masked:
                sp = jnp.where(valid, sp, 0.0)
                lb = jnp.where(valid, lb, MASKED_LOG_WEIGHT)
            lb_ref[h] = lb
            sp_ref[h] = sp.astype(BF16)

    def weights(j, buf):
        lb_ref, sp_ref = buf
        ws, sums = [], []
        for h in range(n_heads):
            ts = jnp.dot(later_and_sum, sp_ref[h], preferred_element_type=F32)
            ws.append(jnp.exp(lb_ref[h] - ts[0:tq]).astype(BF16))
            sums.append(ts[tq:tq + ones_rows])
        for h in range(n_heads):
            p, hh = divmod(h, 2)
            vt = vt_ref[0, p, j, hh * HEAD_DIM:(hh + 1) * HEAD_DIM, :]
            pv = jnp.dot(vt, ws[h], preferred_element_type=F32)
            carry = carry_ref[h]
            scale = jnp.exp(-carry[0:1, :])
            rows = slice(h * HEAD_DIM, (h + 1) * HEAD_DIM)
            acc_ref[rows, :] += pv * scale
            carry_ref[h] = carry + sums[h]

    def earlier_keys_matter():
        return (jnp.max(jnp.exp(-carry_ref[:, 0, :])) > 0.0).astype(jnp.int32)

    acc_ref[...] = jnp.zeros_like(acc_ref)
    carry_ref[...] = jnp.zeros_like(carry_ref)

    @pl.when(i == 0)
    def _():
        scores(0, buf_a, True)
        weights(0, buf_a)

    @pl.when(i > 0)
    def _():
        scores(i, buf_a, True)
        scores(i - 1, buf_b, False)
        weights(i, buf_a)
        weights(i - 1, buf_b)

    rest = i - 2

    @pl.when(jnp.where(rest >= 0, earlier_keys_matter(), 0) > 0)
    def _():
        scores(rest, buf_a, False)

        def two_tiles(c):
            m, _ = c
            j = rest - 1 - 2 * m
            scores(j, buf_b, False)
            weights(j + 1, buf_a)
            scores(j - 1, buf_a, False)
            weights(j, buf_b)
            return m + 1, earlier_keys_matter()

        _, alive = lax.while_loop(lambda c: (c[0] < rest // 2) & (c[1] > 0), two_tiles,
                                  (jnp.int32(0), jnp.int32(1)))

        @pl.when((alive > 0) & (rest % 2 == 1))
        def _():
            scores(0, buf_b, False)
            weights(1, buf_a)
            weights(0, buf_b)

        @pl.when((alive > 0) & (rest % 2 == 0))
        def _():
            weights(0, buf_a)

    for p in range(n_pairs):
        o_ref[:, p * LANES:(p + 1) * LANES] = acc_ref[p * LANES:(p + 1) * LANES, :].T.astype(o_ref.dtype)


def _sb_attention(qt, k, vt):
    batch, width, seq = qt.shape
    n_pairs, nq, _, tq = vt.shape[1:]
    n_heads = 2 * n_pairs
    return pl.pallas_call(
        functools.partial(_sb_attn_body, tq=tq, n_pairs=n_pairs),
        grid=(batch, nq),
        in_specs=[
            pl.BlockSpec((1, width, tq), lambda b, i: (b, 0, i)),
            pl.BlockSpec((seq, width), lambda b, i: (b, 0)),
            pl.BlockSpec((1, n_pairs, nq, LANES, tq), lambda b, i: (b, 0, 0, 0, 0)),
        ],
        out_specs=pl.BlockSpec((tq, width), lambda b, i: (b * nq + i, 0)),
        out_shape=jax.ShapeDtypeStruct((batch * seq, width), BF16),
        scratch_shapes=[pltpu.VMEM((width, tq), F32), pltpu.VMEM((n_heads, 16, tq), F32),
                        pltpu.VMEM((n_heads, tq, tq), F32), pltpu.VMEM((n_heads, tq, tq), BF16),
                        pltpu.VMEM((n_heads, tq, tq), F32), pltpu.VMEM((n_heads, tq, tq), BF16)],
        compiler_params=_params("parallel", "arbitrary"),
        name="sb_attention",
    )(qt, k, vt)


def _ssm_tables(lam_re, lam_im, log_dt, b_re, b_im, c_re, c_im, n_chunks):
    g, p = lam_re.shape
    c = SSM_CHUNK
    lr = jnp.minimum(lam_re.astype(F32), SSM_MAX_RE)
    li = lam_im.astype(F32)
    dt = jnp.exp(log_dt.astype(F32))[:, None]

    def lbar_pow(k):
        mag = jnp.exp(k * (lr * dt))
        ang = k * (li * dt)
        return mag * jnp.cos(ang), mag * jnp.sin(ang)

    ab_re, ab_im = lbar_pow(1.0)
    den = lr * lr + li * li
    nr = ab_re - 1.0
    ni = ab_im
    f_re = (nr * lr + ni * li) / den
    f_im = (ni * lr - nr * li) / den
    br = b_re.astype(F32)
    bi = b_im.astype(F32)
    bb_re = f_re[..., None] * br - f_im[..., None] * bi
    bb_im = f_re[..., None] * bi + f_im[..., None] * br
    cr = c_re.astype(F32)
    ci = c_im.astype(F32)

    eye = jnp.eye(g, dtype=F32)

    def block_diag(x):
        _, a, b = x.shape
        return (x[:, :, None, :] * eye[:, None, :, None]).reshape(g * a, g * b)

    b_in = jnp.concatenate([block_diag(bb_re.transpose(0, 2, 1)),
                            block_diag(bb_im.transpose(0, 2, 1))], axis=1)
    c_out = jnp.concatenate([block_diag(cr.transpose(0, 2, 1)),
                             block_diag(-ci.transpose(0, 2, 1))], axis=0)
    lbar = jnp.stack([ab_re.reshape(-1), ab_im.reshape(-1)], axis=0)

    steps = []
    s = 1
    while s < n_chunks:
        a_re, a_im = lbar_pow(float(c * s))
        steps.append(jnp.stack([a_re.reshape(-1, MXU_DIM), a_im.reshape(-1, MXU_DIM)], axis=1))
        s *= 2
    step = jnp.stack(steps, axis=1)
    return b_in.astype(BF16), c_out.astype(BF16), lbar, step


def _ssm_body(u_ref, b_ref, c_ref, lbar_ref, step_ref, y_ref, z_ref, *, n_chunks, batch):
    s = pl.program_id(0)
    c = u_ref.shape[0]
    n_blk = z_ref.shape[0]
    half = n_blk // 2
    bw = z_ref.shape[2]

    @pl.when(s == 0)
    def _():
        z_ref[...] = jnp.zeros_like(z_ref)

    def advance(ut):
        for pb in range(half):
            lo, hi = pb * bw, (pb + 1) * bw
            re = z_ref[pb]
            im = z_ref[half + pb]
            a_re = lbar_ref[0:1, lo:hi]
            a_im = lbar_ref[1:2, lo:hi]
            x_re = jnp.dot(ut, b_ref[:, lo:hi], preferred_element_type=F32)
            x_im = jnp.dot(ut, b_ref[:, half * bw + lo:half * bw + hi], preferred_element_type=F32)
            z_ref[pb] = a_re * re - a_im * im + x_re
            z_ref[half + pb] = a_re * im + a_im * re + x_im

    @pl.when(s < c)
    def _():
        advance(u_ref[s])

    @pl.when(s == c)
    def _():
        chunk = lax.broadcasted_iota(jnp.int32, (n_chunks, bw), 0)

        def shifted(x, n):
            return jnp.where(chunk >= n, pltpu.roll(x, n, axis=0), 0.0)

        def scan_block(idx, carry):
            b = idx // half
            pb = idx % half
            rows = pl.ds(pl.multiple_of(b * n_chunks, n_chunks), n_chunks)
            re = z_ref[pb, rows, :]
            im = z_ref[half + pb, rows, :]
            n = 1
            k = 0
            while n < n_chunks:
                a_re = step_ref[pb, k, 0:1, :]
                a_im = step_ref[pb, k, 1:2, :]
                re_s = shifted(re, n)
                im_s = shifted(im, n)
                re, im = re + a_re * re_s - a_im * im_s, im + a_re * im_s + a_im * re_s
                n *= 2
                k += 1
            z_ref[pb, rows, :] = shifted(re, 1)
            z_ref[half + pb, rows, :] = shifted(im, 1)
            return carry

        lax.fori_loop(0, batch * half, scan_block, 0)

    @pl.when(s >= c)
    def _():
        advance(u_ref[s - c])
        state = jnp.concatenate([z_ref[blk].astype(BF16) for blk in range(n_blk)], axis=1)
        y_ref[0] = jnp.dot(state, c_ref[...], preferred_element_type=F32).astype(y_ref.dtype)


def _ssm_scan(u3, tables, *, batch, seq):
    b_in, c_out, lbar, step = tables
    c, rows, width = u3.shape
    n_chunks = seq // c
    assert rows == batch * n_chunks
    n_blk = b_in.shape[1] // MXU_DIM
    whole = lambda shape: pl.BlockSpec(shape, lambda s: (0,) * len(shape), pipeline_mode=pl.Buffered(1))
    return pl.pallas_call(
        functools.partial(_ssm_body, n_chunks=n_chunks, batch=batch),
        grid=(2 * c,),
        in_specs=[whole(u3.shape), whole(b_in.shape), whole(c_out.shape), whole(lbar.shape), whole(step.shape)],
        out_specs=pl.BlockSpec((1, rows, width), lambda s: (jnp.maximum(s - c, 0), 0, 0)),
        out_shape=jax.ShapeDtypeStruct((c, rows, width), BF16),
        scratch_shapes=[pltpu.VMEM((n_blk, rows, MXU_DIM), F32)],
        compiler_params=_params("arbitrary"),
        name="ssm_scan",
    )(u3, b_in, c_out, lbar, step)


def _gelu_tanh(x):
    return 0.5 * x * (1.0 + jnp.tanh(0.7978845608028654 * (x + 0.044715 * (x * x * x))))


def _top2(xn, wr):
    logits = lax.dot_general(wr, xn, (((1,), (1,)), ((), ())), preferred_element_type=F32,
                             precision=lax.Precision.HIGHEST)
    n_e = logits.shape[0]
    e_idx = lax.broadcasted_iota(jnp.int32, logits.shape, 0)
    m1 = jnp.max(logits, axis=0, keepdims=True)
    i1 = jnp.min(jnp.where(logits == m1, e_idx, n_e), axis=0, keepdims=True)
    rest = jnp.where(e_idx == i1, -jnp.inf, logits)
    m2 = jnp.max(rest, axis=0, keepdims=True)
    i2 = jnp.min(jnp.where(rest == m2, e_idx, n_e), axis=0, keepdims=True)
    e2 = jnp.exp(m2 - m1)
    w1 = 1.0 / (1.0 + e2)
    return jnp.concatenate([i1, i2], axis=0), jnp.concatenate([w1, e2 * w1], axis=0)


def _merge_body(attn_ref, yssm_ref, sc_ref, halo_ref, gate_ref, h_ref, dskip_ref, wglu_ref, convw_ref,
                wa_ref, wb_ref, wc_ref, wout_ref, gn_ref, *rest, tiles_per_seq, w_ssm, w_conv, route):
    if route:
        wr_ref, hout_ref, xn_ref, idx_ref, wgt_ref, y_ref = rest
    else:
        hout_ref, xn_ref, y_ref = rest
    i = pl.program_id(0)
    tm = h_ref.shape[0]
    d = h_ref.shape[1]

    n_t = yssm_ref.shape[0]
    n_lb = w_ssm // LANES
    for step in range(n_t):
        for lb in range(n_lb):
            y_ref[lb, pl.ds(step, tm // n_t, stride=n_t), :] = (
                yssm_ref[step, :, lb * LANES:(lb + 1) * LANES].astype(F32))
    y = jnp.concatenate([y_ref[lb] for lb in range(n_lb)], axis=1)
    u = sc_ref[:, 0:w_ssm].astype(F32)
    yb = _gelu_tanh(y + dskip_ref[...] * u)
    yb = yb * jax.nn.sigmoid(jnp.dot(yb.astype(BF16), wglu_ref[...], preferred_element_type=F32))
    y_b = jnp.dot(yb.astype(BF16), wb_ref[...], preferred_element_type=F32)

    o_h, o_b, o_c = w_ssm, w_ssm + w_conv, w_ssm + 2 * w_conv
    uc = sc_ref[:, o_c:o_c + w_conv].astype(F32) * sc_ref[:, o_h:o_h + w_conv].astype(F32)
    halo = halo_ref[:, o_c:o_c + w_conv].astype(F32) * halo_ref[:, o_h:o_h + w_conv].astype(F32)
    halo = jnp.where(i % tiles_per_seq == 0, 0.0, halo)
    row = lax.broadcasted_iota(jnp.int32, (tm, w_conv), 0)
    prev1 = jnp.where(row == 0, halo[SUBLANES - 1:SUBLANES, :], pltpu.roll(uc, 1, axis=0))
    prev2 = jnp.where(row == 0, halo[SUBLANES - 2:SUBLANES - 1, :],
                      jnp.where(row == 1, halo[SUBLANES - 1:SUBLANES, :], pltpu.roll(uc, 2, axis=0)))
    conv = convw_ref[0:1, :] * prev2 + convw_ref[1:2, :] * prev1 + convw_ref[2:3, :] * uc
    yc = sc_ref[:, o_b:o_b + w_conv].astype(F32) * conv
    y_c = jnp.dot(yc.astype(BF16), wc_ref[...], preferred_element_type=F32)

    y_a = jnp.dot(attn_ref[...], wa_ref[...], preferred_element_type=F32)

    merged = (gate_ref[:, 0:d].astype(F32) * y_a + gate_ref[:, d:2 * d].astype(F32) * y_b
              + gate_ref[:, 2 * d:3 * d].astype(F32) * y_c)
    hn = h_ref[...] + jnp.dot(merged.astype(BF16), wout_ref[...], preferred_element_type=F32)
    hout_ref[...] = hn
    xn = _rms(hn, gn_ref[...])
    xn_ref[...] = xn.astype(xn_ref.dtype)
    if route:
        idx, wgt = _top2(xn, wr_ref[...])
        idx_ref[0] = idx
        wgt_ref[0] = wgt


def _merge(attn, yssm, sc, gates, h, d_skip, w_glu, conv_w, w_a, w_b, w_c, w_out, g_ffn, w_router, *, seq):
    t, d = h.shape
    tm = TM_MERGE
    nt = t // tm
    n_t, _, w_ssm = yssm.shape
    w_conv = conv_w.shape[1]
    w_attn = attn.shape[1]
    n_sc = sc.shape[1]
    route = w_router is not None
    body = functools.partial(_merge_body, tiles_per_seq=seq // tm, w_ssm=w_ssm, w_conv=w_conv, route=route)
    full = lambda shape: pl.BlockSpec(shape, lambda i: (0,) * len(shape))
    row = pl.BlockSpec((tm, d), lambda i: (i, 0))
    halo_blocks = tm // SUBLANES
    in_specs = [
        pl.BlockSpec((tm, w_attn), lambda i: (i, 0)),
        pl.BlockSpec((n_t, tm // n_t, w_ssm), lambda i: (0, i, 0)),
        pl.BlockSpec((tm, n_sc), lambda i: (i, 0)),
        pl.BlockSpec((SUBLANES, n_sc), lambda i: (jnp.maximum(i * halo_blocks - 1, 0), 0)),
        pl.BlockSpec((tm, N_BRANCH * d), lambda i: (i, 0)),
        row,
        full((1, w_ssm)), full((w_ssm, w_ssm)), full((CONV_K, w_conv)),
        full((w_attn, d)), full((w_ssm, d)), full((w_conv, d)), full((d, d)), full((1, d)),
    ]
    args = [attn, yssm, sc, sc, gates, h, d_skip.reshape(1, w_ssm).astype(F32), w_glu, conv_w.astype(F32),
            w_a, w_b, w_c, w_out, g_ffn.reshape(1, d)]
    out_specs = [row, row]
    out_shape = [jax.ShapeDtypeStruct((t, d), F32), jax.ShapeDtypeStruct((t, d), F32 if route else BF16)]
    if route:
        n_e = w_router.shape[1]
        in_specs.append(full((n_e, d)))
        args.append(w_router.T.astype(F32))
        lanes = pl.BlockSpec((1, TOP_K, tm), lambda i: (i, 0, 0))
        out_specs += [lanes, lanes]
        out_shape += [jax.ShapeDtypeStruct((nt, TOP_K, tm), jnp.int32), jax.ShapeDtypeStruct((nt, TOP_K, tm), F32)]
    outs = pl.pallas_call(
        body,
        grid=(nt,),
        in_specs=in_specs,
        out_specs=out_specs,
        out_shape=out_shape,
        scratch_shapes=[pltpu.VMEM((w_ssm // LANES, tm, LANES), F32)],
        compiler_params=_params("parallel"),
        name="merge",
    )(*args)
    if not route:
        return outs
    hn, xn, idx, wgt = outs
    return hn, xn, idx.transpose(1, 0, 2).reshape(TOP_K, t), wgt.transpose(0, 2, 1).reshape(t, TOP_K)


def _swiglu_hidden(x, wg_ref, wu_ref, act_ref, j):
    gate = jnp.dot(x, wg_ref[...], preferred_element_type=F32)
    up = jnp.dot(x, wu_ref[...], preferred_element_type=F32)
    act_ref[j] = (gate * jax.nn.sigmoid(gate) * up).astype(act_ref.dtype)


def _swiglu_down(act_ref, wd_ref):
    act = jnp.concatenate([act_ref[jj] for jj in range(act_ref.shape[0])], axis=1)
    return jnp.dot(act, wd_ref[...], preferred_element_type=F32)


def _load_as_bf16(src_hbm, dst_ref, stage_ref, sem):
    rows = stage_ref.shape[1]
    n_chunks = src_hbm.shape[0] // rows

    def copy(c):
        return pltpu.make_async_copy(src_hbm.at[pl.ds(c * rows, rows)], stage_ref.at[c % 2], sem.at[c % 2])

    copy(0).start()
    for c in range(n_chunks):
        if c + 1 < n_chunks:
            copy(c + 1).start()
        copy(c).wait()
        dst_ref[c * rows:(c + 1) * rows, :] = stage_ref[c % 2].astype(dst_ref.dtype)


def _dense_ffn_body(x_ref, h_ref, wg_hbm, wu_hbm, wd_hbm, e0_hbm, e1_hbm, e2_hbm, o_ref, n0_hbm, n1_hbm, n2_hbm,
                    wg_ref, wu_ref, wd_ref, act_ref, st0_ref, st1_ref, st2_ref, nr0_ref, nr1_ref, nr2_ref,
                    in_sem, out_sem):
    s = pl.program_id(0)
    n_steps = pl.num_programs(0)
    stages = (st0_ref, st1_ref, st2_ref)
    narrow = (nr0_ref, nr1_ref, nr2_ref)

    def fetch(step, slot):
        return [pltpu.make_async_copy(src.at[pl.ds(step * st.shape[1], st.shape[1])], st.at[slot], in_sem.at[slot])
                for src, st in zip((e0_hbm, e1_hbm, e2_hbm), stages)]

    def flush(step, slot):
        return [pltpu.make_async_copy(nr.at[slot], dst.at[pl.ds(step * nr.shape[1], nr.shape[1])], out_sem.at[slot])
                for nr, dst in zip(narrow, (n0_hbm, n1_hbm, n2_hbm))]

    @pl.when(s == 0)
    def _():
        _load_as_bf16(wg_hbm, wg_ref, st0_ref, in_sem)
        _load_as_bf16(wu_hbm, wu_ref, st1_ref, in_sem)
        _load_as_bf16(wd_hbm, wd_ref, st2_ref, in_sem)
        for cp in fetch(0, 0):
            cp.start()

    @pl.when(s + 1 < n_steps)
    def _():
        for cp in fetch(s + 1, (s + 1) % 2):
            cp.start()

    x = x_ref[...]
    tf = act_ref.shape[2]
    for j in range(act_ref.shape[0]):
        _swiglu_hidden(x, wg_ref.at[:, j * tf:(j + 1) * tf], wu_ref.at[:, j * tf:(j + 1) * tf], act_ref, j)
    o_ref[...] = h_ref[...] + _swiglu_down(act_ref, wd_ref)

    slot = s % 2
    for cp in fetch(s, slot):
        cp.wait()

    @pl.when(s >= 2)
    def _():
        for cp in flush(s - 2, slot):
            cp.wait()

    for st, nr in zip(stages, narrow):
        nr[slot] = st[slot].astype(nr.dtype)
    for cp in flush(s, slot):
        cp.start()

    @pl.when(s == n_steps - 1)
    def _():
        @pl.when(s >= 1)
        def _():
            for cp in flush(s - 1, 1 - slot):
                cp.wait()

        for cp in flush(s, slot):
            cp.wait()


def _dense_ffn(xn, h, w_gate, w_up, w_down, extra):
    t, d = h.shape
    f = w_gate.shape[1]
    tm, tf = TM_FFN, TF_FFN
    n_steps = t // tm
    rows = [e.shape[0] // n_steps for e in extra]
    assert all(e.shape[0] == r * n_steps and r % 16 == 0 for e, r in zip(extra, rows))
    assert w_gate.shape[0] % rows[0] == 0 and w_up.shape[0] % rows[1] == 0 and w_down.shape[0] % rows[2] == 0
    assert extra[0].shape[1] == f and extra[1].shape[1] == f and extra[2].shape[1] == d
    row = pl.BlockSpec((tm, d), lambda i: (i, 0))
    hbm = pl.BlockSpec(memory_space=pl.ANY)
    outs = pl.pallas_call(
        _dense_ffn_body,
        grid=(n_steps,),
        in_specs=[row, row] + [hbm] * 6,
        out_specs=[row, hbm, hbm, hbm],
        out_shape=[jax.ShapeDtypeStruct((t, d), F32)] + [jax.ShapeDtypeStruct(e.shape, BF16) for e in extra],
        scratch_shapes=[pltpu.VMEM((d, f), BF16), pltpu.VMEM((d, f), BF16), pltpu.VMEM((f, d), BF16),
                        pltpu.VMEM((f // tf, tm, tf), BF16)]
                       + [pltpu.VMEM((2, r, e.shape[1]), F32) for e, r in zip(extra, rows)]
                       + [pltpu.VMEM((2, r, e.shape[1]), BF16) for e, r in zip(extra, rows)]
                       + [pltpu.SemaphoreType.DMA((2,)), pltpu.SemaphoreType.DMA((2,))],
        compiler_params=_params("arbitrary"),
        name="dense_ffn",
    )(xn, h, w_gate, w_up, w_down, *extra)
    return outs[0], outs[1:]


def _scatter_rows_body(zero_start_ref, s1_ref, s2_ref, h_ref, xs_ref, zeros_ref, tile_ref, zero_sem, sem):
    i = pl.program_id(0)
    tm = h_ref.shape[0]

    @pl.when(i == 0)
    def _():
        zeros_ref[...] = jnp.zeros_like(zeros_ref)

        def fill(k, c):
            dst = xs_ref.at[pl.ds(pl.multiple_of(zero_start_ref[k], SUBLANES), zeros_ref.shape[0])]
            cp = pltpu.make_async_copy(zeros_ref, dst, zero_sem)
            cp.start()
            cp.wait()
            return c

        lax.fori_loop(0, zero_start_ref.shape[0], fill, 0)

    half = i % 2

    def drain(which):
        for _ in range(TOP_K):
            pltpu.make_async_copy(tile_ref.at[which], xs_ref.at[pl.ds(0, tm)], sem.at[which]).wait()

    @pl.when(i >= 2)
    def _():
        drain(half)

    tile_ref[half] = h_ref[...]

    def issue(r, c):
        src = tile_ref.at[half, pl.ds(r, 1)]
        pltpu.make_async_copy(src, xs_ref.at[pl.ds(s1_ref[r], 1)], sem.at[half]).start(priority=0)
        pltpu.make_async_copy(src, xs_ref.at[pl.ds(s2_ref[r], 1)], sem.at[half]).start(priority=1)
        return c

    lax.fori_loop(0, tm, issue, 0, unroll=8)

    @pl.when(i == pl.num_programs(0) - 1)
    def _():
        @pl.when(i >= 1)
        def _():
            drain(1 - half)

        drain(half)


def _scatter_rows(h, slot1, slot2, zero_start, cap, zero_rows):
    t, d = h.shape
    tm = TM_SCATTER
    smem_tile = pl.BlockSpec((tm,), lambda i, zs: (i,), memory_space=pltpu.SMEM)
    grid_spec = pltpu.PrefetchScalarGridSpec(
        num_scalar_prefetch=1,
        grid=(t // tm,),
        in_specs=[smem_tile, smem_tile, pl.BlockSpec((tm, d), lambda i, zs: (i, 0))],
        out_specs=pl.BlockSpec(memory_space=pl.ANY),
        scratch_shapes=[pltpu.VMEM((zero_rows, d), h.dtype), pltpu.VMEM((2, tm, d), h.dtype),
                        pltpu.SemaphoreType.DMA(()), pltpu.SemaphoreType.DMA((2,))],
    )
    return pl.pallas_call(
        _scatter_rows_body,
        grid_spec=grid_spec,
        out_shape=jax.ShapeDtypeStruct((cap, d), h.dtype),
        compiler_params=_params("arbitrary"),
        name="scatter_rows",
    )(zero_start, slot1, slot2, h)


def _moe_ffn_body(tile_expert_ref, tile_valid_ref, x_ref, wg_ref, wu_ref, wd_ref, o_ref, act_ref):
    i = pl.program_id(0)
    j = pl.program_id(1)
    last = j == pl.num_programs(1) - 1
    valid = tile_valid_ref[i] > 0

    @pl.when(valid)
    def _():
        _swiglu_hidden(x_ref[...].astype(BF16), wg_ref.at[0], wu_ref.at[0], act_ref, j)

        @pl.when(last)
        def _():
            o_ref[...] = _swiglu_down(act_ref, wd_ref.at[0])

    @pl.when(last & jnp.logical_not(valid))
    def _():
        o_ref[...] = jnp.zeros_like(o_ref)


def _moe_ffn(xs, tile_expert, tile_valid, w_gate, w_up, w_down):
    cap, d = xs.shape
    f = w_gate.shape[2]
    tm, tf = TM_MOE, TF_FFN
    grid_spec = pltpu.PrefetchScalarGridSpec(
        num_scalar_prefetch=2,
        grid=(cap // tm, f // tf),
        in_specs=[
            pl.BlockSpec((tm, d), lambda i, j, te, tv: (i, 0)),
            pl.BlockSpec((1, d, tf), lambda i, j, te, tv: (te[i], 0, j)),
            pl.BlockSpec((1, d, tf), lambda i, j, te, tv: (te[i], 0, j)),
            pl.BlockSpec((1, f, d), lambda i, j, te, tv: (te[i], 0, 0)),
        ],
        out_specs=pl.BlockSpec((tm, d), lambda i, j, te, tv: (i, 0)),
        scratch_shapes=[pltpu.VMEM((f // tf, tm, tf), BF16)],
    )
    return pl.pallas_call(
        _moe_ffn_body,
        grid_spec=grid_spec,
        out_shape=jax.ShapeDtypeStruct((cap, d), F32),
        compiler_params=_params("parallel", "arbitrary"),
        name="moe_ffn",
    )(tile_expert, tile_valid, xs, w_gate, w_up, w_down)


def _moe_plan(idx, n_experts, tm):
    t = idx.shape[1]
    pairs = t * TOP_K
    cap = pairs + n_experts * tm
    e_flat = idx.reshape(1, pairs)
    onehot = (e_flat == jnp.arange(n_experts, dtype=jnp.int32)[:, None]).astype(jnp.int32)
    rank = jnp.sum(onehot * (jnp.cumsum(onehot, axis=1) - onehot), axis=0)
    counts = jnp.sum(onehot, axis=1)
    padded = ((counts + tm - 1) // tm) * tm
    ends = jnp.cumsum(padded)
    starts = ends - padded
    slot = jnp.sum(onehot * starts[:, None], axis=0) + rank
    tile_start = jnp.arange(cap // tm, dtype=jnp.int32) * tm
    tile_expert = jnp.sum((tile_start[:, None] >= ends[None, :]).astype(jnp.int32), axis=1)
    tile_valid = (tile_expert < n_experts).astype(jnp.int32)
    last_used = jnp.max(jnp.where(counts > 0, jnp.arange(n_experts, dtype=jnp.int32), 0))
    tile_expert = jnp.minimum(tile_expert, last_used).astype(jnp.int32)
    zero_rows = tm + SUBLANES
    tail = ends[-1] + jnp.arange(n_experts, dtype=jnp.int32) * tm
    first = jnp.concatenate([starts + counts, tail]) // SUBLANES * SUBLANES
    zero_start = jnp.minimum(first, cap - zero_rows).astype(jnp.int32)
    return slot.reshape(TOP_K, t).astype(jnp.int32), cap, tile_expert, tile_valid, zero_start, zero_rows


def _combine_body(s1_ref, s2_ref, next1_ref, next2_ref, h_ref, wgt_ref, g_ref, ys_ref, o_ref, y1_ref, y2_ref, sem,
                  *, apply_norm):
    i = pl.program_id(0)
    tm = h_ref.shape[0]

    def request(idx1_ref, idx2_ref, half):
        def issue(r, c):
            pltpu.make_async_copy(ys_ref.at[pl.ds(idx1_ref[r], 1)], y1_ref.at[half, pl.ds(r, 1)],
                                  sem.at[half]).start(priority=0)
            pltpu.make_async_copy(ys_ref.at[pl.ds(idx2_ref[r], 1)], y2_ref.at[half, pl.ds(r, 1)],
                                  sem.at[half]).start(priority=1)
            return c

        lax.fori_loop(0, tm, issue, 0, unroll=8)

    @pl.when(i == 0)
    def _():
        request(s1_ref, s2_ref, 0)

    @pl.when(i + 1 < pl.num_programs(0))
    def _():
        request(next1_ref, next2_ref, (i + 1) % 2)

    half = i % 2
    pltpu.make_async_copy(ys_ref.at[pl.ds(0, tm)], y1_ref.at[half], sem.at[half]).wait()
    pltpu.make_async_copy(ys_ref.at[pl.ds(0, tm)], y2_ref.at[half], sem.at[half]).wait()
    w = wgt_ref[...]
    hn = h_ref[...] + w[:, 0:1] * y1_ref[half] + w[:, 1:2] * y2_ref[half]
    o_ref[...] = _rms(hn, g_ref[...]) if apply_norm else hn


def _combine(h, ys, slot, wgt, g, *, apply_norm):
    t, d = h.shape
    tm = TM_COMBINE
    n_tiles = t // tm
    this_tile = pl.BlockSpec((tm,), lambda i: (i,), memory_space=pltpu.SMEM)
    next_tile = pl.BlockSpec((tm,), lambda i: (jnp.minimum(i + 1, n_tiles - 1),), memory_space=pltpu.SMEM)
    row = pl.BlockSpec((tm, d), lambda i: (i, 0))
    s1, s2 = slot[0], slot[1]
    return pl.pallas_call(
        functools.partial(_combine_body, apply_norm=apply_norm),
        grid=(n_tiles,),
        in_specs=[this_tile, this_tile, next_tile, next_tile, row,
                  pl.BlockSpec((tm, TOP_K), lambda i: (i, 0)),
                  pl.BlockSpec((1, d), lambda i: (0, 0)),
                  pl.BlockSpec(memory_space=pl.ANY)],
        out_specs=row,
        out_shape=jax.ShapeDtypeStruct((t, d), F32),
        scratch_shapes=[pltpu.VMEM((2, tm, d), F32), pltpu.VMEM((2, tm, d), F32), pltpu.SemaphoreType.DMA((2,))],
        compiler_params=_params("arbitrary"),
        name="combine",
    )(s1, s2, s1, s2, h, wgt, g.reshape(1, d), ys)


def _plain_norm_body(h_ref, g_ref, o_ref):
    o_ref[...] = _rms(h_ref[...], g_ref[...])


def _plain_norm(h, g):
    t, d = h.shape
    tm = TM_NORM
    row = pl.BlockSpec((tm, d), lambda i: (i, 0))
    return pl.pallas_call(
        _plain_norm_body,
        grid=(t // tm,),
        in_specs=[row, pl.BlockSpec((1, d), lambda i: (0, 0))],
        out_specs=row,
        out_shape=jax.ShapeDtypeStruct((t, d), F32),
        compiler_params=_params("parallel"),
        name="plain_norm",
    )(h, g.reshape(1, d))


def kernel(x, norm_mix, w_in, ssm_lambda_re, ssm_lambda_im, ssm_log_dt, ssm_b_re, ssm_b_im, ssm_c_re,
           ssm_c_im, ssm_d, ssm_w_glu, conv_w, w_br_a, w_br_b, w_br_c, w_out, norm_ffn, dense_w_gate,
           dense_w_up, dense_w_down, moe_w_router, moe_w_gate, moe_w_up, moe_w_down, final_norm):
    batch, seq, d = x.shape
    depth = w_in.shape[0]
    t = batch * seq
    w_attn = w_br_a.shape[1]
    w_ssm = w_br_b.shape[1]
    w_conv = w_br_c.shape[1]
    n_sc = w_ssm + 3 * w_conv
    n_gate = N_BRANCH * d
    n_experts = moe_w_router.shape[-1]

    h = x.reshape(t, d)
    normed = False
    for i in range(depth):
        qt, k, vt, u3, sc, gates = _inproj(h, norm_mix[i], w_in, i, batch=batch, seq=seq, width=w_attn,
                                           w_ssm=w_ssm, n_sc=n_sc, n_gate=n_gate)
        attn = _sb_attention(qt, k, vt)
        tables = _ssm_tables(ssm_lambda_re[i], ssm_lambda_im[i], ssm_log_dt[i], ssm_b_re[i], ssm_b_im[i],
                             ssm_c_re[i], ssm_c_im[i], seq // SSM_CHUNK)
        yssm = _ssm_scan(u3, tables, batch=batch, seq=seq)
        dense = i % 2 == 0
        j = i // 2
        merged = _merge(attn, yssm, sc, gates, h, ssm_d[i], ssm_w_glu[i].astype(BF16), conv_w[i],
                        w_br_a[i].astype(BF16), w_br_b[i].astype(BF16), w_br_c[i].astype(BF16),
                        w_out[i].astype(BF16), norm_ffn[i], None if dense else moe_w_router[j], seq=seq)
        if dense:
            h, xn = merged
            assert i + 1 < depth
            f = moe_w_gate.shape[-1]
            expert_f32 = (moe_w_gate[j].reshape(n_experts * d, f), moe_w_up[j].reshape(n_experts * d, f),
                          moe_w_down[j].reshape(n_experts * f, d))
            h, expert_bf16 = _dense_ffn(xn, h, dense_w_gate[j], dense_w_up[j], dense_w_down[j], expert_f32)
        else:
            h, xn, idx, wgt = merged
            slot, cap, tile_expert, tile_valid, zero_start, zero_rows = _moe_plan(idx, n_experts, TM_MOE)
            xs = _scatter_rows(xn, slot[0], slot[1], zero_start, cap, zero_rows)
            e_gate, e_up, e_down = expert_bf16
            ys = _moe_ffn(xs, tile_expert, tile_valid, e_gate.reshape(n_experts, d, -1),
                          e_up.reshape(n_experts, d, -1), e_down.reshape(n_experts, -1, d))
            last = i == depth - 1
            h = _combine(h, ys, slot, wgt, final_norm if last else norm_ffn[i], apply_norm=last)
            normed = last
    out = h if normed else _plain_norm(h, final_norm)
    return out.reshape(batch, seq, d)
```

```python
import functools

import jax
import jax.numpy as jnp
from jax import lax
from jax.experimental import pallas as pl
from jax.experimental.pallas import tpu as pltpu

F32 = jnp.float32
BF16 = jnp.bfloat16

EPS = 1e-6
HEAD_DIM = 64
SSM_CHUNK = 16
SSM_MAX_RE = -1e-4
CONV_K = 3
N_BRANCH = 3
TOP_K = 2
LOG2E = 1.4426950408889634
MASKED_LOG_WEIGHT = -1e30

LANES = 128
SUBLANES = 8
MXU_DIM = 256
VMEM_LIMIT_BYTES = 56 * 1024 * 1024

TM_PROJ = 512
TQ_ATTN = MXU_DIM
TM_MERGE = 512
TM_FFN = 256
TF_FFN = 1792
TM_MOE = 512
TM_SCATTER = 512
TM_COMBINE = 512
TM_NORM = 512


def _params(*sem):
    return pltpu.CompilerParams(dimension_semantics=sem, vmem_limit_bytes=VMEM_LIMIT_BYTES)


def _rms(x, g):
    ms = jnp.mean(x * x, axis=-1, keepdims=True)
    return x * lax.rsqrt(ms + EPS) * g


def _inproj_body(x_ref, g_ref, w_hbm, qt_ref, k_ref, vt_ref, u3_ref, sc_ref, gate_ref, w_ref, stage_ref, u_ref, sem,
                 *, layer, width, w_ssm, n_sc, n_gate, q_scale):
    @pl.when(pl.program_id(0) == 0)
    def _():
        _load_as_bf16(w_hbm.at[layer], w_ref, stage_ref, sem)

    xn = _rms(x_ref[...], g_ref[...]).astype(BF16)
    tm = xn.shape[0]
    tq = vt_ref.shape[-1]
    chunk = 512

    def proj(c0, n):
        return jnp.dot(xn, w_ref[:, c0:c0 + n], preferred_element_type=F32)

    qt_ref[0] = (proj(0, width) * q_scale).T.astype(BF16)
    k_ref[...] = proj(width, width).astype(BF16)
    v = proj(2 * width, width)
    for kt in range(tm // tq):
        for p in range(width // LANES):
            vt_ref[0, p, kt] = v[kt * tq:(kt + 1) * tq, p * LANES:(p + 1) * LANES].T.astype(BF16)
    for c0 in range(0, n_sc, chunk):
        r = proj(3 * width + c0, chunk)
        sc_ref[:, c0:c0 + chunk] = r.astype(BF16)
        if c0 == 0:
            for lb in range(w_ssm // LANES):
                u_ref[lb] = r[:, lb * LANES:(lb + 1) * LANES]
    n_t = u3_ref.shape[0]
    for step in range(n_t):
        for lb in range(w_ssm // LANES):
            u3_ref[step, :, lb * LANES:(lb + 1) * LANES] = (
                u_ref[lb, pl.ds(step, tm // n_t, stride=n_t), :].astype(BF16))
    for c0 in range(0, n_gate, chunk):
        gate_ref[:, c0:c0 + chunk] = jax.nn.sigmoid(proj(3 * width + n_sc + c0, chunk)).astype(BF16)


def _inproj(h, g, w_in, layer, *, batch, seq, width, w_ssm, n_sc, n_gate):
    t, d = h.shape
    tm = TM_PROJ
    tq = TQ_ATTN
    n_pairs = width // LANES
    tiles_per_seq = seq // tm
    n_cols = 3 * width + n_sc + n_gate
    stage_rows = 64
    assert w_in.shape[1:] == (d, n_cols) and d % stage_rows == 0 and w_ssm <= 512
    body = functools.partial(_inproj_body, layer=layer, width=width, w_ssm=w_ssm, n_sc=n_sc, n_gate=n_gate,
                             q_scale=HEAD_DIM ** -0.5)
    return pl.pallas_call(
        body,
        grid=(t // tm,),
        in_specs=[
            pl.BlockSpec((tm, d), lambda i: (i, 0)),
            pl.BlockSpec((1, d), lambda i: (0, 0)),
            pl.BlockSpec(memory_space=pl.ANY),
        ],
        out_specs=[
            pl.BlockSpec((1, width, tm), lambda i: (i // tiles_per_seq, 0, i % tiles_per_seq)),
            pl.BlockSpec((tm, width), lambda i: (i, 0)),
            pl.BlockSpec((1, n_pairs, tm // tq, LANES, tq),
                         lambda i: (i // tiles_per_seq, 0, i % tiles_per_seq, 0, 0)),
            pl.BlockSpec((SSM_CHUNK, tm // SSM_CHUNK, w_ssm), lambda i: (0, i, 0)),
            pl.BlockSpec((tm, n_sc), lambda i: (i, 0)),
            pl.BlockSpec((tm, n_gate), lambda i: (i, 0)),
        ],
        out_shape=[
            jax.ShapeDtypeStruct((batch, width, seq), BF16),
            jax.ShapeDtypeStruct((t, width), BF16),
            jax.ShapeDtypeStruct((batch, n_pairs, seq // tq, LANES, tq), BF16),
            jax.ShapeDtypeStruct((SSM_CHUNK, t // SSM_CHUNK, w_ssm), BF16),
            jax.ShapeDtypeStruct((t, n_sc), BF16),
            jax.ShapeDtypeStruct((t, n_gate), BF16),
        ],
        scratch_shapes=[pltpu.VMEM((d, n_cols), BF16), pltpu.VMEM((2, stage_rows, n_cols), F32),
                        pltpu.VMEM((w_ssm // LANES, tm, LANES), F32), pltpu.SemaphoreType.DMA((2,))],
        compiler_params=_params("arbitrary"),
        name="inproj",
    )(h, g.reshape(1, d), w_in)


def _sb_attn_body(qt_ref, k_ref, vt_ref, o_ref, acc_ref, carry_ref, lb_a_ref, sp_a_ref, lb_b_ref, sp_b_ref,
                  *, tq, n_pairs):
    i = pl.program_id(1)
    buf_a = (lb_a_ref, sp_a_ref)
    buf_b = (lb_b_ref, sp_b_ref)
    n_heads = 2 * n_pairs
    ones_rows = carry_ref.shape[1]
    key = lax.broadcasted_iota(jnp.int32, (tq, tq), 0)
    qry = lax.broadcasted_iota(jnp.int32, (tq, tq), 1)
    valid = key < qry
    er = lax.broadcasted_iota(jnp.int32, (tq + ones_rows, tq), 0)
    ec = lax.broadcasted_iota(jnp.int32, (tq + ones_rows, tq), 1)
    later_and_sum = jnp.where((ec > er) | (er >= tq), 1.0, 0.0).astype(BF16)

    feat = lax.broadcasted_iota(jnp.int32, (LANES, tq), 0)
    q_heads = []
    for p in range(n_pairs):
        qp = qt_ref[0, p * LANES:(p + 1) * LANES, :]
        zero = jnp.zeros_like(qp)
        q_heads.append(jnp.where(feat < HEAD_DIM, qp, zero))
        q_heads.append(jnp.where(feat >= HEAD_DIM, qp, zero))

    def scores(j, buf, masked):
        lb_ref, sp_ref = buf
        start = pl.multiple_of(j * tq, tq)
        for h in range(n_heads):
            p = h // 2
            kblk = k_ref[pl.ds(start, tq), p * LANES:(p + 1) * LANES]
            z = jnp.dot(kblk, q_heads[h], preferred_element_type=F32)
            l1p = jnp.log(1.0 + jnp.exp2(jnp.abs(z) * (-LOG2E)))
            sp = jnp.maximum(z, 0.0) + l1p
            lb = z - sp
            if masked:
                sp = jnp.where(valid, sp, 0.0)
                lb = jnp.where(valid, lb, MASKED_LOG_WEIGHT)
            lb_ref[h] = lb
            sp_ref[h] = sp.astype(BF16)

    def weights(j, buf):
        lb_ref, sp_ref = buf
        ws, sums = [], []
        for h in range(n_heads):
            ts = jnp.dot(later_and_sum, sp_ref[h], preferred_element_type=F32)
            ws.append(jnp.exp(lb_ref[h] - ts[0:tq]).astype(BF16))
            sums.append(ts[tq:tq + ones_rows])
        for h in range(n_heads):
            p, hh = divmod(h, 2)
            vt = vt_ref[0, p, j, hh * HEAD_DIM:(hh + 1) * HEAD_DIM, :]
            pv = jnp.dot(vt, ws[h], preferred_element_type=F32)
            carry = carry_ref[h]
            scale = jnp.exp(-carry[0:1, :])
            rows = slice(h * HEAD_DIM, (h + 1) * HEAD_DIM)
            acc_ref[rows, :] += pv * scale
            carry_ref[h] = carry + sums[h]

    def earlier_keys_matter():
        return (jnp.max(jnp.exp(-carry_ref[:, 0, :])) > 0.0).astype(jnp.int32)

    acc_ref[...] = jnp.zeros_like(acc_ref)
    carry_ref[...] = jnp.zeros_like(carry_ref)

    @pl.when(i == 0)
    def _():
        scores(0, buf_a, True)
        weights(0, buf_a)

    @pl.when(i > 0)
    def _():
        scores(i, buf_a, True)
        scores(i - 1, buf_b, False)
        weights(i, buf_a)
        weights(i - 1, buf_b)

    rest = i - 2

    @pl.when(jnp.where(rest >= 0, earlier_keys_matter(), 0) > 0)
    def _():
        scores(rest, buf_a, False)

        def two_tiles(c):
            m, _ = c
            j = rest - 1 - 2 * m
            scores(j, buf_b, False)
            weights(j + 1, buf_a)
            scores(j - 1, buf_a, False)
            weights(j, buf_b)
            return m + 1, earlier_keys_matter()

        _, alive = lax.while_loop(lambda c: (c[0] < rest // 2) & (c[1] > 0), two_tiles,
                                  (jnp.int32(0), jnp.int32(1)))

        @pl.when((alive > 0) & (rest % 2 == 1))
        def _():
            scores(0, buf_b, False)
            weights(1, buf_a)
            weights(0, buf_b)

        @pl.when((alive > 0) & (rest % 2 == 0))
        def _():
            weights(0, buf_a)

    for p in range(n_pairs):
        o_ref[:, p * LANES:(p + 1) * LANES] = acc_ref[p * LANES:(p + 1) * LANES, :].T.astype(o_ref.dtype)


def _sb_attention(qt, k, vt):
    batch, width, seq = qt.shape
    n_pairs, nq, _, tq = vt.shape[1:]
    n_heads = 2 * n_pairs
    return pl.pallas_call(
        functools.partial(_sb_attn_body, tq=tq, n_pairs=n_pairs),
        grid=(batch, nq),
        in_specs=[
            pl.BlockSpec((1, width, tq), lambda b, i: (b, 0, i)),
            pl.BlockSpec((seq, width), lambda b, i: (b, 0)),
            pl.BlockSpec((1, n_pairs, nq, LANES, tq), lambda b, i: (b, 0, 0, 0, 0)),
        ],
        out_specs=pl.BlockSpec((tq, width), lambda b, i: (b * nq + i, 0)),
        out_shape=jax.ShapeDtypeStruct((batch * seq, width), BF16),
        scratch_shapes=[pltpu.VMEM((width, tq), F32), pltpu.VMEM((n_heads, 16, tq), F32),
                        pltpu.VMEM((n_heads, tq, tq), F32), pltpu.VMEM((n_heads, tq, tq), BF16),
                        pltpu.VMEM((n_heads, tq, tq), F32), pltpu.VMEM((n_heads, tq, tq), BF16)],
        compiler_params=_params("parallel", "arbitrary"),
        name="sb_attention",
    )(qt, k, vt)


def _ssm_tables(lam_re, lam_im, log_dt, b_re, b_im, c_re, c_im, n_chunks):
    g, p = lam_re.shape
    c = SSM_CHUNK
    lr = jnp.minimum(lam_re.astype(F32), SSM_MAX_RE)
    li = lam_im.astype(F32)
    dt = jnp.exp(log_dt.astype(F32))[:, None]

    def lbar_pow(k):
        mag = jnp.exp(k * (lr * dt))
        ang = k * (li * dt)
        return mag * jnp.cos(ang), mag * jnp.sin(ang)

    ab_re, ab_im = lbar_pow(1.0)
    den = lr * lr + li * li
    nr = ab_re - 1.0
    ni = ab_im
    f_re = (nr * lr + ni * li) / den
    f_im = (ni * lr - nr * li) / den
    br = b_re.astype(F32)
    bi = b_im.astype(F32)
    bb_re = f_re[..., None] * br - f_im[..., None] * bi
    bb_im = f_re[..., None] * bi + f_im[..., None] * br
    cr = c_re.astype(F32)
    ci = c_im.astype(F32)

    eye = jnp.eye(g, dtype=F32)

    def block_diag(x):
        _, a, b = x.shape
        return (x[:, :, None, :] * eye[:, None, :, None]).reshape(g * a, g * b)

    b_in = jnp.concatenate([block_diag(bb_re.transpose(0, 2, 1)),
                            block_diag(bb_im.transpose(0, 2, 1))], axis=1)
    c_out = jnp.concatenate([block_diag(cr.transpose(0, 2, 1)),
                             block_diag(-ci.transpose(0, 2, 1))], axis=0)
    lbar = jnp.stack([ab_re.reshape(-1), ab_im.reshape(-1)], axis=0)

    steps = []
    s = 1
    while s < n_chunks:
        a_re, a_im = lbar_pow(float(c * s))
        steps.append(jnp.stack([a_re.reshape(-1, MXU_DIM), a_im.reshape(-1, MXU_DIM)], axis=1))
        s *= 2
    step = jnp.stack(steps, axis=1)
    return b_in.astype(BF16), c_out.astype(BF16), lbar, step


def _ssm_body(u_ref, b_ref, c_ref, lbar_ref, step_ref, y_ref, z_ref, *, n_chunks, batch):
    s = pl.program_id(0)
    c = u_ref.shape[0]
    n_blk = z_ref.shape[0]
    half = n_blk // 2
    bw = z_ref.shape[2]

    @pl.when(s == 0)
    def _():
        z_ref[...] = jnp.zeros_like(z_ref)

    def advance(ut):
        for pb in range(half):
            lo, hi = pb * bw, (pb + 1) * bw
            re = z_ref[pb]
            im = z_ref[half + pb]
            a_re = lbar_ref[0:1, lo:hi]
            a_im = lbar_ref[1:2, lo:hi]
            x_re = jnp.dot(ut, b_ref[:, lo:hi], preferred_element_type=F32)
            x_im = jnp.dot(ut, b_ref[:, half * bw + lo:half * bw + hi], preferred_element_type=F32)
            z_ref[pb] = a_re * re - a_im * im + x_re
            z_ref[half + pb] = a_re * im + a_im * re + x_im

    @pl.when(s < c)
    def _():
        advance(u_ref[s])

    @pl.when(s == c)
    def _():
        chunk = lax.broadcasted_iota(jnp.int32, (n_chunks, bw), 0)

        def shifted(x, n):
            return jnp.where(chunk >= n, pltpu.roll(x, n, axis=0), 0.0)

        def scan_block(idx, carry):
            b = idx // half
            pb = idx % half
            rows = pl.ds(pl.multiple_of(b * n_chunks, n_chunks), n_chunks)
            re = z_ref[pb, rows, :]
            im = z_ref[half + pb, rows, :]
            n = 1
            k = 0
            while n < n_chunks:
                a_re = step_ref[pb, k, 0:1, :]
                a_im = step_ref[pb, k, 1:2, :]
                re_s = shifted(re, n)
                im_s = shifted(im, n)
                re, im = re + a_re * re_s - a_im * im_s, im + a_re * im_s + a_im * re_s
                n *= 2
                k += 1
            z_ref[pb, rows, :] = shifted(re, 1)
            z_ref[half + pb, rows, :] = shifted(im, 1)
            return carry

        lax.fori_loop(0, batch * half, scan_block, 0)

    def emit():
        state = jnp.concatenate([z_ref[blk].astype(BF16) for blk in range(n_blk)], axis=1)
        y_ref[0] = jnp.dot(state, c_ref[...], preferred_element_type=F32).astype(y_ref.dtype)

    @pl.when(s == c)
    def _():
        advance(u_ref[0])

    @pl.when((s > c) & (s < 2 * c))
    def _():
        emit()
        advance(u_ref[s - c])

    @pl.when(s == 2 * c)
    def _():
        emit()


def _ssm_scan(u3, tables, *, batch, seq):
    b_in, c_out, lbar, step = tables
    c, rows, width = u3.shape
    n_chunks = seq // c
    assert rows == batch * n_chunks
    n_blk = b_in.shape[1] // MXU_DIM
    whole = lambda shape: pl.BlockSpec(shape, lambda s: (0,) * len(shape), pipeline_mode=pl.Buffered(1))
    return pl.pallas_call(
        functools.partial(_ssm_body, n_chunks=n_chunks, batch=batch),
        grid=(2 * c + 1,),
        in_specs=[whole(u3.shape), whole(b_in.shape), whole(c_out.shape), whole(lbar.shape), whole(step.shape)],
        out_specs=pl.BlockSpec((1, rows, width), lambda s: (jnp.maximum(s - c - 1, 0), 0, 0)),
        out_shape=jax.ShapeDtypeStruct((c, rows, width), BF16),
        scratch_shapes=[pltpu.VMEM((n_blk, rows, MXU_DIM), F32)],
        compiler_params=_params("arbitrary"),
        name="ssm_scan",
    )(u3, b_in, c_out, lbar, step)


def _gelu_tanh(x):
    return 0.5 * x * (1.0 + jnp.tanh(0.7978845608028654 * (x + 0.044715 * (x * x * x))))


def _top2(xn, wr):
    logits = lax.dot_general(wr, xn, (((1,), (1,)), ((), ())), preferred_element_type=F32,
                             precision=lax.Precision.HIGHEST)
    n_e = logits.shape[0]
    e_idx = lax.broadcasted_iota(jnp.int32, logits.shape, 0)
    m1 = jnp.max(logits, axis=0, keepdims=True)
    i1 = jnp.min(jnp.where(logits == m1, e_idx, n_e), axis=0, keepdims=True)
    rest = jnp.where(e_idx == i1, -jnp.inf, logits)
    m2 = jnp.max(rest, axis=0, keepdims=True)
    i2 = jnp.min(jnp.where(rest == m2, e_idx, n_e), axis=0, keepdims=True)
    e2 = jnp.exp(m2 - m1)
    w1 = 1.0 / (1.0 + e2)
    return jnp.concatenate([i1, i2], axis=0), jnp.concatenate([w1, e2 * w1], axis=0)


def _merge_body(attn_ref, yssm_ref, sc_ref, halo_ref, gate_ref, h_ref, dskip_ref, wglu_ref, convw_ref,
                wa_ref, wb_ref, wc_ref, wout_ref, gn_ref, *rest, tiles_per_seq, w_ssm, w_conv, route):
    if route:
        wr_ref, hout_ref, xn_ref, idx_ref, wgt_ref, y_ref = rest
    else:
        hout_ref, xn_ref, y_ref = rest
    i = pl.program_id(0)
    tm = h_ref.shape[0]
    d = h_ref.shape[1]

    n_t = yssm_ref.shape[0]
    n_lb = w_ssm // LANES
    for step in range(n_t):
        for lb in range(n_lb):
            y_ref[lb, pl.ds(step, tm // n_t, stride=n_t), :] = (
                yssm_ref[step, :, lb * LANES:(lb + 1) * LANES].astype(F32))
    y = jnp.concatenate([y_ref[lb] for lb in range(n_lb)], axis=1)
    u = sc_ref[:, 0:w_ssm].astype(F32)
    yb = _gelu_tanh(y + dskip_ref[...] * u)
    yb = yb * jax.nn.sigmoid(jnp.dot(yb.astype(BF16), wglu_ref[...], preferred_element_type=F32))
    y_b = jnp.dot(yb.astype(BF16), wb_ref[...], preferred_element_type=F32)

    o_h, o_b, o_c = w_ssm, w_ssm + w_conv, w_ssm + 2 * w_conv
    uc = sc_ref[:, o_c:o_c + w_conv].astype(F32) * sc_ref[:, o_h:o_h + w_conv].astype(F32)
    halo = halo_ref[:, o_c:o_c + w_conv].astype(F32) * halo_ref[:, o_h:o_h + w_conv].astype(F32)
    halo = jnp.where(i % tiles_per_seq == 0, 0.0, halo)
    row = lax.broadcasted_iota(jnp.int32, (tm, w_conv), 0)
    prev1 = jnp.where(row == 0, halo[SUBLANES - 1:SUBLANES, :], pltpu.roll(uc, 1, axis=0))
    prev2 = jnp.where(row == 0, halo[SUBLANES - 2:SUBLANES - 1, :],
                      jnp.where(row == 1, halo[SUBLANES - 1:SUBLANES, :], pltpu.roll(uc, 2, axis=0)))
    conv = convw_ref[0:1, :] * prev2 + convw_ref[1:2, :] * prev1 + convw_ref[2:3, :] * uc
    yc = sc_ref[:, o_b:o_b + w_conv].astype(F32) * conv
    y_c = jnp.dot(yc.astype(BF16), wc_ref[...], preferred_element_type=F32)

    y_a = jnp.dot(attn_ref[...], wa_ref[...], preferred_element_type=F32)

    merged = (gate_ref[:, 0:d] * y_a.astype(BF16) + gate_ref[:, d:2 * d] * y_b.astype(BF16)
              + gate_ref[:, 2 * d:3 * d] * y_c.astype(BF16))
    hn = h_ref[...] + jnp.dot(merged, wout_ref[...], preferred_element_type=F32)
    hout_ref[...] = hn
    xn = _rms(hn, gn_ref[...])
    xn_ref[...] = xn.astype(xn_ref.dtype)
    if route:
        idx, wgt = _top2(xn, wr_ref[...])
        idx_ref[0] = idx
        wgt_ref[0] = wgt


def _merge(attn, yssm, sc, gates, h, d_skip, w_glu, conv_w, w_a, w_b, w_c, w_out, g_ffn, w_router, *, seq):
    t, d = h.shape
    tm = TM_MERGE
    nt = t // tm
    n_t, _, w_ssm = yssm.shape
    w_conv = conv_w.shape[1]
    w_attn = attn.shape[1]
    n_sc = sc.shape[1]
    route = w_router is not None
    body = functools.partial(_merge_body, tiles_per_seq=seq // tm, w_ssm=w_ssm, w_conv=w_conv, route=route)
    full = lambda shape: pl.BlockSpec(shape, lambda i: (0,) * len(shape))
    row = pl.BlockSpec((tm, d), lambda i: (i, 0))
    halo_blocks = tm // SUBLANES
    in_specs = [
        pl.BlockSpec((tm, w_attn), lambda i: (i, 0)),
        pl.BlockSpec((n_t, tm // n_t, w_ssm), lambda i: (0, i, 0)),
        pl.BlockSpec((tm, n_sc), lambda i: (i, 0)),
        pl.BlockSpec((SUBLANES, n_sc), lambda i: (jnp.maximum(i * halo_blocks - 1, 0), 0)),
        pl.BlockSpec((tm, N_BRANCH * d), lambda i: (i, 0)),
        row,
        full((1, w_ssm)), full((w_ssm, w_ssm)), full((CONV_K, w_conv)),
        full((w_attn, d)), full((w_ssm, d)), full((w_conv, d)), full((d, d)), full((1, d)),
    ]
    args = [attn, yssm, sc, sc, gates, h, d_skip.reshape(1, w_ssm).astype(F32), w_glu, conv_w.astype(F32),
            w_a, w_b, w_c, w_out, g_ffn.reshape(1, d)]
    out_specs = [row, row]
    out_shape = [jax.ShapeDtypeStruct((t, d), F32), jax.ShapeDtypeStruct((t, d), F32 if route else BF16)]
    if route:
        n_e = w_router.shape[1]
        in_specs.append(full((n_e, d)))
        args.append(w_router.T.astype(F32))
        lanes = pl.BlockSpec((1, TOP_K, tm), lambda i: (i, 0, 0))
        out_specs += [lanes, lanes]
        out_shape += [jax.ShapeDtypeStruct((nt, TOP_K, tm), jnp.int32), jax.ShapeDtypeStruct((nt, TOP_K, tm), F32)]
    outs = pl.pallas_call(
        body,
        grid=(nt,),
        in_specs=in_specs,
        out_specs=out_specs,
        out_shape=out_shape,
        scratch_shapes=[pltpu.VMEM((w_ssm // LANES, tm, LANES), F32)],
        compiler_params=_params("parallel"),
        name="merge",
    )(*args)
    if not route:
        return outs
    hn, xn, idx, wgt = outs
    return hn, xn, idx.transpose(1, 0, 2).reshape(TOP_K, t), wgt.transpose(0, 2, 1).reshape(t, TOP_K)


def _swiglu_hidden(x, wg_ref, wu_ref, act_ref, j):
    gate = jnp.dot(x, wg_ref[...], preferred_element_type=F32)
    up = jnp.dot(x, wu_ref[...], preferred_element_type=F32)
    act_ref[j] = (gate * jax.nn.sigmoid(gate) * up).astype(act_ref.dtype)


def _swiglu_down(act_ref, wd_ref):
    act = jnp.concatenate([act_ref[jj] for jj in range(act_ref.shape[0])], axis=1)
    return jnp.dot(act, wd_ref[...], preferred_element_type=F32)


def _load_as_bf16(src_hbm, dst_ref, stage_ref, sem):
    rows = stage_ref.shape[1]
    n_chunks = src_hbm.shape[0] // rows

    def copy(c):
        return pltpu.make_async_copy(src_hbm.at[pl.ds(c * rows, rows)], stage_ref.at[c % 2], sem.at[c % 2])

    copy(0).start()
    for c in range(n_chunks):
        if c + 1 < n_chunks:
            copy(c + 1).start()
        copy(c).wait()
        dst_ref[c * rows:(c + 1) * rows, :] = stage_ref[c % 2].astype(dst_ref.dtype)


def _dense_ffn_body(x_ref, h_ref, wg_hbm, wu_hbm, wd_hbm, e0_hbm, e1_hbm, e2_hbm, o_ref, n0_hbm, n1_hbm, n2_hbm,
                    wg_ref, wu_ref, wd_ref, act_ref, st0_ref, st1_ref, st2_ref, nr0_ref, nr1_ref, nr2_ref,
                    in_sem, out_sem):
    s = pl.program_id(0)
    n_steps = pl.num_programs(0)
    stages = (st0_ref, st1_ref, st2_ref)
    narrow = (nr0_ref, nr1_ref, nr2_ref)

    def fetch(step, slot):
        return [pltpu.make_async_copy(src.at[pl.ds(step * st.shape[1], st.shape[1])], st.at[slot], in_sem.at[slot])
                for src, st in zip((e0_hbm, e1_hbm, e2_hbm), stages)]

    def flush(step, slot):
        return [pltpu.make_async_copy(nr.at[slot], dst.at[pl.ds(step * nr.shape[1], nr.shape[1])], out_sem.at[slot])
                for nr, dst in zip(narrow, (n0_hbm, n1_hbm, n2_hbm))]

    @pl.when(s == 0)
    def _():
        _load_as_bf16(wg_hbm, wg_ref, st0_ref, in_sem)
        _load_as_bf16(wu_hbm, wu_ref, st1_ref, in_sem)
        _load_as_bf16(wd_hbm, wd_ref, st2_ref, in_sem)
        for cp in fetch(0, 0):
            cp.start()

    @pl.when(s + 1 < n_steps)
    def _():
        for cp in fetch(s + 1, (s + 1) % 2):
            cp.start()

    x = x_ref[...]
    tf = act_ref.shape[2]
    for j in range(act_ref.shape[0]):
        _swiglu_hidden(x, wg_ref.at[:, j * tf:(j + 1) * tf], wu_ref.at[:, j * tf:(j + 1) * tf], act_ref, j)
    o_ref[...] = h_ref[...] + _swiglu_down(act_ref, wd_ref)

    slot = s % 2
    for cp in fetch(s, slot):
        cp.wait()

    @pl.when(s >= 2)
    def _():
        for cp in flush(s - 2, slot):
            cp.wait()

    for st, nr in zip(stages, narrow):
        nr[slot] = st[slot].astype(nr.dtype)
    for cp in flush(s, slot):
        cp.start()

    @pl.when(s == n_steps - 1)
    def _():
        @pl.when(s >= 1)
        def _():
            for cp in flush(s - 1, 1 - slot):
                cp.wait()

        for cp in flush(s, slot):
            cp.wait()


def _dense_ffn(xn, h, w_gate, w_up, w_down, extra):
    t, d = h.shape
    f = w_gate.shape[1]
    tm, tf = TM_FFN, TF_FFN
    n_steps = t // tm
    rows = [e.shape[0] // n_steps for e in extra]
    assert all(e.shape[0] == r * n_steps and r % 16 == 0 for e, r in zip(extra, rows))
    assert w_gate.shape[0] % rows[0] == 0 and w_up.shape[0] % rows[1] == 0 and w_down.shape[0] % rows[2] == 0
    assert extra[0].shape[1] == f and extra[1].shape[1] == f and extra[2].shape[1] == d
    row = pl.BlockSpec((tm, d), lambda i: (i, 0))
    hbm = pl.BlockSpec(memory_space=pl.ANY)
    outs = pl.pallas_call(
        _dense_ffn_body,
        grid=(n_steps,),
        in_specs=[row, row] + [hbm] * 6,
        out_specs=[row, hbm, hbm, hbm],
        out_shape=[jax.ShapeDtypeStruct((t, d), F32)] + [jax.ShapeDtypeStruct(e.shape, BF16) for e in extra],
        scratch_shapes=[pltpu.VMEM((d, f), BF16), pltpu.VMEM((d, f), BF16), pltpu.VMEM((f, d), BF16),
                        pltpu.VMEM((f // tf, tm, tf), BF16)]
                       + [pltpu.VMEM((2, r, e.shape[1]), F32) for e, r in zip(extra, rows)]
                       + [pltpu.VMEM((2, r, e.shape[1]), BF16) for e, r in zip(extra, rows)]
                       + [pltpu.SemaphoreType.DMA((2,)), pltpu.SemaphoreType.DMA((2,))],
        compiler_params=_params("arbitrary"),
        name="dense_ffn",
    )(xn, h, w_gate, w_up, w_down, *extra)
    return outs[0], outs[1:]


def _scatter_rows_body(zero_start_ref, s1_ref, s2_ref, h_ref, xs_ref, zeros_ref, tile_ref, zero_sem, sem):
    i = pl.program_id(0)
    tm = h_ref.shape[0]

    @pl.when(i == 0)
    def _():
        zeros_ref[...] = jnp.zeros_like(zeros_ref)

        def fill(k, c):
            dst = xs_ref.at[pl.ds(pl.multiple_of(zero_start_ref[k], SUBLANES), zeros_ref.shape[0])]
            cp = pltpu.make_async_copy(zeros_ref, dst, zero_sem)
            cp.start()
            cp.wait()
            return c

        lax.fori_loop(0, zero_start_ref.shape[0], fill, 0)

    half = i % 2

    def drain(which):
        for _ in range(TOP_K):
            pltpu.make_async_copy(tile_ref.at[which], xs_ref.at[pl.ds(0, tm)], sem.at[which]).wait()

    @pl.when(i >= 2)
    def _():
        drain(half)

    tile_ref[half] = h_ref[...]

    def issue(r, c):
        src = tile_ref.at[half, pl.ds(r, 1)]
        pltpu.make_async_copy(src, xs_ref.at[pl.ds(s1_ref[r], 1)], sem.at[half]).start(priority=0)
        pltpu.make_async_copy(src, xs_ref.at[pl.ds(s2_ref[r], 1)], sem.at[half]).start(priority=1)
        return c

    lax.fori_loop(0, tm, issue, 0, unroll=8)

    @pl.when(i == pl.num_programs(0) - 1)
    def _():
        @pl.when(i >= 1)
        def _():
            drain(1 - half)

        drain(half)


def _scatter_rows(h, slot1, slot2, zero_start, cap, zero_rows):
    t, d = h.shape
    tm = TM_SCATTER
    smem_tile = pl.BlockSpec((tm,), lambda i, zs: (i,), memory_space=pltpu.SMEM)
    grid_spec = pltpu.PrefetchScalarGridSpec(
        num_scalar_prefetch=1,
        grid=(t // tm,),
        in_specs=[smem_tile, smem_tile, pl.BlockSpec((tm, d), lambda i, zs: (i, 0))],
        out_specs=pl.BlockSpec(memory_space=pl.ANY),
        scratch_shapes=[pltpu.VMEM((zero_rows, d), h.dtype), pltpu.VMEM((2, tm, d), h.dtype),
                        pltpu.SemaphoreType.DMA(()), pltpu.SemaphoreType.DMA((2,))],
    )
    return pl.pallas_call(
        _scatter_rows_body,
        grid_spec=grid_spec,
        out_shape=jax.ShapeDtypeStruct((cap, d), h.dtype),
        compiler_params=_params("arbitrary"),
        name="scatter_rows",
    )(zero_start, slot1, slot2, h)


def _moe_ffn_body(tile_expert_ref, tile_valid_ref, x_ref, wg_ref, wu_ref, wd_ref, o_ref, act_ref):
    i = pl.program_id(0)
    j = pl.program_id(1)
    last = j == pl.num_programs(1) - 1
    valid = tile_valid_ref[i] > 0

    @pl.when(valid)
    def _():
        _swiglu_hidden(x_ref[...].astype(BF16), wg_ref.at[0], wu_ref.at[0], act_ref, j)

        @pl.when(last)
        def _():
            o_ref[...] = _swiglu_down(act_ref, wd_ref.at[0])

    @pl.when(last & jnp.logical_not(valid))
    def _():
        o_ref[...] = jnp.zeros_like(o_ref)


def _moe_ffn(xs, tile_expert, tile_valid, w_gate, w_up, w_down):
    cap, d = xs.shape
    f = w_gate.shape[2]
    tm, tf = TM_MOE, TF_FFN
    grid_spec = pltpu.PrefetchScalarGridSpec(
        num_scalar_prefetch=2,
        grid=(cap // tm, f // tf),
        in_specs=[
            pl.BlockSpec((tm, d), lambda i, j, te, tv: (i, 0)),
            pl.BlockSpec((1, d, tf), lambda i, j, te, tv: (te[i], 0, j)),
            pl.BlockSpec((1, d, tf), lambda i, j, te, tv: (te[i], 0, j)),
            pl.BlockSpec((1, f, d), lambda i, j, te, tv: (te[i], 0, 0)),
        ],
        out_specs=pl.BlockSpec((tm, d), lambda i, j, te, tv: (i, 0)),
        scratch_shapes=[pltpu.VMEM((f // tf, tm, tf), BF16)],
    )
    return pl.pallas_call(
        _moe_ffn_body,
        grid_spec=grid_spec,
        out_shape=jax.ShapeDtypeStruct((cap, d), F32),
        compiler_params=_params("parallel", "arbitrary"),
        name="moe_ffn",
    )(tile_expert, tile_valid, xs, w_gate, w_up, w_down)


def _moe_plan(idx, n_experts, tm):
    t = idx.shape[1]
    pairs = t * TOP_K
    cap = pairs + n_experts * tm
    e_flat = idx.reshape(1, pairs)
    onehot = (e_flat == jnp.arange(n_experts, dtype=jnp.int32)[:, None]).astype(jnp.int32)
    rank = jnp.sum(onehot * (jnp.cumsum(onehot, axis=1) - onehot), axis=0)
    counts = jnp.sum(onehot, axis=1)
    padded = ((counts + tm - 1) // tm) * tm
    ends = jnp.cumsum(padded)
    starts = ends - padded
    slot = jnp.sum(onehot * starts[:, None], axis=0) + rank
    tile_start = jnp.arange(cap // tm, dtype=jnp.int32) * tm
    tile_expert = jnp.sum((tile_start[:, None] >= ends[None, :]).astype(jnp.int32), axis=1)
    tile_valid = (tile_expert < n_experts).astype(jnp.int32)
    last_used = jnp.max(jnp.where(counts > 0, jnp.arange(n_experts, dtype=jnp.int32), 0))
    tile_expert = jnp.minimum(tile_expert, last_used).astype(jnp.int32)
    zero_rows = tm + SUBLANES
    tail = ends[-1] + jnp.arange(n_experts, dtype=jnp.int32) * tm
    first = jnp.concatenate([starts + counts, tail]) // SUBLANES * SUBLANES
    zero_start = jnp.minimum(first, cap - zero_rows).astype(jnp.int32)
    return slot.reshape(TOP_K, t).astype(jnp.int32), cap, tile_expert, tile_valid, zero_start, zero_rows


def _combine_body(s1_ref, s2_ref, next1_ref, next2_ref, h_ref, wgt_ref, g_ref, ys_ref, o_ref, y1_ref, y2_ref, sem,
                  *, apply_norm):
    i = pl.program_id(0)
    tm = h_ref.shape[0]

    def request(idx1_ref, idx2_ref, half):
        def issue(r, c):
            pltpu.make_async_copy(ys_ref.at[pl.ds(idx1_ref[r], 1)], y1_ref.at[half, pl.ds(r, 1)],
                                  sem.at[half]).start(priority=0)
            pltpu.make_async_copy(ys_ref.at[pl.ds(idx2_ref[r], 1)], y2_ref.at[half, pl.ds(r, 1)],
                                  sem.at[half]).start(priority=1)
            return c

        lax.fori_loop(0, tm, issue, 0, unroll=8)

    @pl.when(i == 0)
    def _():
        request(s1_ref, s2_ref, 0)

    @pl.when(i + 1 < pl.num_programs(0))
    def _():
        request(next1_ref, next2_ref, (i + 1) % 2)

    half = i % 2
    pltpu.make_async_copy(ys_ref.at[pl.ds(0, tm)], y1_ref.at[half], sem.at[half]).wait()
    pltpu.make_async_copy(ys_ref.at[pl.ds(0, tm)], y2_ref.at[half], sem.at[half]).wait()
    w = wgt_ref[...]
    hn = h_ref[...] + w[:, 0:1] * y1_ref[half] + w[:, 1:2] * y2_ref[half]
    o_ref[...] = _rms(hn, g_ref[...]) if apply_norm else hn


def _combine(h, ys, slot, wgt, g, *, apply_norm):
    t, d = h.shape
    tm = TM_COMBINE
    n_tiles = t // tm
    this_tile = pl.BlockSpec((tm,), lambda i: (i,), memory_space=pltpu.SMEM)
    next_tile = pl.BlockSpec((tm,), lambda i: (jnp.minimum(i + 1, n_tiles - 1),), memory_space=pltpu.SMEM)
    row = pl.BlockSpec((tm, d), lambda i: (i, 0))
    s1, s2 = slot[0], slot[1]
    return pl.pallas_call(
        functools.partial(_combine_body, apply_norm=apply_norm),
        grid=(n_tiles,),
        in_specs=[this_tile, this_tile, next_tile, next_tile, row,
                  pl.BlockSpec((tm, TOP_K), lambda i: (i, 0)),
                  pl.BlockSpec((1, d), lambda i: (0, 0)),
                  pl.BlockSpec(memory_space=pl.ANY)],
        out_specs=row,
        out_shape=jax.ShapeDtypeStruct((t, d), F32),
        scratch_shapes=[pltpu.VMEM((2, tm, d), F32), pltpu.VMEM((2, tm, d), F32), pltpu.SemaphoreType.DMA((2,))],
        compiler_params=_params("arbitrary"),
        name="combine",
    )(s1, s2, s1, s2, h, wgt, g.reshape(1, d), ys)


def _plain_norm_body(h_ref, g_ref, o_ref):
    o_ref[...] = _rms(h_ref[...], g_ref[...])


def _plain_norm(h, g):
    t, d = h.shape
    tm = TM_NORM
    row = pl.BlockSpec((tm, d), lambda i: (i, 0))
    return pl.pallas_call(
        _plain_norm_body,
        grid=(t // tm,),
        in_specs=[row, pl.BlockSpec((1, d), lambda i: (0, 0))],
        out_specs=row,
        out_shape=jax.ShapeDtypeStruct((t, d), F32),
        compiler_params=_params("parallel"),
        name="plain_norm",
    )(h, g.reshape(1, d))


def kernel(x, norm_mix, w_in, ssm_lambda_re, ssm_lambda_im, ssm_log_dt, ssm_b_re, ssm_b_im, ssm_c_re,
           ssm_c_im, ssm_d, ssm_w_glu, conv_w, w_br_a, w_br_b, w_br_c, w_out, norm_ffn, dense_w_gate,
           dense_w_up, dense_w_down, moe_w_router, moe_w_gate, moe_w_up, moe_w_down, final_norm):
    batch, seq, d = x.shape
    depth = w_in.shape[0]
    t = batch * seq
    w_attn = w_br_a.shape[1]
    w_ssm = w_br_b.shape[1]
    w_conv = w_br_c.shape[1]
    n_sc = w_ssm + 3 * w_conv
    n_gate = N_BRANCH * d
    n_experts = moe_w_router.shape[-1]

    h = x.reshape(t, d)
    normed = False
    for i in range(depth):
        qt, k, vt, u3, sc, gates = _inproj(h, norm_mix[i], w_in, i, batch=batch, seq=seq, width=w_attn,
                                           w_ssm=w_ssm, n_sc=n_sc, n_gate=n_gate)
        attn = _sb_attention(qt, k, vt)
        tables = _ssm_tables(ssm_lambda_re[i], ssm_lambda_im[i], ssm_log_dt[i], ssm_b_re[i], ssm_b_im[i],
                             ssm_c_re[i], ssm_c_im[i], seq // SSM_CHUNK)
        yssm = _ssm_scan(u3, tables, batch=batch, seq=seq)
        dense = i % 2 == 0
        j = i // 2
        merged = _merge(attn, yssm, sc, gates, h, ssm_d[i], ssm_w_glu[i].astype(BF16), conv_w[i],
                        w_br_a[i].astype(BF16), w_br_b[i].astype(BF16), w_br_c[i].astype(BF16),
                        w_out[i].astype(BF16), norm_ffn[i], None if dense else moe_w_router[j], seq=seq)
        if dense:
            h, xn = merged
            assert i + 1 < depth
            f = moe_w_gate.shape[-1]
            expert_f32 = (moe_w_gate[j].reshape(n_experts * d, f), moe_w_up[j].reshape(n_experts * d, f),
                          moe_w_down[j].reshape(n_experts * f, d))
            h, expert_bf16 = _dense_ffn(xn, h, dense_w_gate[j], dense_w_up[j], dense_w_down[j], expert_f32)
        else:
            h, xn, idx, wgt = merged
            slot, cap, tile_expert, tile_valid, zero_start, zero_rows = _moe_plan(idx, n_experts, TM_MOE)
            xs = _scatter_rows(xn, slot[0], slot[1], zero_start, cap, zero_rows)
            e_gate, e_up, e_down = expert_bf16
            ys = _moe_ffn(xs, tile_expert, tile_valid, e_gate.reshape(n_experts, d, -1),
                          e_up.reshape(n_experts, d, -1), e_down.reshape(n_experts, -1, d))
            last = i == depth - 1
            h = _combine(h, ys, slot, wgt, final_norm if last else norm_ffn[i], apply_norm=last)
            normed = last
    out = h if normed else _plain_norm(h, final_norm)
    return out.reshape(batch, seq, d)
```

```python
import functools

import jax
import jax.numpy as jnp
from jax import lax
from jax.experimental import pallas as pl
from jax.experimental.pallas import tpu as pltpu

F32 = jnp.float32
BF16 = jnp.bfloat16

EPS = 1e-6
HEAD_DIM = 64
SSM_CHUNK = 16
SSM_MAX_RE = -1e-4
CONV_K = 3
N_BRANCH = 3
TOP_K = 2
LOG2E = 1.4426950408889634
MASKED_LOG_WEIGHT = -1e30

LANES = 128
SUBLANES = 8
MXU_DIM = 256
VMEM_LIMIT_BYTES = 56 * 1024 * 1024

TM_PROJ = 512
TQ_ATTN = MXU_DIM
TM_MERGE = 512
TM_FFN = 256
TF_FFN = 1792
TM_MOE = 512
TM_SCATTER = 512
TM_COMBINE = 512
TM_NORM = 512


def _params(*sem):
    return pltpu.CompilerParams(dimension_semantics=sem, vmem_limit_bytes=VMEM_LIMIT_BYTES)


def _rms(x, g):
    ms = jnp.mean(x * x, axis=-1, keepdims=True)
    return x * lax.rsqrt(ms + EPS) * g


def _inproj_body(x_ref, g_ref, w_hbm, qt_ref, k_ref, vt_ref, u3_ref, sc_ref, gate_ref, w_ref, stage_ref, u_ref, sem,
                 *, layer, width, w_ssm, n_sc, n_gate, q_scale):
    @pl.when(pl.program_id(0) == 0)
    def _():
        _load_as_bf16(w_hbm.at[layer], w_ref, stage_ref, sem)

    xn = _rms(x_ref[...], g_ref[...]).astype(BF16)
    tm = xn.shape[0]
    tq = vt_ref.shape[-1]
    chunk = 512

    def proj(c0, n):
        return jnp.dot(xn, w_ref[:, c0:c0 + n], preferred_element_type=F32)

    qt_ref[0] = (proj(0, width) * q_scale).T.astype(BF16)
    k_ref[...] = proj(width, width).astype(BF16)
    v = proj(2 * width, width)
    for kt in range(tm // tq):
        for p in range(width // LANES):
            vt_ref[0, p, kt] = v[kt * tq:(kt + 1) * tq, p * LANES:(p + 1) * LANES].T.astype(BF16)
    for c0 in range(0, n_sc, chunk):
        r = proj(3 * width + c0, chunk)
        sc_ref[:, c0:c0 + chunk] = r.astype(BF16)
        if c0 == 0:
            for lb in range(w_ssm // LANES):
                u_ref[lb] = r[:, lb * LANES:(lb + 1) * LANES]
    n_t = u3_ref.shape[0]
    for step in range(n_t):
        for lb in range(w_ssm // LANES):
            u3_ref[step, :, lb * LANES:(lb + 1) * LANES] = (
                u_ref[lb, pl.ds(step, tm // n_t, stride=n_t), :].astype(BF16))
    for c0 in range(0, n_gate, chunk):
        gate_ref[:, c0:c0 + chunk] = jax.nn.sigmoid(proj(3 * width + n_sc + c0, chunk)).astype(BF16)


def _inproj(h, g, w_in, layer, *, batch, seq, width, w_ssm, n_sc, n_gate):
    t, d = h.shape
    tm = TM_PROJ
    tq = TQ_ATTN
    n_pairs = width // LANES
    tiles_per_seq = seq // tm
    n_cols = 3 * width + n_sc + n_gate
    stage_rows = 64
    assert w_in.shape[1:] == (d, n_cols) and d % stage_rows == 0 and w_ssm <= 512
    body = functools.partial(_inproj_body, layer=layer, width=width, w_ssm=w_ssm, n_sc=n_sc, n_gate=n_gate,
                             q_scale=HEAD_DIM ** -0.5)
    return pl.pallas_call(
        body,
        grid=(t // tm,),
        in_specs=[
            pl.BlockSpec((tm, d), lambda i: (i, 0)),
            pl.BlockSpec((1, d), lambda i: (0, 0)),
            pl.BlockSpec(memory_space=pl.ANY),
        ],
        out_specs=[
            pl.BlockSpec((1, width, tm), lambda i: (i // tiles_per_seq, 0, i % tiles_per_seq)),
            pl.BlockSpec((tm, width), lambda i: (i, 0)),
            pl.BlockSpec((1, n_pairs, tm // tq, LANES, tq),
                         lambda i: (i // tiles_per_seq, 0, i % tiles_per_seq, 0, 0)),
            pl.BlockSpec((SSM_CHUNK, tm // SSM_CHUNK, w_ssm), lambda i: (0, i, 0)),
            pl.BlockSpec((tm, n_sc), lambda i: (i, 0)),
            pl.BlockSpec((tm, n_gate), lambda i: (i, 0)),
        ],
        out_shape=[
            jax.ShapeDtypeStruct((batch, width, seq), BF16),
            jax.ShapeDtypeStruct((t, width), BF16),
            jax.ShapeDtypeStruct((batch, n_pairs, seq // tq, LANES, tq), BF16),
            jax.ShapeDtypeStruct((SSM_CHUNK, t // SSM_CHUNK, w_ssm), BF16),
            jax.ShapeDtypeStruct((t, n_sc), BF16),
            jax.ShapeDtypeStruct((t, n_gate), BF16),
        ],
        scratch_shapes=[pltpu.VMEM((d, n_cols), BF16), pltpu.VMEM((2, stage_rows, n_cols), F32),
                        pltpu.VMEM((w_ssm // LANES, tm, LANES), F32), pltpu.SemaphoreType.DMA((2,))],
        compiler_params=_params("arbitrary"),
        name="inproj",
    )(h, g.reshape(1, d), w_in)


def _sb_attn_body(qt_ref, k_ref, vt_ref, o_ref, acc_ref, carry_ref, lb_a_ref, sp_a_ref, lb_b_ref, sp_b_ref,
                  *, tq, n_pairs):
    i = pl.program_id(1)
    buf_a = (lb_a_ref, sp_a_ref)
    buf_b = (lb_b_ref, sp_b_ref)
    n_heads = 2 * n_pairs
    ones_rows = carry_ref.shape[1]
    key = lax.broadcasted_iota(jnp.int32, (tq, tq), 0)
    qry = lax.broadcasted_iota(jnp.int32, (tq, tq), 1)
    valid = key < qry
    er = lax.broadcasted_iota(jnp.int32, (tq + ones_rows, tq), 0)
    ec = lax.broadcasted_iota(jnp.int32, (tq + ones_rows, tq), 1)
    later_and_sum = jnp.where((ec > er) | (er >= tq), 1.0, 0.0).astype(BF16)

    feat = lax.broadcasted_iota(jnp.int32, (LANES, tq), 0)
    q_heads = []
    for p in range(n_pairs):
        qp = qt_ref[0, p * LANES:(p + 1) * LANES, :]
        zero = jnp.zeros_like(qp)
        q_heads.append(jnp.where(feat < HEAD_DIM, qp, zero))
        q_heads.append(jnp.where(feat >= HEAD_DIM, qp, zero))

    def scores(j, buf, masked):
        lb_ref, sp_ref = buf
        start = pl.multiple_of(j * tq, tq)
        for h in range(n_heads):
            p = h // 2
            kblk = k_ref[pl.ds(start, tq), p * LANES:(p + 1) * LANES]
            z = jnp.dot(kblk, q_heads[h], preferred_element_type=F32)
            l1p = jnp.log(1.0 + jnp.exp2(jnp.abs(z) * (-LOG2E)))
            sp = jnp.maximum(z, 0.0) + l1p
            lb = z - sp
            if masked:
                sp = jnp.where(valid, sp, 0.0)
                lb = jnp.where(valid, lb, MASKED_LOG_WEIGHT)
            lb_ref[h] = lb
            sp_ref[h] = sp.astype(BF16)

    def weights(j, buf):
        lb_ref, sp_ref = buf
        ws, sums = [], []
        for h in range(n_heads):
            ts = jnp.dot(later_and_sum, sp_ref[h], preferred_element_type=F32)
            ws.append(jnp.exp(lb_ref[h] - ts[0:tq]).astype(BF16))
            sums.append(ts[tq:tq + ones_rows])
        for h in range(n_heads):
            p, hh = divmod(h, 2)
            vt = vt_ref[0, p, j, hh * HEAD_DIM:(hh + 1) * HEAD_DIM, :]
            pv = jnp.dot(vt, ws[h], preferred_element_type=F32)
            carry = carry_ref[h]
            scale = jnp.exp(-carry[0:1, :])
            rows = slice(h * HEAD_DIM, (h + 1) * HEAD_DIM)
            acc_ref[rows, :] += pv * scale
            carry_ref[h] = carry + sums[h]

    def earlier_keys_matter():
        return (jnp.max(jnp.exp(-carry_ref[:, 0, :])) > 0.0).astype(jnp.int32)

    acc_ref[...] = jnp.zeros_like(acc_ref)
    carry_ref[...] = jnp.zeros_like(carry_ref)

    @pl.when(i == 0)
    def _():
        scores(0, buf_a, True)
        weights(0, buf_a)

    @pl.when(i > 0)
    def _():
        scores(i, buf_a, True)
        scores(i - 1, buf_b, False)
        weights(i, buf_a)
        weights(i - 1, buf_b)

    rest = i - 2

    @pl.when(jnp.where(rest >= 0, earlier_keys_matter(), 0) > 0)
    def _():
        scores(rest, buf_a, False)

        def two_tiles(c):
            m, _ = c
            j = rest - 1 - 2 * m
            scores(j, buf_b, False)
            weights(j + 1, buf_a)
            scores(j - 1, buf_a, False)
            weights(j, buf_b)
            return m + 1, earlier_keys_matter()

        _, alive = lax.while_loop(lambda c: (c[0] < rest // 2) & (c[1] > 0), two_tiles,
                                  (jnp.int32(0), jnp.int32(1)))

        @pl.when((alive > 0) & (rest % 2 == 1))
        def _():
            scores(0, buf_b, False)
            weights(1, buf_a)
            weights(0, buf_b)

        @pl.when((alive > 0) & (rest % 2 == 0))
        def _():
            weights(0, buf_a)

    for p in range(n_pairs):
        o_ref[:, p * LANES:(p + 1) * LANES] = acc_ref[p * LANES:(p + 1) * LANES, :].T.astype(o_ref.dtype)


def _sb_attention(qt, k, vt):
    batch, width, seq = qt.shape
    n_pairs, nq, _, tq = vt.shape[1:]
    n_heads = 2 * n_pairs
    return pl.pallas_call(
        functools.partial(_sb_attn_body, tq=tq, n_pairs=n_pairs),
        grid=(batch, nq),
        in_specs=[
            pl.BlockSpec((1, width, tq), lambda b, i: (b, 0, i)),
            pl.BlockSpec((seq, width), lambda b, i: (b, 0)),
            pl.BlockSpec((1, n_pairs, nq, LANES, tq), lambda b, i: (b, 0, 0, 0, 0)),
        ],
        out_specs=pl.BlockSpec((tq, width), lambda b, i: (b * nq + i, 0)),
        out_shape=jax.ShapeDtypeStruct((batch * seq, width), BF16),
        scratch_shapes=[pltpu.VMEM((width, tq), F32), pltpu.VMEM((n_heads, 16, tq), F32),
                        pltpu.VMEM((n_heads, tq, tq), F32), pltpu.VMEM((n_heads, tq, tq), BF16),
                        pltpu.VMEM((n_heads, tq, tq), F32), pltpu.VMEM((n_heads, tq, tq), BF16)],
        compiler_params=_params("parallel", "arbitrary"),
        name="sb_attention",
    )(qt, k, vt)


def _ssm_tables(lam_re, lam_im, log_dt, b_re, b_im, c_re, c_im, n_chunks):
    g, p = lam_re.shape
    c = SSM_CHUNK
    lr = jnp.minimum(lam_re.astype(F32), SSM_MAX_RE)
    li = lam_im.astype(F32)
    dt = jnp.exp(log_dt.astype(F32))[:, None]

    def lbar_pow(k):
        mag = jnp.exp(k * (lr * dt))
        ang = k * (li * dt)
        return mag * jnp.cos(ang), mag * jnp.sin(ang)

    ab_re, ab_im = lbar_pow(1.0)
    den = lr * lr + li * li
    nr = ab_re - 1.0
    ni = ab_im
    f_re = (nr * lr + ni * li) / den
    f_im = (ni * lr - nr * li) / den
    br = b_re.astype(F32)
    bi = b_im.astype(F32)
    bb_re = f_re[..., None] * br - f_im[..., None] * bi
    bb_im = f_re[..., None] * bi + f_im[..., None] * br
    cr = c_re.astype(F32)
    ci = c_im.astype(F32)

    eye = jnp.eye(g, dtype=F32)

    def block_diag(x):
        _, a, b = x.shape
        return (x[:, :, None, :] * eye[:, None, :, None]).reshape(g * a, g * b)

    b_in = jnp.concatenate([block_diag(bb_re.transpose(0, 2, 1)),
                            block_diag(bb_im.transpose(0, 2, 1))], axis=1)
    c_out = jnp.concatenate([block_diag(cr.transpose(0, 2, 1)),
                             block_diag(-ci.transpose(0, 2, 1))], axis=0)
    lbar = jnp.stack([ab_re.reshape(-1), ab_im.reshape(-1)], axis=0)

    steps = []
    s = 1
    while s < n_chunks:
        a_re, a_im = lbar_pow(float(c * s))
        steps.append(jnp.stack([a_re.reshape(-1, MXU_DIM), a_im.reshape(-1, MXU_DIM)], axis=1))
        s *= 2
    step = jnp.stack(steps, axis=1)
    return b_in.astype(BF16), c_out.astype(BF16), lbar, step


def _ssm_body(u_ref, b_ref, c_ref, lbar_ref, step_ref, y_ref, z_ref, *, n_chunks, batch):
    s = pl.program_id(0)
    c = u_ref.shape[0]
    n_blk = z_ref.shape[0]
    half = n_blk // 2
    bw = z_ref.shape[2]

    @pl.when(s == 0)
    def _():
        z_ref[...] = jnp.zeros_like(z_ref)

    def advance(ut):
        for pb in range(half):
            lo, hi = pb * bw, (pb + 1) * bw
            re = z_ref[pb]
            im = z_ref[half + pb]
            a_re = lbar_ref[0:1, lo:hi]
            a_im = lbar_ref[1:2, lo:hi]
            x_re = jnp.dot(ut, b_ref[:, lo:hi], preferred_element_type=F32)
            x_im = jnp.dot(ut, b_ref[:, half * bw + lo:half * bw + hi], preferred_element_type=F32)
            z_ref[pb] = a_re * re - a_im * im + x_re
            z_ref[half + pb] = a_re * im + a_im * re + x_im

    @pl.when(s < c)
    def _():
        advance(u_ref[s])

    @pl.when(s == c)
    def _():
        chunk = lax.broadcasted_iota(jnp.int32, (n_chunks, bw), 0)

        def shifted(x, n):
            return jnp.where(chunk >= n, pltpu.roll(x, n, axis=0), 0.0)

        def scan_block(idx, carry):
            b = idx // half
            pb = idx % half
            rows = pl.ds(pl.multiple_of(b * n_chunks, n_chunks), n_chunks)
            re = z_ref[pb, rows, :]
            im = z_ref[half + pb, rows, :]
            n = 1
            k = 0
            while n < n_chunks:
                a_re = step_ref[pb, k, 0:1, :]
                a_im = step_ref[pb, k, 1:2, :]
                re_s = shifted(re, n)
                im_s = shifted(im, n)
                re, im = re + a_re * re_s - a_im * im_s, im + a_re * im_s + a_im * re_s
                n *= 2
                k += 1
            z_ref[pb, rows, :] = shifted(re, 1)
            z_ref[half + pb, rows, :] = shifted(im, 1)
            return carry

        lax.fori_loop(0, batch * half, scan_block, 0)

    def emit():
        state = jnp.concatenate([z_ref[blk].astype(BF16) for blk in range(n_blk)], axis=1)
        y_ref[0] = jnp.dot(state, c_ref[...], preferred_element_type=F32).astype(y_ref.dtype)

    @pl.when(s == c)
    def _():
        advance(u_ref[0])

    @pl.when((s > c) & (s < 2 * c))
    def _():
        emit()
        advance(u_ref[s - c])

    @pl.when(s == 2 * c)
    def _():
        emit()


def _ssm_scan(u3, tables, *, batch, seq):
    b_in, c_out, lbar, step = tables
    c, rows, width = u3.shape
    n_chunks = seq // c
    assert rows == batch * n_chunks
    n_blk = b_in.shape[1] // MXU_DIM
    whole = lambda shape: pl.BlockSpec(shape, lambda s: (0,) * len(shape), pipeline_mode=pl.Buffered(1))
    return pl.pallas_call(
        functools.partial(_ssm_body, n_chunks=n_chunks, batch=batch),
        grid=(2 * c + 1,),
        in_specs=[whole(u3.shape), whole(b_in.shape), whole(c_out.shape), whole(lbar.shape), whole(step.shape)],
        out_specs=pl.BlockSpec((1, rows, width), lambda s: (jnp.maximum(s - c - 1, 0), 0, 0)),
        out_shape=jax.ShapeDtypeStruct((c, rows, width), BF16),
        scratch_shapes=[pltpu.VMEM((n_blk, rows, MXU_DIM), F32)],
        compiler_params=_params("arbitrary"),
        name="ssm_scan",
    )(u3, b_in, c_out, lbar, step)


def _gelu_tanh(x):
    return 0.5 * x * (1.0 + jnp.tanh(0.7978845608028654 * (x + 0.044715 * (x * x * x))))


def _top2(xn, wr):
    logits = lax.dot_general(wr, xn, (((1,), (1,)), ((), ())), preferred_element_type=F32,
                             precision=lax.Precision.HIGHEST)
    n_e = logits.shape[0]
    e_idx = lax.broadcasted_iota(jnp.int32, logits.shape, 0)
    m1 = jnp.max(logits, axis=0, keepdims=True)
    i1 = jnp.min(jnp.where(logits == m1, e_idx, n_e), axis=0, keepdims=True)
    rest = jnp.where(e_idx == i1, -jnp.inf, logits)
    m2 = jnp.max(rest, axis=0, keepdims=True)
    i2 = jnp.min(jnp.where(rest == m2, e_idx, n_e), axis=0, keepdims=True)
    e2 = jnp.exp(m2 - m1)
    w1 = 1.0 / (1.0 + e2)
    return jnp.concatenate([i1, i2], axis=0), jnp.concatenate([w1, e2 * w1], axis=0)


def _merge_body(attn_ref, yssm_ref, sc_ref, halo_ref, gate_ref, h_ref, dskip_ref, wglu_ref, convw_ref,
                wa_ref, wb_ref, wc_ref, wout_ref, gn_ref, *rest, tiles_per_seq, w_ssm, w_conv, route):
    if route:
        wr_ref, hout_ref, xn_ref, idx_ref, wgt_ref, y_ref = rest
    else:
        hout_ref, xn_ref, y_ref = rest
    i = pl.program_id(0)
    tm = h_ref.shape[0]
    d = h_ref.shape[1]

    n_t = yssm_ref.shape[0]
    n_lb = w_ssm // LANES
    for step in range(n_t):
        for lb in range(n_lb):
            y_ref[lb, pl.ds(step, tm // n_t, stride=n_t), :] = (
                yssm_ref[step, :, lb * LANES:(lb + 1) * LANES].astype(F32))
    y = jnp.concatenate([y_ref[lb] for lb in range(n_lb)], axis=1)
    u = sc_ref[:, 0:w_ssm].astype(F32)
    yb = _gelu_tanh(y + dskip_ref[...] * u)
    yb = yb * jax.nn.sigmoid(jnp.dot(yb.astype(BF16), wglu_ref[...], preferred_element_type=F32))
    y_b = jnp.dot(yb.astype(BF16), wb_ref[...], preferred_element_type=F32)

    o_h, o_b, o_c = w_ssm, w_ssm + w_conv, w_ssm + 2 * w_conv
    uc = sc_ref[:, o_c:o_c + w_conv].astype(F32) * sc_ref[:, o_h:o_h + w_conv].astype(F32)
    halo = halo_ref[:, o_c:o_c + w_conv].astype(F32) * halo_ref[:, o_h:o_h + w_conv].astype(F32)
    halo = jnp.where(i % tiles_per_seq == 0, 0.0, halo)
    row = lax.broadcasted_iota(jnp.int32, (tm, w_conv), 0)
    prev1 = jnp.where(row == 0, halo[SUBLANES - 1:SUBLANES, :], pltpu.roll(uc, 1, axis=0))
    prev2 = jnp.where(row == 0, halo[SUBLANES - 2:SUBLANES - 1, :],
                      jnp.where(row == 1, halo[SUBLANES - 1:SUBLANES, :], pltpu.roll(uc, 2, axis=0)))
    conv = convw_ref[0:1, :] * prev2 + convw_ref[1:2, :] * prev1 + convw_ref[2:3, :] * uc
    yc = sc_ref[:, o_b:o_b + w_conv].astype(F32) * conv
    y_c = jnp.dot(yc.astype(BF16), wc_ref[...], preferred_element_type=F32)

    y_a = jnp.dot(attn_ref[...], wa_ref[...], preferred_element_type=F32)

    merged = (gate_ref[:, 0:d] * y_a.astype(BF16) + gate_ref[:, d:2 * d] * y_b.astype(BF16)
              + gate_ref[:, 2 * d:3 * d] * y_c.astype(BF16))
    hn = h_ref[...] + jnp.dot(merged, wout_ref[...], preferred_element_type=F32)
    hout_ref[...] = hn
    xn = _rms(hn, gn_ref[...])
    xn_ref[...] = xn.astype(xn_ref.dtype)
    if route:
        idx, wgt = _top2(xn, wr_ref[...])
        idx_ref[0] = idx
        wgt_ref[0] = wgt


def _merge(attn, yssm, sc, gates, h, d_skip, w_glu, conv_w, w_a, w_b, w_c, w_out, g_ffn, w_router, *, seq):
    t, d = h.shape
    tm = TM_MERGE
    nt = t // tm
    n_t, _, w_ssm = yssm.shape
    w_conv = conv_w.shape[1]
    w_attn = attn.shape[1]
    n_sc = sc.shape[1]
    route = w_router is not None
    body = functools.partial(_merge_body, tiles_per_seq=seq // tm, w_ssm=w_ssm, w_conv=w_conv, route=route)
    full = lambda shape: pl.BlockSpec(shape, lambda i: (0,) * len(shape))
    row = pl.BlockSpec((tm, d), lambda i: (i, 0))
    halo_blocks = tm // SUBLANES
    in_specs = [
        pl.BlockSpec((tm, w_attn), lambda i: (i, 0)),
        pl.BlockSpec((n_t, tm // n_t, w_ssm), lambda i: (0, i, 0)),
        pl.BlockSpec((tm, n_sc), lambda i: (i, 0)),
        pl.BlockSpec((SUBLANES, n_sc), lambda i: (jnp.maximum(i * halo_blocks - 1, 0), 0)),
        pl.BlockSpec((tm, N_BRANCH * d), lambda i: (i, 0)),
        row,
        full((1, w_ssm)), full((w_ssm, w_ssm)), full((CONV_K, w_conv)),
        full((w_attn, d)), full((w_ssm, d)), full((w_conv, d)), full((d, d)), full((1, d)),
    ]
    args = [attn, yssm, sc, sc, gates, h, d_skip.reshape(1, w_ssm).astype(F32), w_glu, conv_w.astype(F32),
            w_a, w_b, w_c, w_out, g_ffn.reshape(1, d)]
    out_specs = [row, row]
    out_shape = [jax.ShapeDtypeStruct((t, d), F32), jax.ShapeDtypeStruct((t, d), F32 if route else BF16)]
    if route:
        n_e = w_router.shape[1]
        in_specs.append(full((n_e, d)))
        args.append(w_router.T.astype(F32))
        lanes = pl.BlockSpec((1, TOP_K, tm), lambda i: (i, 0, 0))
        out_specs += [lanes, lanes]
        out_shape += [jax.ShapeDtypeStruct((nt, TOP_K, tm), jnp.int32), jax.ShapeDtypeStruct((nt, TOP_K, tm), F32)]
    outs = pl.pallas_call(
        body,
        grid=(nt,),
        in_specs=in_specs,
        out_specs=out_specs,
        out_shape=out_shape,
        scratch_shapes=[pltpu.VMEM((w_ssm // LANES, tm, LANES), F32)],
        compiler_params=_params("parallel"),
        name="merge",
    )(*args)
    if not route:
        return outs
    hn, xn, idx, wgt = outs
    return hn, xn, idx.transpose(1, 0, 2).reshape(TOP_K, t), wgt.transpose(0, 2, 1).reshape(t, TOP_K)


def _swiglu_hidden(x, wg_ref, wu_ref, act_ref, j):
    gate = jnp.dot(x, wg_ref[...], preferred_element_type=F32)
    up = jnp.dot(x, wu_ref[...], preferred_element_type=F32)
    act_ref[j] = (gate * jax.nn.sigmoid(gate) * up).astype(act_ref.dtype)


def _swiglu_down(act_ref, wd_ref):
    act = jnp.concatenate([act_ref[jj] for jj in range(act_ref.shape[0])], axis=1)
    return jnp.dot(act, wd_ref[...], preferred_element_type=F32)


def _load_as_bf16(src_hbm, dst_ref, stage_ref, sem):
    rows = stage_ref.shape[1]
    n_chunks = src_hbm.shape[0] // rows

    def copy(c):
        return pltpu.make_async_copy(src_hbm.at[pl.ds(c * rows, rows)], stage_ref.at[c % 2], sem.at[c % 2])

    copy(0).start()
    for c in range(n_chunks):
        if c + 1 < n_chunks:
            copy(c + 1).start()
        copy(c).wait()
        dst_ref[c * rows:(c + 1) * rows, :] = stage_ref[c % 2].astype(dst_ref.dtype)


def _dense_ffn_body(x_ref, h_ref, wg_hbm, wu_hbm, wd_hbm, e0_hbm, e1_hbm, e2_hbm, o_ref, n0_hbm, n1_hbm, n2_hbm,
                    wg_ref, wu_ref, wd_ref, act_ref, st0_ref, st1_ref, st2_ref, nr0_ref, nr1_ref, nr2_ref,
                    in_sem, out_sem):
    s = pl.program_id(0)
    n_steps = pl.num_programs(0) - 1
    stages = (st0_ref, st1_ref, st2_ref)
    narrow = (nr0_ref, nr1_ref, nr2_ref)

    def fetch(step, slot):
        return [pltpu.make_async_copy(src.at[pl.ds(step * st.shape[1], st.shape[1])], st.at[slot], in_sem.at[slot])
                for src, st in zip((e0_hbm, e1_hbm, e2_hbm), stages)]

    def flush(step, slot):
        return [pltpu.make_async_copy(nr.at[slot], dst.at[pl.ds(step * nr.shape[1], nr.shape[1])], out_sem.at[slot])
                for nr, dst in zip(narrow, (n0_hbm, n1_hbm, n2_hbm))]

    @pl.when(s == 0)
    def _():
        _load_as_bf16(wg_hbm, wg_ref, st0_ref, in_sem)
        _load_as_bf16(wu_hbm, wu_ref, st1_ref, in_sem)
        _load_as_bf16(wd_hbm, wd_ref, st2_ref, in_sem)
        for cp in fetch(0, 0):
            cp.start()

    @pl.when(s + 1 < n_steps)
    def _():
        for cp in fetch(s + 1, (s + 1) % 2):
            cp.start()

    tf = act_ref.shape[3]

    def hidden(half):
        x = x_ref[...]
        for j in range(act_ref.shape[1]):
            _swiglu_hidden(x, wg_ref.at[:, j * tf:(j + 1) * tf], wu_ref.at[:, j * tf:(j + 1) * tf],
                           act_ref.at[half], j)

    def down(half):
        o_ref[...] = h_ref[...] + _swiglu_down(act_ref.at[half], wd_ref)

    @pl.when(s == 0)
    def _():
        hidden(0)

    @pl.when((s >= 1) & (s < n_steps))
    def _():
        down((s - 1) % 2)
        hidden(s % 2)

    @pl.when(s == n_steps)
    def _():
        down((s - 1) % 2)

    @pl.when(s < n_steps)
    def _():
        slot = s % 2
        for cp in fetch(s, slot):
            cp.wait()

        @pl.when(s >= 2)
        def _():
            for cp in flush(s - 2, slot):
                cp.wait()

        for st, nr in zip(stages, narrow):
            nr[slot] = st[slot].astype(nr.dtype)
        for cp in flush(s, slot):
            cp.start()

        @pl.when(s == n_steps - 1)
        def _():
            @pl.when(s >= 1)
            def _():
                for cp in flush(s - 1, 1 - slot):
                    cp.wait()

            for cp in flush(s, slot):
                cp.wait()


def _dense_ffn(xn, h, w_gate, w_up, w_down, extra):
    t, d = h.shape
    f = w_gate.shape[1]
    tm, tf = TM_FFN, TF_FFN
    n_steps = t // tm
    rows = [e.shape[0] // n_steps for e in extra]
    assert all(e.shape[0] == r * n_steps and r % 16 == 0 for e, r in zip(extra, rows))
    assert w_gate.shape[0] % rows[0] == 0 and w_up.shape[0] % rows[1] == 0 and w_down.shape[0] % rows[2] == 0
    assert extra[0].shape[1] == f and extra[1].shape[1] == f and extra[2].shape[1] == d
    this_tile = pl.BlockSpec((tm, d), lambda i: (jnp.minimum(i, n_steps - 1), 0))
    last_tile = pl.BlockSpec((tm, d), lambda i: (jnp.maximum(i - 1, 0), 0))
    hbm = pl.BlockSpec(memory_space=pl.ANY)
    outs = pl.pallas_call(
        _dense_ffn_body,
        grid=(n_steps + 1,),
        in_specs=[this_tile, last_tile] + [hbm] * 6,
        out_specs=[last_tile, hbm, hbm, hbm],
        out_shape=[jax.ShapeDtypeStruct((t, d), F32)] + [jax.ShapeDtypeStruct(e.shape, BF16) for e in extra],
        scratch_shapes=[pltpu.VMEM((d, f), BF16), pltpu.VMEM((d, f), BF16), pltpu.VMEM((f, d), BF16),
                        pltpu.VMEM((2, f // tf, tm, tf), BF16)]
                       + [pltpu.VMEM((2, r, e.shape[1]), F32) for e, r in zip(extra, rows)]
                       + [pltpu.VMEM((2, r, e.shape[1]), BF16) for e, r in zip(extra, rows)]
                       + [pltpu.SemaphoreType.DMA((2,)), pltpu.SemaphoreType.DMA((2,))],
        compiler_params=_params("arbitrary"),
        name="dense_ffn",
    )(xn, h, w_gate, w_up, w_down, *extra)
    return outs[0], outs[1:]


def _scatter_rows_body(zero_start_ref, s1_ref, s2_ref, h_ref, xs_ref, zeros_ref, tile_ref, zero_sem, sem):
    i = pl.program_id(0)
    tm = h_ref.shape[0]

    @pl.when(i == 0)
    def _():
        zeros_ref[...] = jnp.zeros_like(zeros_ref)

        def fill(k, c):
            dst = xs_ref.at[pl.ds(pl.multiple_of(zero_start_ref[k], SUBLANES), zeros_ref.shape[0])]
            cp = pltpu.make_async_copy(zeros_ref, dst, zero_sem)
            cp.start()
            cp.wait()
            return c

        lax.fori_loop(0, zero_start_ref.shape[0], fill, 0)

    half = i % 2

    def drain(which):
        for _ in range(TOP_K):
            pltpu.make_async_copy(tile_ref.at[which], xs_ref.at[pl.ds(0, tm)], sem.at[which]).wait()

    @pl.when(i >= 2)
    def _():
        drain(half)

    tile_ref[half] = h_ref[...]

    def issue(r, c):
        src = tile_ref.at[half, pl.ds(r, 1)]
        pltpu.make_async_copy(src, xs_ref.at[pl.ds(s1_ref[r], 1)], sem.at[half]).start(priority=0)
        pltpu.make_async_copy(src, xs_ref.at[pl.ds(s2_ref[r], 1)], sem.at[half]).start(priority=1)
        return c

    lax.fori_loop(0, tm, issue, 0, unroll=8)

    @pl.when(i == pl.num_programs(0) - 1)
    def _():
        @pl.when(i >= 1)
        def _():
            drain(1 - half)

        drain(half)


def _scatter_rows(h, slot1, slot2, zero_start, cap, zero_rows):
    t, d = h.shape
    tm = TM_SCATTER
    smem_tile = pl.BlockSpec((tm,), lambda i, zs: (i,), memory_space=pltpu.SMEM)
    grid_spec = pltpu.PrefetchScalarGridSpec(
        num_scalar_prefetch=1,
        grid=(t // tm,),
        in_specs=[smem_tile, smem_tile, pl.BlockSpec((tm, d), lambda i, zs: (i, 0))],
        out_specs=pl.BlockSpec(memory_space=pl.ANY),
        scratch_shapes=[pltpu.VMEM((zero_rows, d), h.dtype), pltpu.VMEM((2, tm, d), h.dtype),
                        pltpu.SemaphoreType.DMA(()), pltpu.SemaphoreType.DMA((2,))],
    )
    return pl.pallas_call(
        _scatter_rows_body,
        grid_spec=grid_spec,
        out_shape=jax.ShapeDtypeStruct((cap, d), h.dtype),
        compiler_params=_params("arbitrary"),
        name="scatter_rows",
    )(zero_start, slot1, slot2, h)


def _moe_ffn_body(tile_expert_ref, tile_valid_ref, x_ref, wg_ref, wu_ref, wd_ref, o_ref, act_ref):
    i = pl.program_id(0)
    j = pl.program_id(1)
    last = j == pl.num_programs(1) - 1
    valid = tile_valid_ref[i] > 0

    @pl.when(valid)
    def _():
        _swiglu_hidden(x_ref[...].astype(BF16), wg_ref.at[0], wu_ref.at[0], act_ref, j)

        @pl.when(last)
        def _():
            o_ref[...] = _swiglu_down(act_ref, wd_ref.at[0])

    @pl.when(last & jnp.logical_not(valid))
    def _():
        o_ref[...] = jnp.zeros_like(o_ref)


def _moe_ffn(xs, tile_expert, tile_valid, w_gate, w_up, w_down):
    cap, d = xs.shape
    f = w_gate.shape[2]
    tm, tf = TM_MOE, TF_FFN
    grid_spec = pltpu.PrefetchScalarGridSpec(
        num_scalar_prefetch=2,
        grid=(cap // tm, f // tf),
        in_specs=[
            pl.BlockSpec((tm, d), lambda i, j, te, tv: (i, 0)),
            pl.BlockSpec((1, d, tf), lambda i, j, te, tv: (te[i], 0, j)),
            pl.BlockSpec((1, d, tf), lambda i, j, te, tv: (te[i], 0, j)),
            pl.BlockSpec((1, f, d), lambda i, j, te, tv: (te[i], 0, 0)),
        ],
        out_specs=pl.BlockSpec((tm, d), lambda i, j, te, tv: (i, 0)),
        scratch_shapes=[pltpu.VMEM((f // tf, tm, tf), BF16)],
    )
    return pl.pallas_call(
        _moe_ffn_body,
        grid_spec=grid_spec,
        out_shape=jax.ShapeDtypeStruct((cap, d), F32),
        compiler_params=_params("parallel", "arbitrary"),
        name="moe_ffn",
    )(tile_expert, tile_valid, xs, w_gate, w_up, w_down)


def _moe_plan(idx, n_experts, tm):
    t = idx.shape[1]
    pairs = t * TOP_K
    cap = pairs + n_experts * tm
    e_flat = idx.reshape(1, pairs)
    onehot = (e_flat == jnp.arange(n_experts, dtype=jnp.int32)[:, None]).astype(jnp.int32)
    rank = jnp.sum(onehot * (jnp.cumsum(onehot, axis=1) - onehot), axis=0)
    counts = jnp.sum(onehot, axis=1)
    padded = ((counts + tm - 1) // tm) * tm
    ends = jnp.cumsum(padded)
    starts = ends - padded
    slot = jnp.sum(onehot * starts[:, None], axis=0) + rank
    tile_start = jnp.arange(cap // tm, dtype=jnp.int32) * tm
    tile_expert = jnp.sum((tile_start[:, None] >= ends[None, :]).astype(jnp.int32), axis=1)
    tile_valid = (tile_expert < n_experts).astype(jnp.int32)
    last_used = jnp.max(jnp.where(counts > 0, jnp.arange(n_experts, dtype=jnp.int32), 0))
    tile_expert = jnp.minimum(tile_expert, last_used).astype(jnp.int32)
    zero_rows = tm + SUBLANES
    tail = ends[-1] + jnp.arange(n_experts, dtype=jnp.int32) * tm
    first = jnp.concatenate([starts + counts, tail]) // SUBLANES * SUBLANES
    zero_start = jnp.minimum(first, cap - zero_rows).astype(jnp.int32)
    return slot.reshape(TOP_K, t).astype(jnp.int32), cap, tile_expert, tile_valid, zero_start, zero_rows


def _combine_body(s1_ref, s2_ref, next1_ref, next2_ref, h_ref, wgt_ref, g_ref, ys_ref, o_ref, y1_ref, y2_ref, sem,
                  *, apply_norm):
    i = pl.program_id(0)
    tm = h_ref.shape[0]

    def request(idx1_ref, idx2_ref, half):
        def issue(r, c):
            pltpu.make_async_copy(ys_ref.at[pl.ds(idx1_ref[r], 1)], y1_ref.at[half, pl.ds(r, 1)],
                                  sem.at[half]).start(priority=0)
            pltpu.make_async_copy(ys_ref.at[pl.ds(idx2_ref[r], 1)], y2_ref.at[half, pl.ds(r, 1)],
                                  sem.at[half]).start(priority=1)
            return c

        lax.fori_loop(0, tm, issue, 0, unroll=8)

    @pl.when(i == 0)
    def _():
        request(s1_ref, s2_ref, 0)

    @pl.when(i + 1 < pl.num_programs(0))
    def _():
        request(next1_ref, next2_ref, (i + 1) % 2)

    half = i % 2
    pltpu.make_async_copy(ys_ref.at[pl.ds(0, tm)], y1_ref.at[half], sem.at[half]).wait()
    pltpu.make_async_copy(ys_ref.at[pl.ds(0, tm)], y2_ref.at[half], sem.at[half]).wait()
    w = wgt_ref[...]
    hn = h_ref[...] + w[:, 0:1] * y1_ref[half] + w[:, 1:2] * y2_ref[half]
    o_ref[...] = _rms(hn, g_ref[...]) if apply_norm else hn


def _combine(h, ys, slot, wgt, g, *, apply_norm):
    t, d = h.shape
    tm = TM_COMBINE
    n_tiles = t // tm
    this_tile = pl.BlockSpec((tm,), lambda i: (i,), memory_space=pltpu.SMEM)
    next_tile = pl.BlockSpec((tm,), lambda i: (jnp.minimum(i + 1, n_tiles - 1),), memory_space=pltpu.SMEM)
    row = pl.BlockSpec((tm, d), lambda i: (i, 0))
    s1, s2 = slot[0], slot[1]
    return pl.pallas_call(
        functools.partial(_combine_body, apply_norm=apply_norm),
        grid=(n_tiles,),
        in_specs=[this_tile, this_tile, next_tile, next_tile, row,
                  pl.BlockSpec((tm, TOP_K), lambda i: (i, 0)),
                  pl.BlockSpec((1, d), lambda i: (0, 0)),
                  pl.BlockSpec(memory_space=pl.ANY)],
        out_specs=row,
        out_shape=jax.ShapeDtypeStruct((t, d), F32),
        scratch_shapes=[pltpu.VMEM((2, tm, d), F32), pltpu.VMEM((2, tm, d), F32), pltpu.SemaphoreType.DMA((2,))],
        compiler_params=_params("arbitrary"),
        name="combine",
    )(s1, s2, s1, s2, h, wgt, g.reshape(1, d), ys)


def _plain_norm_body(h_ref, g_ref, o_ref):
    o_ref[...] = _rms(h_ref[...], g_ref[...])


def _plain_norm(h, g):
    t, d = h.shape
    tm = TM_NORM
    row = pl.BlockSpec((tm, d), lambda i: (i, 0))
    return pl.pallas_call(
        _plain_norm_body,
        grid=(t // tm,),
        in_specs=[row, pl.BlockSpec((1, d), lambda i: (0, 0))],
        out_specs=row,
        out_shape=jax.ShapeDtypeStruct((t, d), F32),
        compiler_params=_params("parallel"),
        name="plain_norm",
    )(h, g.reshape(1, d))


def kernel(x, norm_mix, w_in, ssm_lambda_re, ssm_lambda_im, ssm_log_dt, ssm_b_re, ssm_b_im, ssm_c_re,
           ssm_c_im, ssm_d, ssm_w_glu, conv_w, w_br_a, w_br_b, w_br_c, w_out, norm_ffn, dense_w_gate,
           dense_w_up, dense_w_down, moe_w_router, moe_w_gate, moe_w_up, moe_w_down, final_norm):
    batch, seq, d = x.shape
    depth = w_in.shape[0]
    t = batch * seq
    w_attn = w_br_a.shape[1]
    w_ssm = w_br_b.shape[1]
    w_conv = w_br_c.shape[1]
    n_sc = w_ssm + 3 * w_conv
    n_gate = N_BRANCH * d
    n_experts = moe_w_router.shape[-1]

    h = x.reshape(t, d)
    normed = False
    for i in range(depth):
        qt, k, vt, u3, sc, gates = _inproj(h, norm_mix[i], w_in, i, batch=batch, seq=seq, width=w_attn,
                                           w_ssm=w_ssm, n_sc=n_sc, n_gate=n_gate)
        attn = _sb_attention(qt, k, vt)
        tables = _ssm_tables(ssm_lambda_re[i], ssm_lambda_im[i], ssm_log_dt[i], ssm_b_re[i], ssm_b_im[i],
                             ssm_c_re[i], ssm_c_im[i], seq // SSM_CHUNK)
        yssm = _ssm_scan(u3, tables, batch=batch, seq=seq)
        dense = i % 2 == 0
        j = i // 2
        merged = _merge(attn, yssm, sc, gates, h, ssm_d[i], ssm_w_glu[i].astype(BF16), conv_w[i],
                        w_br_a[i].astype(BF16), w_br_b[i].astype(BF16), w_br_c[i].astype(BF16),
                        w_out[i].astype(BF16), norm_ffn[i], None if dense else moe_w_router[j], seq=seq)
        if dense:
            h, xn = merged
            assert i + 1 < depth
            f = moe_w_gate.shape[-1]
            expert_f32 = (moe_w_gate[j].reshape(n_experts * d, f), moe_w_up[j].reshape(n_experts * d, f),
                          moe_w_down[j].reshape(n_experts * f, d))
            h, expert_bf16 = _dense_ffn(xn, h, dense_w_gate[j], dense_w_up[j], dense_w_down[j], expert_f32)
        else:
            h, xn, idx, wgt = merged
            slot, cap, tile_expert, tile_valid, zero_start, zero_rows = _moe_plan(idx, n_experts, TM_MOE)
            xs = _scatter_rows(xn, slot[0], slot[1], zero_start, cap, zero_rows)
            e_gate, e_up, e_down = expert_bf16
            ys = _moe_ffn(xs, tile_expert, tile_valid, e_gate.reshape(n_experts, d, -1),
                          e_up.reshape(n_experts, d, -1), e_down.reshape(n_experts, -1, d))
            last = i == depth - 1
            h = _combine(h, ys, slot, wgt, final_norm if last else norm_ffn[i], apply_norm=last)
            normed = last
    out = h if normed else _plain_norm(h, final_norm)
    return out.reshape(batch, seq, d)
```
